```python
import math
import jax, jax.numpy as jnp
from jax import lax
import numpy as np

D_MODEL = 1024
BATCH = 32
SEQ = 2048
DEPTH = 1

CHUNK = 64
D_MIX = 2 * D_MODEL
SSD_WIDTH = D_MIX // 2
SSD_HEAD_DIM = 64
SSD_HEADS = SSD_WIDTH // SSD_HEAD_DIM
SSD_GROUPS = 2
SSD_STATE = 128
SSD_CONV = 4
SSD_XBC = SSD_WIDTH + 2 * SSD_GROUPS * SSD_STATE
ML_WIDTH = D_MIX - SSD_WIDTH
ML_HEADS = 8
ML_HEAD_DIM = ML_WIDTH // ML_HEADS
ML_CONV = 4
IN_COLS = (SSD_WIDTH + SSD_XBC + SSD_HEADS) + (4 * ML_WIDTH + 2 * ML_HEADS)
MOE_GROUPS = 4
EXPERTS_PER_GROUP = 8
N_EXPERTS = MOE_GROUPS * EXPERTS_PER_GROUP
TOP_K = 2
D_EXPERT = D_MODEL // 2
EPS = 1e-6
STAB_INIT = -1e30

kernel_name = "hybrid_ssd_mlstm_hiermoe_block"


def rmsnorm(x, w):
    xf = x.astype(jnp.float32)
    y = xf * lax.rsqrt(jnp.mean(xf * xf, axis=-1, keepdims=True) + EPS)
    return (y * w.astype(jnp.float32)).astype(x.dtype)


def group_rmsnorm(y, w, groups):
    b, l, c = y.shape
    yg = y.reshape(b, l, groups, c // groups)
    yg = yg * lax.rsqrt(jnp.mean(yg * yg, axis=-1, keepdims=True) + EPS)
    return yg.reshape(b, l, c) * w.astype(jnp.float32)


def causal_dwconv(x, w, b):
    k, c = w.shape
    y = lax.conv_general_dilated(x, w[:, None, :].astype(x.dtype), window_strides=(1,),
                                 padding=[(k - 1, 0)], dimension_numbers=("NWC", "WIO", "NWC"),
                                 feature_group_count=c)
    return y + b.astype(x.dtype)


def ssd_mixer(z, xbc, dt_raw, conv_w, conv_b, dt_bias, a_log, d_skip, norm_w):
    f32 = jnp.float32
    bsz, seq, _ = z.shape
    nc = seq // CHUNK
    r = SSD_HEADS // SSD_GROUPS
    xbc = jax.nn.silu(causal_dwconv(xbc, conv_w, conv_b)).astype(f32)
    xs = xbc[..., :SSD_WIDTH].reshape(bsz, seq, SSD_HEADS, SSD_HEAD_DIM)
    bm = xbc[..., SSD_WIDTH:SSD_WIDTH + SSD_GROUPS * SSD_STATE].reshape(bsz, seq, SSD_GROUPS, SSD_STATE)
    cm = xbc[..., SSD_WIDTH + SSD_GROUPS * SSD_STATE:].reshape(bsz, seq, SSD_GROUPS, SSD_STATE)
    dt = jax.nn.softplus(dt_raw.astype(f32) + dt_bias.astype(f32))
    a = -jnp.exp(a_log.astype(f32))
    xdt = (xs * dt[..., None]).reshape(bsz, nc, CHUNK, SSD_GROUPS, r, SSD_HEAD_DIM)
    xdt = jnp.moveaxis(xdt, 1, 0)
    adt = jnp.transpose((dt * a).reshape(bsz, nc, CHUNK, SSD_GROUPS, r), (1, 0, 3, 4, 2))
    bc = jnp.moveaxis(bm.reshape(bsz, nc, CHUNK, SSD_GROUPS, SSD_STATE), 1, 0)
    cc = jnp.moveaxis(cm.reshape(bsz, nc, CHUNK, SSD_GROUPS, SSD_STATE), 1, 0)
    mask = jnp.tril(jnp.ones((CHUNK, CHUNK), dtype=bool))

    def step(state, inp):
        x_c, a_c, b_c, c_c = inp
        a_cs = jnp.cumsum(a_c, axis=-1)
        seg = jnp.where(mask, a_cs[..., :, None] - a_cs[..., None, :], -jnp.inf)
        lmat = jnp.exp(seg)
        cb = jnp.einsum('blgn,bsgn->bgls', c_c, b_c)
        y = jnp.einsum('bgls,bgrls,bsgrp->blgrp', cb, lmat, x_c)
        y = y + jnp.einsum('blgn,bgrpn,bgrl->blgrp', c_c, state, jnp.exp(a_cs))
        decay = jnp.exp(a_cs[..., -1:] - a_cs)
        state = state * jnp.exp(a_cs[..., -1])[..., None, None] + \
            jnp.einsum('blgn,bgrl,blgrp->bgrpn', b_c, decay, x_c)
        return state, y

    state0 = jnp.zeros((bsz, SSD_GROUPS, r, SSD_HEAD_DIM, SSD_STATE), f32)
    _, y = lax.scan(step, state0, (xdt, adt, bc, cc))
    y = jnp.moveaxis(y, 0, 1).reshape(bsz, seq, SSD_HEADS, SSD_HEAD_DIM)
    y = y + d_skip.astype(f32)[:, None] * xs
    y = y.reshape(bsz, seq, SSD_WIDTH) * jax.nn.silu(z.astype(f32))
    return group_rmsnorm(y, norm_w, SSD_GROUPS).astype(z.dtype)


def mlstm_mixer(qk, v, o_pre, i_pre, f_pre, conv_w, conv_b, i_bias, f_bias, norm_w):
    f32 = jnp.float32
    bsz, seq, _ = v.shape
    nc = seq // CHUNK
    qk = jax.nn.silu(causal_dwconv(qk, conv_w, conv_b)).astype(f32)
    shp = (bsz, nc, CHUNK, ML_HEADS, ML_HEAD_DIM)
    q = jnp.moveaxis(qk[..., :ML_WIDTH].reshape(shp), 1, 0)
    k = jnp.moveaxis(qk[..., ML_WIDTH:].reshape(shp), 1, 0) * (ML_HEAD_DIM ** -0.5)
    vv = jnp.moveaxis(v.astype(f32).reshape(shp), 1, 0)
    ig = i_pre.astype(f32) + i_bias.astype(f32)
    lf = jax.nn.log_sigmoid(f_pre.astype(f32) + f_bias.astype(f32))
    ig = jnp.transpose(ig.reshape(bsz, nc, CHUNK, ML_HEADS), (1, 0, 3, 2))
    lf = jnp.transpose(lf.reshape(bsz, nc, CHUNK, ML_HEADS), (1, 0, 3, 2))
    mask = jnp.tril(jnp.ones((CHUNK, CHUNK), dtype=bool))

    def step(carry, inp):
        cmem, nmem, m = carry
        q_c, k_c, v_c, i_c, f_c = inp
        bcum = jnp.cumsum(f_c, axis=-1)
        dmat = jnp.where(mask, bcum[..., :, None] - bcum[..., None, :] + i_c[..., None, :], -jnp.inf)
        m_inter = bcum + m[..., None]
        m_t = jnp.maximum(m_inter, jnp.max(dmat, axis=-1))
        w = jnp.exp(dmat - m_t[..., None])
        s = jnp.einsum('blhd,bshd->bhls', q_c, k_c) * w
        inter = jnp.exp(m_inter - m_t)
        num = jnp.einsum('bhls,bshv->blhv', s, v_c) + \
            jnp.einsum('blhk,bhkv->blhv', q_c, cmem) * jnp.swapaxes(inter, 1, 2)[..., None]
        den = jnp.sum(s, axis=-1) + inter * jnp.einsum('blhk,bhk->bhl', q_c, nmem)
        den = jnp.maximum(jnp.abs(den), jnp.exp(-m_t))
        h = num / jnp.swapaxes(den, 1, 2)[..., None]
        g = bcum[..., -1]
        a = g[..., None] - bcum + i_c
        m_new = jnp.maximum(g + m, jnp.max(a, axis=-1))
        wc = jnp.exp(a - m_new[..., None])
        carry_scale = jnp.exp(g + m - m_new)
        cmem = cmem * carry_scale[..., None, None] + jnp.einsum('bhs,bshk,bshv->bhkv', wc, k_c, v_c)
        nmem = nmem * carry_scale[..., None] + jnp.einsum('bhs,bshk->bhk', wc, k_c)
        return (cmem, nmem, m_new), h

    carry0 = (jnp.zeros((bsz, ML_HEADS, ML_HEAD_DIM, ML_HEAD_DIM), f32),
              jnp.zeros((bsz, ML_HEADS, ML_HEAD_DIM), f32),
              jnp.full((bsz, ML_HEADS), STAB_INIT, f32))
    _, h = lax.scan(step, carry0, (q, k, vv, ig, lf))
    h = jnp.moveaxis(h, 0, 1).reshape(bsz, seq, ML_HEADS, ML_HEAD_DIM)
    mu = jnp.mean(h, axis=-1, keepdims=True)
    var = jnp.mean((h - mu) ** 2, axis=-1, keepdims=True)
    h = ((h - mu) * lax.rsqrt(var + EPS)).reshape(bsz, seq, ML_WIDTH) * norm_w.astype(f32)
    h = jax.nn.sigmoid(o_pre.astype(f32)) * h
    return h.astype(v.dtype)


def hier_moe(h, wg, bg, we, be, w_gate, w_up, w_down):
    bsz, seq, d = h.shape
    t = bsz * seq
    xf = h.reshape(t, d)
    x32 = xf.astype(jnp.float32)
    p_group = jax.nn.softmax(x32 @ wg.astype(jnp.float32) + bg.astype(jnp.float32), axis=-1)
    pg, g_sel = lax.top_k(p_group, 1)
    e_logits = (x32 @ we.astype(jnp.float32) + be.astype(jnp.float32)).reshape(t, MOE_GROUPS, EXPERTS_PER_GROUP)
    e_in = jnp.take_along_axis(e_logits, g_sel[:, :, None], axis=1)[:, 0]
    top_v, top_i = lax.top_k(e_in, TOP_K)
    gates = jax.nn.softmax(top_v, axis=-1) * pg
    ids = (g_sel * EXPERTS_PER_GROUP + top_i).reshape(-1).astype(jnp.int32)
    order = jnp.argsort(ids)
    tok = order // TOP_K
    xs = xf[tok]
    sizes = jnp.bincount(ids, length=N_EXPERTS).astype(jnp.int32)
    hg = lax.ragged_dot(xs, w_gate.astype(xs.dtype), sizes)
    hu = lax.ragged_dot(xs, w_up.astype(xs.dtype), sizes)
    ys = lax.ragged_dot(jax.nn.silu(hg) * hu, w_down.astype(xs.dtype), sizes)
    ys = ys * gates.reshape(-1)[order][:, None].astype(ys.dtype)
    y = jax.ops.segment_sum(ys, tok, num_segments=t)
    return y.reshape(bsz, seq, d).astype(h.dtype)


def setup_inputs(seed: int = 0) -> dict:
    key = jax.random.key(seed)
    ks = jax.random.split(key, 26)
    nrm = lambda k, s, sc: jax.random.normal(k, s, jnp.float32) * sc
    dt0 = jnp.exp(jax.random.uniform(ks[6], (DEPTH, SSD_HEADS), jnp.float32, math.log(1e-3), math.log(1e-1)))
    return {
        "x": jax.random.normal(ks[0], (BATCH, SEQ, D_MODEL), jnp.float32),
        "norm1_w": 1.0 + nrm(ks[1], (DEPTH, D_MODEL), 0.02),
        "w_in": nrm(ks[2], (DEPTH, D_MODEL, IN_COLS), D_MODEL ** -0.5),
        "ssd_conv_w": nrm(ks[3], (DEPTH, SSD_CONV, SSD_XBC), SSD_CONV ** -0.5),
        "ssd_conv_b": nrm(ks[4], (DEPTH, SSD_XBC), 0.02),
        "ssd_dt_bias": dt0 + jnp.log(-jnp.expm1(-dt0)),
        "ssd_a_log": jnp.log(jax.random.uniform(ks[5], (DEPTH, SSD_HEADS), jnp.float32, 1.0, 16.0)),
        "ssd_d": 1.0 + nrm(ks[7], (DEPTH, SSD_HEADS), 0.1),
        "ssd_norm_w": 1.0 + nrm(ks[8], (DEPTH, SSD_WIDTH), 0.02),
        "ml_conv_w": nrm(ks[9], (DEPTH, ML_CONV, 2 * ML_WIDTH), ML_CONV ** -0.5),
        "ml_conv_b": nrm(ks[10], (DEPTH, 2 * ML_WIDTH), 0.02),
        "ml_i_bias": nrm(ks[11], (DEPTH, ML_HEADS), 0.1),
        "ml_f_bias": 3.0 + 3.0 * jax.random.uniform(ks[12], (DEPTH, ML_HEADS), jnp.float32),
        "ml_norm_w": 1.0 + nrm(ks[13], (DEPTH, ML_WIDTH), 0.02),
        "w_out": nrm(ks[14], (DEPTH, D_MIX, D_MODEL), D_MIX ** -0.5),
        "norm2_w": 1.0 + nrm(ks[15], (DEPTH, D_MODEL), 0.02),
        "router_g_w": nrm(ks[16], (DEPTH, D_MODEL, MOE_GROUPS), D_MODEL ** -0.5),
        "router_g_b": nrm(ks[17], (DEPTH, MOE_GROUPS), 0.01),
        "router_e_w": nrm(ks[18], (DEPTH, D_MODEL, N_EXPERTS), D_MODEL ** -0.5),
        "router_e_b": nrm(ks[19], (DEPTH, N_EXPERTS), 0.01),
        "exp_w_gate": nrm(ks[20], (DEPTH, N_EXPERTS, D_MODEL, D_EXPERT), D_MODEL ** -0.5),
        "exp_w_up": nrm(ks[21], (DEPTH, N_EXPERTS, D_MODEL, D_EXPERT), D_MODEL ** -0.5),
        "exp_w_down": nrm(ks[22], (DEPTH, N_EXPERTS, D_EXPERT, D_MODEL), D_EXPERT ** -0.5),
        "norm_f_w": 1.0 + nrm(ks[23], (D_MODEL,), 0.02),
    }


def reference(x, norm1_w, w_in, ssd_conv_w, ssd_conv_b, ssd_dt_bias, ssd_a_log, ssd_d, ssd_norm_w,
              ml_conv_w, ml_conv_b, ml_i_bias, ml_f_bias, ml_norm_w, w_out, norm2_w,
              router_g_w, router_g_b, router_e_w, router_e_b, exp_w_gate, exp_w_up, exp_w_down, norm_f_w):
    o1 = SSD_WIDTH
    o2 = o1 + SSD_XBC
    o3 = o2 + SSD_HEADS
    o4 = o3 + 2 * ML_WIDTH
    o5 = o4 + ML_WIDTH
    o6 = o5 + ML_WIDTH
    o7 = o6 + ML_HEADS
    for layer in range(DEPTH):
        h = rmsnorm(x, norm1_w[layer])
        p = jnp.einsum('bsd,de->bse', h, w_in[layer])
        y_ssd = ssd_mixer(p[..., :o1], p[..., o1:o2], p[..., o2:o3],
                          ssd_conv_w[layer], ssd_conv_b[layer], ssd_dt_bias[layer],
                          ssd_a_log[layer], ssd_d[layer], ssd_norm_w[layer])
        y_ml = mlstm_mixer(p[..., o3:o4], p[..., o4:o5], p[..., o5:o6], p[..., o6:o7], p[..., o7:],
                           ml_conv_w[layer], ml_conv_b[layer], ml_i_bias[layer], ml_f_bias[layer],
                           ml_norm_w[layer])
        mix = jnp.concatenate([y_ssd, y_ml], axis=-1)
        x = x + jnp.einsum('bse,ed->bsd', mix, w_out[layer])
        h2 = rmsnorm(x, norm2_w[layer])
        x = x + hier_moe(h2, router_g_w[layer], router_g_b[layer], router_e_w[layer], router_e_b[layer],
                         exp_w_gate[layer], exp_w_up[layer], exp_w_down[layer])
    return rmsnorm(x, norm_f_w)
```

```python
import functools

import jax
import jax.numpy as jnp
from jax import lax
from jax.experimental import pallas as pl
from jax.experimental.pallas import tpu as pltpu

D_MODEL = 1024
CHUNK = 64
SSD_WIDTH = 1024
SSD_HEAD_DIM = 64
SSD_HEADS = 16
SSD_GROUPS = 2
SSD_STATE = 128
SSD_CONV = 4
SSD_XBC = SSD_WIDTH + 2 * SSD_GROUPS * SSD_STATE
ML_WIDTH = 1024
ML_HEADS = 8
ML_HEAD_DIM = 128
ML_CONV = 4
D_MIX = SSD_WIDTH + ML_WIDTH
MOE_GROUPS = 4
EXPERTS_PER_GROUP = 8
N_EXPERTS = 32
TOP_K = 2
D_EXPERT = 512
EPS = 1e-6
STAB_INIT = -1e30

LANES = 128
GATE_COLS = 2 * LANES
CONV_PAD = 8
LOGIT_ROWS = 64
V7X_VMEM_LIMIT = 56 * 1024 * 1024

HI = lax.Precision.HIGHEST
F32 = jnp.float32
BF16 = jnp.bfloat16
NEG_INF = float("-inf")


def _dot(a, b, precision=None):
    return jnp.dot(a, b, preferred_element_type=F32, precision=precision)


def _dot_nt(a, b):
    return lax.dot_general(a, b, (((1,), (1,)), ((), ())), preferred_element_type=F32)


def _dot_tn(a, b):
    return lax.dot_general(a, b, (((0,), (0,)), ((), ())), preferred_element_type=F32)


def _silu(x):
    return x * jax.nn.sigmoid(x)


def _softplus(x):
    return jnp.maximum(x, 0.0) + jnp.log1p(jnp.exp(-jnp.abs(x)))


def _rms(x, w):
    return x * lax.rsqrt(jnp.mean(x * x, axis=-1, keepdims=True) + EPS) * w


def _mixer_kernel(x_ref, n1_ref, wz_ref, wxbc_ref, wqk_ref, wv_ref, wo_ref, wg_ref,
                  scw_ref, scb_ref, dtb_ref, alog_ref, dskip_ref, snw_ref,
                  mcw_ref, mcb_ref, gb_ref, mnw_ref, wout_ref, n2_ref, wr_ref,
                  e16_ref, e2_ref, tril_ref, sle_ref, sge_ref, mle_ref, meq_ref, mge_ref,
                  x1_ref, h2_ref, lg_ref,
                  xbc_buf, qk_buf, z_buf, v_buf, o_buf, g_buf, xbc_c, qk_c, mix_buf,
                  sst, mst, mm, *, tl):
    nchunk = tl // CHUNK

    @pl.when(pl.program_id(1) == 0)
    def _():
        xbc_buf[0:CONV_PAD, :] = jnp.zeros((CONV_PAD, SSD_XBC), F32)
        qk_buf[0:CONV_PAD, :] = jnp.zeros((CONV_PAD, 2 * ML_WIDTH), F32)
        sst[...] = jnp.zeros_like(sst)
        mst[...] = jnp.zeros_like(mst)
        mm[...] = jnp.full(mm.shape, STAB_INIT, F32)

    h = _rms(x_ref[...], n1_ref[...]).astype(BF16)
    z_buf[...] = _dot(h, wz_ref[...])
    xbc_buf[CONV_PAD:CONV_PAD + tl, :] = _dot(h, wxbc_ref[...])
    qk_buf[CONV_PAD:CONV_PAD + tl, :] = _dot(h, wqk_ref[...])
    v_buf[...] = _dot(h, wv_ref[...])
    o_buf[...] = _dot(h, wo_ref[...])
    g_buf[...] = _dot(h, wg_ref[...])

    first = CONV_PAD - (SSD_CONV - 1)
    for c in range(nchunk):
        r0 = c * CHUNK
        acc = scb_ref[...] + scw_ref[0:1, :] * xbc_buf[r0 + first:r0 + first + CHUNK, :]
        for j in range(1, SSD_CONV):
            acc = acc + scw_ref[j:j + 1, :] * xbc_buf[r0 + first + j:r0 + first + j + CHUNK, :]
        xbc_c[r0:r0 + CHUNK, :] = _silu(acc)
        acc = mcb_ref[...] + mcw_ref[0:1, :] * qk_buf[r0 + first:r0 + first + CHUNK, :]
        for j in range(1, ML_CONV):
            acc = acc + mcw_ref[j:j + 1, :] * qk_buf[r0 + first + j:r0 + first + j + CHUNK, :]
        qk_c[r0:r0 + CHUNK, :] = _silu(acc)
    xbc_buf[0:CONV_PAD, :] = xbc_buf[tl:tl + CONV_PAD, :]
    qk_buf[0:CONV_PAD, :] = qk_buf[tl:tl + CONV_PAD, :]

    def chunk_body(c, carry):
        rows = pl.ds(pl.multiple_of(c * CHUNK, CHUNK), CHUNK)
        tril = tril_ref[...]
        ones8 = jnp.ones((8, CHUNK), F32)

        xs = xbc_c[rows, 0:SSD_WIDTH]
        bm = xbc_c[rows, SSD_WIDTH:SSD_WIDTH + 2 * SSD_STATE].astype(BF16)
        cm = xbc_c[rows, SSD_WIDTH + 2 * SSD_STATE:SSD_XBC].astype(BF16)
        dt = _softplus(g_buf[rows, 0:SSD_HEADS] + dtb_ref[...])
        dt_e = _dot(dt, e16_ref[...], HI)
        adt_e = dt_e * (-jnp.exp(alog_ref[...]))
        a_col = _dot(tril, adt_e, HI)
        a_row = _dot(ones8, adt_e * sle_ref[...], HI)[0:1, :]
        lmat = jnp.exp(jnp.where(sge_ref[...] > 0.5, a_col - a_row, NEG_INF))
        cb = jnp.concatenate(
            [_dot_nt(cm[:, g * SSD_STATE:(g + 1) * SSD_STATE],
                     jnp.concatenate([bm[:, g * SSD_STATE:(g + 1) * SSD_STATE]] * (SSD_HEADS // SSD_GROUPS), axis=0))
             for g in range(SSD_GROUPS)], axis=1)
        m_all = (cb * lmat).astype(BF16)
        xdt = xs * dt_e
        lane = lax.broadcasted_iota(jnp.int32, (CHUNK, LANES), 1)
        y_parts = []
        for j in range(SSD_HEADS // 2):
            xp = xdt[:, j * LANES:(j + 1) * LANES]
            xbd = jnp.concatenate([jnp.where(lane < SSD_HEAD_DIM, xp, 0.0),
                                   jnp.where(lane >= SSD_HEAD_DIM, xp, 0.0)], axis=0).astype(BF16)
            y_parts.append(_dot(m_all[:, j * LANES:(j + 1) * LANES], xbd))
        y = jnp.concatenate(y_parts, axis=1)
        half = SSD_WIDTH // SSD_GROUPS
        y_int = jnp.concatenate(
            [_dot(cm[:, g * SSD_STATE:(g + 1) * SSD_STATE], sst[g].astype(BF16)) for g in range(SSD_GROUPS)], axis=1)
        y = y + y_int * jnp.exp(a_col)
        a_last = a_col[CHUNK - 1:CHUNK, :]
        xd_b = (xdt * jnp.exp(a_last - a_col)).astype(BF16)
        st_scale = jnp.exp(a_last)
        for g in range(SSD_GROUPS):
            sst[g] = sst[g] * st_scale[:, g * half:(g + 1) * half] + _dot_tn(
                bm[:, g * SSD_STATE:(g + 1) * SSD_STATE], xd_b[:, g * half:(g + 1) * half])
        y = (y + dskip_ref[...] * xs) * _silu(z_buf[rows, :])
        for g in range(SSD_GROUPS):
            yg = y[:, g * half:(g + 1) * half]
            yg = yg * lax.rsqrt(jnp.mean(yg * yg, axis=-1, keepdims=True) + EPS) * snw_ref[:, g * half:(g + 1) * half]
            mix_buf[rows, g * half:(g + 1) * half] = yg.astype(BF16)

        gi = g_buf[rows, LANES:LANES + 2 * ML_HEADS] + gb_ref[...]
        col16 = lax.broadcasted_iota(jnp.int32, (CHUNK, 2 * ML_HEADS), 1)
        gi = jnp.where(col16 < ML_HEADS, gi, jnp.minimum(gi, 0.0) - jnp.log1p(jnp.exp(-jnp.abs(gi))))
        gi_e = _dot(gi, e2_ref[...], HI)
        ig_e = gi_e[:, 0:ML_WIDTH]
        lf_e = gi_e[:, ML_WIDTH:2 * ML_WIDTH]
        b_col = _dot(tril, lf_e, HI)
        c_row = _dot(ones8, ig_e * meq_ref[...] - lf_e * mle_ref[...], HI)[0:1, :]
        dmat = jnp.where(mge_ref[...] > 0.5, b_col + c_row, NEG_INF)
        onecol = jnp.where(lane == 0, 1.0, 0.0).astype(BF16)
        for hd in range(ML_HEADS):
            sl = slice(hd * LANES, (hd + 1) * LANES)
            q_h = qk_c[rows, hd * LANES:(hd + 1) * LANES].astype(BF16)
            k_f = qk_c[rows, ML_WIDTH + hd * LANES:ML_WIDTH + (hd + 1) * LANES] * (ML_HEAD_DIM ** -0.5)
            v_aug = jnp.concatenate([v_buf[rows, hd * LANES:(hd + 1) * LANES].astype(BF16), onecol], axis=1)
            d_h = dmat[:, sl]
            bc_h = b_col[:, sl]
            m_prev = mm[hd:hd + 1, :]
            m_inter = bc_h + m_prev
            m_t = jnp.maximum(m_inter, jnp.max(d_h, axis=-1, keepdims=True))
            w = jnp.exp(d_h - m_t)
            s = _dot_nt(q_h, k_f.astype(BF16))
            sw = (s * w[:, 0:CHUNK]).astype(BF16)
            inter = jnp.exp(m_inter - m_t)
            nd = _dot(sw, v_aug) + _dot(q_h, mst[hd].astype(BF16)) * jnp.concatenate([inter, inter], axis=1)
            den = jnp.maximum(jnp.abs(nd[:, LANES:LANES + 1]), jnp.exp(-m_t[:, 0:1]))
            hh = nd[:, 0:LANES] / den
            mu = jnp.mean(hh, axis=-1, keepdims=True)
            hc = hh - mu
            var = jnp.mean(hc * hc, axis=-1, keepdims=True)
            hn = hc * lax.rsqrt(var + EPS) * mnw_ref[:, sl]
            mix_buf[rows, SSD_WIDTH + hd * LANES:SSD_WIDTH + (hd + 1) * LANES] = (
                jax.nn.sigmoid(o_buf[rows, hd * LANES:(hd + 1) * LANES]) * hn).astype(BF16)
            g_row = bc_h[CHUNK - 1:CHUNK, :]
            a = g_row - bc_h + ig_e[:, sl]
            m_new = jnp.maximum(g_row + m_prev, jnp.max(a, axis=0, keepdims=True))
            kw = (k_f * jnp.exp(a - m_new)).astype(BF16)
            cs = jnp.exp(g_row + m_prev - m_new)
            mst[hd] = mst[hd] * jnp.concatenate([cs, cs], axis=1) + _dot_tn(kw, v_aug)
            mm[hd:hd + 1, :] = m_new
        return carry

    lax.fori_loop(0, nchunk, chunk_body, 0)

    x1 = x_ref[...] + _dot(mix_buf[...], wout_ref[...])
    x1_ref[...] = x1
    h2 = _rms(x1, n2_ref[...])
    h2_ref[...] = h2
    lg_ref[...] = _dot_nt(wr_ref[...], h2.astype(BF16))


def _const_spec(shape):
    nd = len(shape)
    return pl.BlockSpec(shape, lambda b, l: (0,) * nd, pipeline_mode=pl.Buffered(1))


def _mixer_call(x, consts, tl):
    bsz, seq, _ = x.shape
    nl = seq // tl
    tok_spec = pl.BlockSpec((None, tl, D_MODEL), lambda b, l: (b, l, 0))
    in_specs = [tok_spec] + [_const_spec(c.shape) for c in consts]
    out_specs = [tok_spec, tok_spec, pl.BlockSpec((LOGIT_ROWS, tl), lambda b, l: (0, b * nl + l))]
    out_shape = [jax.ShapeDtypeStruct((bsz, seq, D_MODEL), F32),
                 jax.ShapeDtypeStruct((bsz, seq, D_MODEL), F32),
                 jax.ShapeDtypeStruct((LOGIT_ROWS, bsz * seq), F32)]
    scratch = [
        pltpu.VMEM((tl + CONV_PAD, SSD_XBC), F32),
        pltpu.VMEM((tl + CONV_PAD, 2 * ML_WIDTH), F32),
        pltpu.VMEM((tl, SSD_WIDTH), F32),
        pltpu.VMEM((tl, ML_WIDTH), F32),
        pltpu.VMEM((tl, ML_WIDTH), F32),
        pltpu.VMEM((tl, GATE_COLS), F32),
        pltpu.VMEM((tl, SSD_XBC), F32),
        pltpu.VMEM((tl, 2 * ML_WIDTH), F32),
        pltpu.VMEM((tl, D_MIX), BF16),
        pltpu.VMEM((SSD_GROUPS, SSD_STATE, SSD_WIDTH // SSD_GROUPS), F32),
        pltpu.VMEM((ML_HEADS, ML_HEAD_DIM, 2 * LANES), F32),
        pltpu.VMEM((ML_HEADS, LANES), F32),
    ]
    return pl.pallas_call(
        functools.partial(_mixer_kernel, tl=tl),
        grid=(bsz, nl),
        in_specs=in_specs,
        out_specs=out_specs,
        out_shape=out_shape,
        scratch_shapes=scratch,
        compiler_params=pltpu.CompilerParams(
            dimension_semantics=("arbitrary", "arbitrary"), vmem_limit_bytes=V7X_VMEM_LIMIT),
        name="mixer",
    )(x, *consts)


def _mixer_consts(norm1_w, w_in, ssd_conv_w, ssd_conv_b, ssd_dt_bias, ssd_a_log, ssd_d, ssd_norm_w,
                  ml_conv_w, ml_conv_b, ml_i_bias, ml_f_bias, ml_norm_w, w_out, norm2_w,
                  router_g_w, router_e_w):
    o1 = SSD_WIDTH
    o2 = o1 + SSD_XBC
    o3 = o2 + SSD_HEADS
    o4 = o3 + 2 * ML_WIDTH
    o5 = o4 + ML_WIDTH
    o6 = o5 + ML_WIDTH
    o7 = o6 + ML_HEADS
    zpad = lambda n: jnp.zeros((D_MODEL, n), F32)
    w_gate = jnp.concatenate([w_in[:, o2:o3], zpad(LANES - SSD_HEADS),
                              w_in[:, o6:o7], w_in[:, o7:], zpad(LANES - 2 * ML_HEADS)], axis=1)
    row = lambda v: v.reshape(1, -1).astype(F32)
    w_route = jnp.concatenate([router_e_w, router_g_w,
                               jnp.zeros((D_MODEL, LOGIT_ROWS - N_EXPERTS - MOE_GROUPS), F32)], axis=1).T

    c1024 = jnp.arange(SSD_WIDTH)
    r64 = jnp.arange(CHUNK)[:, None]
    e16 = (c1024[None, :] // SSD_HEAD_DIM == jnp.arange(SSD_HEADS)[:, None]).astype(F32)
    hid = jnp.arange(2 * ML_HEADS)[:, None]
    c2048 = jnp.arange(2 * ML_WIDTH)[None, :]
    e2 = (c2048 // LANES == hid).astype(F32)
    tril = (jnp.arange(CHUNK)[None, :] <= r64).astype(F32)
    s_pos = (c1024 % SSD_HEAD_DIM)[None, :]
    sle = (r64 <= s_pos).astype(F32)
    sge = (r64 >= s_pos).astype(F32)
    m_pos = (c1024 % LANES)[None, :]
    mle = ((r64 <= m_pos) & (m_pos < CHUNK)).astype(F32)
    meq = (r64 == m_pos).astype(F32)
    mge = (r64 >= m_pos).astype(F32)
    return [
        row(norm1_w),
        w_in[:, :o1].astype(BF16), w_in[:, o1:o2].astype(BF16), w_in[:, o3:o4].astype(BF16),
        w_in[:, o4:o5].astype(BF16), w_in[:, o5:o6].astype(BF16), w_gate.astype(BF16),
        ssd_conv_w.astype(F32), row(ssd_conv_b), row(ssd_dt_bias),
        row(jnp.repeat(ssd_a_log, SSD_HEAD_DIM)), row(jnp.repeat(ssd_d, SSD_HEAD_DIM)), row(ssd_norm_w),
        ml_conv_w.astype(F32), row(ml_conv_b), row(jnp.concatenate([ml_i_bias, ml_f_bias])), row(ml_norm_w),
        w_out.astype(BF16), row(norm2_w), w_route.astype(BF16),
        e16, e2, tril, sle, sge, mle, meq, mge,
    ]


def _route_kernel(lg_ref, bias_ref, striu_ref, meta_ref, gate_ref, cnt_ref, cnt):
    @pl.when(pl.program_id(0) == 0)
    def _():
        cnt[...] = jnp.zeros_like(cnt)

    tr = lg_ref.shape[1]
    lg = lg_ref[...] + bias_ref[...]
    el = lg[0:N_EXPERTS]
    gl = lg[N_EXPERTS:N_EXPERTS + 8]
    gmax = jnp.max(gl, axis=0, keepdims=True)
    pg = 1.0 / jnp.sum(jnp.exp(gl - gmax), axis=0, keepdims=True)
    grow = lax.broadcasted_iota(jnp.int32, gl.shape, 0)
    gsel = jnp.min(jnp.where(gl == gmax, grow, 8), axis=0, keepdims=True)
    erow = lax.broadcasted_iota(jnp.int32, el.shape, 0)
    m1 = jnp.where((erow >> 3) == gsel, el, NEG_INF)
    v1 = jnp.max(m1, axis=0, keepdims=True)
    i1 = jnp.min(jnp.where(m1 == v1, erow, N_EXPERTS), axis=0, keepdims=True)
    m2 = jnp.where(erow == i1, NEG_INF, m1)
    v2 = jnp.max(m2, axis=0, keepdims=True)
    i2 = jnp.min(jnp.where(m2 == v2, erow, N_EXPERTS), axis=0, keepdims=True)
    e2 = jnp.exp(v2 - v1)
    g1 = pg / (1.0 + e2)
    g2 = g1 * e2
    hit1 = erow == i1
    hit2 = erow == i2
    oh = jnp.where(hit1 | hit2, 1.0, 0.0)
    base = cnt[:, 0:1] + _dot(oh.astype(BF16), striu_ref[...])
    r1 = jnp.sum(jnp.where(hit1, base, 0.0), axis=0, keepdims=True)
    r2 = jnp.sum(jnp.where(hit2, base, 0.0), axis=0, keepdims=True)
    cnt[...] = cnt[...] + jnp.sum(oh, axis=1, keepdims=True)
    zi = jnp.zeros((4, tr), jnp.int32)
    meta_ref[...] = jnp.concatenate([i1, i2, r1.astype(jnp.int32), r2.astype(jnp.int32), zi], axis=0)
    gate_ref[...] = jnp.concatenate([g1, g2, jnp.zeros((6, tr), F32)], axis=0)
    cnt_ref[...] = cnt[...]


def _route_call(lg_t, bias_col, tr):
    t = lg_t.shape[1]
    striu = (jnp.arange(tr)[:, None] < jnp.arange(tr)[None, :]).astype(BF16)
    return pl.pallas_call(
        _route_kernel,
        grid=(t // tr,),
        in_specs=[pl.BlockSpec((LOGIT_ROWS, tr), lambda i: (0, i)),
                  pl.BlockSpec((LOGIT_ROWS, 1), lambda i: (0, 0)),
                  pl.BlockSpec((tr, tr), lambda i: (0, 0))],
        out_specs=[pl.BlockSpec((8, tr), lambda i: (0, i)),
                   pl.BlockSpec((8, tr), lambda i: (0, i)),
                   pl.BlockSpec((N_EXPERTS, LANES), lambda i: (0, 0))],
        out_shape=[jax.ShapeDtypeStruct((8, t), jnp.int32),
                   jax.ShapeDtypeStruct((8, t), F32),
                   jax.ShapeDtypeStruct((N_EXPERTS, LANES), F32)],
        scratch_shapes=[pltpu.VMEM((N_EXPERTS, LANES), F32)],
        compiler_params=pltpu.CompilerParams(dimension_semantics=("arbitrary",)),
        name="route",
    )(lg_t, bias_col, striu)


def _dispatch_kernel(starts_ref, meta_ref, h2_ref, xs_ref, sem):
    td = h2_ref.shape[0]

    def row_copy(t, pos):
        return pltpu.make_async_copy(h2_ref.at[pl.ds(t, 1)], xs_ref.at[pl.ds(pos, 1)], sem)

    def issue(t, carry):
        for k in range(TOP_K):
            row_copy(t, starts_ref[meta_ref[k, t]] + meta_ref[TOP_K + k, t]).start()
        return carry

    lax.fori_loop(0, td, issue, 0)
    for _ in range(TOP_K):
        pltpu.make_async_copy(h2_ref, xs_ref.at[pl.ds(0, td)], sem).wait()


def _dispatch_call(starts, meta_blk, h2, td):
    t = h2.shape[0]
    grid_spec = pltpu.PrefetchScalarGridSpec(
        num_scalar_prefetch=1,
        grid=(t // td,),
        in_specs=[pl.BlockSpec((None, 2 * TOP_K, td), lambda i, s: (i, 0, 0), memory_space=pltpu.SMEM),
                  pl.BlockSpec((td, D_MODEL), lambda i, s: (i, 0))],
        out_specs=pl.BlockSpec(memory_space=pl.ANY),
        scratch_shapes=[pltpu.SemaphoreType.DMA(())],
    )
    return pl.pallas_call(
        _dispatch_kernel,
        grid_spec=grid_spec,
        out_shape=jax.ShapeDtypeStruct((t * TOP_K, D_MODEL), F32),
        compiler_params=pltpu.CompilerParams(dimension_semantics=("arbitrary",)),
        name="dispatch",
    )(starts, meta_blk, h2)


def _experts_kernel(tile_ref, exp_ref, starts_ref, xs_ref, wg_ref, wu_ref, wd_ref, ys_ref):
    i = pl.program_id(0)
    tm = xs_ref.shape[0]
    e = exp_ref[i]
    tile = tile_ref[i]
    x = xs_ref[...].astype(BF16)
    act = (_silu(_dot(x, wg_ref[...])) * _dot(x, wu_ref[...])).astype(BF16)
    y = _dot(act, wd_ref[...])
    rows = tile * tm + lax.broadcasted_iota(jnp.int32, (tm, 1), 0)
    y = jnp.where((rows >= starts_ref[e]) & (rows < starts_ref[e + 1]), y, 0.0)
    is_first = jnp.logical_or(i == 0, tile_ref[jnp.maximum(i - 1, 0)] != tile)

    @pl.when(is_first)
    def _():
        ys_ref[...] = y

    @pl.when(jnp.logical_not(is_first))
    def _():
        ys_ref[...] = ys_ref[...] + y


def _experts_call(item_tile, item_exp, starts_ext, xs, wg, wu, wd, tm):
    n = xs.shape[0]
    n_items = item_tile.shape[0]
    w_idx = lambda i, tile, ex, st: (jnp.minimum(ex[i], N_EXPERTS - 1), 0, 0)
    grid_spec = pltpu.PrefetchScalarGridSpec(
        num_scalar_prefetch=3,
        grid=(n_items,),
        in_specs=[pl.BlockSpec((tm, D_MODEL), lambda i, tile, ex, st: (tile[i], 0)),
                  pl.BlockSpec((None, D_MODEL, D_EXPERT), w_idx),
                  pl.BlockSpec((None, D_MODEL, D_EXPERT), w_idx),
                  pl.BlockSpec((None, D_EXPERT, D_MODEL), w_idx)],
        out_specs=pl.BlockSpec((tm, D_MODEL), lambda i, tile, ex, st: (tile[i], 0)),
    )
    return pl.pallas_call(
        _experts_kernel,
        grid_spec=grid_spec,
        out_shape=jax.ShapeDtypeStruct((n, D_MODEL), F32),
        compiler_params=pltpu.CompilerParams(
            dimension_semantics=("arbitrary",), vmem_limit_bytes=V7X_VMEM_LIMIT),
        name="experts",
    )(item_tile, item_exp, starts_ext, xs, wg, wu, wd)


def _expert_items(counts, n_rows, tm):
    n_tiles = n_rows // tm
    n_items = n_tiles + N_EXPERTS - 1
    starts = jnp.concatenate([jnp.zeros((1,), jnp.int32), jnp.cumsum(counts).astype(jnp.int32)])
    lo = starts[:-1] // tm
    hi = (starts[1:] - 1) // tm
    per = jnp.where(counts > 0, hi - lo + 1, 0)
    cum = jnp.cumsum(per)
    idx = jnp.arange(n_items, dtype=jnp.int32)
    e = jnp.searchsorted(cum, idx, side="right").astype(jnp.int32)
    valid = idx < cum[-1]
    e_c = jnp.minimum(e, N_EXPERTS - 1)
    tile = lo[e_c] + idx - (cum[e_c] - per[e_c])
    item_tile = jnp.where(valid, tile, n_tiles - 1).astype(jnp.int32)
    item_exp = jnp.where(valid, e_c, N_EXPERTS).astype(jnp.int32)
    starts_ext = jnp.concatenate([starts, starts[-1:]])
    return item_tile, item_exp, starts, starts_ext


def _combine_kernel(starts_ref, meta_ref, x1_ref, gate_ref, nf_ref, ys_ref, out_ref, ybuf, sem):
    tc = x1_ref.shape[0]

    def issue(t, carry):
        for k in range(TOP_K):
            pos = starts_ref[meta_ref[k, t]] + meta_ref[TOP_K + k, t]
            pltpu.make_async_copy(ys_ref.at[pl.ds(pos, 1)], ybuf.at[k, pl.ds(t, 1)], sem).start()
        return carry

    lax.fori_loop(0, tc, issue, 0)
    for k in range(TOP_K):
        pltpu.make_async_copy(ys_ref.at[pl.ds(0, tc)], ybuf.at[k], sem).wait()
    g = gate_ref[...]
    x2 = x1_ref[...] + g[:, 0:1] * ybuf[0] + g[:, 1:2] * ybuf[1]
    out_ref[...] = _rms(x2, nf_ref[...])


def _combine_call(starts, meta_blk, x1, gates_t, nf_w, ys, tc):
    t = x1.shape[0]
    grid_spec = pltpu.PrefetchScalarGridSpec(
        num_scalar_prefetch=1,
        grid=(t // tc,),
        in_specs=[pl.BlockSpec((None, 2 * TOP_K, tc), lambda i, s: (i, 0, 0), memory_space=pltpu.SMEM),
                  pl.BlockSpec((tc, D_MODEL), lambda i, s: (i, 0)),
                  pl.BlockSpec((tc, TOP_K), lambda i, s: (i, 0)),
                  pl.BlockSpec((1, D_MODEL), lambda i, s: (0, 0)),
                  pl.BlockSpec(memory_space=pl.ANY)],
        out_specs=pl.BlockSpec((tc, D_MODEL), lambda i, s: (i, 0)),
        scratch_shapes=[pltpu.VMEM((TOP_K, tc, D_MODEL), F32), pltpu.SemaphoreType.DMA(())],
    )
    return pl.pallas_call(
        _combine_kernel,
        grid_spec=grid_spec,
        out_shape=jax.ShapeDtypeStruct((t, D_MODEL), F32),
        compiler_params=pltpu.CompilerParams(dimension_semantics=("arbitrary",)),
        name="combine",
    )(starts, meta_blk, x1, gates_t, nf_w, ys)


def _tile(n, pref):
    return pref if n % pref == 0 else n


def kernel(x, norm1_w, w_in, ssd_conv_w, ssd_conv_b, ssd_dt_bias, ssd_a_log, ssd_d, ssd_norm_w, ml_conv_w, ml_conv_b, ml_i_bias, ml_f_bias, ml_norm_w, w_out, norm2_w, router_g_w, router_g_b, router_e_w, router_e_b, exp_w_gate, exp_w_up, exp_w_down, norm_f_w):
    bsz, seq, d = x.shape
    assert d == D_MODEL and seq % CHUNK == 0 and norm1_w.shape[0] == 1
    t = bsz * seq
    tl = _tile(seq, 256)
    tr = _tile(t, 512)
    tm = _tile(t * TOP_K, 512)

    consts = _mixer_consts(norm1_w[0], w_in[0], ssd_conv_w[0], ssd_conv_b[0], ssd_dt_bias[0], ssd_a_log[0],
                           ssd_d[0], ssd_norm_w[0], ml_conv_w[0], ml_conv_b[0], ml_i_bias[0], ml_f_bias[0],
                           ml_norm_w[0], w_out[0], norm2_w[0], router_g_w[0], router_e_w[0])
    x1, h2, lg_t = _mixer_call(x, consts, tl)
    x1 = x1.reshape(t, D_MODEL)
    h2 = h2.reshape(t, D_MODEL)

    bias_col = jnp.concatenate([router_e_b[0], router_g_b[0],
                                jnp.full((LOGIT_ROWS - N_EXPERTS - MOE_GROUPS,), STAB_INIT, F32)]).reshape(-1, 1)
    meta, gates, cnt = _route_call(lg_t, bias_col.astype(F32), tr)
    counts = cnt[:, 0].astype(jnp.int32)
    item_tile, item_exp, starts, starts_ext = _expert_items(counts, t * TOP_K, tm)
    meta_blk = meta[0:2 * TOP_K].reshape(2 * TOP_K, t // tr, tr).transpose(1, 0, 2)

    xs = _dispatch_call(starts, meta_blk, h2, tr)
    ys = _experts_call(item_tile, item_exp, starts_ext, xs, exp_w_gate[0].astype(BF16),
                       exp_w_up[0].astype(BF16), exp_w_down[0].astype(BF16), tm)
    out = _combine_call(starts, meta_blk, x1, gates[0:TOP_K].T, norm_f_w.reshape(1, -1).astype(F32), ys, tr)
    return out.reshape(bsz, seq, D_MODEL)
```

```python
import functools

import jax
import jax.numpy as jnp
import numpy as np
from jax import lax
from jax.experimental import pallas as pl
from jax.experimental.pallas import tpu as pltpu

D_MODEL = 1024
CHUNK = 64
SSD_WIDTH = 1024
SSD_HEAD_DIM = 64
SSD_HEADS = 16
SSD_GROUPS = 2
SSD_STATE = 128
SSD_CONV = 4
SSD_XBC = SSD_WIDTH + 2 * SSD_GROUPS * SSD_STATE
ML_WIDTH = 1024
ML_HEADS = 8
ML_HEAD_DIM = 128
ML_CONV = 4
D_MIX = SSD_WIDTH + ML_WIDTH
MOE_GROUPS = 4
EXPERTS_PER_GROUP = 8
N_EXPERTS = 32
TOP_K = 2
D_EXPERT = 512
EPS = 1e-6
STAB_INIT = -1e30

LANES = 128
GATE_COLS = 2 * LANES
CONV_PAD = 8
LOGIT_ROWS = 64
V7X_VMEM_LIMIT = 56 * 1024 * 1024

F32 = jnp.float32
BF16 = jnp.bfloat16
NEG_INF = float("-inf")


def _dot(a, b, precision=None):
    return jnp.dot(a, b, preferred_element_type=F32, precision=precision)


def _dot_nt(a, b):
    return lax.dot_general(a, b, (((1,), (1,)), ((), ())), preferred_element_type=F32)


def _dot_tn(a, b):
    return lax.dot_general(a, b, (((0,), (0,)), ((), ())), preferred_element_type=F32)


def _split3(x, axis):
    hi = x.astype(BF16)
    r1 = x - hi.astype(F32)
    mid = r1.astype(BF16)
    lo = (r1 - mid.astype(F32)).astype(BF16)
    return jnp.concatenate([hi, mid, lo], axis=axis)


def _silu(x):
    return x * jax.nn.sigmoid(x)


def _softplus(x):
    return jnp.maximum(x, 0.0) + jnp.log1p(jnp.exp(-jnp.abs(x)))


def _rms(x, w):
    return x * lax.rsqrt(jnp.mean(x * x, axis=-1, keepdims=True) + EPS) * w


def _mixer_kernel(x_ref, n1_ref, wz_ref, wxbc_ref, wqk_ref, wv_ref, wo_ref, wg_ref,
                  scw_ref, scb_ref, dtb_ref, alog_ref, dskip_ref, snw_ref,
                  mcw_ref, mcb_ref, gb_ref, mnw_ref, wout_ref, n2_ref, wr_ref,
                  e16_ref, e2_ref, tril_ref, seq_ref, sge_ref, meq_ref, mge_ref,
                  x1_ref, h2_ref, lg_ref,
                  xbc_buf, qk_buf, z_buf, v_buf, o_buf, g_buf, xbc_c, qk_c, mix_buf,
                  sst, mst, mm, *, tl):
    nchunk = tl // CHUNK

    @pl.when(pl.program_id(1) == 0)
    def _():
        xbc_buf[0:CONV_PAD, :] = jnp.zeros((CONV_PAD, SSD_XBC), F32)
        qk_buf[0:CONV_PAD, :] = jnp.zeros((CONV_PAD, 2 * ML_WIDTH), F32)
        sst[...] = jnp.zeros_like(sst)
        mst[...] = jnp.zeros_like(mst)
        mm[...] = jnp.full(mm.shape, STAB_INIT, F32)

    h = _rms(x_ref[...], n1_ref[...]).astype(BF16)
    z_buf[...] = _dot(h, wz_ref[...])
    xbc_buf[CONV_PAD:CONV_PAD + tl, :] = _dot(h, wxbc_ref[...])
    qk_buf[CONV_PAD:CONV_PAD + tl, :] = _dot(h, wqk_ref[...])
    v_buf[...] = _dot(h, wv_ref[...])
    o_buf[...] = _dot(h, wo_ref[...])
    g_buf[...] = _dot(h, wg_ref[...])

    first = CONV_PAD - (SSD_CONV - 1)
    for c in range(nchunk):
        r0 = c * CHUNK
        acc = scb_ref[...] + scw_ref[0:1, :] * xbc_buf[r0 + first:r0 + first + CHUNK, :]
        for j in range(1, SSD_CONV):
            acc = acc + scw_ref[j:j + 1, :] * xbc_buf[r0 + first + j:r0 + first + j + CHUNK, :]
        xbc_c[r0:r0 + CHUNK, :] = _silu(acc)
        acc = mcb_ref[...] + mcw_ref[0:1, :] * qk_buf[r0 + first:r0 + first + CHUNK, :]
        for j in range(1, ML_CONV):
            acc = acc + mcw_ref[j:j + 1, :] * qk_buf[r0 + first + j:r0 + first + j + CHUNK, :]
        qk_c[r0:r0 + CHUNK, :] = _silu(acc)
    xbc_buf[0:CONV_PAD, :] = xbc_buf[tl:tl + CONV_PAD, :]
    qk_buf[0:CONV_PAD, :] = qk_buf[tl:tl + CONV_PAD, :]

    def chunk_body(c, carry):
        rows = pl.ds(pl.multiple_of(c * CHUNK, CHUNK), CHUNK)
        tril3 = tril_ref[...]

        xs = xbc_c[rows, 0:SSD_WIDTH]
        bm = xbc_c[rows, SSD_WIDTH:SSD_WIDTH + 2 * SSD_STATE].astype(BF16)
        cm = xbc_c[rows, SSD_WIDTH + 2 * SSD_STATE:SSD_XBC].astype(BF16)
        dt = _softplus(g_buf[rows, 0:SSD_HEADS] + dtb_ref[...])
        a_cs = _dot(tril3, _split3(dt * (-jnp.exp(alog_ref[...])), 0))
        dt_e = _dot(_split3(dt, 1), e16_ref[...])
        a_col = _dot(_split3(a_cs, 1), e16_ref[...])
        a_row = jnp.sum(jnp.where(seq_ref[...] > 0.5, a_col, 0.0), axis=0, keepdims=True)
        lmat = jnp.exp(jnp.where(sge_ref[...] > 0.5, a_col - a_row, NEG_INF))
        cb = jnp.concatenate(
            [_dot_nt(cm[:, g * SSD_STATE:(g + 1) * SSD_STATE],
                     jnp.concatenate([bm[:, g * SSD_STATE:(g + 1) * SSD_STATE]] * (SSD_HEADS // SSD_GROUPS), axis=0))
             for g in range(SSD_GROUPS)], axis=1)
        m_all = (cb * lmat).astype(BF16)
        xdt = xs * dt_e
        lane = lax.broadcasted_iota(jnp.int32, (CHUNK, LANES), 1)
        y_parts = []
        for j in range(SSD_HEADS // 2):
            xp = xdt[:, j * LANES:(j + 1) * LANES]
            xbd = jnp.concatenate([jnp.where(lane < SSD_HEAD_DIM, xp, 0.0),
                                   jnp.where(lane >= SSD_HEAD_DIM, xp, 0.0)], axis=0).astype(BF16)
            y_parts.append(_dot(m_all[:, j * LANES:(j + 1) * LANES], xbd))
        y = jnp.concatenate(y_parts, axis=1)
        half = SSD_WIDTH // SSD_GROUPS
        y_int = jnp.concatenate(
            [_dot(cm[:, g * SSD_STATE:(g + 1) * SSD_STATE], sst[g].astype(BF16)) for g in range(SSD_GROUPS)], axis=1)
        y = y + y_int * jnp.exp(a_col)
        a_last = a_col[CHUNK - 1:CHUNK, :]
        xd_b = (xdt * jnp.exp(a_last - a_col)).astype(BF16)
        st_scale = jnp.exp(a_last)
        for g in range(SSD_GROUPS):
            sst[g] = sst[g] * st_scale[:, g * half:(g + 1) * half] + _dot_tn(
                bm[:, g * SSD_STATE:(g + 1) * SSD_STATE], xd_b[:, g * half:(g + 1) * half])
        y = (y + dskip_ref[...] * xs) * _silu(z_buf[rows, :])
        for g in range(SSD_GROUPS):
            yg = y[:, g * half:(g + 1) * half]
            yg = yg * lax.rsqrt(jnp.mean(yg * yg, axis=-1, keepdims=True) + EPS) * snw_ref[:, g * half:(g + 1) * half]
            mix_buf[rows, g * half:(g + 1) * half] = yg.astype(BF16)

        gi = g_buf[rows, LANES:LANES + 2 * ML_HEADS] + gb_ref[...]
        col16 = lax.broadcasted_iota(jnp.int32, (CHUNK, 2 * ML_HEADS), 1)
        gi = jnp.where(col16 < ML_HEADS, gi, jnp.minimum(gi, 0.0) - jnp.log1p(jnp.exp(-jnp.abs(gi))))
        gi = jnp.where(col16 < ML_HEADS, gi, _dot(tril3, _split3(gi, 0)))
        gi_e = _dot(_split3(gi, 1), e2_ref[...])
        ig_e = gi_e[:, 0:ML_WIDTH]
        b_col = gi_e[:, ML_WIDTH:2 * ML_WIDTH]
        c_row = jnp.sum(jnp.where(meq_ref[...] > 0.5, ig_e - b_col, 0.0), axis=0, keepdims=True)
        dmat = jnp.where(mge_ref[...] > 0.5, b_col + c_row, NEG_INF)
        onecol = jnp.where(lane == 0, 1.0, 0.0).astype(BF16)
        for hd in range(ML_HEADS):
            sl = slice(hd * LANES, (hd + 1) * LANES)
            q_h = qk_c[rows, hd * LANES:(hd + 1) * LANES].astype(BF16)
            k_f = qk_c[rows, ML_WIDTH + hd * LANES:ML_WIDTH + (hd + 1) * LANES] * (ML_HEAD_DIM ** -0.5)
            v_aug = jnp.concatenate([v_buf[rows, hd * LANES:(hd + 1) * LANES].astype(BF16), onecol], axis=1)
            d_h = dmat[:, sl]
            bc_h = b_col[:, sl]
            m_prev = mm[hd:hd + 1, :]
            m_inter = bc_h + m_prev
            m_t = jnp.maximum(m_inter, jnp.max(d_h, axis=-1, keepdims=True))
            w = jnp.exp(d_h - m_t)
            s = _dot_nt(q_h, k_f.astype(BF16))
            sw = (s * w[:, 0:CHUNK]).astype(BF16)
            inter = jnp.exp(m_inter - m_t)
            nd = _dot(sw, v_aug) + _dot(q_h, mst[hd].astype(BF16)) * jnp.concatenate([inter, inter], axis=1)
            den = jnp.maximum(jnp.abs(nd[:, LANES:LANES + 1]), jnp.exp(-m_t[:, 0:1]))
            hh = nd[:, 0:LANES] / den
            mu = jnp.mean(hh, axis=-1, keepdims=True)
            hc = hh - mu
            var = jnp.mean(hc * hc, axis=-1, keepdims=True)
            hn = hc * lax.rsqrt(var + EPS) * mnw_ref[:, sl]
            mix_buf[rows, SSD_WIDTH + hd * LANES:SSD_WIDTH + (hd + 1) * LANES] = (
                jax.nn.sigmoid(o_buf[rows, hd * LANES:(hd + 1) * LANES]) * hn).astype(BF16)
            g_row = bc_h[CHUNK - 1:CHUNK, :]
            a = g_row - bc_h + ig_e[:, sl]
            m_new = jnp.maximum(g_row + m_prev, jnp.max(a, axis=0, keepdims=True))
            kw = (k_f * jnp.exp(a - m_new)).astype(BF16)
            cs = jnp.exp(g_row + m_prev - m_new)
            mst[hd] = mst[hd] * jnp.concatenate([cs, cs], axis=1) + _dot_tn(kw, v_aug)
            mm[hd:hd + 1, :] = m_new
        return carry

    lax.fori_loop(0, nchunk, chunk_body, 0, unroll=True)

    x1 = x_ref[...] + _dot(mix_buf[...], wout_ref[...])
    x1_ref[...] = x1
    h2 = _rms(x1, n2_ref[...])
    h2_ref[...] = h2
    lg_ref[...] = _dot_nt(wr_ref[...], h2.astype(BF16))


def _const_spec(shape):
    nd = len(shape)
    return pl.BlockSpec(shape, lambda b, l: (0,) * nd, pipeline_mode=pl.Buffered(1))


def _mixer_call(x, consts, tl):
    bsz, seq, _ = x.shape
    nl = seq // tl
    tok_spec = pl.BlockSpec((None, tl, D_MODEL), lambda b, l: (b, l, 0))
    in_specs = [tok_spec] + [_const_spec(c.shape) for c in consts]
    out_specs = [tok_spec, tok_spec, pl.BlockSpec((LOGIT_ROWS, tl), lambda b, l: (0, b * nl + l))]
    out_shape = [jax.ShapeDtypeStruct((bsz, seq, D_MODEL), F32),
                 jax.ShapeDtypeStruct((bsz, seq, D_MODEL), F32),
                 jax.ShapeDtypeStruct((LOGIT_ROWS, bsz * seq), F32)]
    scratch = [
        pltpu.VMEM((tl + CONV_PAD, SSD_XBC), F32),
        pltpu.VMEM((tl + CONV_PAD, 2 * ML_WIDTH), F32),
        pltpu.VMEM((tl, SSD_WIDTH), F32),
        pltpu.VMEM((tl, ML_WIDTH), F32),
        pltpu.VMEM((tl, ML_WIDTH), F32),
        pltpu.VMEM((tl, GATE_COLS), F32),
        pltpu.VMEM((tl, SSD_XBC), F32),
        pltpu.VMEM((tl, 2 * ML_WIDTH), F32),
        pltpu.VMEM((tl, D_MIX), BF16),
        pltpu.VMEM((SSD_GROUPS, SSD_STATE, SSD_WIDTH // SSD_GROUPS), F32),
        pltpu.VMEM((ML_HEADS, ML_HEAD_DIM, 2 * LANES), F32),
        pltpu.VMEM((ML_HEADS, LANES), F32),
    ]
    return pl.pallas_call(
        functools.partial(_mixer_kernel, tl=tl),
        grid=(bsz, nl),
        in_specs=in_specs,
        out_specs=out_specs,
        out_shape=out_shape,
        scratch_shapes=scratch,
        compiler_params=pltpu.CompilerParams(
            dimension_semantics=("arbitrary", "arbitrary"), vmem_limit_bytes=V7X_VMEM_LIMIT),
        name="mixer",
    )(x, *consts)


def _mixer_consts(norm1_w, w_in, ssd_conv_w, ssd_conv_b, ssd_dt_bias, ssd_a_log, ssd_d, ssd_norm_w,
                  ml_conv_w, ml_conv_b, ml_i_bias, ml_f_bias, ml_norm_w, w_out, norm2_w,
                  router_g_w, router_e_w):
    o1 = SSD_WIDTH
    o2 = o1 + SSD_XBC
    o3 = o2 + SSD_HEADS
    o4 = o3 + 2 * ML_WIDTH
    o5 = o4 + ML_WIDTH
    o6 = o5 + ML_WIDTH
    o7 = o6 + ML_HEADS
    zpad = lambda n: jnp.zeros((D_MODEL, n), F32)
    w_gate = jnp.concatenate([w_in[:, o2:o3], zpad(LANES - SSD_HEADS),
                              w_in[:, o6:o7], w_in[:, o7:], zpad(LANES - 2 * ML_HEADS)], axis=1)
    row = lambda v: v.reshape(1, -1).astype(F32)
    w_route = jnp.concatenate([router_e_w, router_g_w,
                               jnp.zeros((D_MODEL, LOGIT_ROWS - N_EXPERTS - MOE_GROUPS), F32)], axis=1).T

    c1024 = np.arange(SSD_WIDTH)
    r64 = np.arange(CHUNK)[:, None]
    e16 = np.tile(c1024[None, :] // SSD_HEAD_DIM == np.arange(SSD_HEADS)[:, None], (3, 1))
    e2 = np.tile(np.arange(2 * ML_WIDTH)[None, :] // LANES == np.arange(2 * ML_HEADS)[:, None], (3, 1))
    tril = np.tile(np.arange(CHUNK)[None, :] <= r64, (1, 3))
    s_pos = (c1024 % SSD_HEAD_DIM)[None, :]
    m_pos = (c1024 % LANES)[None, :]
    as_bf16 = lambda m: jnp.asarray(m.astype(np.float32), BF16)
    as_f32 = lambda m: jnp.asarray(m.astype(np.float32))
    e16, e2, tril = as_bf16(e16), as_bf16(e2), as_bf16(tril)
    seq, sge, meq, mge = as_f32(r64 == s_pos), as_f32(r64 >= s_pos), as_f32(r64 == m_pos), as_f32(r64 >= m_pos)
    return [
        row(norm1_w),
        w_in[:, :o1].astype(BF16), w_in[:, o1:o2].astype(BF16), w_in[:, o3:o4].astype(BF16),
        w_in[:, o4:o5].astype(BF16), w_in[:, o5:o6].astype(BF16), w_gate.astype(BF16),
        ssd_conv_w.astype(F32), row(ssd_conv_b), row(ssd_dt_bias),
        row(ssd_a_log), row(jnp.repeat(ssd_d, SSD_HEAD_DIM)), row(ssd_norm_w),
        ml_conv_w.astype(F32), row(ml_conv_b), row(jnp.concatenate([ml_i_bias, ml_f_bias])), row(ml_norm_w),
        w_out.astype(BF16), row(norm2_w), w_route.astype(BF16),
        e16, e2, tril, seq, sge, meq, mge,
    ]


def _route_kernel(lg_ref, bias_ref, striu_ref, meta_ref, gate_ref, cnt_ref, cnt):
    @pl.when(pl.program_id(0) == 0)
    def _():
        cnt[...] = jnp.zeros_like(cnt)

    tr = lg_ref.shape[1]
    lg = lg_ref[...] + bias_ref[...]
    el = lg[0:N_EXPERTS]
    gl = lg[N_EXPERTS:N_EXPERTS + 8]
    gmax = jnp.max(gl, axis=0, keepdims=True)
    pg = 1.0 / jnp.sum(jnp.exp(gl - gmax), axis=0, keepdims=True)
    grow = lax.broadcasted_iota(jnp.int32, gl.shape, 0)
    gsel = jnp.min(jnp.where(gl == gmax, grow, 8), axis=0, keepdims=True)
    erow = lax.broadcasted_iota(jnp.int32, el.shape, 0)
    m1 = jnp.where((erow >> 3) == gsel, el, NEG_INF)
    v1 = jnp.max(m1, axis=0, keepdims=True)
    i1 = jnp.min(jnp.where(m1 == v1, erow, N_EXPERTS), axis=0, keepdims=True)
    m2 = jnp.where(erow == i1, NEG_INF, m1)
    v2 = jnp.max(m2, axis=0, keepdims=True)
    i2 = jnp.min(jnp.where(m2 == v2, erow, N_EXPERTS), axis=0, keepdims=True)
    e2 = jnp.exp(v2 - v1)
    g1 = pg / (1.0 + e2)
    g2 = g1 * e2
    hit1 = erow == i1
    hit2 = erow == i2
    oh = jnp.where(hit1 | hit2, 1.0, 0.0)
    base = cnt[:, 0:1] + _dot(oh.astype(BF16), striu_ref[...])
    r1 = jnp.sum(jnp.where(hit1, base, 0.0), axis=0, keepdims=True)
    r2 = jnp.sum(jnp.where(hit2, base, 0.0), axis=0, keepdims=True)
    cnt[...] = cnt[...] + jnp.sum(oh, axis=1, keepdims=True)
    zi = jnp.zeros((4, tr), jnp.int32)
    meta_ref[...] = jnp.concatenate([i1, i2, r1.astype(jnp.int32), r2.astype(jnp.int32), zi], axis=0)
    gate_ref[...] = jnp.concatenate([g1, g2, jnp.zeros((6, tr), F32)], axis=0)
    cnt_ref[...] = cnt[...]


def _route_call(lg_t, bias_col, tr):
    t = lg_t.shape[1]
    striu = (jnp.arange(tr)[:, None] < jnp.arange(tr)[None, :]).astype(BF16)
    return pl.pallas_call(
        _route_kernel,
        grid=(t // tr,),
        in_specs=[pl.BlockSpec((LOGIT_ROWS, tr), lambda i: (0, i)),
                  pl.BlockSpec((LOGIT_ROWS, 1), lambda i: (0, 0)),
                  pl.BlockSpec((tr, tr), lambda i: (0, 0))],
        out_specs=[pl.BlockSpec((8, tr), lambda i: (0, i)),
                   pl.BlockSpec((8, tr), lambda i: (0, i)),
                   pl.BlockSpec((N_EXPERTS, LANES), lambda i: (0, 0))],
        out_shape=[jax.ShapeDtypeStruct((8, t), jnp.int32),
                   jax.ShapeDtypeStruct((8, t), F32),
                   jax.ShapeDtypeStruct((N_EXPERTS, LANES), F32)],
        scratch_shapes=[pltpu.VMEM((N_EXPERTS, LANES), F32)],
        compiler_params=pltpu.CompilerParams(dimension_semantics=("arbitrary",)),
        name="route",
    )(lg_t, bias_col, striu)


def _dispatch_kernel(starts_ref, meta_ref, h2_ref, xs_ref, sem):
    td = h2_ref.shape[0]

    def row_copy(t, pos):
        return pltpu.make_async_copy(h2_ref.at[pl.ds(t, 1)], xs_ref.at[pl.ds(pos, 1)], sem)

    def issue(t, carry):
        for k in range(TOP_K):
            row_copy(t, starts_ref[meta_ref[k, t]] + meta_ref[TOP_K + k, t]).start()
        return carry

    lax.fori_loop(0, td, issue, 0)
    for _ in range(TOP_K):
        pltpu.make_async_copy(h2_ref, xs_ref.at[pl.ds(0, td)], sem).wait()


def _dispatch_call(starts, meta_blk, h2, td):
    t = h2.shape[0]
    grid_spec = pltpu.PrefetchScalarGridSpec(
        num_scalar_prefetch=1,
        grid=(t // td,),
        in_specs=[pl.BlockSpec((None, 2 * TOP_K, td), lambda i, s: (i, 0, 0), memory_space=pltpu.SMEM),
                  pl.BlockSpec((td, D_MODEL), lambda i, s: (i, 0))],
        out_specs=pl.BlockSpec(memory_space=pl.ANY),
        scratch_shapes=[pltpu.SemaphoreType.DMA(())],
    )
    return pl.pallas_call(
        _dispatch_kernel,
        grid_spec=grid_spec,
        out_shape=jax.ShapeDtypeStruct((t * TOP_K, D_MODEL), F32),
        compiler_params=pltpu.CompilerParams(dimension_semantics=("arbitrary",)),
        name="dispatch",
    )(starts, meta_blk, h2)


def _experts_kernel(tile_ref, exp_ref, starts_ref, xs_ref, wg_ref, wu_ref, wd_ref, ys_ref):
    i = pl.program_id(0)
    tm = xs_ref.shape[0]
    e = exp_ref[i]
    tile = tile_ref[i]
    x = xs_ref[...].astype(BF16)
    act = (_silu(_dot(x, wg_ref[...])) * _dot(x, wu_ref[...])).astype(BF16)
    y = _dot(act, wd_ref[...])
    rows = tile * tm + lax.broadcasted_iota(jnp.int32, (tm, 1), 0)
    y = jnp.where((rows >= starts_ref[e]) & (rows < starts_ref[e + 1]), y, 0.0)
    is_first = jnp.logical_or(i == 0, tile_ref[jnp.maximum(i - 1, 0)] != tile)

    @pl.when(is_first)
    def _():
        ys_ref[...] = y

    @pl.when(jnp.logical_not(is_first))
    def _():
        ys_ref[...] = ys_ref[...] + y


def _experts_call(item_tile, item_exp, starts_ext, xs, wg, wu, wd, tm):
    n = xs.shape[0]
    n_items = item_tile.shape[0]
    w_idx = lambda i, tile, ex, st: (jnp.minimum(ex[i], N_EXPERTS - 1), 0, 0)
    grid_spec = pltpu.PrefetchScalarGridSpec(
        num_scalar_prefetch=3,
        grid=(n_items,),
        in_specs=[pl.BlockSpec((tm, D_MODEL), lambda i, tile, ex, st: (tile[i], 0)),
                  pl.BlockSpec((None, D_MODEL, D_EXPERT), w_idx),
                  pl.BlockSpec((None, D_MODEL, D_EXPERT), w_idx),
                  pl.BlockSpec((None, D_EXPERT, D_MODEL), w_idx)],
        out_specs=pl.BlockSpec((tm, D_MODEL), lambda i, tile, ex, st: (tile[i], 0)),
    )
    return pl.pallas_call(
        _experts_kernel,
        grid_spec=grid_spec,
        out_shape=jax.ShapeDtypeStruct((n, D_MODEL), F32),
        compiler_params=pltpu.CompilerParams(
            dimension_semantics=("arbitrary",), vmem_limit_bytes=V7X_VMEM_LIMIT),
        name="experts",
    )(item_tile, item_exp, starts_ext, xs, wg, wu, wd)


def _expert_items(counts, n_rows, tm):
    n_tiles = n_rows // tm
    n_items = n_tiles + N_EXPERTS - 1
    starts = jnp.concatenate([jnp.zeros((1,), jnp.int32), jnp.cumsum(counts).astype(jnp.int32)])
    lo = starts[:-1] // tm
    hi = (starts[1:] - 1) // tm
    per = jnp.where(counts > 0, hi - lo + 1, 0)
    cum = jnp.cumsum(per)
    idx = jnp.arange(n_items, dtype=jnp.int32)
    e = jnp.sum(idx[:, None] >= cum[None, :], axis=1).astype(jnp.int32)
    valid = idx < cum[-1]
    e_c = jnp.minimum(e, N_EXPERTS - 1)
    tile = lo[e_c] + idx - (cum[e_c] - per[e_c])
    item_tile = jnp.where(valid, tile, n_tiles - 1).astype(jnp.int32)
    item_exp = jnp.where(valid, e_c, N_EXPERTS).astype(jnp.int32)
    starts_ext = jnp.concatenate([starts, starts[-1:]])
    return item_tile, item_exp, starts, starts_ext


def _combine_kernel(starts_ref, meta_ref, x1_ref, gate_ref, nf_ref, ys_ref, out_ref, ybuf, sem):
    tc = x1_ref.shape[0]

    def issue(t, carry):
        for k in range(TOP_K):
            pos = starts_ref[meta_ref[k, t]] + meta_ref[TOP_K + k, t]
            pltpu.make_async_copy(ys_ref.at[pl.ds(pos, 1)], ybuf.at[k, pl.ds(t, 1)], sem).start()
        return carry

    lax.fori_loop(0, tc, issue, 0)
    for k in range(TOP_K):
        pltpu.make_async_copy(ys_ref.at[pl.ds(0, tc)], ybuf.at[k], sem).wait()
    g = gate_ref[...]
    x2 = x1_ref[...] + g[:, 0:1] * ybuf[0] + g[:, 1:2] * ybuf[1]
    out_ref[...] = _rms(x2, nf_ref[...])


def _combine_call(starts, meta_blk, x1, gates_t, nf_w, ys, tc):
    t = x1.shape[0]
    grid_spec = pltpu.PrefetchScalarGridSpec(
        num_scalar_prefetch=1,
        grid=(t // tc,),
        in_specs=[pl.BlockSpec((None, 2 * TOP_K, tc), lambda i, s: (i, 0, 0), memory_space=pltpu.SMEM),
                  pl.BlockSpec((tc, D_MODEL), lambda i, s: (i, 0)),
                  pl.BlockSpec((tc, TOP_K), lambda i, s: (i, 0)),
                  pl.BlockSpec((1, D_MODEL), lambda i, s: (0, 0)),
                  pl.BlockSpec(memory_space=pl.ANY)],
        out_specs=pl.BlockSpec((tc, D_MODEL), lambda i, s: (i, 0)),
        scratch_shapes=[pltpu.VMEM((TOP_K, tc, D_MODEL), F32), pltpu.SemaphoreType.DMA(())],
    )
    return pl.pallas_call(
        _combine_kernel,
        grid_spec=grid_spec,
        out_shape=jax.ShapeDtypeStruct((t, D_MODEL), F32),
        compiler_params=pltpu.CompilerParams(dimension_semantics=("arbitrary",)),
        name="combine",
    )(starts, meta_blk, x1, gates_t, nf_w, ys)


def _tile(n, pref):
    return pref if n % pref == 0 else n


def kernel(x, norm1_w, w_in, ssd_conv_w, ssd_conv_b, ssd_dt_bias, ssd_a_log, ssd_d, ssd_norm_w, ml_conv_w, ml_conv_b, ml_i_bias, ml_f_bias, ml_norm_w, w_out, norm2_w, router_g_w, router_g_b, router_e_w, router_e_b, exp_w_gate, exp_w_up, exp_w_down, norm_f_w):
    bsz, seq, d = x.shape
    assert d == D_MODEL and seq % CHUNK == 0 and norm1_w.shape[0] == 1
    t = bsz * seq
    tl = _tile(seq, 256)
    tr = _tile(t, 512)
    tm = _tile(t * TOP_K, 512)

    consts = _mixer_consts(norm1_w[0], w_in[0], ssd_conv_w[0], ssd_conv_b[0], ssd_dt_bias[0], ssd_a_log[0],
                           ssd_d[0], ssd_norm_w[0], ml_conv_w[0], ml_conv_b[0], ml_i_bias[0], ml_f_bias[0],
                           ml_norm_w[0], w_out[0], norm2_w[0], router_g_w[0], router_e_w[0])
    x1, h2, lg_t = _mixer_call(x, consts, tl)
    x1 = x1.reshape(t, D_MODEL)
    h2 = h2.reshape(t, D_MODEL)

    bias_col = jnp.concatenate([router_e_b[0], router_g_b[0],
                                jnp.full((LOGIT_ROWS - N_EXPERTS - MOE_GROUPS,), STAB_INIT, F32)]).reshape(-1, 1)
    meta, gates, cnt = _route_call(lg_t, bias_col.astype(F32), tr)
    counts = cnt[:, 0].astype(jnp.int32)
    item_tile, item_exp, starts, starts_ext = _expert_items(counts, t * TOP_K, tm)
    meta_blk = meta[0:2 * TOP_K].reshape(2 * TOP_K, t // tr, tr).transpose(1, 0, 2)

    xs = _dispatch_call(starts, meta_blk, h2, tr)
    ys = _experts_call(item_tile, item_exp, starts_ext, xs, exp_w_gate[0].astype(BF16),
                       exp_w_up[0].astype(BF16), exp_w_down[0].astype(BF16), tm)
    out = _combine_call(starts, meta_blk, x1, gates[0:TOP_K].T, norm_f_w.reshape(1, -1).astype(F32), ys, tr)
    return out.reshape(bsz, seq, D_MODEL)
```

```python
import functools

import jax
import jax.numpy as jnp
import numpy as np
from jax import lax
from jax.experimental import pallas as pl
from jax.experimental.pallas import tpu as pltpu

D_MODEL = 1024
CHUNK = 64
SSD_WIDTH = 1024
SSD_HEAD_DIM = 64
SSD_HEADS = 16
SSD_GROUPS = 2
SSD_STATE = 128
SSD_CONV = 4
SSD_XBC = SSD_WIDTH + 2 * SSD_GROUPS * SSD_STATE
ML_WIDTH = 1024
ML_HEADS = 8
ML_HEAD_DIM = 128
ML_CONV = 4
D_MIX = SSD_WIDTH + ML_WIDTH
MOE_GROUPS = 4
EXPERTS_PER_GROUP = 8
N_EXPERTS = 32
TOP_K = 2
D_EXPERT = 512
EPS = 1e-6
STAB_INIT = -1e30

LANES = 128
GATE_COLS = 2 * LANES
CONV_PAD = 8
LOGIT_ROWS = 64
V7X_VMEM_LIMIT = 56 * 1024 * 1024

F32 = jnp.float32
BF16 = jnp.bfloat16
NEG_INF = float("-inf")


def _dot(a, b, precision=None):
    return jnp.dot(a, b, preferred_element_type=F32, precision=precision)


def _dot_nt(a, b):
    return lax.dot_general(a, b, (((1,), (1,)), ((), ())), preferred_element_type=F32)


def _dot_tn(a, b):
    return lax.dot_general(a, b, (((0,), (0,)), ((), ())), preferred_element_type=F32)


def _split3(x, axis):
    hi = x.astype(BF16)
    r1 = x - hi.astype(F32)
    mid = r1.astype(BF16)
    lo = (r1 - mid.astype(F32)).astype(BF16)
    return jnp.concatenate([hi, mid, lo], axis=axis)


def _silu(x):
    return x * jax.nn.sigmoid(x)


def _softplus(x):
    return jnp.maximum(x, 0.0) + jnp.log1p(jnp.exp(-jnp.abs(x)))


def _rms(x, w):
    return x * lax.rsqrt(jnp.mean(x * x, axis=-1, keepdims=True) + EPS) * w


def _mixer_kernel(*refs, tl, nl):
    for parity in range(2):
        @pl.when(pl.program_id(0) % 2 == parity)
        def _(parity=parity):
            _mixer_step(*refs, tl=tl, nl=nl, slot_a=parity)


def _mixer_step(x_ref, xc_ref, n1_ref, wz_ref, wxbc_ref, wqk_ref, wv_ref, wo_ref, wg_ref,
                  scw_ref, scb_ref, dtb_ref, alog_ref, dskip_ref, snw_ref,
                  mcw_ref, mcb_ref, gb_ref, mnw_ref, wout_ref, n2_ref, wr_ref,
                  e16_ref, e2_ref, tril_ref, seq_ref, sge_ref, meq_ref, mge_ref,
                  x1_ref, h2_ref, lg_ref,
                  xbc_buf, qk_buf, z_buf, v_buf, o_buf, g_buf, xbc_c, qk_c, mix_buf,
                  xbc_tail, qk_tail, sst, mst, mm, h_buf, *, tl, nl, slot_a):
    nchunk = tl // CHUNK
    step = pl.program_id(0)
    slot_b = 1 - slot_a
    slot_c = slot_a

    @pl.when(step == 0)
    def _():
        xbc_buf[...] = jnp.zeros_like(xbc_buf)
        qk_buf[...] = jnp.zeros_like(qk_buf)
        z_buf[...] = jnp.zeros_like(z_buf)
        v_buf[...] = jnp.zeros_like(v_buf)
        o_buf[...] = jnp.zeros_like(o_buf)
        g_buf[...] = jnp.zeros_like(g_buf)
        mix_buf[...] = jnp.zeros_like(mix_buf)
        xbc_tail[...] = jnp.zeros_like(xbc_tail)
        qk_tail[...] = jnp.zeros_like(qk_tail)
        sst[...] = jnp.zeros_like(sst)
        mst[...] = jnp.zeros_like(mst)
        mm[...] = jnp.full(mm.shape, STAB_INIT, F32)

    h_buf[...] = _rms(x_ref[...], n1_ref[...]).astype(BF16)
    nblk = 2 * LANES
    tasks = []

    def proj_task(w_ref, store):
        for b0 in range(0, w_ref.shape[1], nblk):
            tasks.append(lambda b0=b0: store(slice(b0, b0 + nblk), _dot(h_buf[...], w_ref[:, b0:b0 + nblk])))

    def out_task(cols):
        x1_ref[:, cols] = xc_ref[:, cols] + _dot(mix_buf[slot_c], wout_ref[:, cols])

    def route_task():
        h2 = _rms(x1_ref[...], n2_ref[...])
        h2_ref[...] = h2
        lg_ref[...] = _dot_nt(wr_ref[...], h2.astype(BF16))

    for b0 in range(0, D_MODEL, nblk):
        tasks.append(lambda b0=b0: out_task(slice(b0, b0 + nblk)))
    tasks.append(route_task)

    def put(buf, r0=0):
        def store(cols, val):
            buf[slot_a, r0:r0 + tl, cols] = val
        return store

    proj_task(wz_ref, put(z_buf))
    proj_task(wg_ref, put(g_buf))
    proj_task(wxbc_ref, put(xbc_buf, CONV_PAD))
    proj_task(wqk_ref, put(qk_buf, CONV_PAD))
    proj_task(wv_ref, put(v_buf))
    proj_task(wo_ref, put(o_buf))
    per_chunk = -(-len(tasks) // nchunk)

    def run_task(c, k):
        idx = c * per_chunk + k
        if idx < len(tasks):
            tasks[idx]()

    seq_start = (step + nl - 1) % nl == 0
    xbc_buf[slot_b, 0:CONV_PAD, :] = jnp.where(seq_start, 0.0, xbc_tail[...])
    qk_buf[slot_b, 0:CONV_PAD, :] = jnp.where(seq_start, 0.0, qk_tail[...])
    sst[...] = jnp.where(seq_start, 0.0, sst[...])
    mst[...] = jnp.where(seq_start, 0.0, mst[...])
    mm[...] = jnp.where(seq_start, STAB_INIT, mm[...])

    first = CONV_PAD - (SSD_CONV - 1)
    xbc_tail[...] = xbc_buf[slot_b, tl:tl + CONV_PAD, :]
    qk_tail[...] = qk_buf[slot_b, tl:tl + CONV_PAD, :]

    for c in range(nchunk):
        run_task(c, 0)
        r0 = c * CHUNK
        acc = scb_ref[...] + scw_ref[0:1, :] * xbc_buf[slot_b, r0 + first:r0 + first + CHUNK, :]
        for j in range(1, SSD_CONV):
            acc = acc + scw_ref[j:j + 1, :] * xbc_buf[slot_b, r0 + first + j:r0 + first + j + CHUNK, :]
        xbc_c[r0:r0 + CHUNK, :] = _silu(acc)
        run_task(c, 1)
        acc = mcb_ref[...] + mcw_ref[0:1, :] * qk_buf[slot_b, r0 + first:r0 + first + CHUNK, :]
        for j in range(1, ML_CONV):
            acc = acc + mcw_ref[j:j + 1, :] * qk_buf[slot_b, r0 + first + j:r0 + first + j + CHUNK, :]
        qk_c[r0:r0 + CHUNK, :] = _silu(acc)
        run_task(c, 2)

        rows = slice(c * CHUNK, (c + 1) * CHUNK)
        tril3 = tril_ref[...]

        xs = xbc_c[rows, 0:SSD_WIDTH]
        bm = xbc_c[rows, SSD_WIDTH:SSD_WIDTH + 2 * SSD_STATE].astype(BF16)
        cm = xbc_c[rows, SSD_WIDTH + 2 * SSD_STATE:SSD_XBC].astype(BF16)
        dt = _softplus(g_buf[slot_b, rows, 0:SSD_HEADS] + dtb_ref[...])
        a_cs = _dot(tril3, _split3(dt * (-jnp.exp(alog_ref[...])), 0))
        dt_e = _dot(_split3(dt, 1), e16_ref[...])
        a_col = _dot(_split3(a_cs, 1), e16_ref[...])
        a_row = jnp.sum(jnp.where(seq_ref[...] > 0.5, a_col, 0.0), axis=0, keepdims=True)
        lmat = jnp.exp(jnp.where(sge_ref[...] > 0.5, a_col - a_row, NEG_INF))
        cb = jnp.concatenate(
            [_dot_nt(cm[:, g * SSD_STATE:(g + 1) * SSD_STATE],
                     jnp.concatenate([bm[:, g * SSD_STATE:(g + 1) * SSD_STATE]] * (SSD_HEADS // SSD_GROUPS), axis=0))
             for g in range(SSD_GROUPS)], axis=1)
        m_all = (cb * lmat).astype(BF16)
        run_task(c, 3)
        xdt = xs * dt_e
        lane = lax.broadcasted_iota(jnp.int32, (CHUNK, LANES), 1)
        y_parts = []
        for j in range(SSD_HEADS // 2):
            xp = xdt[:, j * LANES:(j + 1) * LANES]
            xbd = jnp.concatenate([jnp.where(lane < SSD_HEAD_DIM, xp, 0.0),
                                   jnp.where(lane >= SSD_HEAD_DIM, xp, 0.0)], axis=0).astype(BF16)
            y_parts.append(_dot(m_all[:, j * LANES:(j + 1) * LANES], xbd))
        y = jnp.concatenate(y_parts, axis=1)
        half = SSD_WIDTH // SSD_GROUPS
        y_int = jnp.concatenate(
            [_dot(cm[:, g * SSD_STATE:(g + 1) * SSD_STATE], sst[g].astype(BF16)) for g in range(SSD_GROUPS)], axis=1)
        y = y + y_int * jnp.exp(a_col)
        a_last = a_col[CHUNK - 1:CHUNK, :]
        xd_b = (xdt * jnp.exp(a_last - a_col)).astype(BF16)
        st_scale = jnp.exp(a_last)
        for g in range(SSD_GROUPS):
            sst[g] = sst[g] * st_scale[:, g * half:(g + 1) * half] + _dot_tn(
                bm[:, g * SSD_STATE:(g + 1) * SSD_STATE], xd_b[:, g * half:(g + 1) * half])
        y = (y + dskip_ref[...] * xs) * _silu(z_buf[slot_b, rows, :])
        for g in range(SSD_GROUPS):
            yg = y[:, g * half:(g + 1) * half]
            yg = yg * lax.rsqrt(jnp.mean(yg * yg, axis=-1, keepdims=True) + EPS) * snw_ref[:, g * half:(g + 1) * half]
            mix_buf[slot_b, rows, g * half:(g + 1) * half] = yg.astype(BF16)

        run_task(c, 4)

        gi = g_buf[slot_b, rows, LANES:LANES + 2 * ML_HEADS] + gb_ref[...]
        col16 = lax.broadcasted_iota(jnp.int32, (CHUNK, 2 * ML_HEADS), 1)
        gi = jnp.where(col16 < ML_HEADS, gi, jnp.minimum(gi, 0.0) - jnp.log1p(jnp.exp(-jnp.abs(gi))))
        gi = jnp.where(col16 < ML_HEADS, gi, _dot(tril3, _split3(gi, 0)))
        gi_e = _dot(_split3(gi, 1), e2_ref[...])
        ig_e = gi_e[:, 0:ML_WIDTH]
        b_col = gi_e[:, ML_WIDTH:2 * ML_WIDTH]
        c_row = jnp.sum(jnp.where(meq_ref[...] > 0.5, ig_e - b_col, 0.0), axis=0, keepdims=True)
        dmat = jnp.where(mge_ref[...] > 0.5, b_col + c_row, NEG_INF)
        heads = range(ML_HEADS)
        hs = lambda arr, hd: arr[:, hd * LANES:(hd + 1) * LANES]
        ones_blk = jnp.ones((CHUNK, LANES), BF16)
        m_prev = mm[...]
        m_inter = b_col + m_prev
        q = [qk_c[rows, hd * LANES:(hd + 1) * LANES].astype(BF16) for hd in heads]
        kf = [qk_c[rows, ML_WIDTH + hd * LANES:ML_WIDTH + (hd + 1) * LANES] * (ML_HEAD_DIM ** -0.5) for hd in heads]
        v_aug = [jnp.concatenate([v_buf[slot_b, rows, hd * LANES:(hd + 1) * LANES].astype(BF16), ones_blk], axis=1)
                 for hd in heads]
        s = [_dot_nt(q[hd], kf[hd].astype(BF16)) for hd in heads]
        qc = [_dot(q[hd], mst[hd].astype(BF16)) for hd in heads]
        run_task(c, 5)
        m_t = jnp.maximum(m_inter, jnp.concatenate(
            [jnp.broadcast_to(jnp.max(hs(dmat, hd), axis=-1, keepdims=True), (CHUNK, LANES)) for hd in heads], axis=1))
        w = jnp.exp(dmat - m_t)
        inter = jnp.exp(m_inter - m_t)
        den_floor = jnp.exp(-m_t)
        nd = [_dot((s[hd] * hs(w, hd)[:, 0:CHUNK]).astype(BF16), v_aug[hd])
              + qc[hd] * jnp.concatenate([hs(inter, hd)] * 2, axis=1) for hd in heads]
        run_task(c, 6)
        hh = [nd[hd][:, 0:LANES] / jnp.maximum(jnp.abs(nd[hd][:, LANES:2 * LANES]), hs(den_floor, hd)) for hd in heads]
        hc = [hh[hd] - jnp.mean(hh[hd], axis=-1, keepdims=True) for hd in heads]
        var = [jnp.mean(hc[hd] * hc[hd], axis=-1, keepdims=True) for hd in heads]
        for hd in heads:
            hn = hc[hd] * lax.rsqrt(var[hd] + EPS) * mnw_ref[:, hd * LANES:(hd + 1) * LANES]
            mix_buf[slot_b, rows, SSD_WIDTH + hd * LANES:SSD_WIDTH + (hd + 1) * LANES] = (
                jax.nn.sigmoid(o_buf[slot_b, rows, hd * LANES:(hd + 1) * LANES]) * hn).astype(BF16)
        run_task(c, 7)
        g_row = b_col[CHUNK - 1:CHUNK, :]
        a = g_row - b_col + ig_e
        m_new = jnp.maximum(g_row + m_prev, jnp.max(a, axis=0, keepdims=True))
        wk = jnp.exp(a - m_new)
        cs = jnp.exp(g_row + m_prev - m_new)
        for hd in heads:
            mst[hd] = mst[hd] * jnp.concatenate([hs(cs, hd)] * 2, axis=1) + _dot_tn(
                (kf[hd] * hs(wk, hd)).astype(BF16), v_aug[hd])
        mm[...] = m_new


def _const_spec(shape):
    nd = len(shape)
    return pl.BlockSpec(shape, lambda s: (0,) * nd, pipeline_mode=pl.Buffered(1))


def _mixer_call(x, consts, tl):
    bsz, seq, _ = x.shape
    nl = seq // tl
    n_tiles = bsz * nl
    x_tiles = x.reshape(n_tiles, tl, D_MODEL)
    tile_in = lambda s: (jnp.minimum(s, n_tiles - 1), 0, 0)
    tile_out = lambda s: (jnp.maximum(s - 2, 0), 0, 0)
    in_specs = ([pl.BlockSpec((None, tl, D_MODEL), tile_in), pl.BlockSpec((None, tl, D_MODEL), tile_out)]
                + [_const_spec(c.shape) for c in consts])
    out_specs = [pl.BlockSpec((None, tl, D_MODEL), tile_out), pl.BlockSpec((None, tl, D_MODEL), tile_out),
                 pl.BlockSpec((LOGIT_ROWS, tl), lambda s: (0, jnp.maximum(s - 2, 0)))]
    out_shape = [jax.ShapeDtypeStruct((n_tiles, tl, D_MODEL), F32),
                 jax.ShapeDtypeStruct((n_tiles, tl, D_MODEL), F32),
                 jax.ShapeDtypeStruct((LOGIT_ROWS, bsz * seq), F32)]
    scratch = [
        pltpu.VMEM((2, tl + CONV_PAD, SSD_XBC), F32),
        pltpu.VMEM((2, tl + CONV_PAD, 2 * ML_WIDTH), F32),
        pltpu.VMEM((2, tl, SSD_WIDTH), F32),
        pltpu.VMEM((2, tl, ML_WIDTH), F32),
        pltpu.VMEM((2, tl, ML_WIDTH), F32),
        pltpu.VMEM((2, tl, GATE_COLS), F32),
        pltpu.VMEM((tl, SSD_XBC), F32),
        pltpu.VMEM((tl, 2 * ML_WIDTH), F32),
        pltpu.VMEM((2, tl, D_MIX), BF16),
        pltpu.VMEM((CONV_PAD, SSD_XBC), F32),
        pltpu.VMEM((CONV_PAD, 2 * ML_WIDTH), F32),
        pltpu.VMEM((SSD_GROUPS, SSD_STATE, SSD_WIDTH // SSD_GROUPS), F32),
        pltpu.VMEM((ML_HEADS, ML_HEAD_DIM, 2 * LANES), F32),
        pltpu.VMEM((1, ML_WIDTH), F32),
        pltpu.VMEM((tl, D_MODEL), BF16),
    ]
    return pl.pallas_call(
        functools.partial(_mixer_kernel, tl=tl, nl=nl),
        grid=(n_tiles + 2,),
        in_specs=in_specs,
        out_specs=out_specs,
        out_shape=out_shape,
        scratch_shapes=scratch,
        compiler_params=pltpu.CompilerParams(
            dimension_semantics=("arbitrary",), vmem_limit_bytes=V7X_VMEM_LIMIT),
        name="mixer",
    )(x_tiles, x_tiles, *consts)


def _mixer_consts(norm1_w, w_in, ssd_conv_w, ssd_conv_b, ssd_dt_bias, ssd_a_log, ssd_d, ssd_norm_w,
                  ml_conv_w, ml_conv_b, ml_i_bias, ml_f_bias, ml_norm_w, w_out, norm2_w,
                  router_g_w, router_e_w):
    o1 = SSD_WIDTH
    o2 = o1 + SSD_XBC
    o3 = o2 + SSD_HEADS
    o4 = o3 + 2 * ML_WIDTH
    o5 = o4 + ML_WIDTH
    o6 = o5 + ML_WIDTH
    o7 = o6 + ML_HEADS
    zpad = lambda n: jnp.zeros((D_MODEL, n), F32)
    w_gate = jnp.concatenate([w_in[:, o2:o3], zpad(LANES - SSD_HEADS),
                              w_in[:, o6:o7], w_in[:, o7:], zpad(LANES - 2 * ML_HEADS)], axis=1)
    row = lambda v: v.reshape(1, -1).astype(F32)
    w_route = jnp.concatenate([router_e_w, router_g_w,
                               jnp.zeros((D_MODEL, LOGIT_ROWS - N_EXPERTS - MOE_GROUPS), F32)], axis=1).T

    c1024 = np.arange(SSD_WIDTH)
    r64 = np.arange(CHUNK)[:, None]
    e16 = np.tile(c1024[None, :] // SSD_HEAD_DIM == np.arange(SSD_HEADS)[:, None], (3, 1))
    e2 = np.tile(np.arange(2 * ML_WIDTH)[None, :] // LANES == np.arange(2 * ML_HEADS)[:, None], (3, 1))
    tril = np.tile(np.arange(CHUNK)[None, :] <= r64, (1, 3))
    s_pos = (c1024 % SSD_HEAD_DIM)[None, :]
    m_pos = (c1024 % LANES)[None, :]
    as_bf16 = lambda m: jnp.asarray(m.astype(np.float32), BF16)
    as_f32 = lambda m: jnp.asarray(m.astype(np.float32))
    e16, e2, tril = as_bf16(e16), as_bf16(e2), as_bf16(tril)
    seq, sge, meq, mge = as_f32(r64 == s_pos), as_f32(r64 >= s_pos), as_f32(r64 == m_pos), as_f32(r64 >= m_pos)
    return [
        row(norm1_w),
        w_in[:, :o1].astype(BF16), w_in[:, o1:o2].astype(BF16), w_in[:, o3:o4].astype(BF16),
        w_in[:, o4:o5].astype(BF16), w_in[:, o5:o6].astype(BF16), w_gate.astype(BF16),
        ssd_conv_w.astype(F32), row(ssd_conv_b), row(ssd_dt_bias),
        row(ssd_a_log), row(jnp.repeat(ssd_d, SSD_HEAD_DIM)), row(ssd_norm_w),
        ml_conv_w.astype(F32), row(ml_conv_b), row(jnp.concatenate([ml_i_bias, ml_f_bias])), row(ml_norm_w),
        w_out.astype(BF16), row(norm2_w), w_route.astype(BF16),
        e16, e2, tril, seq, sge, meq, mge,
    ]


def _route_kernel(lg_ref, bias_ref, striu_ref, meta_ref, gate_ref, cnt_ref, cnt):
    @pl.when(pl.program_id(0) == 0)
    def _():
        cnt[...] = jnp.zeros_like(cnt)

    tr = lg_ref.shape[1]
    lg = lg_ref[...] + bias_ref[...]
    el = lg[0:N_EXPERTS]
    gl = lg[N_EXPERTS:N_EXPERTS + 8]
    gmax = jnp.max(gl, axis=0, keepdims=True)
    pg = 1.0 / jnp.sum(jnp.exp(gl - gmax), axis=0, keepdims=True)
    grow = lax.broadcasted_iota(jnp.int32, gl.shape, 0)
    gsel = jnp.min(jnp.where(gl == gmax, grow, 8), axis=0, keepdims=True)
    erow = lax.broadcasted_iota(jnp.int32, el.shape, 0)
    m1 = jnp.where((erow >> 3) == gsel, el, NEG_INF)
    v1 = jnp.max(m1, axis=0, keepdims=True)
    i1 = jnp.min(jnp.where(m1 == v1, erow, N_EXPERTS), axis=0, keepdims=True)
    m2 = jnp.where(erow == i1, NEG_INF, m1)
    v2 = jnp.max(m2, axis=0, keepdims=True)
    i2 = jnp.min(jnp.where(m2 == v2, erow, N_EXPERTS), axis=0, keepdims=True)
    e2 = jnp.exp(v2 - v1)
    g1 = pg / (1.0 + e2)
    g2 = g1 * e2
    hit1 = erow == i1
    hit2 = erow == i2
    oh = jnp.where(hit1 | hit2, 1.0, 0.0)
    base = cnt[:, 0:1] + _dot(oh.astype(BF16), striu_ref[...])
    r1 = jnp.sum(jnp.where(hit1, base, 0.0), axis=0, keepdims=True)
    r2 = jnp.sum(jnp.where(hit2, base, 0.0), axis=0, keepdims=True)
    cnt[...] = cnt[...] + jnp.sum(oh, axis=1, keepdims=True)
    zi = jnp.zeros((4, tr), jnp.int32)
    meta_ref[...] = jnp.concatenate([i1, i2, r1.astype(jnp.int32), r2.astype(jnp.int32), zi], axis=0)
    gate_ref[...] = jnp.concatenate([g1, g2, jnp.zeros((6, tr), F32)], axis=0)
    cnt_ref[...] = cnt[...]


def _route_call(lg_t, bias_col, tr):
    t = lg_t.shape[1]
    striu = (jnp.arange(tr)[:, None] < jnp.arange(tr)[None, :]).astype(BF16)
    return pl.pallas_call(
        _route_kernel,
        grid=(t // tr,),
        in_specs=[pl.BlockSpec((LOGIT_ROWS, tr), lambda i: (0, i)),
                  pl.BlockSpec((LOGIT_ROWS, 1), lambda i: (0, 0)),
                  pl.BlockSpec((tr, tr), lambda i: (0, 0))],
        out_specs=[pl.BlockSpec((8, tr), lambda i: (0, i)),
                   pl.BlockSpec((8, tr), lambda i: (0, i)),
                   pl.BlockSpec((N_EXPERTS, LANES), lambda i: (0, 0))],
        out_shape=[jax.ShapeDtypeStruct((8, t), jnp.int32),
                   jax.ShapeDtypeStruct((8, t), F32),
                   jax.ShapeDtypeStruct((N_EXPERTS, LANES), F32)],
        scratch_shapes=[pltpu.VMEM((N_EXPERTS, LANES), F32)],
        compiler_params=pltpu.CompilerParams(dimension_semantics=("arbitrary",)),
        name="route",
    )(lg_t, bias_col, striu)


def _dispatch_kernel(starts_ref, meta_ref, h2_ref, xs_ref, sem):
    td = h2_ref.shape[0]

    def row_copy(t, pos):
        return pltpu.make_async_copy(h2_ref.at[pl.ds(t, 1)], xs_ref.at[pl.ds(pos, 1)], sem)

    def issue(t, carry):
        for k in range(TOP_K):
            row_copy(t, starts_ref[meta_ref[k, t]] + meta_ref[TOP_K + k, t]).start()
        return carry

    lax.fori_loop(0, td, issue, 0)
    for _ in range(TOP_K):
        pltpu.make_async_copy(h2_ref, xs_ref.at[pl.ds(0, td)], sem).wait()


def _dispatch_call(starts, meta_blk, h2, td):
    t = h2.shape[0]
    grid_spec = pltpu.PrefetchScalarGridSpec(
        num_scalar_prefetch=1,
        grid=(t // td,),
        in_specs=[pl.BlockSpec((None, 2 * TOP_K, td), lambda i, s: (i, 0, 0), memory_space=pltpu.SMEM),
                  pl.BlockSpec((td, D_MODEL), lambda i, s: (i, 0))],
        out_specs=pl.BlockSpec(memory_space=pl.ANY),
        scratch_shapes=[pltpu.SemaphoreType.DMA(())],
    )
    return pl.pallas_call(
        _dispatch_kernel,
        grid_spec=grid_spec,
        out_shape=jax.ShapeDtypeStruct((t * TOP_K, D_MODEL), F32),
        compiler_params=pltpu.CompilerParams(dimension_semantics=("arbitrary",)),
        name="dispatch",
    )(starts, meta_blk, h2)


def _experts_kernel(tile_ref, exp_ref, starts_ref, xs_ref, wg_ref, wu_ref, wd_ref, ys_ref):
    i = pl.program_id(0)
    tm = xs_ref.shape[0]
    e = exp_ref[i]
    tile = tile_ref[i]
    x = xs_ref[...].astype(BF16)
    act = (_silu(_dot(x, wg_ref[...])) * _dot(x, wu_ref[...])).astype(BF16)
    y = _dot(act, wd_ref[...])
    rows = tile * tm + lax.broadcasted_iota(jnp.int32, (tm, 1), 0)
    y = jnp.where((rows >= starts_ref[e]) & (rows < starts_ref[e + 1]), y, 0.0)
    is_first = jnp.logical_or(i == 0, tile_ref[jnp.maximum(i - 1, 0)] != tile)

    @pl.when(is_first)
    def _():
        ys_ref[...] = y

    @pl.when(jnp.logical_not(is_first))
    def _():
        ys_ref[...] = ys_ref[...] + y


def _experts_call(item_tile, item_exp, starts_ext, xs, wg, wu, wd, tm):
    n = xs.shape[0]
    n_items = item_tile.shape[0]
    w_idx = lambda i, tile, ex, st: (jnp.minimum(ex[i], N_EXPERTS - 1), 0, 0)
    grid_spec = pltpu.PrefetchScalarGridSpec(
        num_scalar_prefetch=3,
        grid=(n_items,),
        in_specs=[pl.BlockSpec((tm, D_MODEL), lambda i, tile, ex, st: (tile[i], 0)),
                  pl.BlockSpec((None, D_MODEL, D_EXPERT), w_idx),
                  pl.BlockSpec((None, D_MODEL, D_EXPERT), w_idx),
                  pl.BlockSpec((None, D_EXPERT, D_MODEL), w_idx)],
        out_specs=pl.BlockSpec((tm, D_MODEL), lambda i, tile, ex, st: (tile[i], 0)),
    )
    return pl.pallas_call(
        _experts_kernel,
        grid_spec=grid_spec,
        out_shape=jax.ShapeDtypeStruct((n, D_MODEL), F32),
        compiler_params=pltpu.CompilerParams(
            dimension_semantics=("arbitrary",), vmem_limit_bytes=V7X_VMEM_LIMIT),
        name="experts",
    )(item_tile, item_exp, starts_ext, xs, wg, wu, wd)


def _expert_items(counts, n_rows, tm):
    n_tiles = n_rows // tm
    n_items = n_tiles + N_EXPERTS - 1
    starts = jnp.concatenate([jnp.zeros((1,), jnp.int32), jnp.cumsum(counts).astype(jnp.int32)])
    lo = starts[:-1] // tm
    hi = (starts[1:] - 1) // tm
    per = jnp.where(counts > 0, hi - lo + 1, 0)
    cum = jnp.cumsum(per)
    idx = jnp.arange(n_items, dtype=jnp.int32)
    e = jnp.sum(idx[:, None] >= cum[None, :], axis=1).astype(jnp.int32)
    valid = idx < cum[-1]
    e_c = jnp.minimum(e, N_EXPERTS - 1)
    tile = lo[e_c] + idx - (cum[e_c] - per[e_c])
    item_tile = jnp.where(valid, tile, n_tiles - 1).astype(jnp.int32)
    item_exp = jnp.where(valid, e_c, N_EXPERTS).astype(jnp.int32)
    starts_ext = jnp.concatenate([starts, starts[-1:]])
    return item_tile, item_exp, starts, starts_ext


def _combine_kernel(starts_ref, meta_ref, x1_ref, gate_ref, nf_ref, ys_ref, out_ref, ybuf, sem):
    tc = x1_ref.shape[0]

    def issue(t, carry):
        for k in range(TOP_K):
            pos = starts_ref[meta_ref[k, t]] + meta_ref[TOP_K + k, t]
            pltpu.make_async_copy(ys_ref.at[pl.ds(pos, 1)], ybuf.at[k, pl.ds(t, 1)], sem).start()
        return carry

    lax.fori_loop(0, tc, issue, 0)
    for k in range(TOP_K):
        pltpu.make_async_copy(ys_ref.at[pl.ds(0, tc)], ybuf.at[k], sem).wait()
    g = gate_ref[...]
    x2 = x1_ref[...] + g[:, 0:1] * ybuf[0] + g[:, 1:2] * ybuf[1]
    out_ref[...] = _rms(x2, nf_ref[...])


def _combine_call(starts, meta_blk, x1, gates_t, nf_w, ys, tc):
    t = x1.shape[0]
    grid_spec = pltpu.PrefetchScalarGridSpec(
        num_scalar_prefetch=1,
        grid=(t // tc,),
        in_specs=[pl.BlockSpec((None, 2 * TOP_K, tc), lambda i, s: (i, 0, 0), memory_space=pltpu.SMEM),
                  pl.BlockSpec((tc, D_MODEL), lambda i, s: (i, 0)),
                  pl.BlockSpec((tc, TOP_K), lambda i, s: (i, 0)),
                  pl.BlockSpec((1, D_MODEL), lambda i, s: (0, 0)),
                  pl.BlockSpec(memory_space=pl.ANY)],
        out_specs=pl.BlockSpec((tc, D_MODEL), lambda i, s: (i, 0)),
        scratch_shapes=[pltpu.VMEM((TOP_K, tc, D_MODEL), F32), pltpu.SemaphoreType.DMA(())],
    )
    return pl.pallas_call(
        _combine_kernel,
        grid_spec=grid_spec,
        out_shape=jax.ShapeDtypeStruct((t, D_MODEL), F32),
        compiler_params=pltpu.CompilerParams(dimension_semantics=("arbitrary",)),
        name="combine",
    )(starts, meta_blk, x1, gates_t, nf_w, ys)


def _tile(n, pref):
    return pref if n % pref == 0 else n


def kernel(x, norm1_w, w_in, ssd_conv_w, ssd_conv_b, ssd_dt_bias, ssd_a_log, ssd_d, ssd_norm_w, ml_conv_w, ml_conv_b, ml_i_bias, ml_f_bias, ml_norm_w, w_out, norm2_w, router_g_w, router_g_b, router_e_w, router_e_b, exp_w_gate, exp_w_up, exp_w_down, norm_f_w):
    bsz, seq, d = x.shape
    assert d == D_MODEL and seq % CHUNK == 0 and norm1_w.shape[0] == 1
    t = bsz * seq
    tl = _tile(seq, 256)
    tr = _tile(t, 512)
    tm = _tile(t * TOP_K, 512)

    consts = _mixer_consts(norm1_w[0], w_in[0], ssd_conv_w[0], ssd_conv_b[0], ssd_dt_bias[0], ssd_a_log[0],
                           ssd_d[0], ssd_norm_w[0], ml_conv_w[0], ml_conv_b[0], ml_i_bias[0], ml_f_bias[0],
                           ml_norm_w[0], w_out[0], norm2_w[0], router_g_w[0], router_e_w[0])
    x1, h2, lg_t = _mixer_call(x, consts, tl)
    x1 = x1.reshape(t, D_MODEL)
    h2 = h2.reshape(t, D_MODEL)

    bias_col = jnp.concatenate([router_e_b[0], router_g_b[0],
                                jnp.full((LOGIT_ROWS - N_EXPERTS - MOE_GROUPS,), STAB_INIT, F32)]).reshape(-1, 1)
    meta, gates, cnt = _route_call(lg_t, bias_col.astype(F32), tr)
    counts = cnt[:, 0].astype(jnp.int32)
    item_tile, item_exp, starts, starts_ext = _expert_items(counts, t * TOP_K, tm)
    meta_blk = meta[0:2 * TOP_K].reshape(2 * TOP_K, t // tr, tr).transpose(1, 0, 2)

    xs = _dispatch_call(starts, meta_blk, h2, tr)
    ys = _experts_call(item_tile, item_exp, starts_ext, xs, exp_w_gate[0].astype(BF16),
                       exp_w_up[0].astype(BF16), exp_w_down[0].astype(BF16), tm)
    out = _combine_call(starts, meta_blk, x1, gates[0:TOP_K].T, norm_f_w.reshape(1, -1).astype(F32), ys, tr)
    return out.reshape(bsz, seq, D_MODEL)
```

```python
import functools

import jax
import jax.numpy as jnp
import numpy as np
from jax import lax
from jax.experimental import pallas as pl
from jax.experimental.pallas import tpu as pltpu

D_MODEL = 1024
CHUNK = 64
SSD_WIDTH = 1024
SSD_HEAD_DIM = 64
SSD_HEADS = 16
SSD_GROUPS = 2
SSD_STATE = 128
SSD_CONV = 4
SSD_XBC = SSD_WIDTH + 2 * SSD_GROUPS * SSD_STATE
ML_WIDTH = 1024
ML_HEADS = 8
ML_HEAD_DIM = 128
ML_CONV = 4
D_MIX = SSD_WIDTH + ML_WIDTH
MOE_GROUPS = 4
EXPERTS_PER_GROUP = 8
N_EXPERTS = 32
TOP_K = 2
D_EXPERT = 512
EPS = 1e-6
STAB_INIT = -1e30

LANES = 128
GATE_COLS = 2 * LANES
ROW_ALIGN = 8
CONV_PAD = 8
LOGIT_ROWS = 64
V7X_VMEM_LIMIT = 56 * 1024 * 1024

F32 = jnp.float32
BF16 = jnp.bfloat16
NEG_INF = float("-inf")


def _dot(a, b, precision=None):
    return jnp.dot(a, b, preferred_element_type=F32, precision=precision)


def _dot_nt(a, b):
    return lax.dot_general(a, b, (((1,), (1,)), ((), ())), preferred_element_type=F32)


def _dot_tn(a, b):
    return lax.dot_general(a, b, (((0,), (0,)), ((), ())), preferred_element_type=F32)


def _split3(x, axis):
    hi = x.astype(BF16)
    r1 = x - hi.astype(F32)
    mid = r1.astype(BF16)
    lo = (r1 - mid.astype(F32)).astype(BF16)
    return jnp.concatenate([hi, mid, lo], axis=axis)


def _silu(x):
    return x * jax.nn.sigmoid(x)


def _softplus(x):
    return jnp.maximum(x, 0.0) + jnp.log1p(jnp.exp(-jnp.abs(x)))


def _rms(x, w):
    return x * lax.rsqrt(jnp.mean(x * x, axis=-1, keepdims=True) + EPS) * w


def _mixer_kernel(*refs, tl, nl):
    for parity in range(2):
        @pl.when(pl.program_id(0) % 2 == parity)
        def _(parity=parity):
            _mixer_step(*refs, tl=tl, nl=nl, slot_a=parity)


def _mixer_step(x_ref, xc_ref, n1_ref, wz_ref, wxbc_ref, wqk_ref, wv_ref, wo_ref, wg_ref,
                  scw_ref, scb_ref, dtb_ref, alog_ref, dskip_ref, snw_ref,
                  mcw_ref, mcb_ref, gb_ref, mnw_ref, wout_ref, n2_ref, wr_ref,
                  e16_ref, e2_ref, tril_ref, seq_ref, sge_ref, meq_ref, mge_ref,
                  x1_ref, h2_ref, lg_ref,
                  xbc_buf, qk_buf, z_buf, v_buf, o_buf, g_buf, xbc_c, qk_c, mix_buf,
                  xbc_tail, qk_tail, sst, mst, mm, h_buf, *, tl, nl, slot_a):
    nchunk = tl // CHUNK
    step = pl.program_id(0)
    slot_b = 1 - slot_a
    slot_c = slot_a

    @pl.when(step == 0)
    def _():
        xbc_buf[...] = jnp.zeros_like(xbc_buf)
        qk_buf[...] = jnp.zeros_like(qk_buf)
        z_buf[...] = jnp.zeros_like(z_buf)
        v_buf[...] = jnp.zeros_like(v_buf)
        o_buf[...] = jnp.zeros_like(o_buf)
        g_buf[...] = jnp.zeros_like(g_buf)
        mix_buf[...] = jnp.zeros_like(mix_buf)
        xbc_tail[...] = jnp.zeros_like(xbc_tail)
        qk_tail[...] = jnp.zeros_like(qk_tail)
        sst[...] = jnp.zeros_like(sst)
        mst[...] = jnp.zeros_like(mst)
        mm[...] = jnp.full(mm.shape, STAB_INIT, F32)

    h_buf[...] = _rms(x_ref[...], n1_ref[...]).astype(BF16)
    nblk = 2 * LANES
    tasks = []

    def proj_task(w_ref, store):
        for b0 in range(0, w_ref.shape[1], nblk):
            tasks.append(lambda b0=b0: store(slice(b0, b0 + nblk), _dot(h_buf[...], w_ref[:, b0:b0 + nblk])))

    def out_task(cols):
        x1_ref[:, cols] = xc_ref[:, cols] + _dot(mix_buf[slot_c], wout_ref[:, cols])

    def route_task():
        h2 = _rms(x1_ref[...], n2_ref[...])
        h2 = h2.astype(BF16)
        h2_ref[...] = h2
        lg_ref[...] = _dot_nt(wr_ref[...], h2)

    for b0 in range(0, D_MODEL, nblk):
        tasks.append(lambda b0=b0: out_task(slice(b0, b0 + nblk)))
    tasks.append(route_task)

    def put(buf, r0=0):
        def store(cols, val):
            buf[slot_a, r0:r0 + tl, cols] = val
        return store

    proj_task(wz_ref, put(z_buf))
    proj_task(wg_ref, put(g_buf))
    proj_task(wxbc_ref, put(xbc_buf, CONV_PAD))
    proj_task(wqk_ref, put(qk_buf, CONV_PAD))
    proj_task(wv_ref, put(v_buf))
    proj_task(wo_ref, put(o_buf))
    per_chunk = -(-len(tasks) // nchunk)

    def run_task(c, k):
        idx = c * per_chunk + k
        if idx < len(tasks):
            tasks[idx]()

    seq_start = (step + nl - 1) % nl == 0
    xbc_buf[slot_b, 0:CONV_PAD, :] = jnp.where(seq_start, 0.0, xbc_tail[...])
    qk_buf[slot_b, 0:CONV_PAD, :] = jnp.where(seq_start, 0.0, qk_tail[...])
    sst[...] = jnp.where(seq_start, 0.0, sst[...])
    mst[...] = jnp.where(seq_start, 0.0, mst[...])
    mm[...] = jnp.where(seq_start, STAB_INIT, mm[...])

    first = CONV_PAD - (SSD_CONV - 1)
    xbc_tail[...] = xbc_buf[slot_b, tl:tl + CONV_PAD, :]
    qk_tail[...] = qk_buf[slot_b, tl:tl + CONV_PAD, :]

    for c in range(nchunk):
        run_task(c, 0)
        r0 = c * CHUNK
        acc = scb_ref[...] + scw_ref[0:1, :] * xbc_buf[slot_b, r0 + first:r0 + first + CHUNK, :]
        for j in range(1, SSD_CONV):
            acc = acc + scw_ref[j:j + 1, :] * xbc_buf[slot_b, r0 + first + j:r0 + first + j + CHUNK, :]
        xbc_c[r0:r0 + CHUNK, :] = _silu(acc)
        run_task(c, 1)
        acc = mcb_ref[...] + mcw_ref[0:1, :] * qk_buf[slot_b, r0 + first:r0 + first + CHUNK, :]
        for j in range(1, ML_CONV):
            acc = acc + mcw_ref[j:j + 1, :] * qk_buf[slot_b, r0 + first + j:r0 + first + j + CHUNK, :]
        qk_c[r0:r0 + CHUNK, :] = _silu(acc)
        run_task(c, 2)

        rows = slice(c * CHUNK, (c + 1) * CHUNK)
        tril3 = tril_ref[...]

        xs = xbc_c[rows, 0:SSD_WIDTH]
        bm = xbc_c[rows, SSD_WIDTH:SSD_WIDTH + 2 * SSD_STATE].astype(BF16)
        cm = xbc_c[rows, SSD_WIDTH + 2 * SSD_STATE:SSD_XBC].astype(BF16)
        dt = _softplus(g_buf[slot_b, rows, 0:SSD_HEADS] + dtb_ref[...])
        a_cs = _dot(tril3, _split3(dt * (-jnp.exp(alog_ref[...])), 0))
        dt_e = _dot(_split3(dt, 1), e16_ref[...])
        a_col = _dot(_split3(a_cs, 1), e16_ref[...])
        a_row = jnp.sum(jnp.where(seq_ref[...] > 0.5, a_col, 0.0), axis=0, keepdims=True)
        lmat = jnp.exp(jnp.where(sge_ref[...] > 0.5, a_col - a_row, NEG_INF))
        cb = jnp.concatenate(
            [_dot_nt(cm[:, g * SSD_STATE:(g + 1) * SSD_STATE],
                     jnp.concatenate([bm[:, g * SSD_STATE:(g + 1) * SSD_STATE]] * (SSD_HEADS // SSD_GROUPS), axis=0))
             for g in range(SSD_GROUPS)], axis=1)
        m_all = (cb * lmat).astype(BF16)
        run_task(c, 3)
        xdt = xs * dt_e
        lane = lax.broadcasted_iota(jnp.int32, (CHUNK, LANES), 1)
        y_parts = []
        for j in range(SSD_HEADS // 2):
            xp = xdt[:, j * LANES:(j + 1) * LANES]
            xbd = jnp.concatenate([jnp.where(lane < SSD_HEAD_DIM, xp, 0.0),
                                   jnp.where(lane >= SSD_HEAD_DIM, xp, 0.0)], axis=0).astype(BF16)
            y_parts.append(_dot(m_all[:, j * LANES:(j + 1) * LANES], xbd))
        y = jnp.concatenate(y_parts, axis=1)
        half = SSD_WIDTH // SSD_GROUPS
        y_int = jnp.concatenate(
            [_dot(cm[:, g * SSD_STATE:(g + 1) * SSD_STATE], sst[g].astype(BF16)) for g in range(SSD_GROUPS)], axis=1)
        y = y + y_int * jnp.exp(a_col)
        a_last = a_col[CHUNK - 1:CHUNK, :]
        xd_b = (xdt * jnp.exp(a_last - a_col)).astype(BF16)
        st_scale = jnp.exp(a_last)
        for g in range(SSD_GROUPS):
            sst[g] = sst[g] * st_scale[:, g * half:(g + 1) * half] + _dot_tn(
                bm[:, g * SSD_STATE:(g + 1) * SSD_STATE], xd_b[:, g * half:(g + 1) * half])
        y = (y + dskip_ref[...] * xs) * _silu(z_buf[slot_b, rows, :])
        for g in range(SSD_GROUPS):
            yg = y[:, g * half:(g + 1) * half]
            yg = yg * lax.rsqrt(jnp.mean(yg * yg, axis=-1, keepdims=True) + EPS) * snw_ref[:, g * half:(g + 1) * half]
            mix_buf[slot_b, rows, g * half:(g + 1) * half] = yg.astype(BF16)

        run_task(c, 4)

        gi = g_buf[slot_b, rows, LANES:LANES + 2 * ML_HEADS] + gb_ref[...]
        col16 = lax.broadcasted_iota(jnp.int32, (CHUNK, 2 * ML_HEADS), 1)
        gi = jnp.where(col16 < ML_HEADS, gi, jnp.minimum(gi, 0.0) - jnp.log1p(jnp.exp(-jnp.abs(gi))))
        gi = jnp.where(col16 < ML_HEADS, gi, _dot(tril3, _split3(gi, 0)))
        gi_e = _dot(_split3(gi, 1), e2_ref[...])
        ig_e = gi_e[:, 0:ML_WIDTH]
        b_col = gi_e[:, ML_WIDTH:2 * ML_WIDTH]
        c_row = jnp.sum(jnp.where(meq_ref[...] > 0.5, ig_e - b_col, 0.0), axis=0, keepdims=True)
        dmat = jnp.where(mge_ref[...] > 0.5, b_col + c_row, NEG_INF)
        heads = range(ML_HEADS)
        hs = lambda arr, hd: arr[:, hd * LANES:(hd + 1) * LANES]
        ones_blk = jnp.ones((CHUNK, LANES), BF16)
        m_prev = mm[...]
        m_inter = b_col + m_prev
        q = [qk_c[rows, hd * LANES:(hd + 1) * LANES].astype(BF16) for hd in heads]
        kf = [qk_c[rows, ML_WIDTH + hd * LANES:ML_WIDTH + (hd + 1) * LANES] * (ML_HEAD_DIM ** -0.5) for hd in heads]
        v_aug = [jnp.concatenate([v_buf[slot_b, rows, hd * LANES:(hd + 1) * LANES].astype(BF16), ones_blk], axis=1)
                 for hd in heads]
        s = [_dot_nt(q[hd], kf[hd].astype(BF16)) for hd in heads]
        qc = [_dot(q[hd], mst[hd].astype(BF16)) for hd in heads]
        run_task(c, 5)
        m_t = jnp.maximum(m_inter, jnp.concatenate(
            [jnp.broadcast_to(jnp.max(hs(dmat, hd), axis=-1, keepdims=True), (CHUNK, LANES)) for hd in heads], axis=1))
        w = jnp.exp(dmat - m_t)
        inter = jnp.exp(m_inter - m_t)
        den_floor = jnp.exp(-m_t)
        nd = [_dot((s[hd] * hs(w, hd)[:, 0:CHUNK]).astype(BF16), v_aug[hd])
              + qc[hd] * jnp.concatenate([hs(inter, hd)] * 2, axis=1) for hd in heads]
        run_task(c, 6)
        hh = [nd[hd][:, 0:LANES] / jnp.maximum(jnp.abs(nd[hd][:, LANES:2 * LANES]), hs(den_floor, hd)) for hd in heads]
        hc = [hh[hd] - jnp.mean(hh[hd], axis=-1, keepdims=True) for hd in heads]
        var = [jnp.mean(hc[hd] * hc[hd], axis=-1, keepdims=True) for hd in heads]
        for hd in heads:
            hn = hc[hd] * lax.rsqrt(var[hd] + EPS) * mnw_ref[:, hd * LANES:(hd + 1) * LANES]
            mix_buf[slot_b, rows, SSD_WIDTH + hd * LANES:SSD_WIDTH + (hd + 1) * LANES] = (
                jax.nn.sigmoid(o_buf[slot_b, rows, hd * LANES:(hd + 1) * LANES]) * hn).astype(BF16)
        run_task(c, 7)
        g_row = b_col[CHUNK - 1:CHUNK, :]
        a = g_row - b_col + ig_e
        m_new = jnp.maximum(g_row + m_prev, jnp.max(a, axis=0, keepdims=True))
        wk = jnp.exp(a - m_new)
        cs = jnp.exp(g_row + m_prev - m_new)
        for hd in heads:
            mst[hd] = mst[hd] * jnp.concatenate([hs(cs, hd)] * 2, axis=1) + _dot_tn(
                (kf[hd] * hs(wk, hd)).astype(BF16), v_aug[hd])
        mm[...] = m_new


def _const_spec(shape):
    nd = len(shape)
    return pl.BlockSpec(shape, lambda s: (0,) * nd, pipeline_mode=pl.Buffered(1))


def _mixer_call(x, consts, tl):
    bsz, seq, _ = x.shape
    nl = seq // tl
    n_tiles = bsz * nl
    x_tiles = x.reshape(n_tiles, tl, D_MODEL)
    tile_in = lambda s: (jnp.minimum(s, n_tiles - 1), 0, 0)
    tile_out = lambda s: (jnp.maximum(s - 2, 0), 0, 0)
    in_specs = ([pl.BlockSpec((None, tl, D_MODEL), tile_in), pl.BlockSpec((None, tl, D_MODEL), tile_out)]
                + [_const_spec(c.shape) for c in consts])
    out_specs = [pl.BlockSpec((None, tl, D_MODEL), tile_out), pl.BlockSpec((None, tl, D_MODEL), tile_out),
                 pl.BlockSpec((LOGIT_ROWS, tl), lambda s: (0, jnp.maximum(s - 2, 0)))]
    out_shape = [jax.ShapeDtypeStruct((n_tiles, tl, D_MODEL), F32),
                 jax.ShapeDtypeStruct((n_tiles, tl, D_MODEL), BF16),
                 jax.ShapeDtypeStruct((LOGIT_ROWS, bsz * seq), F32)]
    scratch = [
        pltpu.VMEM((2, tl + CONV_PAD, SSD_XBC), F32),
        pltpu.VMEM((2, tl + CONV_PAD, 2 * ML_WIDTH), F32),
        pltpu.VMEM((2, tl, SSD_WIDTH), F32),
        pltpu.VMEM((2, tl, ML_WIDTH), F32),
        pltpu.VMEM((2, tl, ML_WIDTH), F32),
        pltpu.VMEM((2, tl, GATE_COLS), F32),
        pltpu.VMEM((tl, SSD_XBC), F32),
        pltpu.VMEM((tl, 2 * ML_WIDTH), F32),
        pltpu.VMEM((2, tl, D_MIX), BF16),
        pltpu.VMEM((CONV_PAD, SSD_XBC), F32),
        pltpu.VMEM((CONV_PAD, 2 * ML_WIDTH), F32),
        pltpu.VMEM((SSD_GROUPS, SSD_STATE, SSD_WIDTH // SSD_GROUPS), F32),
        pltpu.VMEM((ML_HEADS, ML_HEAD_DIM, 2 * LANES), F32),
        pltpu.VMEM((1, ML_WIDTH), F32),
        pltpu.VMEM((tl, D_MODEL), BF16),
    ]
    return pl.pallas_call(
        functools.partial(_mixer_kernel, tl=tl, nl=nl),
        grid=(n_tiles + 2,),
        in_specs=in_specs,
        out_specs=out_specs,
        out_shape=out_shape,
        scratch_shapes=scratch,
        compiler_params=pltpu.CompilerParams(
            dimension_semantics=("arbitrary",), vmem_limit_bytes=V7X_VMEM_LIMIT),
        name="mixer",
    )(x_tiles, x_tiles, *consts)


def _mixer_consts(norm1_w, w_in, ssd_conv_w, ssd_conv_b, ssd_dt_bias, ssd_a_log, ssd_d, ssd_norm_w,
                  ml_conv_w, ml_conv_b, ml_i_bias, ml_f_bias, ml_norm_w, w_out, norm2_w,
                  router_g_w, router_e_w):
    o1 = SSD_WIDTH
    o2 = o1 + SSD_XBC
    o3 = o2 + SSD_HEADS
    o4 = o3 + 2 * ML_WIDTH
    o5 = o4 + ML_WIDTH
    o6 = o5 + ML_WIDTH
    o7 = o6 + ML_HEADS
    zpad = lambda n: jnp.zeros((D_MODEL, n), F32)
    w_gate = jnp.concatenate([w_in[:, o2:o3], zpad(LANES - SSD_HEADS),
                              w_in[:, o6:o7], w_in[:, o7:], zpad(LANES - 2 * ML_HEADS)], axis=1)
    row = lambda v: v.reshape(1, -1).astype(F32)
    w_route = jnp.concatenate([router_e_w, router_g_w,
                               jnp.zeros((D_MODEL, LOGIT_ROWS - N_EXPERTS - MOE_GROUPS), F32)], axis=1).T

    c1024 = np.arange(SSD_WIDTH)
    r64 = np.arange(CHUNK)[:, None]
    e16 = np.tile(c1024[None, :] // SSD_HEAD_DIM == np.arange(SSD_HEADS)[:, None], (3, 1))
    e2 = np.tile(np.arange(2 * ML_WIDTH)[None, :] // LANES == np.arange(2 * ML_HEADS)[:, None], (3, 1))
    tril = np.tile(np.arange(CHUNK)[None, :] <= r64, (1, 3))
    s_pos = (c1024 % SSD_HEAD_DIM)[None, :]
    m_pos = (c1024 % LANES)[None, :]
    as_bf16 = lambda m: jnp.asarray(m.astype(np.float32), BF16)
    as_f32 = lambda m: jnp.asarray(m.astype(np.float32))
    e16, e2, tril = as_bf16(e16), as_bf16(e2), as_bf16(tril)
    seq, sge, meq, mge = as_f32(r64 == s_pos), as_f32(r64 >= s_pos), as_f32(r64 == m_pos), as_f32(r64 >= m_pos)
    return [
        row(norm1_w),
        w_in[:, :o1].astype(BF16), w_in[:, o1:o2].astype(BF16), w_in[:, o3:o4].astype(BF16),
        w_in[:, o4:o5].astype(BF16), w_in[:, o5:o6].astype(BF16), w_gate.astype(BF16),
        ssd_conv_w.astype(F32), row(ssd_conv_b), row(ssd_dt_bias),
        row(ssd_a_log), row(jnp.repeat(ssd_d, SSD_HEAD_DIM)), row(ssd_norm_w),
        ml_conv_w.astype(F32), row(ml_conv_b), row(jnp.concatenate([ml_i_bias, ml_f_bias])), row(ml_norm_w),
        w_out.astype(BF16), row(norm2_w), w_route.astype(BF16),
        e16, e2, tril, seq, sge, meq, mge,
    ]


def _route_kernel(lg_ref, bias_ref, striu_ref, lpos_ref, gate_ref, cnt_ref):
    tr = lg_ref.shape[1]
    lg = lg_ref[...] + bias_ref[...]
    el = lg[0:N_EXPERTS]
    gl = lg[N_EXPERTS:N_EXPERTS + 8]
    gmax = jnp.max(gl, axis=0, keepdims=True)
    pg = 1.0 / jnp.sum(jnp.exp(gl - gmax), axis=0, keepdims=True)
    grow = lax.broadcasted_iota(jnp.int32, gl.shape, 0)
    gsel = jnp.min(jnp.where(gl == gmax, grow, 8), axis=0, keepdims=True)
    erow = lax.broadcasted_iota(jnp.int32, el.shape, 0)
    m1 = jnp.where((erow >> 3) == gsel, el, NEG_INF)
    v1 = jnp.max(m1, axis=0, keepdims=True)
    i1 = jnp.min(jnp.where(m1 == v1, erow, N_EXPERTS), axis=0, keepdims=True)
    m2 = jnp.where(erow == i1, NEG_INF, m1)
    v2 = jnp.max(m2, axis=0, keepdims=True)
    i2 = jnp.min(jnp.where(m2 == v2, erow, N_EXPERTS), axis=0, keepdims=True)
    e2 = jnp.exp(v2 - v1)
    g1 = pg / (1.0 + e2)
    g2 = g1 * e2
    hit1 = erow == i1
    hit2 = erow == i2
    oh = jnp.where(hit1, 1.0, jnp.where(hit2, 1.0, 0.0))
    earlier = _dot(oh.astype(BF16), striu_ref[...])
    cnt = jnp.sum(oh, axis=1, keepdims=True)
    seg = jnp.floor((cnt + (ROW_ALIGN - 1)) * (1.0 / ROW_ALIGN)) * ROW_ALIGN
    l1 = jnp.sum(jnp.where(hit1, earlier, jnp.where(erow < i1, seg, 0.0)), axis=0, keepdims=True)
    l2 = jnp.sum(jnp.where(hit2, earlier, jnp.where(erow < i2, seg, 0.0)), axis=0, keepdims=True)
    lpos_ref[...] = jnp.concatenate([l1.astype(jnp.int32), l2.astype(jnp.int32), jnp.zeros((6, tr), jnp.int32)], axis=0)
    gate_ref[...] = jnp.concatenate([g1, g2, jnp.zeros((6, tr), F32)], axis=0)
    cnt_ref[...] = jnp.broadcast_to(cnt, cnt_ref.shape)


def _route_call(lg_t, bias_col, tr):
    t = lg_t.shape[1]
    striu = jnp.asarray(np.triu(np.ones((tr, tr), np.float32), 1), BF16)
    return pl.pallas_call(
        _route_kernel,
        grid=(t // tr,),
        in_specs=[pl.BlockSpec((LOGIT_ROWS, tr), lambda i: (0, i)),
                  pl.BlockSpec((LOGIT_ROWS, 1), lambda i: (0, 0)),
                  pl.BlockSpec((tr, tr), lambda i: (0, 0))],
        out_specs=[pl.BlockSpec((8, tr), lambda i: (0, i)),
                   pl.BlockSpec((8, tr), lambda i: (0, i)),
                   pl.BlockSpec((None, N_EXPERTS, LANES), lambda i: (i, 0, 0))],
        out_shape=[jax.ShapeDtypeStruct((8, t), jnp.int32),
                   jax.ShapeDtypeStruct((8, t), F32),
                   jax.ShapeDtypeStruct((t // tr, N_EXPERTS, LANES), F32)],
        compiler_params=pltpu.CompilerParams(dimension_semantics=("arbitrary",)),
        name="route",
    )(lg_t, bias_col, striu)


def _run_copies(runs_ref, copy_run, max_piece, filler_far):
    for e in range(N_EXPERTS + 1):
        dst = runs_ref[0, e] if e < N_EXPERTS else filler_far
        src = runs_ref[1, e]
        n = runs_ref[2, e]
        p = max_piece
        while p >= ROW_ALIGN:
            done = n & ~(2 * p - 1)

            @pl.when((n & p) != 0)
            def _(p=p, done=done, dst=dst, src=src):
                copy_run(pl.multiple_of(dst + done, ROW_ALIGN), pl.multiple_of(src + done, ROW_ALIGN), p)
            p //= 2


def _dispatch_kernel(runs_ref, lpos_ref, h2_ref, xs_ref, loc, sem, *, max_spare):
    tr = h2_ref.shape[0]
    nloc = loc.shape[0]
    jrow = lax.broadcasted_iota(jnp.int32, (nloc, tr), 0)
    perm = jnp.where(jrow == lpos_ref[0:1, :], 1.0, jnp.where(jrow == lpos_ref[1:2, :], 1.0, 0.0)).astype(BF16)
    loc[...] = _dot(perm, h2_ref[...])

    def copy_run(dst, src, n):
        pltpu.make_async_copy(loc.at[pl.ds(src, n)], xs_ref.at[pl.ds(dst, n)], sem).start()

    used = runs_ref[0, N_EXPERTS]
    _run_copies(runs_ref, copy_run, tr, used)
    pltpu.make_async_copy(loc, xs_ref.at[pl.ds(0, nloc)], sem).wait()

    @pl.when(pl.program_id(0) == pl.num_programs(0) - 1)
    def _():
        loc[0:tr, :] = jnp.zeros((tr, D_MODEL), F32)
        spare = xs_ref.shape[0] - used

        def zero_copy(off, n):
            return pltpu.make_async_copy(loc.at[pl.ds(0, n)], xs_ref.at[pl.ds(pl.multiple_of(used + off, ROW_ALIGN), n)], sem)

        def pieces(do):
            for i in range(max_spare // tr):
                @pl.when((i + 1) * tr <= spare)
                def _(i=i):
                    do(zero_copy(i * tr, tr))
            p = tr // 2
            while p >= ROW_ALIGN:
                @pl.when((spare & p) != 0)
                def _(p=p):
                    do(zero_copy(spare & ~(2 * p - 1), p))
                p //= 2

        pieces(lambda c: c.start())
        pieces(lambda c: c.wait())


def _dispatch_call(runs, lpos, h2, tr, n_rows):
    t = h2.shape[0]
    return pl.pallas_call(
        functools.partial(_dispatch_kernel, max_spare=n_rows - t * TOP_K),
        grid=(t // tr,),
        in_specs=[pl.BlockSpec((None, 3, N_EXPERTS + 1), lambda i: (i, 0, 0), memory_space=pltpu.SMEM),
                  pl.BlockSpec((8, tr), lambda i: (0, i)),
                  pl.BlockSpec((tr, D_MODEL), lambda i: (i, 0))],
        out_specs=pl.BlockSpec(memory_space=pl.ANY),
        out_shape=jax.ShapeDtypeStruct((n_rows, D_MODEL), F32),
        scratch_shapes=[pltpu.VMEM((_local_rows(tr), D_MODEL), F32), pltpu.SemaphoreType.DMA(())],
        compiler_params=pltpu.CompilerParams(dimension_semantics=("arbitrary",), vmem_limit_bytes=V7X_VMEM_LIMIT),
        name="dispatch",
    )(runs, lpos, h2)


def _experts_kernel(tile_ref, exp_ref, starts_ref, xs_ref, wg_ref, wu_ref, wd_ref, ys_ref):
    i = pl.program_id(0)
    tm = xs_ref.shape[0]
    e = exp_ref[i]
    tile = tile_ref[i]
    is_first = jnp.logical_or(i == 0, tile_ref[jnp.maximum(i - 1, 0)] != tile)

    def masked_y():
        x = xs_ref[...].astype(BF16)
        act = (_silu(_dot(x, wg_ref[...])) * _dot(x, wu_ref[...])).astype(BF16)
        y = _dot(act, wd_ref[...])
        rows = tile * tm + lax.broadcasted_iota(jnp.int32, (tm, 1), 0)
        return jnp.where((rows >= starts_ref[e]) & (rows < starts_ref[e + 1]), y, 0.0)

    @pl.when(jnp.logical_and(e < N_EXPERTS, is_first))
    def _():
        ys_ref[...] = masked_y()

    @pl.when(jnp.logical_and(e < N_EXPERTS, jnp.logical_not(is_first)))
    def _():
        ys_ref[...] = ys_ref[...] + masked_y()

    @pl.when(e == N_EXPERTS)
    def _():
        ys_ref[...] = jnp.zeros_like(ys_ref)


def _experts_call(item_tile, item_exp, starts_ext, xs, wg, wu, wd, tm):
    n = xs.shape[0]
    n_items = item_tile.shape[0]
    w_idx = lambda i, tile, ex, st: (jnp.minimum(ex[i], N_EXPERTS - 1), 0, 0)
    grid_spec = pltpu.PrefetchScalarGridSpec(
        num_scalar_prefetch=3,
        grid=(n_items,),
        in_specs=[pl.BlockSpec((tm, D_MODEL), lambda i, tile, ex, st: (tile[i], 0)),
                  pl.BlockSpec((None, D_MODEL, D_EXPERT), w_idx),
                  pl.BlockSpec((None, D_MODEL, D_EXPERT), w_idx),
                  pl.BlockSpec((None, D_EXPERT, D_MODEL), w_idx)],
        out_specs=pl.BlockSpec((tm, D_MODEL), lambda i, tile, ex, st: (tile[i], 0)),
    )
    return pl.pallas_call(
        _experts_kernel,
        grid_spec=grid_spec,
        out_shape=jax.ShapeDtypeStruct((n, D_MODEL), F32),
        compiler_params=pltpu.CompilerParams(
            dimension_semantics=("arbitrary",), vmem_limit_bytes=V7X_VMEM_LIMIT),
        name="experts",
    )(item_tile, item_exp, starts_ext, xs, wg, wu, wd)


def _local_rows(tr):
    return TOP_K * tr + N_EXPERTS * ROW_ALIGN


def _expert_items(counts, n_rows, tm):
    n_tiles = n_rows // tm
    n_items = n_tiles + N_EXPERTS - 1
    starts = jnp.concatenate([jnp.zeros((1,), jnp.int32), jnp.cumsum(counts).astype(jnp.int32)])
    lo = starts[:-1] // tm
    hi = (starts[1:] - 1) // tm
    per = jnp.where(counts > 0, hi - lo + 1, 0)
    cum = jnp.cumsum(per)
    idx = jnp.arange(n_items, dtype=jnp.int32)
    e = jnp.sum(idx[:, None] >= cum[None, :], axis=1).astype(jnp.int32)
    valid = idx < cum[-1]
    e_c = jnp.minimum(e, N_EXPERTS - 1)
    tile = lo[e_c] + idx - (cum[e_c] - per[e_c])
    spare_tile = -(-starts[-1] // tm) + idx - cum[-1]
    item_tile = jnp.where(valid, tile, jnp.minimum(spare_tile, n_tiles - 1)).astype(jnp.int32)
    item_exp = jnp.where(valid, e_c, jnp.where(spare_tile < n_tiles, N_EXPERTS, N_EXPERTS + 1)).astype(jnp.int32)
    starts_ext = jnp.concatenate([starts, starts[-1:], starts[-1:]])
    return item_tile, item_exp, starts, starts_ext


def _combine_kernel(runs_ref, lpos_ref, gate_ref, x1_ref, nf_ref, ys_ref, out_ref, loc, sem):
    tr = x1_ref.shape[0]

    def copy_run(dst, src, n):
        pltpu.make_async_copy(ys_ref.at[pl.ds(dst, n)], loc.at[pl.ds(src, n)], sem).start()

    nloc = loc.shape[0]
    _run_copies(runs_ref, copy_run, tr, 0)
    jrow = lax.broadcasted_iota(jnp.int32, (nloc, tr), 0)
    sel = jnp.where(jrow == lpos_ref[0:1, :], gate_ref[0:1, :],
                    jnp.where(jrow == lpos_ref[1:2, :], gate_ref[1:2, :], 0.0)).astype(BF16)
    pltpu.make_async_copy(ys_ref.at[pl.ds(0, nloc)], loc, sem).wait()
    y = _dot_tn(sel, loc[...].astype(BF16))
    out_ref[...] = _rms(x1_ref[...] + y, nf_ref[...])


def _combine_call(runs, lpos, gates, x1, nf_w, ys, tr):
    t = x1.shape[0]
    return pl.pallas_call(
        _combine_kernel,
        grid=(t // tr,),
        in_specs=[pl.BlockSpec((None, 3, N_EXPERTS + 1), lambda i: (i, 0, 0), memory_space=pltpu.SMEM),
                  pl.BlockSpec((8, tr), lambda i: (0, i)),
                  pl.BlockSpec((8, tr), lambda i: (0, i)),
                  pl.BlockSpec((tr, D_MODEL), lambda i: (i, 0)),
                  pl.BlockSpec((1, D_MODEL), lambda i: (0, 0)),
                  pl.BlockSpec(memory_space=pl.ANY)],
        out_specs=pl.BlockSpec((tr, D_MODEL), lambda i: (i, 0)),
        out_shape=jax.ShapeDtypeStruct((t, D_MODEL), F32),
        scratch_shapes=[pltpu.VMEM((_local_rows(tr), D_MODEL), F32), pltpu.SemaphoreType.DMA(())],
        compiler_params=pltpu.CompilerParams(dimension_semantics=("arbitrary",), vmem_limit_bytes=V7X_VMEM_LIMIT),
        name="combine",
    )(runs, lpos, gates, x1, nf_w, ys)


def _tile(n, pref):
    return pref if n % pref == 0 else n


def kernel(x, norm1_w, w_in, ssd_conv_w, ssd_conv_b, ssd_dt_bias, ssd_a_log, ssd_d, ssd_norm_w, ml_conv_w, ml_conv_b, ml_i_bias, ml_f_bias, ml_norm_w, w_out, norm2_w, router_g_w, router_g_b, router_e_w, router_e_b, exp_w_gate, exp_w_up, exp_w_down, norm_f_w):
    bsz, seq, d = x.shape
    assert d == D_MODEL and seq % CHUNK == 0 and norm1_w.shape[0] == 1
    t = bsz * seq
    tl = _tile(seq, 256)
    tr = _tile(t, 512)
    tm = _tile(t * TOP_K, 512)

    consts = _mixer_consts(norm1_w[0], w_in[0], ssd_conv_w[0], ssd_conv_b[0], ssd_dt_bias[0], ssd_a_log[0],
                           ssd_d[0], ssd_norm_w[0], ml_conv_w[0], ml_conv_b[0], ml_i_bias[0], ml_f_bias[0],
                           ml_norm_w[0], w_out[0], norm2_w[0], router_g_w[0], router_e_w[0])
    x1, h2, lg_t = _mixer_call(x, consts, tl)
    x1 = x1.reshape(t, D_MODEL)
    h2 = h2.reshape(t, D_MODEL)

    bias_col = jnp.concatenate([router_e_b[0], router_g_b[0],
                                jnp.full((LOGIT_ROWS - N_EXPERTS - MOE_GROUPS,), STAB_INIT, F32)]).reshape(-1, 1)
    lpos, gates, cnt = _route_call(lg_t, bias_col.astype(F32), tr)

    seg = (cnt[:, :, 0].astype(jnp.int32) + (ROW_ALIGN - 1)) // ROW_ALIGN * ROW_ALIGN
    n_tok_tiles = t // tr
    n_rows = -(-(t * TOP_K + n_tok_tiles * N_EXPERTS * ROW_ALIGN + _local_rows(tr)) // tm) * tm
    item_tile, item_exp, starts, starts_ext = _expert_items(jnp.sum(seg, axis=0), n_rows, tm)
    run_dst = starts[None, :N_EXPERTS] + jnp.cumsum(seg, axis=0) - seg
    run_src = jnp.cumsum(seg, axis=1) - seg
    used = jnp.sum(seg, axis=1, keepdims=True)
    fill = jnp.concatenate([jnp.full_like(used, starts[N_EXPERTS]), used, _local_rows(tr) - used], axis=1)[:, :, None]
    runs = jnp.concatenate([jnp.stack([run_dst, run_src, seg], axis=1), fill], axis=2).astype(jnp.int32)

    xs = _dispatch_call(runs, lpos, h2, tr, n_rows)
    ys = _experts_call(item_tile, item_exp, starts_ext, xs, exp_w_gate[0].astype(BF16),
                       exp_w_up[0].astype(BF16), exp_w_down[0].astype(BF16), tm)
    out = _combine_call(runs, lpos, gates, x1, norm_f_w.reshape(1, -1).astype(F32), ys, tr)
    return out.reshape(bsz, seq, D_MODEL)
```

```python
import functools

import jax
import jax.numpy as jnp
import numpy as np
from jax import lax
from jax.experimental import pallas as pl
from jax.experimental.pallas import tpu as pltpu

D_MODEL = 1024
CHUNK = 64
SSD_WIDTH = 1024
SSD_HEAD_DIM = 64
SSD_HEADS = 16
SSD_GROUPS = 2
SSD_STATE = 128
SSD_CONV = 4
SSD_XBC = SSD_WIDTH + 2 * SSD_GROUPS * SSD_STATE
ML_WIDTH = 1024
ML_HEADS = 8
ML_HEAD_DIM = 128
ML_CONV = 4
D_MIX = SSD_WIDTH + ML_WIDTH
MOE_GROUPS = 4
EXPERTS_PER_GROUP = 8
N_EXPERTS = 32
TOP_K = 2
D_EXPERT = 512
EPS = 1e-6
STAB_INIT = -1e30

LANES = 128
GATE_COLS = 2 * LANES
ROW_ALIGN = 8
CONV_PAD = 8
LOGIT_ROWS = 64
V7X_VMEM_LIMIT = 56 * 1024 * 1024

F32 = jnp.float32
BF16 = jnp.bfloat16
NEG_INF = float("-inf")


def _dot(a, b, precision=None):
    return jnp.dot(a, b, preferred_element_type=F32, precision=precision)


def _dot_nt(a, b):
    return lax.dot_general(a, b, (((1,), (1,)), ((), ())), preferred_element_type=F32)


def _dot_tn(a, b):
    return lax.dot_general(a, b, (((0,), (0,)), ((), ())), preferred_element_type=F32)


def _split3(x, axis):
    hi = x.astype(BF16)
    r1 = x - hi.astype(F32)
    mid = r1.astype(BF16)
    lo = (r1 - mid.astype(F32)).astype(BF16)
    return jnp.concatenate([hi, mid, lo], axis=axis)


def _silu(x):
    return x * jax.nn.sigmoid(x)


def _softplus(x):
    return jnp.maximum(x, 0.0) + jnp.log1p(jnp.exp(-jnp.abs(x)))


def _rms(x, w):
    return x * lax.rsqrt(jnp.mean(x * x, axis=-1, keepdims=True) + EPS) * w


def _mixer_kernel(*refs, tl, nl):
    for parity in range(2):
        @pl.when(pl.program_id(0) % 2 == parity)
        def _(parity=parity):
            _mixer_step(*refs, tl=tl, nl=nl, slot_a=parity)


def _mixer_step(x_ref, xc_ref, n1_ref, wz_ref, wxbc_ref, wqk_ref, wv_ref, wo_ref, wg_ref,
                  scw_ref, scb_ref, dtb_ref, alog_ref, dskip_ref, snw_ref,
                  mcw_ref, mcb_ref, gb_ref, mnw_ref, wout_ref, n2_ref, wr_ref,
                  e16_ref, e2_ref, tril_ref, seq_ref, sge_ref, meq_ref, mge_ref,
                  x1_ref, h2_ref, lg_ref,
                  xbc_buf, qk_buf, z_buf, v_buf, o_buf, g_buf, xbc_c, qk_c, mix_buf,
                  xbc_tail, qk_tail, sst, mst, mm, h_buf, *, tl, nl, slot_a):
    nchunk = tl // CHUNK
    step = pl.program_id(0)
    slot_b = 1 - slot_a
    slot_c = slot_a

    @pl.when(step == 0)
    def _():
        xbc_buf[...] = jnp.zeros_like(xbc_buf)
        qk_buf[...] = jnp.zeros_like(qk_buf)
        z_buf[...] = jnp.zeros_like(z_buf)
        v_buf[...] = jnp.zeros_like(v_buf)
        o_buf[...] = jnp.zeros_like(o_buf)
        g_buf[...] = jnp.zeros_like(g_buf)
        mix_buf[...] = jnp.zeros_like(mix_buf)
        xbc_tail[...] = jnp.zeros_like(xbc_tail)
        qk_tail[...] = jnp.zeros_like(qk_tail)
        sst[...] = jnp.zeros_like(sst)
        mst[...] = jnp.zeros_like(mst)
        mm[...] = jnp.full(mm.shape, STAB_INIT, F32)

    h_buf[...] = _rms(x_ref[...], n1_ref[...]).astype(BF16)
    nblk = 2 * LANES
    tasks = []

    def proj_task(w_ref, store):
        for b0 in range(0, w_ref.shape[1], nblk):
            tasks.append(lambda b0=b0: store(slice(b0, b0 + nblk), _dot(h_buf[...], w_ref[:, b0:b0 + nblk])))

    def out_task(cols):
        x1_ref[:, cols] = xc_ref[:, cols] + _dot(mix_buf[slot_c], wout_ref[:, cols])

    def route_task():
        h2 = _rms(x1_ref[...], n2_ref[...])
        h2 = h2.astype(BF16)
        h2_ref[...] = h2
        lg_ref[...] = _dot_nt(wr_ref[...], h2)

    for b0 in range(0, D_MODEL, nblk):
        tasks.append(lambda b0=b0: out_task(slice(b0, b0 + nblk)))
    tasks.append(route_task)

    def put(buf, r0=0):
        def store(cols, val):
            buf[slot_a, r0:r0 + tl, cols] = val
        return store

    def put_tiles(buf):
        def store(cols, val):
            for k in range(nblk // LANES):
                buf[slot_a, cols.start // LANES + k, CONV_PAD:CONV_PAD + tl, :] = val[:, k * LANES:(k + 1) * LANES]
        return store

    proj_task(wz_ref, put(z_buf))
    proj_task(wg_ref, put(g_buf))
    proj_task(wxbc_ref, put_tiles(xbc_buf))
    proj_task(wqk_ref, put_tiles(qk_buf))
    proj_task(wv_ref, put(v_buf))
    proj_task(wo_ref, put(o_buf))
    per_chunk = -(-len(tasks) // nchunk)

    def run_task(c, k):
        idx = c * per_chunk + k
        if idx < len(tasks):
            tasks[idx]()

    seq_start = (step + nl - 1) % nl == 0
    xbc_buf[slot_b, :, 0:CONV_PAD, :] = jnp.where(seq_start, 0.0, xbc_tail[...])
    qk_buf[slot_b, :, 0:CONV_PAD, :] = jnp.where(seq_start, 0.0, qk_tail[...])
    sst[...] = jnp.where(seq_start, 0.0, sst[...])
    mst[...] = jnp.where(seq_start, 0.0, mst[...])
    mm[...] = jnp.where(seq_start, STAB_INIT, mm[...])

    first = CONV_PAD - (SSD_CONV - 1)
    xbc_tail[...] = xbc_buf[slot_b, :, tl:tl + CONV_PAD, :]
    qk_tail[...] = qk_buf[slot_b, :, tl:tl + CONV_PAD, :]

    def conv_silu(buf, w_ref, b_ref, out, r0):
        for t in range(buf.shape[1]):
            cols = slice(t * LANES, (t + 1) * LANES)
            acc = b_ref[:, cols] + w_ref[0:1, cols] * buf[slot_b, t, r0 + first:r0 + first + CHUNK, :]
            for j in range(1, w_ref.shape[0]):
                acc = acc + w_ref[j:j + 1, cols] * buf[slot_b, t, r0 + first + j:r0 + first + j + CHUNK, :]
            out[r0:r0 + CHUNK, cols] = _silu(acc)

    for c in range(nchunk):
        run_task(c, 0)
        r0 = c * CHUNK
        conv_silu(xbc_buf, scw_ref, scb_ref, xbc_c, r0)
        run_task(c, 1)
        conv_silu(qk_buf, mcw_ref, mcb_ref, qk_c, r0)
        run_task(c, 2)

        rows = slice(c * CHUNK, (c + 1) * CHUNK)
        tril3 = tril_ref[...]

        xs = xbc_c[rows, 0:SSD_WIDTH]
        bm = xbc_c[rows, SSD_WIDTH:SSD_WIDTH + 2 * SSD_STATE].astype(BF16)
        cm = xbc_c[rows, SSD_WIDTH + 2 * SSD_STATE:SSD_XBC].astype(BF16)
        dt = _softplus(g_buf[slot_b, rows, 0:SSD_HEADS] + dtb_ref[...])
        a_cs = _dot(tril3, _split3(dt * (-jnp.exp(alog_ref[...])), 0))
        dt_e = _dot(_split3(dt, 1), e16_ref[...])
        a_col = _dot(_split3(a_cs, 1), e16_ref[...])
        a_row = jnp.sum(jnp.where(seq_ref[...] > 0.5, a_col, 0.0), axis=0, keepdims=True)
        lmat = jnp.exp(jnp.where(sge_ref[...] > 0.5, a_col - a_row, NEG_INF))
        cb = jnp.concatenate(
            [_dot_nt(cm[:, g * SSD_STATE:(g + 1) * SSD_STATE],
                     jnp.concatenate([bm[:, g * SSD_STATE:(g + 1) * SSD_STATE]] * (SSD_HEADS // SSD_GROUPS), axis=0))
             for g in range(SSD_GROUPS)], axis=1)
        m_all = (cb * lmat).astype(BF16)
        run_task(c, 3)
        xdt = xs * dt_e
        lane = lax.broadcasted_iota(jnp.int32, (CHUNK, LANES), 1)
        y_parts = []
        for j in range(SSD_HEADS // 2):
            xp = xdt[:, j * LANES:(j + 1) * LANES]
            xbd = jnp.concatenate([jnp.where(lane < SSD_HEAD_DIM, xp, 0.0),
                                   jnp.where(lane >= SSD_HEAD_DIM, xp, 0.0)], axis=0).astype(BF16)
            y_parts.append(_dot(m_all[:, j * LANES:(j + 1) * LANES], xbd))
        y = jnp.concatenate(y_parts, axis=1)
        half = SSD_WIDTH // SSD_GROUPS
        y_int = jnp.concatenate(
            [_dot(cm[:, g * SSD_STATE:(g + 1) * SSD_STATE], sst[g].astype(BF16)) for g in range(SSD_GROUPS)], axis=1)
        y = y + y_int * jnp.exp(a_col)
        a_last = a_col[CHUNK - 1:CHUNK, :]
        xd_b = (xdt * jnp.exp(a_last - a_col)).astype(BF16)
        st_scale = jnp.exp(a_last)
        for g in range(SSD_GROUPS):
            sst[g] = sst[g] * st_scale[:, g * half:(g + 1) * half] + _dot_tn(
                bm[:, g * SSD_STATE:(g + 1) * SSD_STATE], xd_b[:, g * half:(g + 1) * half])
        y = (y + dskip_ref[...] * xs) * _silu(z_buf[slot_b, rows, :])
        for g in range(SSD_GROUPS):
            yg = y[:, g * half:(g + 1) * half]
            yg = yg * lax.rsqrt(jnp.mean(yg * yg, axis=-1, keepdims=True) + EPS) * snw_ref[:, g * half:(g + 1) * half]
            mix_buf[slot_b, rows, g * half:(g + 1) * half] = yg.astype(BF16)

        run_task(c, 4)

        gi = g_buf[slot_b, rows, LANES:LANES + 2 * ML_HEADS] + gb_ref[...]
        col16 = lax.broadcasted_iota(jnp.int32, (CHUNK, 2 * ML_HEADS), 1)
        gi = jnp.where(col16 < ML_HEADS, gi, jnp.minimum(gi, 0.0) - jnp.log1p(jnp.exp(-jnp.abs(gi))))
        gi = jnp.where(col16 < ML_HEADS, gi, _dot(tril3, _split3(gi, 0)))
        gi_e = _dot(_split3(gi, 1), e2_ref[...])
        ig_e = gi_e[:, 0:ML_WIDTH]
        b_col = gi_e[:, ML_WIDTH:2 * ML_WIDTH]
        c_row = jnp.sum(jnp.where(meq_ref[...] > 0.5, ig_e - b_col, 0.0), axis=0, keepdims=True)
        dmat = jnp.where(mge_ref[...] > 0.5, b_col + c_row, NEG_INF)
        heads = range(ML_HEADS)
        hs = lambda arr, hd: arr[:, hd * LANES:(hd + 1) * LANES]
        ones_blk = jnp.ones((CHUNK, LANES), BF16)
        m_prev = mm[...]
        m_inter = b_col + m_prev
        q = [qk_c[rows, hd * LANES:(hd + 1) * LANES].astype(BF16) for hd in heads]
        kf = [qk_c[rows, ML_WIDTH + hd * LANES:ML_WIDTH + (hd + 1) * LANES] * (ML_HEAD_DIM ** -0.5) for hd in heads]
        v_aug = [jnp.concatenate([v_buf[slot_b, rows, hd * LANES:(hd + 1) * LANES].astype(BF16), ones_blk], axis=1)
                 for hd in heads]
        s = [_dot_nt(q[hd], kf[hd].astype(BF16)) for hd in heads]
        qc = [_dot(q[hd], mst[hd].astype(BF16)) for hd in heads]
        run_task(c, 5)
        m_t = jnp.maximum(m_inter, jnp.concatenate(
            [jnp.broadcast_to(jnp.max(hs(dmat, hd), axis=-1, keepdims=True), (CHUNK, LANES)) for hd in heads], axis=1))
        w = jnp.exp(dmat - m_t)
        inter = jnp.exp(m_inter - m_t)
        den_floor = jnp.exp(-m_t)
        nd = [_dot((s[hd] * hs(w, hd)[:, 0:CHUNK]).astype(BF16), v_aug[hd])
              + qc[hd] * jnp.concatenate([hs(inter, hd)] * 2, axis=1) for hd in heads]
        run_task(c, 6)
        hh = [nd[hd][:, 0:LANES] / jnp.maximum(jnp.abs(nd[hd][:, LANES:2 * LANES]), hs(den_floor, hd)) for hd in heads]
        hc = [hh[hd] - jnp.mean(hh[hd], axis=-1, keepdims=True) for hd in heads]
        var = [jnp.mean(hc[hd] * hc[hd], axis=-1, keepdims=True) for hd in heads]
        for hd in heads:
            hn = hc[hd] * lax.rsqrt(var[hd] + EPS) * mnw_ref[:, hd * LANES:(hd + 1) * LANES]
            mix_buf[slot_b, rows, SSD_WIDTH + hd * LANES:SSD_WIDTH + (hd + 1) * LANES] = (
                jax.nn.sigmoid(o_buf[slot_b, rows, hd * LANES:(hd + 1) * LANES]) * hn).astype(BF16)
        run_task(c, 7)
        g_row = b_col[CHUNK - 1:CHUNK, :]
        a = g_row - b_col + ig_e
        m_new = jnp.maximum(g_row + m_prev, jnp.max(a, axis=0, keepdims=True))
        wk = jnp.exp(a - m_new)
        cs = jnp.exp(g_row + m_prev - m_new)
        for hd in heads:
            mst[hd] = mst[hd] * jnp.concatenate([hs(cs, hd)] * 2, axis=1) + _dot_tn(
                (kf[hd] * hs(wk, hd)).astype(BF16), v_aug[hd])
        mm[...] = m_new


def _const_spec(shape):
    nd = len(shape)
    return pl.BlockSpec(shape, lambda s: (0,) * nd, pipeline_mode=pl.Buffered(1))


def _mixer_call(x, consts, tl):
    bsz, seq, _ = x.shape
    nl = seq // tl
    n_tiles = bsz * nl
    x_tiles = x.reshape(n_tiles, tl, D_MODEL)
    tile_in = lambda s: (jnp.minimum(s, n_tiles - 1), 0, 0)
    tile_out = lambda s: (jnp.maximum(s - 2, 0), 0, 0)
    in_specs = ([pl.BlockSpec((None, tl, D_MODEL), tile_in), pl.BlockSpec((None, tl, D_MODEL), tile_out)]
                + [_const_spec(c.shape) for c in consts])
    out_specs = [pl.BlockSpec((None, tl, D_MODEL), tile_out), pl.BlockSpec((None, tl, D_MODEL), tile_out),
                 pl.BlockSpec((LOGIT_ROWS, tl), lambda s: (0, jnp.maximum(s - 2, 0)))]
    out_shape = [jax.ShapeDtypeStruct((n_tiles, tl, D_MODEL), F32),
                 jax.ShapeDtypeStruct((n_tiles, tl, D_MODEL), BF16),
                 jax.ShapeDtypeStruct((LOGIT_ROWS, bsz * seq), F32)]
    scratch = [
        pltpu.VMEM((2, SSD_XBC // LANES, tl + CONV_PAD, LANES), F32),
        pltpu.VMEM((2, 2 * ML_WIDTH // LANES, tl + CONV_PAD, LANES), F32),
        pltpu.VMEM((2, tl, SSD_WIDTH), F32),
        pltpu.VMEM((2, tl, ML_WIDTH), F32),
        pltpu.VMEM((2, tl, ML_WIDTH), F32),
        pltpu.VMEM((2, tl, GATE_COLS), F32),
        pltpu.VMEM((tl, SSD_XBC), F32),
        pltpu.VMEM((tl, 2 * ML_WIDTH), F32),
        pltpu.VMEM((2, tl, D_MIX), BF16),
        pltpu.VMEM((SSD_XBC // LANES, CONV_PAD, LANES), F32),
        pltpu.VMEM((2 * ML_WIDTH // LANES, CONV_PAD, LANES), F32),
        pltpu.VMEM((SSD_GROUPS, SSD_STATE, SSD_WIDTH // SSD_GROUPS), F32),
        pltpu.VMEM((ML_HEADS, ML_HEAD_DIM, 2 * LANES), F32),
        pltpu.VMEM((1, ML_WIDTH), F32),
        pltpu.VMEM((tl, D_MODEL), BF16),
    ]
    return pl.pallas_call(
        functools.partial(_mixer_kernel, tl=tl, nl=nl),
        grid=(n_tiles + 2,),
        in_specs=in_specs,
        out_specs=out_specs,
        out_shape=out_shape,
        scratch_shapes=scratch,
        compiler_params=pltpu.CompilerParams(
            dimension_semantics=("arbitrary",), vmem_limit_bytes=V7X_VMEM_LIMIT),
        name="mixer",
    )(x_tiles, x_tiles, *consts)


def _mixer_consts(norm1_w, w_in, ssd_conv_w, ssd_conv_b, ssd_dt_bias, ssd_a_log, ssd_d, ssd_norm_w,
                  ml_conv_w, ml_conv_b, ml_i_bias, ml_f_bias, ml_norm_w, w_out, norm2_w,
                  router_g_w, router_e_w):
    o1 = SSD_WIDTH
    o2 = o1 + SSD_XBC
    o3 = o2 + SSD_HEADS
    o4 = o3 + 2 * ML_WIDTH
    o5 = o4 + ML_WIDTH
    o6 = o5 + ML_WIDTH
    o7 = o6 + ML_HEADS
    zpad = lambda n: jnp.zeros((D_MODEL, n), F32)
    w_gate = jnp.concatenate([w_in[:, o2:o3], zpad(LANES - SSD_HEADS),
                              w_in[:, o6:o7], w_in[:, o7:], zpad(LANES - 2 * ML_HEADS)], axis=1)
    row = lambda v: v.reshape(1, -1).astype(F32)
    w_route = jnp.concatenate([router_e_w, router_g_w,
                               jnp.zeros((D_MODEL, LOGIT_ROWS - N_EXPERTS - MOE_GROUPS), F32)], axis=1).T

    c1024 = np.arange(SSD_WIDTH)
    r64 = np.arange(CHUNK)[:, None]
    e16 = np.tile(c1024[None, :] // SSD_HEAD_DIM == np.arange(SSD_HEADS)[:, None], (3, 1))
    e2 = np.tile(np.arange(2 * ML_WIDTH)[None, :] // LANES == np.arange(2 * ML_HEADS)[:, None], (3, 1))
    tril = np.tile(np.arange(CHUNK)[None, :] <= r64, (1, 3))
    s_pos = (c1024 % SSD_HEAD_DIM)[None, :]
    m_pos = (c1024 % LANES)[None, :]
    as_bf16 = lambda m: jnp.asarray(m.astype(np.float32), BF16)
    as_f32 = lambda m: jnp.asarray(m.astype(np.float32))
    e16, e2, tril = as_bf16(e16), as_bf16(e2), as_bf16(tril)
    seq, sge, meq, mge = as_f32(r64 == s_pos), as_f32(r64 >= s_pos), as_f32(r64 == m_pos), as_f32(r64 >= m_pos)
    return [
        row(norm1_w),
        w_in[:, :o1].astype(BF16), w_in[:, o1:o2].astype(BF16), w_in[:, o3:o4].astype(BF16),
        w_in[:, o4:o5].astype(BF16), w_in[:, o5:o6].astype(BF16), w_gate.astype(BF16),
        ssd_conv_w.astype(F32), row(ssd_conv_b), row(ssd_dt_bias),
        row(ssd_a_log), row(jnp.repeat(ssd_d, SSD_HEAD_DIM)), row(ssd_norm_w),
        ml_conv_w.astype(F32), row(ml_conv_b), row(jnp.concatenate([ml_i_bias, ml_f_bias])), row(ml_norm_w),
        w_out.astype(BF16), row(norm2_w), w_route.astype(BF16),
        e16, e2, tril, seq, sge, meq, mge,
    ]


def _route_kernel(lg_ref, bias_ref, striu_ref, lpos_ref, gate_ref, cnt_ref):
    tr = lg_ref.shape[1]
    lg = lg_ref[...] + bias_ref[...]
    el = lg[0:N_EXPERTS]
    gl = lg[N_EXPERTS:N_EXPERTS + 8]
    gmax = jnp.max(gl, axis=0, keepdims=True)
    pg = 1.0 / jnp.sum(jnp.exp(gl - gmax), axis=0, keepdims=True)
    grow = lax.broadcasted_iota(jnp.int32, gl.shape, 0)
    gsel = jnp.min(jnp.where(gl == gmax, grow, 8), axis=0, keepdims=True)
    erow = lax.broadcasted_iota(jnp.int32, el.shape, 0)
    m1 = jnp.where((erow >> 3) == gsel, el, NEG_INF)
    v1 = jnp.max(m1, axis=0, keepdims=True)
    i1 = jnp.min(jnp.where(m1 == v1, erow, N_EXPERTS), axis=0, keepdims=True)
    m2 = jnp.where(erow == i1, NEG_INF, m1)
    v2 = jnp.max(m2, axis=0, keepdims=True)
    i2 = jnp.min(jnp.where(m2 == v2, erow, N_EXPERTS), axis=0, keepdims=True)
    e2 = jnp.exp(v2 - v1)
    g1 = pg / (1.0 + e2)
    g2 = g1 * e2
    hit1 = erow == i1
    hit2 = erow == i2
    oh = jnp.where(hit1, 1.0, jnp.where(hit2, 1.0, 0.0))
    earlier = _dot(oh.astype(BF16), striu_ref[...])
    cnt = jnp.sum(oh, axis=1, keepdims=True)
    seg = jnp.floor((cnt + (ROW_ALIGN - 1)) * (1.0 / ROW_ALIGN)) * ROW_ALIGN
    l1 = jnp.sum(jnp.where(hit1, earlier, jnp.where(erow < i1, seg, 0.0)), axis=0, keepdims=True)
    l2 = jnp.sum(jnp.where(hit2, earlier, jnp.where(erow < i2, seg, 0.0)), axis=0, keepdims=True)
    lpos_ref[...] = jnp.concatenate([l1.astype(jnp.int32), l2.astype(jnp.int32), jnp.zeros((6, tr), jnp.int32)], axis=0)
    gate_ref[...] = jnp.concatenate([g1, g2, jnp.zeros((6, tr), F32)], axis=0)
    cnt_ref[...] = jnp.broadcast_to(cnt, cnt_ref.shape)


def _route_call(lg_t, bias_col, tr):
    t = lg_t.shape[1]
    striu = jnp.asarray(np.triu(np.ones((tr, tr), np.float32), 1), BF16)
    return pl.pallas_call(
        _route_kernel,
        grid=(t // tr,),
        in_specs=[pl.BlockSpec((LOGIT_ROWS, tr), lambda i: (0, i)),
                  pl.BlockSpec((LOGIT_ROWS, 1), lambda i: (0, 0)),
                  pl.BlockSpec((tr, tr), lambda i: (0, 0))],
        out_specs=[pl.BlockSpec((8, tr), lambda i: (0, i)),
                   pl.BlockSpec((8, tr), lambda i: (0, i)),
                   pl.BlockSpec((None, N_EXPERTS, LANES), lambda i: (i, 0, 0))],
        out_shape=[jax.ShapeDtypeStruct((8, t), jnp.int32),
                   jax.ShapeDtypeStruct((8, t), F32),
                   jax.ShapeDtypeStruct((t // tr, N_EXPERTS, LANES), F32)],
        compiler_params=pltpu.CompilerParams(dimension_semantics=("arbitrary",)),
        name="route",
    )(lg_t, bias_col, striu)


def _run_copies(runs_ref, copy_run, max_piece, filler_far):
    for e in range(N_EXPERTS + 1):
        dst = runs_ref[0, e] if e < N_EXPERTS else filler_far
        src = runs_ref[1, e]
        n = runs_ref[2, e]
        p = max_piece
        while p >= ROW_ALIGN:
            done = n & ~(2 * p - 1)

            @pl.when((n & p) != 0)
            def _(p=p, done=done, dst=dst, src=src):
                copy_run(pl.multiple_of(dst + done, ROW_ALIGN), pl.multiple_of(src + done, ROW_ALIGN), p)
            p //= 2


def _dispatch_kernel(runs_ref, lpos_ref, h2_ref, xs_ref, loc, sem, *, max_spare):
    tr = h2_ref.shape[0]
    nloc = loc.shape[1]
    step = pl.program_id(0)
    last = pl.num_programs(0) - 1
    slot = step % 2
    jrow = lax.broadcasted_iota(jnp.int32, (nloc, tr), 0)
    perm = jnp.where(jrow == lpos_ref[0:1, :], 1.0, jnp.where(jrow == lpos_ref[1:2, :], 1.0, 0.0)).astype(BF16)
    loc[slot] = _dot(perm, h2_ref[...])

    def copy_run(dst, src, n):
        pltpu.make_async_copy(loc.at[slot, pl.ds(src, n)], xs_ref.at[pl.ds(dst, n)], sem.at[slot]).start()

    def wait_block(s):
        pltpu.make_async_copy(loc.at[s], xs_ref.at[pl.ds(0, nloc)], sem.at[s]).wait()

    used = runs_ref[0, N_EXPERTS]
    _run_copies(runs_ref, copy_run, tr, used + slot * (nloc - TOP_K * tr))

    @pl.when(step > 0)
    def _():
        wait_block(1 - slot)

    @pl.when(step == last)
    def _():
        wait_block(slot)
        loc[slot, 0:tr, :] = jnp.zeros((tr, D_MODEL), F32)
        spare = xs_ref.shape[0] - used

        def zero_copy(off, n):
            return pltpu.make_async_copy(loc.at[slot, pl.ds(0, n)],
                                         xs_ref.at[pl.ds(pl.multiple_of(used + off, ROW_ALIGN), n)], sem.at[slot])

        def pieces(do):
            for i in range(max_spare // tr):
                @pl.when((i + 1) * tr <= spare)
                def _(i=i):
                    do(zero_copy(i * tr, tr))
            p = tr // 2
            while p >= ROW_ALIGN:
                @pl.when((spare & p) != 0)
                def _(p=p):
                    do(zero_copy(spare & ~(2 * p - 1), p))
                p //= 2

        pieces(lambda c: c.start())
        pieces(lambda c: c.wait())


def _dispatch_call(runs, lpos, h2, tr, n_rows):
    t = h2.shape[0]
    return pl.pallas_call(
        functools.partial(_dispatch_kernel, max_spare=n_rows - t * TOP_K),
        grid=(t // tr,),
        in_specs=[pl.BlockSpec((None, 3, N_EXPERTS + 1), lambda i: (i, 0, 0), memory_space=pltpu.SMEM),
                  pl.BlockSpec((8, tr), lambda i: (0, i)),
                  pl.BlockSpec((tr, D_MODEL), lambda i: (i, 0))],
        out_specs=pl.BlockSpec(memory_space=pl.ANY),
        out_shape=jax.ShapeDtypeStruct((n_rows, D_MODEL), F32),
        scratch_shapes=[pltpu.VMEM((2, _local_rows(tr), D_MODEL), F32), pltpu.SemaphoreType.DMA((2,))],
        compiler_params=pltpu.CompilerParams(dimension_semantics=("arbitrary",), vmem_limit_bytes=V7X_VMEM_LIMIT),
        name="dispatch",
    )(runs, lpos, h2)


def _experts_kernel(tile_ref, exp_ref, starts_ref, xs_ref, wg_ref, wu_ref, wd_ref, ys_ref):
    i = pl.program_id(0)
    tm = xs_ref.shape[0]
    e = exp_ref[i]
    tile = tile_ref[i]
    is_first = jnp.logical_or(i == 0, tile_ref[jnp.maximum(i - 1, 0)] != tile)

    def masked_y():
        x = xs_ref[...].astype(BF16)
        act = (_silu(_dot(x, wg_ref[...])) * _dot(x, wu_ref[...])).astype(BF16)
        y = _dot(act, wd_ref[...])
        rows = tile * tm + lax.broadcasted_iota(jnp.int32, (tm, 1), 0)
        return jnp.where((rows >= starts_ref[e]) & (rows < starts_ref[e + 1]), y, 0.0)

    @pl.when(jnp.logical_and(e < N_EXPERTS, is_first))
    def _():
        ys_ref[...] = masked_y()

    @pl.when(jnp.logical_and(e < N_EXPERTS, jnp.logical_not(is_first)))
    def _():
        ys_ref[...] = ys_ref[...] + masked_y()

    @pl.when(e == N_EXPERTS)
    def _():
        ys_ref[...] = jnp.zeros_like(ys_ref)


def _experts_call(item_tile, item_exp, starts_ext, xs, wg, wu, wd, tm):
    n = xs.shape[0]
    n_items = item_tile.shape[0]
    w_idx = lambda i, tile, ex, st: (jnp.minimum(ex[i], N_EXPERTS - 1), 0, 0)
    grid_spec = pltpu.PrefetchScalarGridSpec(
        num_scalar_prefetch=3,
        grid=(n_items,),
        in_specs=[pl.BlockSpec((tm, D_MODEL), lambda i, tile, ex, st: (tile[i], 0)),
                  pl.BlockSpec((None, D_MODEL, D_EXPERT), w_idx),
                  pl.BlockSpec((None, D_MODEL, D_EXPERT), w_idx),
                  pl.BlockSpec((None, D_EXPERT, D_MODEL), w_idx)],
        out_specs=pl.BlockSpec((tm, D_MODEL), lambda i, tile, ex, st: (tile[i], 0)),
    )
    return pl.pallas_call(
        _experts_kernel,
        grid_spec=grid_spec,
        out_shape=jax.ShapeDtypeStruct((n, D_MODEL), F32),
        compiler_params=pltpu.CompilerParams(
            dimension_semantics=("arbitrary",), vmem_limit_bytes=V7X_VMEM_LIMIT),
        name="experts",
    )(item_tile, item_exp, starts_ext, xs, wg, wu, wd)


def _local_rows(tr):
    return TOP_K * tr + N_EXPERTS * ROW_ALIGN


def _expert_items(counts, n_rows, tm):
    n_tiles = n_rows // tm
    n_items = n_tiles + N_EXPERTS - 1
    starts = jnp.concatenate([jnp.zeros((1,), jnp.int32), jnp.cumsum(counts).astype(jnp.int32)])
    lo = starts[:-1] // tm
    hi = (starts[1:] - 1) // tm
    per = jnp.where(counts > 0, hi - lo + 1, 0)
    cum = jnp.cumsum(per)
    idx = jnp.arange(n_items, dtype=jnp.int32)
    e = jnp.sum(idx[:, None] >= cum[None, :], axis=1).astype(jnp.int32)
    valid = idx < cum[-1]
    e_c = jnp.minimum(e, N_EXPERTS - 1)
    tile = lo[e_c] + idx - (cum[e_c] - per[e_c])
    spare_tile = -(-starts[-1] // tm) + idx - cum[-1]
    item_tile = jnp.where(valid, tile, jnp.minimum(spare_tile, n_tiles - 1)).astype(jnp.int32)
    item_exp = jnp.where(valid, e_c, jnp.where(spare_tile < n_tiles, N_EXPERTS, N_EXPERTS + 1)).astype(jnp.int32)
    starts_ext = jnp.concatenate([starts, starts[-1:], starts[-1:]])
    return item_tile, item_exp, starts, starts_ext


def _combine_kernel(runs_ref, next_runs_ref, lpos_ref, gate_ref, x1_ref, nf_ref, ys_ref, out_ref, loc, sem):
    tr = x1_ref.shape[0]
    nloc = loc.shape[1]
    step = pl.program_id(0)
    slot = step % 2

    def gather(table, s):
        def copy_run(dst, src, n):
            pltpu.make_async_copy(ys_ref.at[pl.ds(dst, n)], loc.at[s, pl.ds(src, n)], sem.at[s]).start()
        _run_copies(table, copy_run, tr, 0)

    @pl.when(step == 0)
    def _():
        gather(runs_ref, slot)

    @pl.when(step + 1 < pl.num_programs(0))
    def _():
        gather(next_runs_ref, 1 - slot)

    jrow = lax.broadcasted_iota(jnp.int32, (nloc, tr), 0)
    sel = jnp.where(jrow == lpos_ref[0:1, :], gate_ref[0:1, :],
                    jnp.where(jrow == lpos_ref[1:2, :], gate_ref[1:2, :], 0.0)).astype(BF16)
    pltpu.make_async_copy(ys_ref.at[pl.ds(0, nloc)], loc.at[slot], sem.at[slot]).wait()
    y = _dot_tn(sel, loc[slot].astype(BF16))
    out_ref[...] = _rms(x1_ref[...] + y, nf_ref[...])


def _combine_call(runs, lpos, gates, x1, nf_w, ys, tr):
    t = x1.shape[0]
    return pl.pallas_call(
        _combine_kernel,
        grid=(t // tr,),
        in_specs=[pl.BlockSpec((None, 3, N_EXPERTS + 1), lambda i: (i, 0, 0), memory_space=pltpu.SMEM),
                  pl.BlockSpec((None, 3, N_EXPERTS + 1), lambda i: (jnp.minimum(i + 1, t // tr - 1), 0, 0),
                               memory_space=pltpu.SMEM),
                  pl.BlockSpec((8, tr), lambda i: (0, i)),
                  pl.BlockSpec((8, tr), lambda i: (0, i)),
                  pl.BlockSpec((tr, D_MODEL), lambda i: (i, 0)),
                  pl.BlockSpec((1, D_MODEL), lambda i: (0, 0)),
                  pl.BlockSpec(memory_space=pl.ANY)],
        out_specs=pl.BlockSpec((tr, D_MODEL), lambda i: (i, 0)),
        out_shape=jax.ShapeDtypeStruct((t, D_MODEL), F32),
        scratch_shapes=[pltpu.VMEM((2, _local_rows(tr), D_MODEL), F32), pltpu.SemaphoreType.DMA((2,))],
        compiler_params=pltpu.CompilerParams(dimension_semantics=("arbitrary",), vmem_limit_bytes=V7X_VMEM_LIMIT),
        name="combine",
    )(runs, runs, lpos, gates, x1, nf_w, ys)


SEQ_TILE = 4 * CHUNK
TOKEN_TILE = 512
ROW_TILE = 512


def kernel(x, norm1_w, w_in, ssd_conv_w, ssd_conv_b, ssd_dt_bias, ssd_a_log, ssd_d, ssd_norm_w, ml_conv_w, ml_conv_b, ml_i_bias, ml_f_bias, ml_norm_w, w_out, norm2_w, router_g_w, router_g_b, router_e_w, router_e_b, exp_w_gate, exp_w_up, exp_w_down, norm_f_w):
    bsz, seq, d = x.shape
    t = bsz * seq
    tl, tr, tm = SEQ_TILE, TOKEN_TILE, ROW_TILE
    assert d == D_MODEL and norm1_w.shape[0] == 1 and seq % tl == 0 and t % tr == 0

    consts = _mixer_consts(norm1_w[0], w_in[0], ssd_conv_w[0], ssd_conv_b[0], ssd_dt_bias[0], ssd_a_log[0],
                           ssd_d[0], ssd_norm_w[0], ml_conv_w[0], ml_conv_b[0], ml_i_bias[0], ml_f_bias[0],
                           ml_norm_w[0], w_out[0], norm2_w[0], router_g_w[0], router_e_w[0])
    x1, h2, lg_t = _mixer_call(x, consts, tl)
    x1 = x1.reshape(t, D_MODEL)
    h2 = h2.reshape(t, D_MODEL)

    bias_col = jnp.concatenate([router_e_b[0], router_g_b[0],
                                jnp.full((LOGIT_ROWS - N_EXPERTS - MOE_GROUPS,), STAB_INIT, F32)]).reshape(-1, 1)
    lpos, gates, cnt = _route_call(lg_t, bias_col.astype(F32), tr)

    seg = (cnt[:, :, 0].astype(jnp.int32) + (ROW_ALIGN - 1)) // ROW_ALIGN * ROW_ALIGN
    n_tok_tiles = t // tr
    n_rows = -(-(t * TOP_K + n_tok_tiles * N_EXPERTS * ROW_ALIGN + _local_rows(tr)) // tm) * tm
    item_tile, item_exp, starts, starts_ext = _expert_items(jnp.sum(seg, axis=0), n_rows, tm)
    run_dst = starts[None, :N_EXPERTS] + jnp.cumsum(seg, axis=0) - seg
    run_src = jnp.cumsum(seg, axis=1) - seg
    used = jnp.sum(seg, axis=1, keepdims=True)
    fill = jnp.concatenate([jnp.full_like(used, starts[N_EXPERTS]), used, _local_rows(tr) - used], axis=1)[:, :, None]
    runs = jnp.concatenate([jnp.stack([run_dst, run_src, seg], axis=1), fill], axis=2).astype(jnp.int32)

    xs = _dispatch_call(runs, lpos, h2, tr, n_rows)
    ys = _experts_call(item_tile, item_exp, starts_ext, xs, exp_w_gate[0].astype(BF16),
                       exp_w_up[0].astype(BF16), exp_w_down[0].astype(BF16), tm)
    out = _combine_call(runs, lpos, gates, x1, norm_f_w.reshape(1, -1).astype(F32), ys, tr)
    return out.reshape(bsz, seq, D_MODEL)
```

```python
import functools

import jax
import jax.numpy as jnp
import numpy as np
from jax import lax
from jax.experimental import pallas as pl
from jax.experimental.pallas import tpu as pltpu

D_MODEL = 1024
CHUNK = 64
SSD_WIDTH = 1024
SSD_HEAD_DIM = 64
SSD_HEADS = 16
SSD_GROUPS = 2
SSD_STATE = 128
SSD_CONV = 4
SSD_XBC = SSD_WIDTH + 2 * SSD_GROUPS * SSD_STATE
ML_WIDTH = 1024
ML_HEADS = 8
ML_HEAD_DIM = 128
ML_CONV = 4
D_MIX = SSD_WIDTH + ML_WIDTH
MOE_GROUPS = 4
EXPERTS_PER_GROUP = 8
N_EXPERTS = 32
TOP_K = 2
D_EXPERT = 512
EPS = 1e-6
STAB_INIT = -1e30

LANES = 128
GATE_COLS = 2 * LANES
ROW_ALIGN = 8
CONV_PAD = 8
LOGIT_ROWS = 64
V7X_VMEM_LIMIT = 56 * 1024 * 1024

F32 = jnp.float32
BF16 = jnp.bfloat16
NEG_INF = float("-inf")


def _dot(a, b, precision=None):
    return jnp.dot(a, b, preferred_element_type=F32, precision=precision)


def _dot_nt(a, b):
    return lax.dot_general(a, b, (((1,), (1,)), ((), ())), preferred_element_type=F32)


def _dot_tn(a, b):
    return lax.dot_general(a, b, (((0,), (0,)), ((), ())), preferred_element_type=F32)


def _split3(x, axis):
    hi = x.astype(BF16)
    r1 = x - hi.astype(F32)
    mid = r1.astype(BF16)
    lo = (r1 - mid.astype(F32)).astype(BF16)
    return jnp.concatenate([hi, mid, lo], axis=axis)


def _silu(x):
    return x * jax.nn.sigmoid(x)


def _softplus(x):
    return jnp.maximum(x, 0.0) + jnp.log1p(jnp.exp(-jnp.abs(x)))


def _rms(x, w):
    return x * lax.rsqrt(jnp.mean(x * x, axis=-1, keepdims=True) + EPS) * w


def _mixer_kernel(*refs, tl, nl):
    for parity in range(2):
        @pl.when(pl.program_id(0) % 2 == parity)
        def _(parity=parity):
            _mixer_step(*refs, tl=tl, nl=nl, slot_a=parity)


def _mixer_step(x_ref, xc_ref, n1_ref, wz_ref, wxbc_ref, wqk_ref, wv_ref, wo_ref, wg_ref,
                  scw_ref, scb_ref, dtb_ref, alog_ref, dskip_ref, snw_ref,
                  mcw_ref, mcb_ref, gb_ref, mnw_ref, wout_ref, n2_ref, wr_ref,
                  e16_ref, e2_ref, tril_ref, seq_ref, sge_ref, meq_ref, mge_ref,
                  x1_ref, h2_ref, lg_ref,
                  xbc_buf, qk_buf, z_buf, v_buf, o_buf, g_buf, xbc_c, qk_c, mix_buf,
                  xbc_tail, qk_tail, sst, mst, mm, h_buf, *, tl, nl, slot_a):
    nchunk = tl // CHUNK
    step = pl.program_id(0)
    slot_b = 1 - slot_a
    slot_c = slot_a

    @pl.when(step == 0)
    def _():
        xbc_buf[...] = jnp.zeros_like(xbc_buf)
        qk_buf[...] = jnp.zeros_like(qk_buf)
        z_buf[...] = jnp.zeros_like(z_buf)
        v_buf[...] = jnp.zeros_like(v_buf)
        o_buf[...] = jnp.zeros_like(o_buf)
        g_buf[...] = jnp.zeros_like(g_buf)
        mix_buf[...] = jnp.zeros_like(mix_buf)
        xbc_tail[...] = jnp.zeros_like(xbc_tail)
        qk_tail[...] = jnp.zeros_like(qk_tail)
        sst[...] = jnp.zeros_like(sst)
        mst[...] = jnp.zeros_like(mst)
        mm[...] = jnp.full(mm.shape, STAB_INIT, F32)

    h_buf[...] = _rms(x_ref[...], n1_ref[...]).astype(BF16)
    nblk = 2 * LANES
    tasks = []

    def proj_task(w_ref, store):
        for b0 in range(0, w_ref.shape[1], nblk):
            tasks.append(lambda b0=b0: store(slice(b0, b0 + nblk), _dot(h_buf[...], w_ref[:, b0:b0 + nblk])))

    def out_task(cols):
        x1_ref[:, cols] = xc_ref[:, cols] + _dot(mix_buf[slot_c], wout_ref[:, cols])

    def route_task():
        h2 = _rms(x1_ref[...], n2_ref[...])
        h2 = h2.astype(BF16)
        h2_ref[...] = h2
        lg_ref[...] = _dot_nt(wr_ref[...], h2)

    for b0 in range(0, D_MODEL, nblk):
        tasks.append(lambda b0=b0: out_task(slice(b0, b0 + nblk)))
    tasks.append(route_task)

    def put(buf, r0=0):
        def store(cols, val):
            buf[slot_a, r0:r0 + tl, cols] = val
        return store

    def put_tiles(buf):
        def store(cols, val):
            for k in range(nblk // LANES):
                buf[slot_a, cols.start // LANES + k, CONV_PAD:CONV_PAD + tl, :] = val[:, k * LANES:(k + 1) * LANES]
        return store

    proj_task(wz_ref, put(z_buf))
    proj_task(wg_ref, put(g_buf))
    proj_task(wxbc_ref, put_tiles(xbc_buf))
    proj_task(wqk_ref, put_tiles(qk_buf))
    proj_task(wv_ref, put(v_buf))
    proj_task(wo_ref, put(o_buf))
    per_chunk = -(-len(tasks) // nchunk)

    def run_task(c, k):
        idx = c * per_chunk + k
        if idx < len(tasks):
            tasks[idx]()

    seq_start = (step + nl - 1) % nl == 0
    xbc_buf[slot_b, :, 0:CONV_PAD, :] = jnp.where(seq_start, 0.0, xbc_tail[...])
    qk_buf[slot_b, :, 0:CONV_PAD, :] = jnp.where(seq_start, 0.0, qk_tail[...])
    sst[...] = jnp.where(seq_start, 0.0, sst[...])
    mst[...] = jnp.where(seq_start, 0.0, mst[...])
    mm[...] = jnp.where(seq_start, STAB_INIT, mm[...])

    first = CONV_PAD - (SSD_CONV - 1)
    xbc_tail[...] = xbc_buf[slot_b, :, tl:tl + CONV_PAD, :]
    qk_tail[...] = qk_buf[slot_b, :, tl:tl + CONV_PAD, :]

    def conv_silu(buf, w_ref, b_ref, out, r0):
        for t in range(buf.shape[1]):
            cols = slice(t * LANES, (t + 1) * LANES)
            acc = b_ref[:, cols] + w_ref[0:1, cols] * buf[slot_b, t, r0 + first:r0 + first + CHUNK, :]
            for j in range(1, w_ref.shape[0]):
                acc = acc + w_ref[j:j + 1, cols] * buf[slot_b, t, r0 + first + j:r0 + first + j + CHUNK, :]
            out[r0:r0 + CHUNK, cols] = _silu(acc)

    def convs(c):
        run_task(c, 0)
        conv_silu(xbc_buf, scw_ref, scb_ref, xbc_c, c * CHUNK)
        run_task(c, 1)
        conv_silu(qk_buf, mcw_ref, mcb_ref, qk_c, c * CHUNK)
        run_task(c, 2)

    convs(0)
    for c in range(nchunk):
        rows = slice(c * CHUNK, (c + 1) * CHUNK)
        tril3 = tril_ref[...]

        xs = xbc_c[rows, 0:SSD_WIDTH]
        bm = xbc_c[rows, SSD_WIDTH:SSD_WIDTH + 2 * SSD_STATE].astype(BF16)
        cm = xbc_c[rows, SSD_WIDTH + 2 * SSD_STATE:SSD_XBC].astype(BF16)
        dt = _softplus(g_buf[slot_b, rows, 0:SSD_HEADS] + dtb_ref[...])
        a_cs = _dot(tril3, _split3(dt * (-jnp.exp(alog_ref[...])), 0))
        dt_e = _dot(_split3(dt, 1), e16_ref[...])
        a_col = _dot(_split3(a_cs, 1), e16_ref[...])
        a_row = jnp.sum(jnp.where(seq_ref[...] > 0.5, a_col, 0.0), axis=0, keepdims=True)
        lmat = jnp.exp(jnp.where(sge_ref[...] > 0.5, a_col - a_row, NEG_INF))
        cb = jnp.concatenate(
            [_dot_nt(cm[:, g * SSD_STATE:(g + 1) * SSD_STATE],
                     jnp.concatenate([bm[:, g * SSD_STATE:(g + 1) * SSD_STATE]] * (SSD_HEADS // SSD_GROUPS), axis=0))
             for g in range(SSD_GROUPS)], axis=1)
        m_all = (cb * lmat).astype(BF16)
        run_task(c, 3)
        xdt = xs * dt_e
        lane = lax.broadcasted_iota(jnp.int32, (CHUNK, LANES), 1)
        y_parts = []
        for j in range(SSD_HEADS // 2):
            xp = xdt[:, j * LANES:(j + 1) * LANES]
            xbd = jnp.concatenate([jnp.where(lane < SSD_HEAD_DIM, xp, 0.0),
                                   jnp.where(lane >= SSD_HEAD_DIM, xp, 0.0)], axis=0).astype(BF16)
            y_parts.append(_dot(m_all[:, j * LANES:(j + 1) * LANES], xbd))
        y = jnp.concatenate(y_parts, axis=1)
        half = SSD_WIDTH // SSD_GROUPS
        y_int = jnp.concatenate(
            [_dot(cm[:, g * SSD_STATE:(g + 1) * SSD_STATE], sst[g].astype(BF16)) for g in range(SSD_GROUPS)], axis=1)
        y = y + y_int * jnp.exp(a_col)
        a_last = a_col[CHUNK - 1:CHUNK, :]
        xd_b = (xdt * jnp.exp(a_last - a_col)).astype(BF16)
        st_scale = jnp.exp(a_last)

        def ssd_update(bm=bm, xd_b=xd_b, st_scale=st_scale):
            for g in range(SSD_GROUPS):
                sst[g] = sst[g] * st_scale[:, g * half:(g + 1) * half] + _dot_tn(
                    bm[:, g * SSD_STATE:(g + 1) * SSD_STATE], xd_b[:, g * half:(g + 1) * half])
        y = (y + dskip_ref[...] * xs) * _silu(z_buf[slot_b, rows, :])
        for g in range(SSD_GROUPS):
            yg = y[:, g * half:(g + 1) * half]
            yg = yg * lax.rsqrt(jnp.mean(yg * yg, axis=-1, keepdims=True) + EPS) * snw_ref[:, g * half:(g + 1) * half]
            mix_buf[slot_b, rows, g * half:(g + 1) * half] = yg.astype(BF16)

        run_task(c, 4)

        gi = g_buf[slot_b, rows, LANES:LANES + 2 * ML_HEADS] + gb_ref[...]
        col16 = lax.broadcasted_iota(jnp.int32, (CHUNK, 2 * ML_HEADS), 1)
        gi = jnp.where(col16 < ML_HEADS, gi, jnp.minimum(gi, 0.0) - jnp.log1p(jnp.exp(-jnp.abs(gi))))
        gi = jnp.where(col16 < ML_HEADS, gi, _dot(tril3, _split3(gi, 0)))
        gi_e = _dot(_split3(gi, 1), e2_ref[...])
        ig_e = gi_e[:, 0:ML_WIDTH]
        b_col = gi_e[:, ML_WIDTH:2 * ML_WIDTH]
        c_row = jnp.sum(jnp.where(meq_ref[...] > 0.5, ig_e - b_col, 0.0), axis=0, keepdims=True)
        dmat = jnp.where(mge_ref[...] > 0.5, b_col + c_row, NEG_INF)
        heads = range(ML_HEADS)
        hs = lambda arr, hd: arr[:, hd * LANES:(hd + 1) * LANES]
        ones_blk = jnp.ones((CHUNK, LANES), BF16)
        m_prev = mm[...]
        m_inter = b_col + m_prev
        q = [qk_c[rows, hd * LANES:(hd + 1) * LANES].astype(BF16) for hd in heads]
        kf = [qk_c[rows, ML_WIDTH + hd * LANES:ML_WIDTH + (hd + 1) * LANES] * (ML_HEAD_DIM ** -0.5) for hd in heads]
        v_aug = [jnp.concatenate([v_buf[slot_b, rows, hd * LANES:(hd + 1) * LANES].astype(BF16), ones_blk], axis=1)
                 for hd in heads]
        s = [_dot_nt(q[hd], kf[hd].astype(BF16)) for hd in heads]
        qc = [_dot(q[hd], mst[hd].astype(BF16)) for hd in heads]
        run_task(c, 5)
        m_t = jnp.maximum(m_inter, jnp.concatenate(
            [jnp.broadcast_to(jnp.max(hs(dmat, hd), axis=-1, keepdims=True), (CHUNK, LANES)) for hd in heads], axis=1))
        w = jnp.exp(dmat - m_t)
        inter = jnp.exp(m_inter - m_t)
        den_floor = jnp.exp(-m_t)
        nd = [_dot((s[hd] * hs(w, hd)[:, 0:CHUNK]).astype(BF16), v_aug[hd])
              + qc[hd] * jnp.concatenate([hs(inter, hd)] * 2, axis=1) for hd in heads]
        run_task(c, 6)
        hh = [nd[hd][:, 0:LANES] / jnp.maximum(jnp.abs(nd[hd][:, LANES:2 * LANES]), hs(den_floor, hd)) for hd in heads]
        hc = [hh[hd] - jnp.mean(hh[hd], axis=-1, keepdims=True) for hd in heads]
        var = [jnp.mean(hc[hd] * hc[hd], axis=-1, keepdims=True) for hd in heads]
        for hd in heads:
            hn = hc[hd] * lax.rsqrt(var[hd] + EPS) * mnw_ref[:, hd * LANES:(hd + 1) * LANES]
            mix_buf[slot_b, rows, SSD_WIDTH + hd * LANES:SSD_WIDTH + (hd + 1) * LANES] = (
                jax.nn.sigmoid(o_buf[slot_b, rows, hd * LANES:(hd + 1) * LANES]) * hn).astype(BF16)
        run_task(c, 7)
        if c + 1 < nchunk:
            convs(c + 1)
        ssd_update()
        g_row = b_col[CHUNK - 1:CHUNK, :]
        a = g_row - b_col + ig_e
        m_new = jnp.maximum(g_row + m_prev, jnp.max(a, axis=0, keepdims=True))
        wk = jnp.exp(a - m_new)
        cs = jnp.exp(g_row + m_prev - m_new)
        for hd in heads:
            mst[hd] = mst[hd] * jnp.concatenate([hs(cs, hd)] * 2, axis=1) + _dot_tn(
                (kf[hd] * hs(wk, hd)).astype(BF16), v_aug[hd])
        mm[...] = m_new


def _const_spec(shape):
    nd = len(shape)
    return pl.BlockSpec(shape, lambda s: (0,) * nd, pipeline_mode=pl.Buffered(1))


def _mixer_call(x, consts, tl):
    bsz, seq, _ = x.shape
    nl = seq // tl
    n_tiles = bsz * nl
    x_tiles = x.reshape(n_tiles, tl, D_MODEL)
    tile_in = lambda s: (jnp.minimum(s, n_tiles - 1), 0, 0)
    tile_out = lambda s: (jnp.maximum(s - 2, 0), 0, 0)
    in_specs = ([pl.BlockSpec((None, tl, D_MODEL), tile_in), pl.BlockSpec((None, tl, D_MODEL), tile_out)]
                + [_const_spec(c.shape) for c in consts])
    out_specs = [pl.BlockSpec((None, tl, D_MODEL), tile_out), pl.BlockSpec((None, tl, D_MODEL), tile_out),
                 pl.BlockSpec((LOGIT_ROWS, tl), lambda s: (0, jnp.maximum(s - 2, 0)))]
    out_shape = [jax.ShapeDtypeStruct((n_tiles, tl, D_MODEL), F32),
                 jax.ShapeDtypeStruct((n_tiles, tl, D_MODEL), BF16),
                 jax.ShapeDtypeStruct((LOGIT_ROWS, bsz * seq), F32)]
    scratch = [
        pltpu.VMEM((2, SSD_XBC // LANES, tl + CONV_PAD, LANES), F32),
        pltpu.VMEM((2, 2 * ML_WIDTH // LANES, tl + CONV_PAD, LANES), F32),
        pltpu.VMEM((2, tl, SSD_WIDTH), F32),
        pltpu.VMEM((2, tl, ML_WIDTH), F32),
        pltpu.VMEM((2, tl, ML_WIDTH), F32),
        pltpu.VMEM((2, tl, GATE_COLS), F32),
        pltpu.VMEM((tl, SSD_XBC), F32),
        pltpu.VMEM((tl, 2 * ML_WIDTH), F32),
        pltpu.VMEM((2, tl, D_MIX), BF16),
        pltpu.VMEM((SSD_XBC // LANES, CONV_PAD, LANES), F32),
        pltpu.VMEM((2 * ML_WIDTH // LANES, CONV_PAD, LANES), F32),
        pltpu.VMEM((SSD_GROUPS, SSD_STATE, SSD_WIDTH // SSD_GROUPS), F32),
        pltpu.VMEM((ML_HEADS, ML_HEAD_DIM, 2 * LANES), F32),
        pltpu.VMEM((1, ML_WIDTH), F32),
        pltpu.VMEM((tl, D_MODEL), BF16),
    ]
    return pl.pallas_call(
        functools.partial(_mixer_kernel, tl=tl, nl=nl),
        grid=(n_tiles + 2,),
        in_specs=in_specs,
        out_specs=out_specs,
        out_shape=out_shape,
        scratch_shapes=scratch,
        compiler_params=pltpu.CompilerParams(
            dimension_semantics=("arbitrary",), vmem_limit_bytes=V7X_VMEM_LIMIT),
        name="mixer",
    )(x_tiles, x_tiles, *consts)


def _mixer_consts(norm1_w, w_in, ssd_conv_w, ssd_conv_b, ssd_dt_bias, ssd_a_log, ssd_d, ssd_norm_w,
                  ml_conv_w, ml_conv_b, ml_i_bias, ml_f_bias, ml_norm_w, w_out, norm2_w,
                  router_g_w, router_e_w):
    o1 = SSD_WIDTH
    o2 = o1 + SSD_XBC
    o3 = o2 + SSD_HEADS
    o4 = o3 + 2 * ML_WIDTH
    o5 = o4 + ML_WIDTH
    o6 = o5 + ML_WIDTH
    o7 = o6 + ML_HEADS
    zpad = lambda n: jnp.zeros((D_MODEL, n), F32)
    w_gate = jnp.concatenate([w_in[:, o2:o3], zpad(LANES - SSD_HEADS),
                              w_in[:, o6:o7], w_in[:, o7:], zpad(LANES - 2 * ML_HEADS)], axis=1)
    row = lambda v: v.reshape(1, -1).astype(F32)
    w_route = jnp.concatenate([router_e_w, router_g_w,
                               jnp.zeros((D_MODEL, LOGIT_ROWS - N_EXPERTS - MOE_GROUPS), F32)], axis=1).T

    c1024 = np.arange(SSD_WIDTH)
    r64 = np.arange(CHUNK)[:, None]
    e16 = np.tile(c1024[None, :] // SSD_HEAD_DIM == np.arange(SSD_HEADS)[:, None], (3, 1))
    e2 = np.tile(np.arange(2 * ML_WIDTH)[None, :] // LANES == np.arange(2 * ML_HEADS)[:, None], (3, 1))
    tril = np.tile(np.arange(CHUNK)[None, :] <= r64, (1, 3))
    s_pos = (c1024 % SSD_HEAD_DIM)[None, :]
    m_pos = (c1024 % LANES)[None, :]
    as_bf16 = lambda m: jnp.asarray(m.astype(np.float32), BF16)
    as_f32 = lambda m: jnp.asarray(m.astype(np.float32))
    e16, e2, tril = as_bf16(e16), as_bf16(e2), as_bf16(tril)
    seq, sge, meq, mge = as_f32(r64 == s_pos), as_f32(r64 >= s_pos), as_f32(r64 == m_pos), as_f32(r64 >= m_pos)
    return [
        row(norm1_w),
        w_in[:, :o1].astype(BF16), w_in[:, o1:o2].astype(BF16), w_in[:, o3:o4].astype(BF16),
        w_in[:, o4:o5].astype(BF16), w_in[:, o5:o6].astype(BF16), w_gate.astype(BF16),
        ssd_conv_w.astype(F32), row(ssd_conv_b), row(ssd_dt_bias),
        row(ssd_a_log), row(jnp.repeat(ssd_d, SSD_HEAD_DIM)), row(ssd_norm_w),
        ml_conv_w.astype(F32), row(ml_conv_b), row(jnp.concatenate([ml_i_bias, ml_f_bias])), row(ml_norm_w),
        w_out.astype(BF16), row(norm2_w), w_route.astype(BF16),
        e16, e2, tril, seq, sge, meq, mge,
    ]


def _route_kernel(lg_ref, bias_ref, striu_ref, lpos_ref, gate_ref, cnt_ref):
    tr = lg_ref.shape[1]
    lg = lg_ref[...] + bias_ref[...]
    el = lg[0:N_EXPERTS]
    gl = lg[N_EXPERTS:N_EXPERTS + 8]
    gmax = jnp.max(gl, axis=0, keepdims=True)
    pg = 1.0 / jnp.sum(jnp.exp(gl - gmax), axis=0, keepdims=True)
    grow = lax.broadcasted_iota(jnp.int32, gl.shape, 0)
    gsel = jnp.min(jnp.where(gl == gmax, grow, 8), axis=0, keepdims=True)
    erow = lax.broadcasted_iota(jnp.int32, el.shape, 0)
    m1 = jnp.where((erow >> 3) == gsel, el, NEG_INF)
    v1 = jnp.max(m1, axis=0, keepdims=True)
    i1 = jnp.min(jnp.where(m1 == v1, erow, N_EXPERTS), axis=0, keepdims=True)
    m2 = jnp.where(erow == i1, NEG_INF, m1)
    v2 = jnp.max(m2, axis=0, keepdims=True)
    i2 = jnp.min(jnp.where(m2 == v2, erow, N_EXPERTS), axis=0, keepdims=True)
    e2 = jnp.exp(v2 - v1)
    g1 = pg / (1.0 + e2)
    g2 = g1 * e2
    hit1 = erow == i1
    hit2 = erow == i2
    oh = jnp.where(hit1, 1.0, jnp.where(hit2, 1.0, 0.0))
    earlier = _dot(oh.astype(BF16), striu_ref[...])
    cnt = jnp.sum(oh, axis=1, keepdims=True)
    seg = jnp.floor((cnt + (ROW_ALIGN - 1)) * (1.0 / ROW_ALIGN)) * ROW_ALIGN
    l1 = jnp.sum(jnp.where(hit1, earlier, jnp.where(erow < i1, seg, 0.0)), axis=0, keepdims=True)
    l2 = jnp.sum(jnp.where(hit2, earlier, jnp.where(erow < i2, seg, 0.0)), axis=0, keepdims=True)
    lpos_ref[...] = jnp.concatenate([l1.astype(jnp.int32), l2.astype(jnp.int32), jnp.zeros((6, tr), jnp.int32)], axis=0)
    gate_ref[...] = jnp.concatenate([g1, g2, jnp.zeros((6, tr), F32)], axis=0)
    cnt_ref[...] = jnp.broadcast_to(cnt, cnt_ref.shape)


def _route_call(lg_t, bias_col, tr):
    t = lg_t.shape[1]
    striu = jnp.asarray(np.triu(np.ones((tr, tr), np.float32), 1), BF16)
    return pl.pallas_call(
        _route_kernel,
        grid=(t // tr,),
        in_specs=[pl.BlockSpec((LOGIT_ROWS, tr), lambda i: (0, i)),
                  pl.BlockSpec((LOGIT_ROWS, 1), lambda i: (0, 0)),
                  pl.BlockSpec((tr, tr), lambda i: (0, 0))],
        out_specs=[pl.BlockSpec((8, tr), lambda i: (0, i)),
                   pl.BlockSpec((8, tr), lambda i: (0, i)),
                   pl.BlockSpec((None, N_EXPERTS, LANES), lambda i: (i, 0, 0))],
        out_shape=[jax.ShapeDtypeStruct((8, t), jnp.int32),
                   jax.ShapeDtypeStruct((8, t), F32),
                   jax.ShapeDtypeStruct((t // tr, N_EXPERTS, LANES), F32)],
        compiler_params=pltpu.CompilerParams(dimension_semantics=("arbitrary",)),
        name="route",
    )(lg_t, bias_col, striu)


def _run_copies(runs_ref, copy_run, max_piece, filler_far):
    for e in range(N_EXPERTS + 1):
        dst = runs_ref[0, e] if e < N_EXPERTS else filler_far
        src = runs_ref[1, e]
        n = runs_ref[2, e]
        p = max_piece
        while p >= ROW_ALIGN:
            done = n & ~(2 * p - 1)

            @pl.when((n & p) != 0)
            def _(p=p, done=done, dst=dst, src=src):
                copy_run(pl.multiple_of(dst + done, ROW_ALIGN), pl.multiple_of(src + done, ROW_ALIGN), p)
            p //= 2


def _dispatch_kernel(runs_ref, lpos_ref, h2_ref, xs_ref, loc, sem, *, max_spare):
    tr = h2_ref.shape[0]
    nloc = loc.shape[1]
    step = pl.program_id(0)
    last = pl.num_programs(0) - 1
    slot = step % 2
    jrow = lax.broadcasted_iota(jnp.int32, (nloc, tr), 0)
    perm = jnp.where(jrow == lpos_ref[0:1, :], 1.0, jnp.where(jrow == lpos_ref[1:2, :], 1.0, 0.0)).astype(BF16)
    loc[slot] = _dot(perm, h2_ref[...])

    def copy_run(dst, src, n):
        pltpu.make_async_copy(loc.at[slot, pl.ds(src, n)], xs_ref.at[pl.ds(dst, n)], sem.at[slot]).start()

    def wait_block(s):
        pltpu.make_async_copy(loc.at[s], xs_ref.at[pl.ds(0, nloc)], sem.at[s]).wait()

    used = runs_ref[0, N_EXPERTS]
    _run_copies(runs_ref, copy_run, tr, used + slot * (nloc - TOP_K * tr))

    @pl.when(step > 0)
    def _():
        wait_block(1 - slot)

    @pl.when(step == last)
    def _():
        wait_block(slot)
        loc[slot, 0:tr, :] = jnp.zeros((tr, D_MODEL), F32)
        spare = xs_ref.shape[0] - used

        def zero_copy(off, n):
            return pltpu.make_async_copy(loc.at[slot, pl.ds(0, n)],
                                         xs_ref.at[pl.ds(pl.multiple_of(used + off, ROW_ALIGN), n)], sem.at[slot])

        def pieces(do):
            for i in range(max_spare // tr):
                @pl.when((i + 1) * tr <= spare)
                def _(i=i):
                    do(zero_copy(i * tr, tr))
            p = tr // 2
            while p >= ROW_ALIGN:
                @pl.when((spare & p) != 0)
                def _(p=p):
                    do(zero_copy(spare & ~(2 * p - 1), p))
                p //= 2

        pieces(lambda c: c.start())
        pieces(lambda c: c.wait())


def _dispatch_call(runs, lpos, h2, tr, n_rows):
    t = h2.shape[0]
    return pl.pallas_call(
        functools.partial(_dispatch_kernel, max_spare=n_rows - t * TOP_K),
        grid=(t // tr,),
        in_specs=[pl.BlockSpec((None, 3, N_EXPERTS + 1), lambda i: (i, 0, 0), memory_space=pltpu.SMEM),
                  pl.BlockSpec((8, tr), lambda i: (0, i)),
                  pl.BlockSpec((tr, D_MODEL), lambda i: (i, 0))],
        out_specs=pl.BlockSpec(memory_space=pl.ANY),
        out_shape=jax.ShapeDtypeStruct((n_rows, D_MODEL), F32),
        scratch_shapes=[pltpu.VMEM((2, _local_rows(tr), D_MODEL), F32), pltpu.SemaphoreType.DMA((2,))],
        compiler_params=pltpu.CompilerParams(dimension_semantics=("arbitrary",), vmem_limit_bytes=V7X_VMEM_LIMIT),
        name="dispatch",
    )(runs, lpos, h2)


def _experts_kernel(tile_ref, exp_ref, starts_ref, xs_ref, wg_ref, wu_ref, wd_ref, ys_ref):
    i = pl.program_id(0)
    tm = xs_ref.shape[0]
    e = exp_ref[i]
    tile = tile_ref[i]
    is_first = jnp.logical_or(i == 0, tile_ref[jnp.maximum(i - 1, 0)] != tile)

    def masked_y():
        x = xs_ref[...].astype(BF16)
        act = (_silu(_dot(x, wg_ref[...].astype(BF16))) * _dot(x, wu_ref[...].astype(BF16))).astype(BF16)
        y = _dot(act, wd_ref[...].astype(BF16))
        rows = tile * tm + lax.broadcasted_iota(jnp.int32, (tm, 1), 0)
        return jnp.where((rows >= starts_ref[e]) & (rows < starts_ref[e + 1]), y, 0.0)

    @pl.when(jnp.logical_and(e < N_EXPERTS, is_first))
    def _():
        ys_ref[...] = masked_y()

    @pl.when(jnp.logical_and(e < N_EXPERTS, jnp.logical_not(is_first)))
    def _():
        ys_ref[...] = ys_ref[...] + masked_y()

    @pl.when(e == N_EXPERTS)
    def _():
        ys_ref[...] = jnp.zeros_like(ys_ref)


def _experts_call(item_tile, item_exp, starts_ext, xs, wg, wu, wd, tm):
    n = xs.shape[0]
    n_items = item_tile.shape[0]
    w_idx = lambda i, tile, ex, st: (jnp.minimum(ex[i], N_EXPERTS - 1), 0, 0)
    grid_spec = pltpu.PrefetchScalarGridSpec(
        num_scalar_prefetch=3,
        grid=(n_items,),
        in_specs=[pl.BlockSpec((tm, D_MODEL), lambda i, tile, ex, st: (tile[i], 0)),
                  pl.BlockSpec((None, D_MODEL, D_EXPERT), w_idx),
                  pl.BlockSpec((None, D_MODEL, D_EXPERT), w_idx),
                  pl.BlockSpec((None, D_EXPERT, D_MODEL), w_idx)],
        out_specs=pl.BlockSpec((tm, D_MODEL), lambda i, tile, ex, st: (tile[i], 0)),
    )
    return pl.pallas_call(
        _experts_kernel,
        grid_spec=grid_spec,
        out_shape=jax.ShapeDtypeStruct((n, D_MODEL), F32),
        compiler_params=pltpu.CompilerParams(
            dimension_semantics=("arbitrary",), vmem_limit_bytes=V7X_VMEM_LIMIT),
        name="experts",
    )(item_tile, item_exp, starts_ext, xs, wg, wu, wd)


def _local_rows(tr):
    return TOP_K * tr + N_EXPERTS * ROW_ALIGN


def _expert_items(counts, n_rows, tm):
    n_tiles = n_rows // tm
    n_items = n_tiles + N_EXPERTS - 1
    starts = jnp.concatenate([jnp.zeros((1,), jnp.int32), jnp.cumsum(counts).astype(jnp.int32)])
    lo = starts[:-1] // tm
    hi = (starts[1:] - 1) // tm
    per = jnp.where(counts > 0, hi - lo + 1, 0)
    cum = jnp.cumsum(per)
    idx = jnp.arange(n_items, dtype=jnp.int32)
    e = jnp.sum(idx[:, None] >= cum[None, :], axis=1).astype(jnp.int32)
    valid = idx < cum[-1]
    e_c = jnp.minimum(e, N_EXPERTS - 1)
    tile = lo[e_c] + idx - (cum[e_c] - per[e_c])
    spare_tile = -(-starts[-1] // tm) + idx - cum[-1]
    item_tile = jnp.where(valid, tile, jnp.minimum(spare_tile, n_tiles - 1)).astype(jnp.int32)
    item_exp = jnp.where(valid, e_c, jnp.where(spare_tile < n_tiles, N_EXPERTS, N_EXPERTS + 1)).astype(jnp.int32)
    starts_ext = jnp.concatenate([starts, starts[-1:], starts[-1:]])
    return item_tile, item_exp, starts, starts_ext


def _combine_kernel(runs_ref, next_runs_ref, lpos_ref, gate_ref, x1_ref, nf_ref, ys_ref, out_ref, loc, sem):
    tr = x1_ref.shape[0]
    nloc = loc.shape[1]
    step = pl.program_id(0)
    slot = step % 2

    def gather(table, s):
        def copy_run(dst, src, n):
            pltpu.make_async_copy(ys_ref.at[pl.ds(dst, n)], loc.at[s, pl.ds(src, n)], sem.at[s]).start()
        _run_copies(table, copy_run, tr, 0)

    @pl.when(step == 0)
    def _():
        gather(runs_ref, slot)

    @pl.when(step + 1 < pl.num_programs(0))
    def _():
        gather(next_runs_ref, 1 - slot)

    jrow = lax.broadcasted_iota(jnp.int32, (nloc, tr), 0)
    sel = jnp.where(jrow == lpos_ref[0:1, :], gate_ref[0:1, :],
                    jnp.where(jrow == lpos_ref[1:2, :], gate_ref[1:2, :], 0.0)).astype(BF16)
    pltpu.make_async_copy(ys_ref.at[pl.ds(0, nloc)], loc.at[slot], sem.at[slot]).wait()
    y = _dot_tn(sel, loc[slot].astype(BF16))
    out_ref[...] = _rms(x1_ref[...] + y, nf_ref[...])


def _combine_call(runs, lpos, gates, x1, nf_w, ys, tr):
    t = x1.shape[0]
    return pl.pallas_call(
        _combine_kernel,
        grid=(t // tr,),
        in_specs=[pl.BlockSpec((None, 3, N_EXPERTS + 1), lambda i: (i, 0, 0), memory_space=pltpu.SMEM),
                  pl.BlockSpec((None, 3, N_EXPERTS + 1), lambda i: (jnp.minimum(i + 1, t // tr - 1), 0, 0),
                               memory_space=pltpu.SMEM),
                  pl.BlockSpec((8, tr), lambda i: (0, i)),
                  pl.BlockSpec((8, tr), lambda i: (0, i)),
                  pl.BlockSpec((tr, D_MODEL), lambda i: (i, 0)),
                  pl.BlockSpec((1, D_MODEL), lambda i: (0, 0)),
                  pl.BlockSpec(memory_space=pl.ANY)],
        out_specs=pl.BlockSpec((tr, D_MODEL), lambda i: (i, 0)),
        out_shape=jax.ShapeDtypeStruct((t, D_MODEL), F32),
        scratch_shapes=[pltpu.VMEM((2, _local_rows(tr), D_MODEL), F32), pltpu.SemaphoreType.DMA((2,))],
        compiler_params=pltpu.CompilerParams(dimension_semantics=("arbitrary",), vmem_limit_bytes=V7X_VMEM_LIMIT),
        name="combine",
    )(runs, runs, lpos, gates, x1, nf_w, ys)


SEQ_TILE = 4 * CHUNK
TOKEN_TILE = 512
ROW_TILE = 512


def kernel(x, norm1_w, w_in, ssd_conv_w, ssd_conv_b, ssd_dt_bias, ssd_a_log, ssd_d, ssd_norm_w, ml_conv_w, ml_conv_b, ml_i_bias, ml_f_bias, ml_norm_w, w_out, norm2_w, router_g_w, router_g_b, router_e_w, router_e_b, exp_w_gate, exp_w_up, exp_w_down, norm_f_w):
    bsz, seq, d = x.shape
    t = bsz * seq
    tl, tr, tm = SEQ_TILE, TOKEN_TILE, ROW_TILE
    assert d == D_MODEL and norm1_w.shape[0] == 1 and seq % tl == 0 and t % tr == 0

    consts = _mixer_consts(norm1_w[0], w_in[0], ssd_conv_w[0], ssd_conv_b[0], ssd_dt_bias[0], ssd_a_log[0],
                           ssd_d[0], ssd_norm_w[0], ml_conv_w[0], ml_conv_b[0], ml_i_bias[0], ml_f_bias[0],
                           ml_norm_w[0], w_out[0], norm2_w[0], router_g_w[0], router_e_w[0])
    x1, h2, lg_t = _mixer_call(x, consts, tl)
    x1 = x1.reshape(t, D_MODEL)
    h2 = h2.reshape(t, D_MODEL)

    bias_col = jnp.concatenate([router_e_b[0], router_g_b[0],
                                jnp.full((LOGIT_ROWS - N_EXPERTS - MOE_GROUPS,), STAB_INIT, F32)]).reshape(-1, 1)
    lpos, gates, cnt = _route_call(lg_t, bias_col.astype(F32), tr)

    seg = (cnt[:, :, 0].astype(jnp.int32) + (ROW_ALIGN - 1)) // ROW_ALIGN * ROW_ALIGN
    n_tok_tiles = t // tr
    n_rows = -(-(t * TOP_K + n_tok_tiles * N_EXPERTS * ROW_ALIGN + _local_rows(tr)) // tm) * tm
    item_tile, item_exp, starts, starts_ext = _expert_items(jnp.sum(seg, axis=0), n_rows, tm)
    run_dst = starts[None, :N_EXPERTS] + jnp.cumsum(seg, axis=0) - seg
    run_src = jnp.cumsum(seg, axis=1) - seg
    used = jnp.sum(seg, axis=1, keepdims=True)
    fill = jnp.concatenate([jnp.full_like(used, starts[N_EXPERTS]), used, _local_rows(tr) - used], axis=1)[:, :, None]
    runs = jnp.concatenate([jnp.stack([run_dst, run_src, seg], axis=1), fill], axis=2).astype(jnp.int32)

    xs = _dispatch_call(runs, lpos, h2, tr, n_rows)
    ys = _experts_call(item_tile, item_exp, starts_ext, xs, exp_w_gate[0], exp_w_up[0], exp_w_down[0], tm)
    out = _combine_call(runs, lpos, gates, x1, norm_f_w.reshape(1, -1).astype(F32), ys, tr)
    return out.reshape(bsz, seq, D_MODEL)
```

```python
import functools

import jax
import jax.numpy as jnp
import numpy as np
from jax import lax
from jax.experimental import pallas as pl
from jax.experimental.pallas import tpu as pltpu

D_MODEL = 1024
CHUNK = 64
SSD_WIDTH = 1024
SSD_HEAD_DIM = 64
SSD_HEADS = 16
SSD_GROUPS = 2
SSD_STATE = 128
SSD_CONV = 4
SSD_XBC = SSD_WIDTH + 2 * SSD_GROUPS * SSD_STATE
ML_WIDTH = 1024
ML_HEADS = 8
ML_HEAD_DIM = 128
ML_CONV = 4
D_MIX = SSD_WIDTH + ML_WIDTH
MOE_GROUPS = 4
EXPERTS_PER_GROUP = 8
N_EXPERTS = 32
TOP_K = 2
D_EXPERT = 512
EPS = 1e-6
STAB_INIT = -1e30

LANES = 128
GATE_COLS = 2 * LANES
ROW_ALIGN = 8
CONV_PAD = 8
LOGIT_ROWS = 64
V7X_VMEM_LIMIT = 56 * 1024 * 1024

F32 = jnp.float32
BF16 = jnp.bfloat16
NEG_INF = float("-inf")


def _dot(a, b, precision=None):
    return jnp.dot(a, b, preferred_element_type=F32, precision=precision)


def _dot_nt(a, b):
    return lax.dot_general(a, b, (((1,), (1,)), ((), ())), preferred_element_type=F32)


def _dot_tn(a, b):
    return lax.dot_general(a, b, (((0,), (0,)), ((), ())), preferred_element_type=F32)


def _split3(x, axis):
    hi = x.astype(BF16)
    r1 = x - hi.astype(F32)
    mid = r1.astype(BF16)
    lo = (r1 - mid.astype(F32)).astype(BF16)
    return jnp.concatenate([hi, mid, lo], axis=axis)


def _silu(x):
    return x * jax.nn.sigmoid(x)


def _softplus(x):
    return jnp.maximum(x, 0.0) + jnp.log1p(jnp.exp(-jnp.abs(x)))


def _rms(x, w):
    return x * lax.rsqrt(jnp.mean(x * x, axis=-1, keepdims=True) + EPS) * w


def _mixer_kernel(*refs, tl, nl):
    for parity in range(2):
        @pl.when(pl.program_id(0) % 2 == parity)
        def _(parity=parity):
            _mixer_step(*refs, tl=tl, nl=nl, slot_a=parity)


def _mixer_step(x_ref, xc_ref, n1_ref, wz_ref, wxbc_ref, wqk_ref, wv_ref, wo_ref, wg_ref,
                  scw_ref, scb_ref, dtb_ref, alog_ref, dskip_ref, snw_ref,
                  mcw_ref, mcb_ref, gb_ref, mnw_ref, wout_ref, n2_ref, wr_ref,
                  e16_ref, e2_ref, tril_ref, seq_ref, sge_ref, meq_ref, mge_ref,
                  x1_ref, h2_ref, lg_ref,
                  xbc_buf, qk_buf, z_buf, v_buf, o_buf, g_buf, xbc_c, qk_c, mix_buf,
                  xbc_tail, qk_tail, sst, mst, mm, h_buf, *, tl, nl, slot_a):
    nchunk = tl // CHUNK
    step = pl.program_id(0)
    slot_b = 1 - slot_a
    slot_c = slot_a

    @pl.when(step == 0)
    def _():
        xbc_buf[...] = jnp.zeros_like(xbc_buf)
        qk_buf[...] = jnp.zeros_like(qk_buf)
        z_buf[...] = jnp.zeros_like(z_buf)
        v_buf[...] = jnp.zeros_like(v_buf)
        o_buf[...] = jnp.zeros_like(o_buf)
        g_buf[...] = jnp.zeros_like(g_buf)
        mix_buf[...] = jnp.zeros_like(mix_buf)
        xbc_tail[...] = jnp.zeros_like(xbc_tail)
        qk_tail[...] = jnp.zeros_like(qk_tail)
        sst[...] = jnp.zeros_like(sst)
        mst[...] = jnp.zeros_like(mst)
        mm[...] = jnp.full(mm.shape, STAB_INIT, F32)

    h_buf[...] = _rms(x_ref[...], n1_ref[...]).astype(BF16)
    nblk = 2 * LANES
    tasks = []

    def proj_task(w_ref, store):
        for b0 in range(0, w_ref.shape[1], nblk):
            tasks.append(lambda b0=b0: store(slice(b0, b0 + nblk), _dot(h_buf[...], w_ref[:, b0:b0 + nblk])))

    def out_task(cols):
        x1_ref[:, cols] = xc_ref[:, cols] + _dot(mix_buf[slot_c], wout_ref[:, cols])

    def route_task():
        h2 = _rms(x1_ref[...], n2_ref[...])
        h2 = h2.astype(BF16)
        h2_ref[...] = h2
        lg_ref[...] = _dot_nt(wr_ref[...], h2)

    for b0 in range(0, D_MODEL, nblk):
        tasks.append(lambda b0=b0: out_task(slice(b0, b0 + nblk)))
    tasks.append(route_task)

    def put(buf, r0=0):
        def store(cols, val):
            buf[slot_a, r0:r0 + tl, cols] = val
        return store

    def put_tiles(buf):
        def store(cols, val):
            for k in range(nblk // LANES):
                buf[slot_a, cols.start // LANES + k, CONV_PAD:CONV_PAD + tl, :] = val[:, k * LANES:(k + 1) * LANES]
        return store

    proj_task(wz_ref, put(z_buf))
    proj_task(wg_ref, put(g_buf))
    proj_task(wxbc_ref, put_tiles(xbc_buf))
    proj_task(wqk_ref, put_tiles(qk_buf))
    proj_task(wv_ref, put(v_buf))
    proj_task(wo_ref, put(o_buf))
    per_chunk = -(-len(tasks) // nchunk)

    def run_task(c, k):
        idx = c * per_chunk + k
        if idx < len(tasks):
            tasks[idx]()

    seq_start = (step + nl - 1) % nl == 0
    xbc_buf[slot_b, :, 0:CONV_PAD, :] = jnp.where(seq_start, 0.0, xbc_tail[...])
    qk_buf[slot_b, :, 0:CONV_PAD, :] = jnp.where(seq_start, 0.0, qk_tail[...])
    sst[...] = jnp.where(seq_start, 0.0, sst[...])
    mst[...] = jnp.where(seq_start, 0.0, mst[...])
    mm[...] = jnp.where(seq_start, STAB_INIT, mm[...])

    first = CONV_PAD - (SSD_CONV - 1)
    xbc_tail[...] = xbc_buf[slot_b, :, tl:tl + CONV_PAD, :]
    qk_tail[...] = qk_buf[slot_b, :, tl:tl + CONV_PAD, :]

    def conv_silu(buf, w_ref, b_ref, out, r0):
        for t in range(buf.shape[1]):
            cols = slice(t * LANES, (t + 1) * LANES)
            acc = b_ref[:, cols] + w_ref[0:1, cols] * buf[slot_b, t, r0 + first:r0 + first + CHUNK, :]
            for j in range(1, w_ref.shape[0]):
                acc = acc + w_ref[j:j + 1, cols] * buf[slot_b, t, r0 + first + j:r0 + first + j + CHUNK, :]
            out[r0:r0 + CHUNK, cols] = _silu(acc)

    def convs(c):
        run_task(c, 0)
        conv_silu(xbc_buf, scw_ref, scb_ref, xbc_c, c * CHUNK)
        run_task(c, 1)
        conv_silu(qk_buf, mcw_ref, mcb_ref, qk_c, c * CHUNK)
        run_task(c, 2)

    for c in range(nchunk):
        convs(c)
        rows = slice(c * CHUNK, (c + 1) * CHUNK)
        tril3 = tril_ref[...]

        xs = xbc_c[rows, 0:SSD_WIDTH]
        bm = xbc_c[rows, SSD_WIDTH:SSD_WIDTH + 2 * SSD_STATE].astype(BF16)
        cm = xbc_c[rows, SSD_WIDTH + 2 * SSD_STATE:SSD_XBC].astype(BF16)
        dt = _softplus(g_buf[slot_b, rows, 0:SSD_HEADS] + dtb_ref[...])
        a_cs = _dot(tril3, _split3(dt * (-jnp.exp(alog_ref[...])), 0))
        dt_e = _dot(_split3(dt, 1), e16_ref[...])
        a_col = _dot(_split3(a_cs, 1), e16_ref[...])
        a_row = jnp.sum(jnp.where(seq_ref[...] > 0.5, a_col, 0.0), axis=0, keepdims=True)
        lmat = jnp.exp(jnp.where(sge_ref[...] > 0.5, a_col - a_row, NEG_INF))
        cb = jnp.concatenate(
            [_dot_nt(cm[:, g * SSD_STATE:(g + 1) * SSD_STATE],
                     jnp.concatenate([bm[:, g * SSD_STATE:(g + 1) * SSD_STATE]] * (SSD_HEADS // SSD_GROUPS), axis=0))
             for g in range(SSD_GROUPS)], axis=1)
        m_all = (cb * lmat).astype(BF16)
        run_task(c, 3)
        xdt = xs * dt_e
        lane = lax.broadcasted_iota(jnp.int32, (CHUNK, LANES), 1)
        y_parts = []
        for j in range(SSD_HEADS // 2):
            xp = xdt[:, j * LANES:(j + 1) * LANES]
            xbd = jnp.concatenate([jnp.where(lane < SSD_HEAD_DIM, xp, 0.0),
                                   jnp.where(lane >= SSD_HEAD_DIM, xp, 0.0)], axis=0).astype(BF16)
            y_parts.append(_dot(m_all[:, j * LANES:(j + 1) * LANES], xbd))
        y = jnp.concatenate(y_parts, axis=1)
        half = SSD_WIDTH // SSD_GROUPS
        y_int = jnp.concatenate(
            [_dot(cm[:, g * SSD_STATE:(g + 1) * SSD_STATE], sst[g].astype(BF16)) for g in range(SSD_GROUPS)], axis=1)
        y = y + y_int * jnp.exp(a_col)
        a_last = a_col[CHUNK - 1:CHUNK, :]
        xd_b = (xdt * jnp.exp(a_last - a_col)).astype(BF16)
        st_scale = jnp.exp(a_last)
        for g in range(SSD_GROUPS):
            sst[g] = sst[g] * st_scale[:, g * half:(g + 1) * half] + _dot_tn(
                bm[:, g * SSD_STATE:(g + 1) * SSD_STATE], xd_b[:, g * half:(g + 1) * half])
        y = (y + dskip_ref[...] * xs) * _silu(z_buf[slot_b, rows, :])
        for g in range(SSD_GROUPS):
            yg = y[:, g * half:(g + 1) * half]
            yg = yg * lax.rsqrt(jnp.mean(yg * yg, axis=-1, keepdims=True) + EPS) * snw_ref[:, g * half:(g + 1) * half]
            mix_buf[slot_b, rows, g * half:(g + 1) * half] = yg.astype(BF16)

        run_task(c, 4)

        gi = g_buf[slot_b, rows, LANES:LANES + 2 * ML_HEADS] + gb_ref[...]
        col16 = lax.broadcasted_iota(jnp.int32, (CHUNK, 2 * ML_HEADS), 1)
        gi = jnp.where(col16 < ML_HEADS, gi, jnp.minimum(gi, 0.0) - jnp.log1p(jnp.exp(-jnp.abs(gi))))
        gi = jnp.where(col16 < ML_HEADS, gi, _dot(tril3, _split3(gi, 0)))
        gi_e = _dot(_split3(gi, 1), e2_ref[...])
        ig_e = gi_e[:, 0:ML_WIDTH]
        b_col = gi_e[:, ML_WIDTH:2 * ML_WIDTH]
        c_row = jnp.sum(jnp.where(meq_ref[...] > 0.5, ig_e - b_col, 0.0), axis=0, keepdims=True)
        dmat = jnp.where(mge_ref[...] > 0.5, b_col + c_row, NEG_INF)
        heads = range(ML_HEADS)
        hs = lambda arr, hd: arr[:, hd * LANES:(hd + 1) * LANES]
        ones_blk = jnp.ones((CHUNK, LANES), BF16)
        m_prev = mm[...]
        m_inter = b_col + m_prev
        q = [qk_c[rows, hd * LANES:(hd + 1) * LANES].astype(BF16) for hd in heads]
        kf = [qk_c[rows, ML_WIDTH + hd * LANES:ML_WIDTH + (hd + 1) * LANES] * (ML_HEAD_DIM ** -0.5) for hd in heads]
        v_aug = [jnp.concatenate([v_buf[slot_b, rows, hd * LANES:(hd + 1) * LANES].astype(BF16), ones_blk], axis=1)
                 for hd in heads]
        s = [_dot_nt(q[hd], kf[hd].astype(BF16)) for hd in heads]
        qc = [_dot(q[hd], mst[hd].astype(BF16)) for hd in heads]
        run_task(c, 5)
        m_t = jnp.maximum(m_inter, jnp.concatenate(
            [jnp.broadcast_to(jnp.max(hs(dmat, hd), axis=-1, keepdims=True), (CHUNK, LANES)) for hd in heads], axis=1))
        w = jnp.exp(dmat - m_t)
        inter = jnp.exp(m_inter - m_t)
        den_floor = jnp.exp(-m_t)
        nd = [_dot((s[hd] * hs(w, hd)[:, 0:CHUNK]).astype(BF16), v_aug[hd])
              + qc[hd] * jnp.concatenate([hs(inter, hd)] * 2, axis=1) for hd in heads]
        run_task(c, 6)
        hh = [nd[hd][:, 0:LANES] / jnp.maximum(jnp.abs(nd[hd][:, LANES:2 * LANES]), hs(den_floor, hd)) for hd in heads]
        hc = [hh[hd] - jnp.mean(hh[hd], axis=-1, keepdims=True) for hd in heads]
        var = [jnp.mean(hc[hd] * hc[hd], axis=-1, keepdims=True) for hd in heads]
        for hd in heads:
            hn = hc[hd] * lax.rsqrt(var[hd] + EPS) * mnw_ref[:, hd * LANES:(hd + 1) * LANES]
            mix_buf[slot_b, rows, SSD_WIDTH + hd * LANES:SSD_WIDTH + (hd + 1) * LANES] = (
                jax.nn.sigmoid(o_buf[slot_b, rows, hd * LANES:(hd + 1) * LANES]) * hn).astype(BF16)
        run_task(c, 7)
        g_row = b_col[CHUNK - 1:CHUNK, :]
        a = g_row - b_col + ig_e
        m_new = jnp.maximum(g_row + m_prev, jnp.max(a, axis=0, keepdims=True))
        wk = jnp.exp(a - m_new)
        cs = jnp.exp(g_row + m_prev - m_new)
        for hd in heads:
            mst[hd] = mst[hd] * jnp.concatenate([hs(cs, hd)] * 2, axis=1) + _dot_tn(
                (kf[hd] * hs(wk, hd)).astype(BF16), v_aug[hd])
        mm[...] = m_new


def _const_spec(shape):
    nd = len(shape)
    return pl.BlockSpec(shape, lambda s: (0,) * nd, pipeline_mode=pl.Buffered(1))


def _mixer_call(x, consts, tl):
    bsz, seq, _ = x.shape
    nl = seq // tl
    n_tiles = bsz * nl
    x_tiles = x.reshape(n_tiles, tl, D_MODEL)
    tile_in = lambda s: (jnp.minimum(s, n_tiles - 1), 0, 0)
    tile_out = lambda s: (jnp.maximum(s - 2, 0), 0, 0)
    in_specs = ([pl.BlockSpec((None, tl, D_MODEL), tile_in), pl.BlockSpec((None, tl, D_MODEL), tile_out)]
                + [_const_spec(c.shape) for c in consts])
    out_specs = [pl.BlockSpec((None, tl, D_MODEL), tile_out), pl.BlockSpec((None, tl, D_MODEL), tile_out),
                 pl.BlockSpec((LOGIT_ROWS, tl), lambda s: (0, jnp.maximum(s - 2, 0)))]
    out_shape = [jax.ShapeDtypeStruct((n_tiles, tl, D_MODEL), F32),
                 jax.ShapeDtypeStruct((n_tiles, tl, D_MODEL), BF16),
                 jax.ShapeDtypeStruct((LOGIT_ROWS, bsz * seq), F32)]
    scratch = [
        pltpu.VMEM((2, SSD_XBC // LANES, tl + CONV_PAD, LANES), F32),
        pltpu.VMEM((2, 2 * ML_WIDTH // LANES, tl + CONV_PAD, LANES), F32),
        pltpu.VMEM((2, tl, SSD_WIDTH), F32),
        pltpu.VMEM((2, tl, ML_WIDTH), F32),
        pltpu.VMEM((2, tl, ML_WIDTH), F32),
        pltpu.VMEM((2, tl, GATE_COLS), F32),
        pltpu.VMEM((tl, SSD_XBC), F32),
        pltpu.VMEM((tl, 2 * ML_WIDTH), F32),
        pltpu.VMEM((2, tl, D_MIX), BF16),
        pltpu.VMEM((SSD_XBC // LANES, CONV_PAD, LANES), F32),
        pltpu.VMEM((2 * ML_WIDTH // LANES, CONV_PAD, LANES), F32),
        pltpu.VMEM((SSD_GROUPS, SSD_STATE, SSD_WIDTH // SSD_GROUPS), F32),
        pltpu.VMEM((ML_HEADS, ML_HEAD_DIM, 2 * LANES), F32),
        pltpu.VMEM((1, ML_WIDTH), F32),
        pltpu.VMEM((tl, D_MODEL), BF16),
    ]
    return pl.pallas_call(
        functools.partial(_mixer_kernel, tl=tl, nl=nl),
        grid=(n_tiles + 2,),
        in_specs=in_specs,
        out_specs=out_specs,
        out_shape=out_shape,
        scratch_shapes=scratch,
        compiler_params=pltpu.CompilerParams(
            dimension_semantics=("arbitrary",), vmem_limit_bytes=V7X_VMEM_LIMIT),
        name="mixer",
    )(x_tiles, x_tiles, *consts)


def _mixer_consts(norm1_w, w_in, ssd_conv_w, ssd_conv_b, ssd_dt_bias, ssd_a_log, ssd_d, ssd_norm_w,
                  ml_conv_w, ml_conv_b, ml_i_bias, ml_f_bias, ml_norm_w, w_out, norm2_w,
                  router_g_w, router_e_w):
    o1 = SSD_WIDTH
    o2 = o1 + SSD_XBC
    o3 = o2 + SSD_HEADS
    o4 = o3 + 2 * ML_WIDTH
    o5 = o4 + ML_WIDTH
    o6 = o5 + ML_WIDTH
    o7 = o6 + ML_HEADS
    zpad = lambda n: jnp.zeros((D_MODEL, n), F32)
    w_gate = jnp.concatenate([w_in[:, o2:o3], zpad(LANES - SSD_HEADS),
                              w_in[:, o6:o7], w_in[:, o7:], zpad(LANES - 2 * ML_HEADS)], axis=1)
    row = lambda v: v.reshape(1, -1).astype(F32)
    w_route = jnp.concatenate([router_e_w, router_g_w,
                               jnp.zeros((D_MODEL, LOGIT_ROWS - N_EXPERTS - MOE_GROUPS), F32)], axis=1).T

    c1024 = np.arange(SSD_WIDTH)
    r64 = np.arange(CHUNK)[:, None]
    e16 = np.tile(c1024[None, :] // SSD_HEAD_DIM == np.arange(SSD_HEADS)[:, None], (3, 1))
    e2 = np.tile(np.arange(2 * ML_WIDTH)[None, :] // LANES == np.arange(2 * ML_HEADS)[:, None], (3, 1))
    tril = np.tile(np.arange(CHUNK)[None, :] <= r64, (1, 3))
    s_pos = (c1024 % SSD_HEAD_DIM)[None, :]
    m_pos = (c1024 % LANES)[None, :]
    as_bf16 = lambda m: jnp.asarray(m.astype(np.float32), BF16)
    as_f32 = lambda m: jnp.asarray(m.astype(np.float32))
    e16, e2, tril = as_bf16(e16), as_bf16(e2), as_bf16(tril)
    seq, sge, meq, mge = as_f32(r64 == s_pos), as_f32(r64 >= s_pos), as_f32(r64 == m_pos), as_f32(r64 >= m_pos)
    return [
        row(norm1_w),
        w_in[:, :o1].astype(BF16), w_in[:, o1:o2].astype(BF16), w_in[:, o3:o4].astype(BF16),
        w_in[:, o4:o5].astype(BF16), w_in[:, o5:o6].astype(BF16), w_gate.astype(BF16),
        ssd_conv_w.astype(F32), row(ssd_conv_b), row(ssd_dt_bias),
        row(ssd_a_log), row(jnp.repeat(ssd_d, SSD_HEAD_DIM)), row(ssd_norm_w),
        ml_conv_w.astype(F32), row(ml_conv_b), row(jnp.concatenate([ml_i_bias, ml_f_bias])), row(ml_norm_w),
        w_out.astype(BF16), row(norm2_w), w_route.astype(BF16),
        e16, e2, tril, seq, sge, meq, mge,
    ]


def _route_kernel(lg_ref, bias_ref, striu_ref, lpos_ref, gate_ref, cnt_ref):
    tr = lg_ref.shape[1]
    lg = lg_ref[...] + bias_ref[...]
    el = lg[0:N_EXPERTS]
    gl = lg[N_EXPERTS:N_EXPERTS + 8]
    gmax = jnp.max(gl, axis=0, keepdims=True)
    pg = 1.0 / jnp.sum(jnp.exp(gl - gmax), axis=0, keepdims=True)
    grow = lax.broadcasted_iota(jnp.int32, gl.shape, 0)
    gsel = jnp.min(jnp.where(gl == gmax, grow, 8), axis=0, keepdims=True)
    erow = lax.broadcasted_iota(jnp.int32, el.shape, 0)
    m1 = jnp.where((erow >> 3) == gsel, el, NEG_INF)
    v1 = jnp.max(m1, axis=0, keepdims=True)
    i1 = jnp.min(jnp.where(m1 == v1, erow, N_EXPERTS), axis=0, keepdims=True)
    m2 = jnp.where(erow == i1, NEG_INF, m1)
    v2 = jnp.max(m2, axis=0, keepdims=True)
    i2 = jnp.min(jnp.where(m2 == v2, erow, N_EXPERTS), axis=0, keepdims=True)
    e2 = jnp.exp(v2 - v1)
    g1 = pg / (1.0 + e2)
    g2 = g1 * e2
    hit1 = erow == i1
    hit2 = erow == i2
    oh = jnp.where(hit1, 1.0, jnp.where(hit2, 1.0, 0.0))
    earlier = _dot(oh.astype(BF16), striu_ref[...])
    cnt = jnp.sum(oh, axis=1, keepdims=True)
    seg = jnp.floor((cnt + (ROW_ALIGN - 1)) * (1.0 / ROW_ALIGN)) * ROW_ALIGN
    l1 = jnp.sum(jnp.where(hit1, earlier, jnp.where(erow < i1, seg, 0.0)), axis=0, keepdims=True)
    l2 = jnp.sum(jnp.where(hit2, earlier, jnp.where(erow < i2, seg, 0.0)), axis=0, keepdims=True)
    lpos_ref[...] = jnp.concatenate([l1.astype(jnp.int32), l2.astype(jnp.int32), jnp.zeros((6, tr), jnp.int32)], axis=0)
    gate_ref[...] = jnp.concatenate([g1, g2, jnp.zeros((6, tr), F32)], axis=0)
    cnt_ref[...] = jnp.broadcast_to(cnt, cnt_ref.shape)


def _route_call(lg_t, bias_col, tr):
    t = lg_t.shape[1]
    striu = jnp.asarray(np.triu(np.ones((tr, tr), np.float32), 1), BF16)
    return pl.pallas_call(
        _route_kernel,
        grid=(t // tr,),
        in_specs=[pl.BlockSpec((LOGIT_ROWS, tr), lambda i: (0, i)),
                  pl.BlockSpec((LOGIT_ROWS, 1), lambda i: (0, 0)),
                  pl.BlockSpec((tr, tr), lambda i: (0, 0))],
        out_specs=[pl.BlockSpec((8, tr), lambda i: (0, i)),
                   pl.BlockSpec((8, tr), lambda i: (0, i)),
                   pl.BlockSpec((None, N_EXPERTS, LANES), lambda i: (i, 0, 0))],
        out_shape=[jax.ShapeDtypeStruct((8, t), jnp.int32),
                   jax.ShapeDtypeStruct((8, t), F32),
                   jax.ShapeDtypeStruct((t // tr, N_EXPERTS, LANES), F32)],
        compiler_params=pltpu.CompilerParams(dimension_semantics=("arbitrary",)),
        name="route",
    )(lg_t, bias_col, striu)


def _run_copies(runs_ref, copy_run, max_piece, filler_far):
    for e in range(N_EXPERTS + 1):
        dst = runs_ref[0, e] if e < N_EXPERTS else filler_far
        src = runs_ref[1, e]
        n = runs_ref[2, e]
        p = max_piece
        while p >= ROW_ALIGN:
            done = n & ~(2 * p - 1)

            @pl.when((n & p) != 0)
            def _(p=p, done=done, dst=dst, src=src):
                copy_run(pl.multiple_of(dst + done, ROW_ALIGN), pl.multiple_of(src + done, ROW_ALIGN), p)
            p //= 2


def _dispatch_kernel(runs_ref, prev_runs_ref, lpos_ref, h2_ref, xs_ref, loc, sem, *, max_spare):
    tr = h2_ref.shape[0]
    nloc = loc.shape[1]
    step = pl.program_id(0)
    last = pl.num_programs(0) - 1
    slot = step % 2
    used = runs_ref[0, N_EXPERTS]

    def wait_block(s):
        pltpu.make_async_copy(loc.at[s], xs_ref.at[pl.ds(0, nloc)], sem.at[s]).wait()

    def send_block(table, s):
        def copy_run(dst, src, n):
            pltpu.make_async_copy(loc.at[s, pl.ds(src, n)], xs_ref.at[pl.ds(dst, n)], sem.at[s]).start()
        _run_copies(table, copy_run, tr, used + s * (nloc - TOP_K * tr))

    @pl.when(step > 1)
    def _():
        wait_block(slot)

    @pl.when(step > 0)
    def _():
        send_block(prev_runs_ref, 1 - slot)

    jrow = lax.broadcasted_iota(jnp.int32, (nloc, tr), 0)
    perm = jnp.where(jrow == lpos_ref[0:1, :], 1.0, jnp.where(jrow == lpos_ref[1:2, :], 1.0, 0.0)).astype(BF16)
    loc[slot] = _dot(perm, h2_ref[...])

    @pl.when(step == last)
    def _():
        send_block(runs_ref, slot)

        @pl.when(step > 0)
        def _():
            wait_block(1 - slot)
        wait_block(slot)
        loc[slot, 0:tr, :] = jnp.zeros((tr, D_MODEL), F32)
        spare = xs_ref.shape[0] - used

        def zero_copy(off, n):
            return pltpu.make_async_copy(loc.at[slot, pl.ds(0, n)],
                                         xs_ref.at[pl.ds(pl.multiple_of(used + off, ROW_ALIGN), n)], sem.at[slot])

        def pieces(do):
            for i in range(max_spare // tr):
                @pl.when((i + 1) * tr <= spare)
                def _(i=i):
                    do(zero_copy(i * tr, tr))
            p = tr // 2
            while p >= ROW_ALIGN:
                @pl.when((spare & p) != 0)
                def _(p=p):
                    do(zero_copy(spare & ~(2 * p - 1), p))
                p //= 2

        pieces(lambda c: c.start())
        pieces(lambda c: c.wait())


def _dispatch_call(runs, lpos, h2, tr, n_rows):
    t = h2.shape[0]
    return pl.pallas_call(
        functools.partial(_dispatch_kernel, max_spare=n_rows - t * TOP_K),
        grid=(t // tr,),
        in_specs=[pl.BlockSpec((None, 3, N_EXPERTS + 1), lambda i: (i, 0, 0), memory_space=pltpu.SMEM),
                  pl.BlockSpec((None, 3, N_EXPERTS + 1), lambda i: (jnp.maximum(i - 1, 0), 0, 0), memory_space=pltpu.SMEM),
                  pl.BlockSpec((8, tr), lambda i: (0, i)),
                  pl.BlockSpec((tr, D_MODEL), lambda i: (i, 0))],
        out_specs=pl.BlockSpec(memory_space=pl.ANY),
        out_shape=jax.ShapeDtypeStruct((n_rows, D_MODEL), F32),
        scratch_shapes=[pltpu.VMEM((2, _local_rows(tr), D_MODEL), F32), pltpu.SemaphoreType.DMA((2,))],
        compiler_params=pltpu.CompilerParams(dimension_semantics=("arbitrary",), vmem_limit_bytes=V7X_VMEM_LIMIT),
        name="dispatch",
    )(runs, runs, lpos, h2)


def _experts_kernel(tile_ref, exp_ref, starts_ref, xs_ref, wg_ref, wu_ref, wd_ref, ys_ref):
    i = pl.program_id(0)
    tm = xs_ref.shape[0]
    e = exp_ref[i]
    tile = tile_ref[i]
    is_first = jnp.logical_or(i == 0, tile_ref[jnp.maximum(i - 1, 0)] != tile)

    def masked_y():
        x = xs_ref[...].astype(BF16)
        act = (_silu(_dot(x, wg_ref[...].astype(BF16))) * _dot(x, wu_ref[...].astype(BF16))).astype(BF16)
        y = _dot(act, wd_ref[...].astype(BF16))
        rows = tile * tm + lax.broadcasted_iota(jnp.int32, (tm, 1), 0)
        return jnp.where((rows >= starts_ref[e]) & (rows < starts_ref[e + 1]), y, 0.0)

    @pl.when(jnp.logical_and(e < N_EXPERTS, is_first))
    def _():
        ys_ref[...] = masked_y()

    @pl.when(jnp.logical_and(e < N_EXPERTS, jnp.logical_not(is_first)))
    def _():
        ys_ref[...] = ys_ref[...] + masked_y()

    @pl.when(e == N_EXPERTS)
    def _():
        ys_ref[...] = jnp.zeros_like(ys_ref)


def _experts_call(item_tile, item_exp, starts_ext, xs, wg, wu, wd, tm):
    n = xs.shape[0]
    n_items = item_tile.shape[0]
    w_idx = lambda i, tile, ex, st: (jnp.minimum(ex[i], N_EXPERTS - 1), 0, 0)
    grid_spec = pltpu.PrefetchScalarGridSpec(
        num_scalar_prefetch=3,
        grid=(n_items,),
        in_specs=[pl.BlockSpec((tm, D_MODEL), lambda i, tile, ex, st: (tile[i], 0)),
                  pl.BlockSpec((None, D_MODEL, D_EXPERT), w_idx),
                  pl.BlockSpec((None, D_MODEL, D_EXPERT), w_idx),
                  pl.BlockSpec((None, D_EXPERT, D_MODEL), w_idx)],
        out_specs=pl.BlockSpec((tm, D_MODEL), lambda i, tile, ex, st: (tile[i], 0)),
    )
    return pl.pallas_call(
        _experts_kernel,
        grid_spec=grid_spec,
        out_shape=jax.ShapeDtypeStruct((n, D_MODEL), F32),
        compiler_params=pltpu.CompilerParams(
            dimension_semantics=("arbitrary",), vmem_limit_bytes=V7X_VMEM_LIMIT),
        name="experts",
    )(item_tile, item_exp, starts_ext, xs, wg, wu, wd)


def _local_rows(tr):
    return TOP_K * tr + N_EXPERTS * ROW_ALIGN


def _expert_items(counts, n_rows, tm):
    n_tiles = n_rows // tm
    n_items = n_tiles + N_EXPERTS - 1
    starts = jnp.concatenate([jnp.zeros((1,), jnp.int32), jnp.cumsum(counts).astype(jnp.int32)])
    lo = starts[:-1] // tm
    hi = (starts[1:] - 1) // tm
    per = jnp.where(counts > 0, hi - lo + 1, 0)
    cum = jnp.cumsum(per)
    idx = jnp.arange(n_items, dtype=jnp.int32)
    e = jnp.sum(idx[:, None] >= cum[None, :], axis=1).astype(jnp.int32)
    valid = idx < cum[-1]
    e_c = jnp.minimum(e, N_EXPERTS - 1)
    tile = lo[e_c] + idx - (cum[e_c] - per[e_c])
    spare_tile = -(-starts[-1] // tm) + idx - cum[-1]
    item_tile = jnp.where(valid, tile, jnp.minimum(spare_tile, n_tiles - 1)).astype(jnp.int32)
    item_exp = jnp.where(valid, e_c, jnp.where(spare_tile < n_tiles, N_EXPERTS, N_EXPERTS + 1)).astype(jnp.int32)
    starts_ext = jnp.concatenate([starts, starts[-1:], starts[-1:]])
    return item_tile, item_exp, starts, starts_ext


def _combine_kernel(runs_ref, next_runs_ref, lpos_ref, gate_ref, x1_ref, nf_ref, ys_ref, out_ref, loc, sem):
    tr = x1_ref.shape[0]
    nloc = loc.shape[1]
    step = pl.program_id(0)
    slot = step % 2

    def gather(table, s):
        def copy_run(dst, src, n):
            pltpu.make_async_copy(ys_ref.at[pl.ds(dst, n)], loc.at[s, pl.ds(src, n)], sem.at[s]).start()
        _run_copies(table, copy_run, tr, 0)

    @pl.when(step == 0)
    def _():
        gather(runs_ref, slot)

    jrow = lax.broadcasted_iota(jnp.int32, (nloc, tr), 0)
    sel = jnp.where(jrow == lpos_ref[0:1, :], gate_ref[0:1, :],
                    jnp.where(jrow == lpos_ref[1:2, :], gate_ref[1:2, :], 0.0)).astype(BF16)
    pltpu.make_async_copy(ys_ref.at[pl.ds(0, nloc)], loc.at[slot], sem.at[slot]).wait()

    @pl.when(step + 1 < pl.num_programs(0))
    def _():
        gather(next_runs_ref, 1 - slot)

    y = _dot_tn(sel, loc[slot].astype(BF16))
    out_ref[...] = _rms(x1_ref[...] + y, nf_ref[...])


def _combine_call(runs, lpos, gates, x1, nf_w, ys, tr):
    t = x1.shape[0]
    return pl.pallas_call(
        _combine_kernel,
        grid=(t // tr,),
        in_specs=[pl.BlockSpec((None, 3, N_EXPERTS + 1), lambda i: (i, 0, 0), memory_space=pltpu.SMEM),
                  pl.BlockSpec((None, 3, N_EXPERTS + 1), lambda i: (jnp.minimum(i + 1, t // tr - 1), 0, 0),
                               memory_space=pltpu.SMEM),
                  pl.BlockSpec((8, tr), lambda i: (0, i)),
                  pl.BlockSpec((8, tr), lambda i: (0, i)),
                  pl.BlockSpec((tr, D_MODEL), lambda i: (i, 0)),
                  pl.BlockSpec((1, D_MODEL), lambda i: (0, 0)),
                  pl.BlockSpec(memory_space=pl.ANY)],
        out_specs=pl.BlockSpec((tr, D_MODEL), lambda i: (i, 0)),
        out_shape=jax.ShapeDtypeStruct((t, D_MODEL), F32),
        scratch_shapes=[pltpu.VMEM((2, _local_rows(tr), D_MODEL), F32), pltpu.SemaphoreType.DMA((2,))],
        compiler_params=pltpu.CompilerParams(dimension_semantics=("arbitrary",), vmem_limit_bytes=V7X_VMEM_LIMIT),
        name="combine",
    )(runs, runs, lpos, gates, x1, nf_w, ys)


SEQ_TILE = 4 * CHUNK
TOKEN_TILE = 512
ROW_TILE = 512


def kernel(x, norm1_w, w_in, ssd_conv_w, ssd_conv_b, ssd_dt_bias, ssd_a_log, ssd_d, ssd_norm_w, ml_conv_w, ml_conv_b, ml_i_bias, ml_f_bias, ml_norm_w, w_out, norm2_w, router_g_w, router_g_b, router_e_w, router_e_b, exp_w_gate, exp_w_up, exp_w_down, norm_f_w):
    bsz, seq, d = x.shape
    t = bsz * seq
    tl, tr, tm = SEQ_TILE, TOKEN_TILE, ROW_TILE
    assert d == D_MODEL and norm1_w.shape[0] == 1 and seq % tl == 0 and t % tr == 0

    consts = _mixer_consts(norm1_w[0], w_in[0], ssd_conv_w[0], ssd_conv_b[0], ssd_dt_bias[0], ssd_a_log[0],
                           ssd_d[0], ssd_norm_w[0], ml_conv_w[0], ml_conv_b[0], ml_i_bias[0], ml_f_bias[0],
                           ml_norm_w[0], w_out[0], norm2_w[0], router_g_w[0], router_e_w[0])
    x1, h2, lg_t = _mixer_call(x, consts, tl)
    x1 = x1.reshape(t, D_MODEL)
    h2 = h2.reshape(t, D_MODEL)

    bias_col = jnp.concatenate([router_e_b[0], router_g_b[0],
                                jnp.full((LOGIT_ROWS - N_EXPERTS - MOE_GROUPS,), STAB_INIT, F32)]).reshape(-1, 1)
    lpos, gates, cnt = _route_call(lg_t, bias_col.astype(F32), tr)

    seg = (cnt[:, :, 0].astype(jnp.int32) + (ROW_ALIGN - 1)) // ROW_ALIGN * ROW_ALIGN
    n_tok_tiles = t // tr
    n_rows = -(-(t * TOP_K + n_tok_tiles * N_EXPERTS * ROW_ALIGN + _local_rows(tr)) // tm) * tm
    item_tile, item_exp, starts, starts_ext = _expert_items(jnp.sum(seg, axis=0), n_rows, tm)
    run_dst = starts[None, :N_EXPERTS] + jnp.cumsum(seg, axis=0) - seg
    run_src = jnp.cumsum(seg, axis=1) - seg
    used = jnp.sum(seg, axis=1, keepdims=True)
    fill = jnp.concatenate([jnp.full_like(used, starts[N_EXPERTS]), used, _local_rows(tr) - used], axis=1)[:, :, None]
    runs = jnp.concatenate([jnp.stack([run_dst, run_src, seg], axis=1), fill], axis=2).astype(jnp.int32)

    xs = _dispatch_call(runs, lpos, h2, tr, n_rows)
    ys = _experts_call(item_tile, item_exp, starts_ext, xs, exp_w_gate[0], exp_w_up[0], exp_w_down[0], tm)
    out = _combine_call(runs, lpos, gates, x1, norm_f_w.reshape(1, -1).astype(F32), ys, tr)
    return out.reshape(bsz, seq, D_MODEL)
```

```python
import functools

import jax
import jax.numpy as jnp
import numpy as np
from jax import lax
from jax.experimental import pallas as pl
from jax.experimental.pallas import tpu as pltpu

D_MODEL = 1024
CHUNK = 64
SSD_WIDTH = 1024
SSD_HEAD_DIM = 64
SSD_HEADS = 16
SSD_GROUPS = 2
SSD_STATE = 128
SSD_CONV = 4
SSD_XBC = SSD_WIDTH + 2 * SSD_GROUPS * SSD_STATE
ML_WIDTH = 1024
ML_HEADS = 8
ML_HEAD_DIM = 128
ML_CONV = 4
D_MIX = SSD_WIDTH + ML_WIDTH
MOE_GROUPS = 4
EXPERTS_PER_GROUP = 8
N_EXPERTS = 32
TOP_K = 2
D_EXPERT = 512
EPS = 1e-6
STAB_INIT = -1e30

LANES = 128
GATE_COLS = 2 * LANES
ROW_ALIGN = 8
CONV_PAD = 8
LOGIT_ROWS = 64
V7X_VMEM_LIMIT = 56 * 1024 * 1024

F32 = jnp.float32
BF16 = jnp.bfloat16
NEG_INF = float("-inf")


def _dot(a, b, precision=None):
    return jnp.dot(a, b, preferred_element_type=F32, precision=precision)


def _dot_nt(a, b):
    return lax.dot_general(a, b, (((1,), (1,)), ((), ())), preferred_element_type=F32)


def _dot_tn(a, b):
    return lax.dot_general(a, b, (((0,), (0,)), ((), ())), preferred_element_type=F32)


def _split3(x, axis):
    hi = x.astype(BF16)
    r1 = x - hi.astype(F32)
    mid = r1.astype(BF16)
    lo = (r1 - mid.astype(F32)).astype(BF16)
    return jnp.concatenate([hi, mid, lo], axis=axis)


def _pack_rows(x, rounded):
    if not rounded:
        x = x.astype(BF16).astype(F32)
    bits = lax.bitcast_convert_type(x, jnp.uint32)
    half = x.shape[1] // 2
    return bits[:, half:] | (bits[:, :half] >> 16)


def _unpack_rows(p):
    lo = lax.bitcast_convert_type(p << 16, F32)
    hi = lax.bitcast_convert_type(p & jnp.uint32(0xFFFF0000), F32)
    return jnp.concatenate([lo, hi], axis=1).astype(BF16)


def _silu(x):
    return x * jax.nn.sigmoid(x)


def _softplus(x):
    return jnp.maximum(x, 0.0) + jnp.log1p(jnp.exp(-jnp.abs(x)))


def _rms(x, w):
    return x * lax.rsqrt(jnp.mean(x * x, axis=-1, keepdims=True) + EPS) * w


def _mixer_kernel(*refs, tl, nl):
    for parity in range(2):
        @pl.when(pl.program_id(0) % 2 == parity)
        def _(parity=parity):
            _mixer_step(*refs, tl=tl, nl=nl, slot_a=parity)


def _mixer_step(x_ref, xc_ref, n1_ref, wz_ref, wxbc_ref, wqk_ref, wv_ref, wo_ref, wg_ref,
                  scw_ref, scb_ref, dtb_ref, alog_ref, dskip_ref, snw_ref,
                  mcw_ref, mcb_ref, gb_ref, mnw_ref, wout_ref, n2_ref, wr_ref,
                  e16_ref, e2_ref, tril_ref, seq_ref, sge_ref, meq_ref, mge_ref,
                  x1_ref, h2_ref, lg_ref,
                  xbc_buf, qk_buf, z_buf, v_buf, o_buf, g_buf, xbc_c, qk_c, mix_buf,
                  xbc_tail, qk_tail, sst, mst, mm, h_buf, *, tl, nl, slot_a):
    nchunk = tl // CHUNK
    step = pl.program_id(0)
    slot_b = 1 - slot_a
    slot_c = slot_a

    @pl.when(step == 0)
    def _():
        xbc_buf[...] = jnp.zeros_like(xbc_buf)
        qk_buf[...] = jnp.zeros_like(qk_buf)
        z_buf[...] = jnp.zeros_like(z_buf)
        v_buf[...] = jnp.zeros_like(v_buf)
        o_buf[...] = jnp.zeros_like(o_buf)
        g_buf[...] = jnp.zeros_like(g_buf)
        mix_buf[...] = jnp.zeros_like(mix_buf)
        xbc_tail[...] = jnp.zeros_like(xbc_tail)
        qk_tail[...] = jnp.zeros_like(qk_tail)
        sst[...] = jnp.zeros_like(sst)
        mst[...] = jnp.zeros_like(mst)
        mm[...] = jnp.full(mm.shape, STAB_INIT, F32)

    h_buf[...] = _rms(x_ref[...], n1_ref[...]).astype(BF16)
    nblk = 2 * LANES
    tasks = []

    def proj_task(w_ref, store):
        for b0 in range(0, w_ref.shape[1], nblk):
            tasks.append(lambda b0=b0: store(slice(b0, b0 + nblk), _dot(h_buf[...], w_ref[:, b0:b0 + nblk])))

    def out_task(cols):
        x1_ref[:, cols] = xc_ref[:, cols] + _dot(mix_buf[slot_c], wout_ref[:, cols])

    def route_task():
        h2 = _rms(x1_ref[...], n2_ref[...])
        h2 = h2.astype(BF16)
        h2_ref[...] = h2
        lg_ref[...] = _dot_nt(wr_ref[...], h2)

    for b0 in range(0, D_MODEL, nblk):
        tasks.append(lambda b0=b0: out_task(slice(b0, b0 + nblk)))
    tasks.append(route_task)

    def put(buf, r0=0):
        def store(cols, val):
            buf[slot_a, r0:r0 + tl, cols] = val
        return store

    def put_tiles(buf):
        def store(cols, val):
            for k in range(nblk // LANES):
                buf[slot_a, cols.start // LANES + k, CONV_PAD:CONV_PAD + tl, :] = val[:, k * LANES:(k + 1) * LANES]
        return store

    proj_task(wz_ref, put(z_buf))
    proj_task(wg_ref, put(g_buf))
    proj_task(wxbc_ref, put_tiles(xbc_buf))
    proj_task(wqk_ref, put_tiles(qk_buf))
    proj_task(wv_ref, put(v_buf))
    proj_task(wo_ref, put(o_buf))
    per_chunk = -(-len(tasks) // nchunk)

    def run_task(c, k):
        idx = c * per_chunk + k
        if idx < len(tasks):
            tasks[idx]()

    seq_start = (step + nl - 1) % nl == 0
    xbc_buf[slot_b, :, 0:CONV_PAD, :] = jnp.where(seq_start, 0.0, xbc_tail[...])
    qk_buf[slot_b, :, 0:CONV_PAD, :] = jnp.where(seq_start, 0.0, qk_tail[...])
    sst[...] = jnp.where(seq_start, 0.0, sst[...])
    mst[...] = jnp.where(seq_start, 0.0, mst[...])
    mm[...] = jnp.where(seq_start, STAB_INIT, mm[...])

    first = CONV_PAD - (SSD_CONV - 1)
    xbc_tail[...] = xbc_buf[slot_b, :, tl:tl + CONV_PAD, :]
    qk_tail[...] = qk_buf[slot_b, :, tl:tl + CONV_PAD, :]

    def conv_silu(buf, w_ref, b_ref, out, r0):
        for t in range(buf.shape[1]):
            cols = slice(t * LANES, (t + 1) * LANES)
            acc = b_ref[:, cols] + w_ref[0:1, cols] * buf[slot_b, t, r0 + first:r0 + first + CHUNK, :]
            for j in range(1, w_ref.shape[0]):
                acc = acc + w_ref[j:j + 1, cols] * buf[slot_b, t, r0 + first + j:r0 + first + j + CHUNK, :]
            out[r0:r0 + CHUNK, cols] = _silu(acc)

    def convs(c):
        run_task(c, 0)
        conv_silu(xbc_buf, scw_ref, scb_ref, xbc_c, c * CHUNK)
        run_task(c, 1)
        conv_silu(qk_buf, mcw_ref, mcb_ref, qk_c, c * CHUNK)
        run_task(c, 2)

    for c in range(nchunk):
        convs(c)
        rows = slice(c * CHUNK, (c + 1) * CHUNK)
        tril3 = tril_ref[...]

        xs = xbc_c[rows, 0:SSD_WIDTH]
        bm = xbc_c[rows, SSD_WIDTH:SSD_WIDTH + 2 * SSD_STATE].astype(BF16)
        cm = xbc_c[rows, SSD_WIDTH + 2 * SSD_STATE:SSD_XBC].astype(BF16)
        dt = _softplus(g_buf[slot_b, rows, 0:SSD_HEADS] + dtb_ref[...])
        a_cs = _dot(tril3, _split3(dt * (-jnp.exp(alog_ref[...])), 0))
        dt_e = _dot(_split3(dt, 1), e16_ref[...])
        a_col = _dot(_split3(a_cs, 1), e16_ref[...])
        a_row = jnp.sum(jnp.where(seq_ref[...] > 0.5, a_col, 0.0), axis=0, keepdims=True)
        lmat = jnp.exp(jnp.where(sge_ref[...] > 0.5, a_col - a_row, NEG_INF))
        cb = jnp.concatenate(
            [_dot_nt(cm[:, g * SSD_STATE:(g + 1) * SSD_STATE],
                     jnp.concatenate([bm[:, g * SSD_STATE:(g + 1) * SSD_STATE]] * (SSD_HEADS // SSD_GROUPS), axis=0))
             for g in range(SSD_GROUPS)], axis=1)
        m_all = (cb * lmat).astype(BF16)
        run_task(c, 3)
        xdt = xs * dt_e
        lane = lax.broadcasted_iota(jnp.int32, (CHUNK, LANES), 1)
        y_parts = []
        for j in range(SSD_HEADS // 2):
            xp = xdt[:, j * LANES:(j + 1) * LANES]
            xbd = jnp.concatenate([jnp.where(lane < SSD_HEAD_DIM, xp, 0.0),
                                   jnp.where(lane >= SSD_HEAD_DIM, xp, 0.0)], axis=0).astype(BF16)
            y_parts.append(_dot(m_all[:, j * LANES:(j + 1) * LANES], xbd))
        y = jnp.concatenate(y_parts, axis=1)
        half = SSD_WIDTH // SSD_GROUPS
        y_int = jnp.concatenate(
            [_dot(cm[:, g * SSD_STATE:(g + 1) * SSD_STATE], sst[g].astype(BF16)) for g in range(SSD_GROUPS)], axis=1)
        y = y + y_int * jnp.exp(a_col)
        a_last = a_col[CHUNK - 1:CHUNK, :]
        xd_b = (xdt * jnp.exp(a_last - a_col)).astype(BF16)
        st_scale = jnp.exp(a_last)
        for g in range(SSD_GROUPS):
            sst[g] = sst[g] * st_scale[:, g * half:(g + 1) * half] + _dot_tn(
                bm[:, g * SSD_STATE:(g + 1) * SSD_STATE], xd_b[:, g * half:(g + 1) * half])
        y = (y + dskip_ref[...] * xs) * _silu(z_buf[slot_b, rows, :])
        for g in range(SSD_GROUPS):
            yg = y[:, g * half:(g + 1) * half]
            yg = yg * lax.rsqrt(jnp.mean(yg * yg, axis=-1, keepdims=True) + EPS) * snw_ref[:, g * half:(g + 1) * half]
            mix_buf[slot_b, rows, g * half:(g + 1) * half] = yg.astype(BF16)

        run_task(c, 4)

        gi = g_buf[slot_b, rows, LANES:LANES + 2 * ML_HEADS] + gb_ref[...]
        col16 = lax.broadcasted_iota(jnp.int32, (CHUNK, 2 * ML_HEADS), 1)
        gi = jnp.where(col16 < ML_HEADS, gi, jnp.minimum(gi, 0.0) - jnp.log1p(jnp.exp(-jnp.abs(gi))))
        gi = jnp.where(col16 < ML_HEADS, gi, _dot(tril3, _split3(gi, 0)))
        gi_e = _dot(_split3(gi, 1), e2_ref[...])
        ig_e = gi_e[:, 0:ML_WIDTH]
        b_col = gi_e[:, ML_WIDTH:2 * ML_WIDTH]
        c_row = jnp.sum(jnp.where(meq_ref[...] > 0.5, ig_e - b_col, 0.0), axis=0, keepdims=True)
        dmat = jnp.where(mge_ref[...] > 0.5, b_col + c_row, NEG_INF)
        heads = range(ML_HEADS)
        hs = lambda arr, hd: arr[:, hd * LANES:(hd + 1) * LANES]
        ones_blk = jnp.ones((CHUNK, LANES), BF16)
        m_prev = mm[...]
        m_inter = b_col + m_prev
        q = [qk_c[rows, hd * LANES:(hd + 1) * LANES].astype(BF16) for hd in heads]
        kf = [qk_c[rows, ML_WIDTH + hd * LANES:ML_WIDTH + (hd + 1) * LANES] * (ML_HEAD_DIM ** -0.5) for hd in heads]
        v_aug = [jnp.concatenate([v_buf[slot_b, rows, hd * LANES:(hd + 1) * LANES].astype(BF16), ones_blk], axis=1)
                 for hd in heads]
        s = [_dot_nt(q[hd], kf[hd].astype(BF16)) for hd in heads]
        qc = [_dot(q[hd], mst[hd].astype(BF16)) for hd in heads]
        run_task(c, 5)
        m_t = jnp.maximum(m_inter, jnp.concatenate(
            [jnp.broadcast_to(jnp.max(hs(dmat, hd), axis=-1, keepdims=True), (CHUNK, LANES)) for hd in heads], axis=1))
        w = jnp.exp(dmat - m_t)
        inter = jnp.exp(m_inter - m_t)
        den_floor = jnp.exp(-m_t)
        nd = [_dot((s[hd] * hs(w, hd)[:, 0:CHUNK]).astype(BF16), v_aug[hd])
              + qc[hd] * jnp.concatenate([hs(inter, hd)] * 2, axis=1) for hd in heads]
        run_task(c, 6)
        hh = [nd[hd][:, 0:LANES] / jnp.maximum(jnp.abs(nd[hd][:, LANES:2 * LANES]), hs(den_floor, hd)) for hd in heads]
        hc = [hh[hd] - jnp.mean(hh[hd], axis=-1, keepdims=True) for hd in heads]
        var = [jnp.mean(hc[hd] * hc[hd], axis=-1, keepdims=True) for hd in heads]
        for hd in heads:
            hn = hc[hd] * lax.rsqrt(var[hd] + EPS) * mnw_ref[:, hd * LANES:(hd + 1) * LANES]
            mix_buf[slot_b, rows, SSD_WIDTH + hd * LANES:SSD_WIDTH + (hd + 1) * LANES] = (
                jax.nn.sigmoid(o_buf[slot_b, rows, hd * LANES:(hd + 1) * LANES]) * hn).astype(BF16)
        run_task(c, 7)
        g_row = b_col[CHUNK - 1:CHUNK, :]
        a = g_row - b_col + ig_e
        m_new = jnp.maximum(g_row + m_prev, jnp.max(a, axis=0, keepdims=True))
        wk = jnp.exp(a - m_new)
        cs = jnp.exp(g_row + m_prev - m_new)
        for hd in heads:
            mst[hd] = mst[hd] * jnp.concatenate([hs(cs, hd)] * 2, axis=1) + _dot_tn(
                (kf[hd] * hs(wk, hd)).astype(BF16), v_aug[hd])
        mm[...] = m_new


def _const_spec(shape):
    nd = len(shape)
    return pl.BlockSpec(shape, lambda s: (0,) * nd, pipeline_mode=pl.Buffered(1))


def _mixer_call(x, consts, tl):
    bsz, seq, _ = x.shape
    nl = seq // tl
    n_tiles = bsz * nl
    x_tiles = x.reshape(n_tiles, tl, D_MODEL)
    tile_in = lambda s: (jnp.minimum(s, n_tiles - 1), 0, 0)
    tile_out = lambda s: (jnp.maximum(s - 2, 0), 0, 0)
    in_specs = ([pl.BlockSpec((None, tl, D_MODEL), tile_in), pl.BlockSpec((None, tl, D_MODEL), tile_out)]
                + [_const_spec(c.shape) for c in consts])
    out_specs = [pl.BlockSpec((None, tl, D_MODEL), tile_out), pl.BlockSpec((None, tl, D_MODEL), tile_out),
                 pl.BlockSpec((LOGIT_ROWS, tl), lambda s: (0, jnp.maximum(s - 2, 0)))]
    out_shape = [jax.ShapeDtypeStruct((n_tiles, tl, D_MODEL), F32),
                 jax.ShapeDtypeStruct((n_tiles, tl, D_MODEL), BF16),
                 jax.ShapeDtypeStruct((LOGIT_ROWS, bsz * seq), F32)]
    scratch = [
        pltpu.VMEM((2, SSD_XBC // LANES, tl + CONV_PAD, LANES), F32),
        pltpu.VMEM((2, 2 * ML_WIDTH // LANES, tl + CONV_PAD, LANES), F32),
        pltpu.VMEM((2, tl, SSD_WIDTH), F32),
        pltpu.VMEM((2, tl, ML_WIDTH), F32),
        pltpu.VMEM((2, tl, ML_WIDTH), F32),
        pltpu.VMEM((2, tl, GATE_COLS), F32),
        pltpu.VMEM((tl, SSD_XBC), F32),
        pltpu.VMEM((tl, 2 * ML_WIDTH), F32),
        pltpu.VMEM((2, tl, D_MIX), BF16),
        pltpu.VMEM((SSD_XBC // LANES, CONV_PAD, LANES), F32),
        pltpu.VMEM((2 * ML_WIDTH // LANES, CONV_PAD, LANES), F32),
        pltpu.VMEM((SSD_GROUPS, SSD_STATE, SSD_WIDTH // SSD_GROUPS), F32),
        pltpu.VMEM((ML_HEADS, ML_HEAD_DIM, 2 * LANES), F32),
        pltpu.VMEM((1, ML_WIDTH), F32),
        pltpu.VMEM((tl, D_MODEL), BF16),
    ]
    return pl.pallas_call(
        functools.partial(_mixer_kernel, tl=tl, nl=nl),
        grid=(n_tiles + 2,),
        in_specs=in_specs,
        out_specs=out_specs,
        out_shape=out_shape,
        scratch_shapes=scratch,
        compiler_params=pltpu.CompilerParams(
            dimension_semantics=("arbitrary",), vmem_limit_bytes=V7X_VMEM_LIMIT),
        name="mixer",
    )(x_tiles, x_tiles, *consts)


def _mixer_consts(norm1_w, w_in, ssd_conv_w, ssd_conv_b, ssd_dt_bias, ssd_a_log, ssd_d, ssd_norm_w,
                  ml_conv_w, ml_conv_b, ml_i_bias, ml_f_bias, ml_norm_w, w_out, norm2_w,
                  router_g_w, router_e_w):
    o1 = SSD_WIDTH
    o2 = o1 + SSD_XBC
    o3 = o2 + SSD_HEADS
    o4 = o3 + 2 * ML_WIDTH
    o5 = o4 + ML_WIDTH
    o6 = o5 + ML_WIDTH
    o7 = o6 + ML_HEADS
    zpad = lambda n: jnp.zeros((D_MODEL, n), F32)
    w_gate = jnp.concatenate([w_in[:, o2:o3], zpad(LANES - SSD_HEADS),
                              w_in[:, o6:o7], w_in[:, o7:], zpad(LANES - 2 * ML_HEADS)], axis=1)
    row = lambda v: v.reshape(1, -1).astype(F32)
    w_route = jnp.concatenate([router_e_w, router_g_w,
                               jnp.zeros((D_MODEL, LOGIT_ROWS - N_EXPERTS - MOE_GROUPS), F32)], axis=1).T

    c1024 = np.arange(SSD_WIDTH)
    r64 = np.arange(CHUNK)[:, None]
    e16 = np.tile(c1024[None, :] // SSD_HEAD_DIM == np.arange(SSD_HEADS)[:, None], (3, 1))
    e2 = np.tile(np.arange(2 * ML_WIDTH)[None, :] // LANES == np.arange(2 * ML_HEADS)[:, None], (3, 1))
    tril = np.tile(np.arange(CHUNK)[None, :] <= r64, (1, 3))
    s_pos = (c1024 % SSD_HEAD_DIM)[None, :]
    m_pos = (c1024 % LANES)[None, :]
    as_bf16 = lambda m: jnp.asarray(m.astype(np.float32), BF16)
    as_f32 = lambda m: jnp.asarray(m.astype(np.float32))
    e16, e2, tril = as_bf16(e16), as_bf16(e2), as_bf16(tril)
    seq, sge, meq, mge = as_f32(r64 == s_pos), as_f32(r64 >= s_pos), as_f32(r64 == m_pos), as_f32(r64 >= m_pos)
    return [
        row(norm1_w),
        w_in[:, :o1].astype(BF16), w_in[:, o1:o2].astype(BF16), w_in[:, o3:o4].astype(BF16),
        w_in[:, o4:o5].astype(BF16), w_in[:, o5:o6].astype(BF16), w_gate.astype(BF16),
        ssd_conv_w.astype(F32), row(ssd_conv_b), row(ssd_dt_bias),
        row(ssd_a_log), row(jnp.repeat(ssd_d, SSD_HEAD_DIM)), row(ssd_norm_w),
        ml_conv_w.astype(F32), row(ml_conv_b), row(jnp.concatenate([ml_i_bias, ml_f_bias])), row(ml_norm_w),
        w_out.astype(BF16), row(norm2_w), w_route.astype(BF16),
        e16, e2, tril, seq, sge, meq, mge,
    ]


def _route_kernel(lg_ref, bias_ref, striu_ref, lpos_ref, gate_ref, cnt_ref):
    tr = lg_ref.shape[1]
    lg = lg_ref[...] + bias_ref[...]
    el = lg[0:N_EXPERTS]
    gl = lg[N_EXPERTS:N_EXPERTS + 8]
    gmax = jnp.max(gl, axis=0, keepdims=True)
    pg = 1.0 / jnp.sum(jnp.exp(gl - gmax), axis=0, keepdims=True)
    grow = lax.broadcasted_iota(jnp.int32, gl.shape, 0)
    gsel = jnp.min(jnp.where(gl == gmax, grow, 8), axis=0, keepdims=True)
    erow = lax.broadcasted_iota(jnp.int32, el.shape, 0)
    m1 = jnp.where((erow >> 3) == gsel, el, NEG_INF)
    v1 = jnp.max(m1, axis=0, keepdims=True)
    i1 = jnp.min(jnp.where(m1 == v1, erow, N_EXPERTS), axis=0, keepdims=True)
    m2 = jnp.where(erow == i1, NEG_INF, m1)
    v2 = jnp.max(m2, axis=0, keepdims=True)
    i2 = jnp.min(jnp.where(m2 == v2, erow, N_EXPERTS), axis=0, keepdims=True)
    e2 = jnp.exp(v2 - v1)
    g1 = pg / (1.0 + e2)
    g2 = g1 * e2
    hit1 = erow == i1
    hit2 = erow == i2
    oh = jnp.where(hit1, 1.0, jnp.where(hit2, 1.0, 0.0))
    earlier = _dot(oh.astype(BF16), striu_ref[...])
    cnt = jnp.sum(oh, axis=1, keepdims=True)
    seg = jnp.floor((cnt + (ROW_ALIGN - 1)) * (1.0 / ROW_ALIGN)) * ROW_ALIGN
    l1 = jnp.sum(jnp.where(hit1, earlier, jnp.where(erow < i1, seg, 0.0)), axis=0, keepdims=True)
    l2 = jnp.sum(jnp.where(hit2, earlier, jnp.where(erow < i2, seg, 0.0)), axis=0, keepdims=True)
    lpos_ref[...] = jnp.concatenate([l1.astype(jnp.int32), l2.astype(jnp.int32), jnp.zeros((6, tr), jnp.int32)], axis=0)
    gate_ref[...] = jnp.concatenate([g1, g2, jnp.zeros((6, tr), F32)], axis=0)
    cnt_ref[...] = jnp.broadcast_to(cnt, cnt_ref.shape)


def _route_call(lg_t, bias_col, tr):
    t = lg_t.shape[1]
    striu = jnp.asarray(np.triu(np.ones((tr, tr), np.float32), 1), BF16)
    return pl.pallas_call(
        _route_kernel,
        grid=(t // tr,),
        in_specs=[pl.BlockSpec((LOGIT_ROWS, tr), lambda i: (0, i)),
                  pl.BlockSpec((LOGIT_ROWS, 1), lambda i: (0, 0)),
                  pl.BlockSpec((tr, tr), lambda i: (0, 0))],
        out_specs=[pl.BlockSpec((8, tr), lambda i: (0, i)),
                   pl.BlockSpec((8, tr), lambda i: (0, i)),
                   pl.BlockSpec((None, N_EXPERTS, LANES), lambda i: (i, 0, 0))],
        out_shape=[jax.ShapeDtypeStruct((8, t), jnp.int32),
                   jax.ShapeDtypeStruct((8, t), F32),
                   jax.ShapeDtypeStruct((t // tr, N_EXPERTS, LANES), F32)],
        compiler_params=pltpu.CompilerParams(dimension_semantics=("arbitrary",)),
        name="route",
    )(lg_t, bias_col, striu)


def _run_copies(runs_ref, copy_run, max_piece, filler_far):
    for e in range(N_EXPERTS + 1):
        dst = runs_ref[0, e] if e < N_EXPERTS else filler_far
        src = runs_ref[1, e]
        n = runs_ref[2, e]
        p = max_piece
        while p >= ROW_ALIGN:
            done = n & ~(2 * p - 1)

            @pl.when((n & p) != 0)
            def _(p=p, done=done, dst=dst, src=src):
                copy_run(pl.multiple_of(dst + done, ROW_ALIGN), pl.multiple_of(src + done, ROW_ALIGN), p)
            p //= 2


def _dispatch_kernel(runs_ref, prev_runs_ref, lpos_ref, h2_ref, xs_ref, loc, sem, *, max_spare):
    tr = h2_ref.shape[0]
    nloc = loc.shape[1]
    step = pl.program_id(0)
    last = pl.num_programs(0) - 1
    slot = step % 2
    used = runs_ref[0, N_EXPERTS]

    def wait_block(s):
        pltpu.make_async_copy(loc.at[s], xs_ref.at[pl.ds(0, nloc)], sem.at[s]).wait()

    def send_block(table, s):
        def copy_run(dst, src, n):
            pltpu.make_async_copy(loc.at[s, pl.ds(src, n)], xs_ref.at[pl.ds(dst, n)], sem.at[s]).start()
        _run_copies(table, copy_run, tr, used + s * (nloc - TOP_K * tr))

    @pl.when(step > 1)
    def _():
        wait_block(slot)

    @pl.when(step > 0)
    def _():
        send_block(prev_runs_ref, 1 - slot)

    jrow = lax.broadcasted_iota(jnp.int32, (nloc, tr), 0)
    perm = jnp.where(jrow == lpos_ref[0:1, :], 1.0, jnp.where(jrow == lpos_ref[1:2, :], 1.0, 0.0)).astype(BF16)
    loc[slot] = _pack_rows(_dot(perm, h2_ref[...]), rounded=True)

    @pl.when(step == last)
    def _():
        send_block(runs_ref, slot)

        @pl.when(step > 0)
        def _():
            wait_block(1 - slot)
        wait_block(slot)
        loc[slot, 0:tr, :] = jnp.zeros((tr, loc.shape[2]), loc.dtype)
        spare = xs_ref.shape[0] - used

        def zero_copy(off, n):
            return pltpu.make_async_copy(loc.at[slot, pl.ds(0, n)],
                                         xs_ref.at[pl.ds(pl.multiple_of(used + off, ROW_ALIGN), n)], sem.at[slot])

        def pieces(do):
            for i in range(max_spare // tr):
                @pl.when((i + 1) * tr <= spare)
                def _(i=i):
                    do(zero_copy(i * tr, tr))
            p = tr // 2
            while p >= ROW_ALIGN:
                @pl.when((spare & p) != 0)
                def _(p=p):
                    do(zero_copy(spare & ~(2 * p - 1), p))
                p //= 2

        pieces(lambda c: c.start())
        pieces(lambda c: c.wait())


def _dispatch_call(runs, lpos, h2, tr, n_rows):
    t = h2.shape[0]
    return pl.pallas_call(
        functools.partial(_dispatch_kernel, max_spare=n_rows - t * TOP_K),
        grid=(t // tr,),
        in_specs=[pl.BlockSpec((None, 3, N_EXPERTS + 1), lambda i: (i, 0, 0), memory_space=pltpu.SMEM),
                  pl.BlockSpec((None, 3, N_EXPERTS + 1), lambda i: (jnp.maximum(i - 1, 0), 0, 0), memory_space=pltpu.SMEM),
                  pl.BlockSpec((8, tr), lambda i: (0, i)),
                  pl.BlockSpec((tr, D_MODEL), lambda i: (i, 0))],
        out_specs=pl.BlockSpec(memory_space=pl.ANY),
        out_shape=jax.ShapeDtypeStruct((n_rows, D_MODEL // 2), jnp.uint32),
        scratch_shapes=[pltpu.VMEM((2, _local_rows(tr), D_MODEL // 2), jnp.uint32), pltpu.SemaphoreType.DMA((2,))],
        compiler_params=pltpu.CompilerParams(dimension_semantics=("arbitrary",), vmem_limit_bytes=V7X_VMEM_LIMIT),
        name="dispatch",
    )(runs, runs, lpos, h2)


def _experts_kernel(tile_ref, exp_ref, starts_ref, xs_ref, wg_ref, wu_ref, wd_ref, ys_ref):
    i = pl.program_id(0)
    tm = xs_ref.shape[0]
    e = exp_ref[i]
    tile = tile_ref[i]
    is_first = jnp.logical_or(i == 0, tile_ref[jnp.maximum(i - 1, 0)] != tile)

    def expert_rows(other):
        x = _unpack_rows(xs_ref[...])
        act = (_silu(_dot(x, wg_ref[...].astype(BF16))) * _dot(x, wu_ref[...].astype(BF16))).astype(BF16)
        y = _pack_rows(_dot(act, wd_ref[...].astype(BF16)), rounded=False)
        rows = tile * tm + lax.broadcasted_iota(jnp.int32, (tm, 1), 0)
        return jnp.where((rows >= starts_ref[e]) & (rows < starts_ref[e + 1]), y, other)

    @pl.when(jnp.logical_and(e < N_EXPERTS, is_first))
    def _():
        ys_ref[...] = expert_rows(jnp.zeros_like(ys_ref))

    @pl.when(jnp.logical_and(e < N_EXPERTS, jnp.logical_not(is_first)))
    def _():
        ys_ref[...] = expert_rows(ys_ref[...])

    @pl.when(e == N_EXPERTS)
    def _():
        ys_ref[...] = jnp.zeros_like(ys_ref)


def _experts_call(item_tile, item_exp, starts_ext, xs, wg, wu, wd, tm):
    n = xs.shape[0]
    n_items = item_tile.shape[0]
    w_idx = lambda i, tile, ex, st: (jnp.minimum(ex[i], N_EXPERTS - 1), 0, 0)
    grid_spec = pltpu.PrefetchScalarGridSpec(
        num_scalar_prefetch=3,
        grid=(n_items,),
        in_specs=[pl.BlockSpec((tm, D_MODEL // 2), lambda i, tile, ex, st: (tile[i], 0)),
                  pl.BlockSpec((None, D_MODEL, D_EXPERT), w_idx),
                  pl.BlockSpec((None, D_MODEL, D_EXPERT), w_idx),
                  pl.BlockSpec((None, D_EXPERT, D_MODEL), w_idx)],
        out_specs=pl.BlockSpec((tm, D_MODEL // 2), lambda i, tile, ex, st: (tile[i], 0)),
    )
    return pl.pallas_call(
        _experts_kernel,
        grid_spec=grid_spec,
        out_shape=jax.ShapeDtypeStruct((n, D_MODEL // 2), jnp.uint32),
        compiler_params=pltpu.CompilerParams(
            dimension_semantics=("arbitrary",), vmem_limit_bytes=V7X_VMEM_LIMIT),
        name="experts",
    )(item_tile, item_exp, starts_ext, xs, wg, wu, wd)


def _local_rows(tr):
    return TOP_K * tr + N_EXPERTS * ROW_ALIGN


def _expert_items(counts, n_rows, tm):
    n_tiles = n_rows // tm
    n_items = n_tiles + N_EXPERTS - 1
    starts = jnp.concatenate([jnp.zeros((1,), jnp.int32), jnp.cumsum(counts).astype(jnp.int32)])
    lo = starts[:-1] // tm
    hi = (starts[1:] - 1) // tm
    per = jnp.where(counts > 0, hi - lo + 1, 0)
    cum = jnp.cumsum(per)
    idx = jnp.arange(n_items, dtype=jnp.int32)
    e = jnp.sum(idx[:, None] >= cum[None, :], axis=1).astype(jnp.int32)
    valid = idx < cum[-1]
    e_c = jnp.minimum(e, N_EXPERTS - 1)
    tile = lo[e_c] + idx - (cum[e_c] - per[e_c])
    spare_tile = -(-starts[-1] // tm) + idx - cum[-1]
    item_tile = jnp.where(valid, tile, jnp.minimum(spare_tile, n_tiles - 1)).astype(jnp.int32)
    item_exp = jnp.where(valid, e_c, jnp.where(spare_tile < n_tiles, N_EXPERTS, N_EXPERTS + 1)).astype(jnp.int32)
    starts_ext = jnp.concatenate([starts, starts[-1:], starts[-1:]])
    return item_tile, item_exp, starts, starts_ext


def _combine_kernel(runs_ref, next_runs_ref, lpos_ref, gate_ref, x1_ref, nf_ref, ys_ref, out_ref, loc, sem):
    tr = x1_ref.shape[0]
    nloc = loc.shape[1]
    step = pl.program_id(0)
    slot = step % 2

    def gather(table, s):
        def copy_run(dst, src, n):
            pltpu.make_async_copy(ys_ref.at[pl.ds(dst, n)], loc.at[s, pl.ds(src, n)], sem.at[s]).start()
        _run_copies(table, copy_run, tr, 0)

    @pl.when(step == 0)
    def _():
        gather(runs_ref, slot)

    jrow = lax.broadcasted_iota(jnp.int32, (nloc, tr), 0)
    sel = jnp.where(jrow == lpos_ref[0:1, :], gate_ref[0:1, :],
                    jnp.where(jrow == lpos_ref[1:2, :], gate_ref[1:2, :], 0.0)).astype(BF16)
    pltpu.make_async_copy(ys_ref.at[pl.ds(0, nloc)], loc.at[slot], sem.at[slot]).wait()

    @pl.when(step + 1 < pl.num_programs(0))
    def _():
        gather(next_runs_ref, 1 - slot)

    y = _dot_tn(sel, _unpack_rows(loc[slot]))
    out_ref[...] = _rms(x1_ref[...] + y, nf_ref[...])


def _combine_call(runs, lpos, gates, x1, nf_w, ys, tr):
    t = x1.shape[0]
    return pl.pallas_call(
        _combine_kernel,
        grid=(t // tr,),
        in_specs=[pl.BlockSpec((None, 3, N_EXPERTS + 1), lambda i: (i, 0, 0), memory_space=pltpu.SMEM),
                  pl.BlockSpec((None, 3, N_EXPERTS + 1), lambda i: (jnp.minimum(i + 1, t // tr - 1), 0, 0),
                               memory_space=pltpu.SMEM),
                  pl.BlockSpec((8, tr), lambda i: (0, i)),
                  pl.BlockSpec((8, tr), lambda i: (0, i)),
                  pl.BlockSpec((tr, D_MODEL), lambda i: (i, 0)),
                  pl.BlockSpec((1, D_MODEL), lambda i: (0, 0)),
                  pl.BlockSpec(memory_space=pl.ANY)],
        out_specs=pl.BlockSpec((tr, D_MODEL), lambda i: (i, 0)),
        out_shape=jax.ShapeDtypeStruct((t, D_MODEL), F32),
        scratch_shapes=[pltpu.VMEM((2, _local_rows(tr), D_MODEL // 2), jnp.uint32), pltpu.SemaphoreType.DMA((2,))],
        compiler_params=pltpu.CompilerParams(dimension_semantics=("arbitrary",), vmem_limit_bytes=V7X_VMEM_LIMIT),
        name="combine",
    )(runs, runs, lpos, gates, x1, nf_w, ys)


SEQ_TILE = 4 * CHUNK
TOKEN_TILE = 512
ROW_TILE = 512


def kernel(x, norm1_w, w_in, ssd_conv_w, ssd_conv_b, ssd_dt_bias, ssd_a_log, ssd_d, ssd_norm_w, ml_conv_w, ml_conv_b, ml_i_bias, ml_f_bias, ml_norm_w, w_out, norm2_w, router_g_w, router_g_b, router_e_w, router_e_b, exp_w_gate, exp_w_up, exp_w_down, norm_f_w):
    bsz, seq, d = x.shape
    t = bsz * seq
    tl, tr, tm = SEQ_TILE, TOKEN_TILE, ROW_TILE
    assert d == D_MODEL and norm1_w.shape[0] == 1 and seq % tl == 0 and t % tr == 0

    consts = _mixer_consts(norm1_w[0], w_in[0], ssd_conv_w[0], ssd_conv_b[0], ssd_dt_bias[0], ssd_a_log[0],
                           ssd_d[0], ssd_norm_w[0], ml_conv_w[0], ml_conv_b[0], ml_i_bias[0], ml_f_bias[0],
                           ml_norm_w[0], w_out[0], norm2_w[0], router_g_w[0], router_e_w[0])
    x1, h2, lg_t = _mixer_call(x, consts, tl)
    x1 = x1.reshape(t, D_MODEL)
    h2 = h2.reshape(t, D_MODEL)

    bias_col = jnp.concatenate([router_e_b[0], router_g_b[0],
                                jnp.full((LOGIT_ROWS - N_EXPERTS - MOE_GROUPS,), STAB_INIT, F32)]).reshape(-1, 1)
    lpos, gates, cnt = _route_call(lg_t, bias_col.astype(F32), tr)

    seg = (cnt[:, :, 0].astype(jnp.int32) + (ROW_ALIGN - 1)) // ROW_ALIGN * ROW_ALIGN
    n_tok_tiles = t // tr
    n_rows = -(-(t * TOP_K + n_tok_tiles * N_EXPERTS * ROW_ALIGN + _local_rows(tr)) // tm) * tm
    item_tile, item_exp, starts, starts_ext = _expert_items(jnp.sum(seg, axis=0), n_rows, tm)
    run_dst = starts[None, :N_EXPERTS] + jnp.cumsum(seg, axis=0) - seg
    run_src = jnp.cumsum(seg, axis=1) - seg
    used = jnp.sum(seg, axis=1, keepdims=True)
    fill = jnp.concatenate([jnp.full_like(used, starts[N_EXPERTS]), used, _local_rows(tr) - used], axis=1)[:, :, None]
    runs = jnp.concatenate([jnp.stack([run_dst, run_src, seg], axis=1), fill], axis=2).astype(jnp.int32)

    xs = _dispatch_call(runs, lpos, h2, tr, n_rows)
    ys = _experts_call(item_tile, item_exp, starts_ext, xs, exp_w_gate[0], exp_w_up[0], exp_w_down[0], tm)
    out = _combine_call(runs, lpos, gates, x1, norm_f_w.reshape(1, -1).astype(F32), ys, tr)
    return out.reshape(bsz, seq, D_MODEL)
```

```python
import functools

import jax
import jax.numpy as jnp
import numpy as np
from jax import lax
from jax.experimental import pallas as pl
from jax.experimental.pallas import tpu as pltpu

D_MODEL = 1024
CHUNK = 64
SSD_WIDTH = 1024
SSD_HEAD_DIM = 64
SSD_HEADS = 16
SSD_GROUPS = 2
SSD_STATE = 128
SSD_CONV = 4
SSD_XBC = SSD_WIDTH + 2 * SSD_GROUPS * SSD_STATE
ML_WIDTH = 1024
ML_HEADS = 8
ML_HEAD_DIM = 128
ML_CONV = 4
D_MIX = SSD_WIDTH + ML_WIDTH
MOE_GROUPS = 4
EXPERTS_PER_GROUP = 8
N_EXPERTS = 32
TOP_K = 2
D_EXPERT = 512
EPS = 1e-6
STAB_INIT = -1e30

LANES = 128
GATE_COLS = 2 * LANES
ROW_ALIGN = 8
CONV_PAD = 8
LOGIT_ROWS = 64
V7X_VMEM_LIMIT = 56 * 1024 * 1024

F32 = jnp.float32
BF16 = jnp.bfloat16
NEG_INF = float("-inf")


def _dot(a, b, precision=None):
    return jnp.dot(a, b, preferred_element_type=F32, precision=precision)


def _dot_nt(a, b):
    return lax.dot_general(a, b, (((1,), (1,)), ((), ())), preferred_element_type=F32)


def _dot_tn(a, b):
    return lax.dot_general(a, b, (((0,), (0,)), ((), ())), preferred_element_type=F32)


def _split3(x, axis):
    hi = x.astype(BF16)
    r1 = x - hi.astype(F32)
    mid = r1.astype(BF16)
    lo = (r1 - mid.astype(F32)).astype(BF16)
    return jnp.concatenate([hi, mid, lo], axis=axis)


def _pack_rows(x, rounded):
    if not rounded:
        x = x.astype(BF16).astype(F32)
    bits = lax.bitcast_convert_type(x, jnp.uint32)
    half = x.shape[1] // 2
    return bits[:, half:] | (bits[:, :half] >> 16)


def _unpack_rows(p):
    lo = lax.bitcast_convert_type(p << 16, F32)
    hi = lax.bitcast_convert_type(p & jnp.uint32(0xFFFF0000), F32)
    return jnp.concatenate([lo, hi], axis=1).astype(BF16)


def _silu(x):
    return x * jax.nn.sigmoid(x)


def _softplus(x):
    return jnp.maximum(x, 0.0) + jnp.log1p(jnp.exp(-jnp.abs(x)))


def _rms(x, w):
    return x * lax.rsqrt(jnp.mean(x * x, axis=-1, keepdims=True) + EPS) * w


def _mixer_kernel(*refs, tl, nl):
    for parity in range(2):
        @pl.when(pl.program_id(0) % 2 == parity)
        def _(parity=parity):
            _mixer_step(*refs, tl=tl, nl=nl, slot_a=parity)


def _mixer_step(x_ref, xc_ref, n1_ref, wz_ref, wxbc_ref, wqk_ref, wv_ref, wo_ref, wg_ref,
                  scw_ref, scb_ref, dtb_ref, alog_ref, dskip_ref, snw_ref,
                  mcw_ref, mcb_ref, gb_ref, mnw_ref, wout_ref, n2_ref, wr_ref,
                  e16_ref, e2_ref, tril_ref, seq_ref, sge_ref, meq_ref, mge_ref,
                  x1_ref, h2_ref, lg_ref,
                  xbc_buf, qk_buf, z_buf, v_buf, o_buf, g_buf, xbc_c, qk_c, mix_buf,
                  xbc_tail, qk_tail, sst, mst, mm, h_buf, *, tl, nl, slot_a):
    nchunk = tl // CHUNK
    step = pl.program_id(0)
    slot_b = 1 - slot_a
    slot_c = slot_a

    @pl.when(step == 0)
    def _():
        xbc_buf[...] = jnp.zeros_like(xbc_buf)
        qk_buf[...] = jnp.zeros_like(qk_buf)
        z_buf[...] = jnp.zeros_like(z_buf)
        v_buf[...] = jnp.zeros_like(v_buf)
        o_buf[...] = jnp.zeros_like(o_buf)
        g_buf[...] = jnp.zeros_like(g_buf)
        mix_buf[...] = jnp.zeros_like(mix_buf)
        xbc_tail[...] = jnp.zeros_like(xbc_tail)
        qk_tail[...] = jnp.zeros_like(qk_tail)
        sst[...] = jnp.zeros_like(sst)
        mst[...] = jnp.zeros_like(mst)
        mm[...] = jnp.full(mm.shape, STAB_INIT, F32)

    h_buf[...] = _rms(x_ref[...], n1_ref[...]).astype(BF16)
    nblk = 2 * LANES
    tasks = []

    def proj_task(w_ref, store):
        for b0 in range(0, w_ref.shape[1], nblk):
            tasks.append(lambda b0=b0: store(slice(b0, b0 + nblk), _dot(h_buf[...], w_ref[:, b0:b0 + nblk])))

    def out_task(cols):
        x1_ref[:, cols] = xc_ref[:, cols] + _dot(mix_buf[slot_c], wout_ref[:, cols])

    def route_task():
        h2 = _rms(x1_ref[...], n2_ref[...])
        h2 = h2.astype(BF16)
        h2_ref[...] = h2
        lg_ref[...] = _dot_nt(wr_ref[...], h2)

    for b0 in range(0, D_MODEL, nblk):
        tasks.append(lambda b0=b0: out_task(slice(b0, b0 + nblk)))
    tasks.append(route_task)

    def put(buf, r0=0):
        def store(cols, val):
            buf[slot_a, r0:r0 + tl, cols] = val
        return store

    def put_tiles(buf):
        def store(cols, val):
            for k in range(nblk // LANES):
                buf[slot_a, cols.start // LANES + k, CONV_PAD:CONV_PAD + tl, :] = val[:, k * LANES:(k + 1) * LANES]
        return store

    proj_task(wz_ref, put(z_buf))
    proj_task(wg_ref, put(g_buf))
    proj_task(wxbc_ref, put_tiles(xbc_buf))
    proj_task(wqk_ref, put_tiles(qk_buf))
    proj_task(wv_ref, put(v_buf))
    proj_task(wo_ref, put(o_buf))
    per_chunk = -(-len(tasks) // nchunk)

    def run_task(c, k):
        idx = c * per_chunk + k
        if idx < len(tasks):
            tasks[idx]()

    seq_start = (step + nl - 1) % nl == 0
    xbc_buf[slot_b, :, 0:CONV_PAD, :] = jnp.where(seq_start, 0.0, xbc_tail[...])
    qk_buf[slot_b, :, 0:CONV_PAD, :] = jnp.where(seq_start, 0.0, qk_tail[...])
    sst[...] = jnp.where(seq_start, 0.0, sst[...])
    mst[...] = jnp.where(seq_start, 0.0, mst[...])
    mm[...] = jnp.where(seq_start, STAB_INIT, mm[...])

    first = CONV_PAD - (SSD_CONV - 1)
    xbc_tail[...] = xbc_buf[slot_b, :, tl:tl + CONV_PAD, :]
    qk_tail[...] = qk_buf[slot_b, :, tl:tl + CONV_PAD, :]

    def conv_silu(buf, w_ref, b_ref, out, r0):
        for t in range(buf.shape[1]):
            cols = slice(t * LANES, (t + 1) * LANES)
            acc = b_ref[:, cols] + w_ref[0:1, cols] * buf[slot_b, t, r0 + first:r0 + first + CHUNK, :]
            for j in range(1, w_ref.shape[0]):
                acc = acc + w_ref[j:j + 1, cols] * buf[slot_b, t, r0 + first + j:r0 + first + j + CHUNK, :]
            out[r0:r0 + CHUNK, cols] = _silu(acc)

    def convs(c):
        run_task(c, 0)
        conv_silu(xbc_buf, scw_ref, scb_ref, xbc_c, c * CHUNK)
        run_task(c, 1)
        conv_silu(qk_buf, mcw_ref, mcb_ref, qk_c, c * CHUNK)
        run_task(c, 2)

    for c in range(nchunk):
        convs(c)
        rows = slice(c * CHUNK, (c + 1) * CHUNK)
        tril3 = tril_ref[...]

        xs = xbc_c[rows, 0:SSD_WIDTH]
        bm = xbc_c[rows, SSD_WIDTH:SSD_WIDTH + 2 * SSD_STATE].astype(BF16)
        cm = xbc_c[rows, SSD_WIDTH + 2 * SSD_STATE:SSD_XBC].astype(BF16)
        dt = _softplus(g_buf[slot_b, rows, 0:SSD_HEADS] + dtb_ref[...])
        a_cs = _dot(tril3, _split3(dt * (-jnp.exp(alog_ref[...])), 0))
        dt_e = _dot(_split3(dt, 1), e16_ref[...])
        a_col = _dot(_split3(a_cs, 1), e16_ref[...])
        a_row = jnp.sum(a_col * seq_ref[...], axis=0, keepdims=True)
        lmat = jnp.exp(a_col - a_row + sge_ref[...])
        cb = jnp.concatenate(
            [_dot_nt(cm[:, g * SSD_STATE:(g + 1) * SSD_STATE],
                     jnp.concatenate([bm[:, g * SSD_STATE:(g + 1) * SSD_STATE]] * (SSD_HEADS // SSD_GROUPS), axis=0))
             for g in range(SSD_GROUPS)], axis=1)
        m_all = (cb * lmat).astype(BF16)
        run_task(c, 3)
        xdt = xs * dt_e
        lane = lax.broadcasted_iota(jnp.int32, (CHUNK, LANES), 1)
        y_parts = []
        for j in range(SSD_HEADS // 2):
            xp = xdt[:, j * LANES:(j + 1) * LANES]
            xbd = jnp.concatenate([jnp.where(lane < SSD_HEAD_DIM, xp, 0.0),
                                   jnp.where(lane >= SSD_HEAD_DIM, xp, 0.0)], axis=0).astype(BF16)
            y_parts.append(_dot(m_all[:, j * LANES:(j + 1) * LANES], xbd))
        y = jnp.concatenate(y_parts, axis=1)
        half = SSD_WIDTH // SSD_GROUPS
        y_int = jnp.concatenate(
            [_dot(cm[:, g * SSD_STATE:(g + 1) * SSD_STATE], sst[g].astype(BF16)) for g in range(SSD_GROUPS)], axis=1)
        y = y + y_int * jnp.exp(a_col)
        a_last = a_col[CHUNK - 1:CHUNK, :]
        xd_b = (xdt * jnp.exp(a_last - a_col)).astype(BF16)
        st_scale = jnp.exp(a_last)
        for g in range(SSD_GROUPS):
            sst[g] = sst[g] * st_scale[:, g * half:(g + 1) * half] + _dot_tn(
                bm[:, g * SSD_STATE:(g + 1) * SSD_STATE], xd_b[:, g * half:(g + 1) * half])
        y = (y + dskip_ref[...] * xs) * _silu(z_buf[slot_b, rows, :])
        for g in range(SSD_GROUPS):
            yg = y[:, g * half:(g + 1) * half]
            yg = yg * lax.rsqrt(jnp.mean(yg * yg, axis=-1, keepdims=True) + EPS) * snw_ref[:, g * half:(g + 1) * half]
            mix_buf[slot_b, rows, g * half:(g + 1) * half] = yg.astype(BF16)

        run_task(c, 4)

        gi = g_buf[slot_b, rows, LANES:LANES + 2 * ML_HEADS] + gb_ref[...]
        col16 = lax.broadcasted_iota(jnp.int32, (CHUNK, 2 * ML_HEADS), 1)
        gi = jnp.where(col16 < ML_HEADS, gi, jnp.minimum(gi, 0.0) - jnp.log1p(jnp.exp(-jnp.abs(gi))))
        gi = jnp.where(col16 < ML_HEADS, gi, _dot(tril3, _split3(gi, 0)))
        gi_e = _dot(_split3(gi, 1), e2_ref[...])
        ig_e = gi_e[:, 0:ML_WIDTH]
        b_col = gi_e[:, ML_WIDTH:2 * ML_WIDTH]
        c_row = jnp.sum((ig_e - b_col) * meq_ref[...], axis=0, keepdims=True)
        dmat = b_col + c_row + mge_ref[...]
        heads = range(ML_HEADS)
        hs = lambda arr, hd: arr[:, hd * LANES:(hd + 1) * LANES]
        ones_blk = jnp.ones((CHUNK, LANES), BF16)
        m_prev = mm[...]
        m_inter = b_col + m_prev
        q = [qk_c[rows, hd * LANES:(hd + 1) * LANES].astype(BF16) for hd in heads]
        kf = [qk_c[rows, ML_WIDTH + hd * LANES:ML_WIDTH + (hd + 1) * LANES] * (ML_HEAD_DIM ** -0.5) for hd in heads]
        v_aug = [jnp.concatenate([v_buf[slot_b, rows, hd * LANES:(hd + 1) * LANES].astype(BF16), ones_blk], axis=1)
                 for hd in heads]
        s = [_dot_nt(q[hd], kf[hd].astype(BF16)) for hd in heads]
        qc = [_dot(q[hd], mst[hd].astype(BF16)) for hd in heads]
        run_task(c, 5)
        m_t = jnp.maximum(m_inter, jnp.concatenate(
            [jnp.broadcast_to(jnp.max(hs(dmat, hd), axis=-1, keepdims=True), (CHUNK, LANES)) for hd in heads], axis=1))
        w = jnp.exp(dmat - m_t)
        inter = jnp.exp(m_inter - m_t)
        den_floor = jnp.exp(-m_t)
        nd = [_dot((s[hd] * hs(w, hd)[:, 0:CHUNK]).astype(BF16), v_aug[hd])
              + qc[hd] * jnp.concatenate([hs(inter, hd)] * 2, axis=1) for hd in heads]
        run_task(c, 6)
        hh = [nd[hd][:, 0:LANES] / jnp.maximum(jnp.abs(nd[hd][:, LANES:2 * LANES]), hs(den_floor, hd)) for hd in heads]
        hc = [hh[hd] - jnp.mean(hh[hd], axis=-1, keepdims=True) for hd in heads]
        var = [jnp.mean(hc[hd] * hc[hd], axis=-1, keepdims=True) for hd in heads]
        for hd in heads:
            hn = hc[hd] * lax.rsqrt(var[hd] + EPS) * mnw_ref[:, hd * LANES:(hd + 1) * LANES]
            mix_buf[slot_b, rows, SSD_WIDTH + hd * LANES:SSD_WIDTH + (hd + 1) * LANES] = (
                jax.nn.sigmoid(o_buf[slot_b, rows, hd * LANES:(hd + 1) * LANES]) * hn).astype(BF16)
        run_task(c, 7)
        g_row = b_col[CHUNK - 1:CHUNK, :]
        a = g_row - b_col + ig_e
        m_new = jnp.maximum(g_row + m_prev, jnp.max(a, axis=0, keepdims=True))
        wk = jnp.exp(a - m_new)
        cs = jnp.exp(g_row + m_prev - m_new)
        for hd in heads:
            mst[hd] = mst[hd] * jnp.concatenate([hs(cs, hd)] * 2, axis=1) + _dot_tn(
                (kf[hd] * hs(wk, hd)).astype(BF16), v_aug[hd])
        mm[...] = m_new


def _const_spec(shape):
    nd = len(shape)
    return pl.BlockSpec(shape, lambda s: (0,) * nd, pipeline_mode=pl.Buffered(1))


def _mixer_call(x, consts, tl):
    bsz, seq, _ = x.shape
    nl = seq // tl
    n_tiles = bsz * nl
    x_tiles = x.reshape(n_tiles, tl, D_MODEL)
    tile_in = lambda s: (jnp.minimum(s, n_tiles - 1), 0, 0)
    tile_out = lambda s: (jnp.maximum(s - 2, 0), 0, 0)
    in_specs = ([pl.BlockSpec((None, tl, D_MODEL), tile_in), pl.BlockSpec((None, tl, D_MODEL), tile_out)]
                + [_const_spec(c.shape) for c in consts])
    out_specs = [pl.BlockSpec((None, tl, D_MODEL), tile_out), pl.BlockSpec((None, tl, D_MODEL), tile_out),
                 pl.BlockSpec((LOGIT_ROWS, tl), lambda s: (0, jnp.maximum(s - 2, 0)))]
    out_shape = [jax.ShapeDtypeStruct((n_tiles, tl, D_MODEL), F32),
                 jax.ShapeDtypeStruct((n_tiles, tl, D_MODEL), BF16),
                 jax.ShapeDtypeStruct((LOGIT_ROWS, bsz * seq), F32)]
    scratch = [
        pltpu.VMEM((2, SSD_XBC // LANES, tl + CONV_PAD, LANES), F32),
        pltpu.VMEM((2, 2 * ML_WIDTH // LANES, tl + CONV_PAD, LANES), F32),
        pltpu.VMEM((2, tl, SSD_WIDTH), F32),
        pltpu.VMEM((2, tl, ML_WIDTH), F32),
        pltpu.VMEM((2, tl, ML_WIDTH), F32),
        pltpu.VMEM((2, tl, GATE_COLS), F32),
        pltpu.VMEM((tl, SSD_XBC), F32),
        pltpu.VMEM((tl, 2 * ML_WIDTH), F32),
        pltpu.VMEM((2, tl, D_MIX), BF16),
        pltpu.VMEM((SSD_XBC // LANES, CONV_PAD, LANES), F32),
        pltpu.VMEM((2 * ML_WIDTH // LANES, CONV_PAD, LANES), F32),
        pltpu.VMEM((SSD_GROUPS, SSD_STATE, SSD_WIDTH // SSD_GROUPS), F32),
        pltpu.VMEM((ML_HEADS, ML_HEAD_DIM, 2 * LANES), F32),
        pltpu.VMEM((1, ML_WIDTH), F32),
        pltpu.VMEM((tl, D_MODEL), BF16),
    ]
    return pl.pallas_call(
        functools.partial(_mixer_kernel, tl=tl, nl=nl),
        grid=(n_tiles + 2,),
        in_specs=in_specs,
        out_specs=out_specs,
        out_shape=out_shape,
        scratch_shapes=scratch,
        compiler_params=pltpu.CompilerParams(
            dimension_semantics=("arbitrary",), vmem_limit_bytes=V7X_VMEM_LIMIT),
        name="mixer",
    )(x_tiles, x_tiles, *consts)


def _mixer_consts(norm1_w, w_in, ssd_conv_w, ssd_conv_b, ssd_dt_bias, ssd_a_log, ssd_d, ssd_norm_w,
                  ml_conv_w, ml_conv_b, ml_i_bias, ml_f_bias, ml_norm_w, w_out, norm2_w,
                  router_g_w, router_e_w):
    o1 = SSD_WIDTH
    o2 = o1 + SSD_XBC
    o3 = o2 + SSD_HEADS
    o4 = o3 + 2 * ML_WIDTH
    o5 = o4 + ML_WIDTH
    o6 = o5 + ML_WIDTH
    o7 = o6 + ML_HEADS
    zpad = lambda n: jnp.zeros((D_MODEL, n), F32)
    w_gate = jnp.concatenate([w_in[:, o2:o3], zpad(LANES - SSD_HEADS),
                              w_in[:, o6:o7], w_in[:, o7:], zpad(LANES - 2 * ML_HEADS)], axis=1)
    row = lambda v: v.reshape(1, -1).astype(F32)
    w_route = jnp.concatenate([router_e_w, router_g_w,
                               jnp.zeros((D_MODEL, LOGIT_ROWS - N_EXPERTS - MOE_GROUPS), F32)], axis=1).T

    c1024 = np.arange(SSD_WIDTH)
    r64 = np.arange(CHUNK)[:, None]
    e16 = np.tile(c1024[None, :] // SSD_HEAD_DIM == np.arange(SSD_HEADS)[:, None], (3, 1))
    e2 = np.tile(np.arange(2 * ML_WIDTH)[None, :] // LANES == np.arange(2 * ML_HEADS)[:, None], (3, 1))
    tril = np.tile(np.arange(CHUNK)[None, :] <= r64, (1, 3))
    s_pos = (c1024 % SSD_HEAD_DIM)[None, :]
    m_pos = (c1024 % LANES)[None, :]
    as_bf16 = lambda m: jnp.asarray(m.astype(np.float32), BF16)
    as_f32 = lambda m: jnp.asarray(m.astype(np.float32))
    e16, e2, tril = as_bf16(e16), as_bf16(e2), as_bf16(tril)
    causal = lambda keep: jnp.asarray(np.where(keep, 0.0, -np.inf).astype(np.float32))
    seq, sge, meq, mge = as_f32(r64 == s_pos), causal(r64 >= s_pos), as_f32(r64 == m_pos), causal(r64 >= m_pos)
    return [
        row(norm1_w),
        w_in[:, :o1].astype(BF16), w_in[:, o1:o2].astype(BF16), w_in[:, o3:o4].astype(BF16),
        w_in[:, o4:o5].astype(BF16), w_in[:, o5:o6].astype(BF16), w_gate.astype(BF16),
        ssd_conv_w.astype(F32), row(ssd_conv_b), row(ssd_dt_bias),
        row(ssd_a_log), row(jnp.repeat(ssd_d, SSD_HEAD_DIM)), row(ssd_norm_w),
        ml_conv_w.astype(F32), row(ml_conv_b), row(jnp.concatenate([ml_i_bias, ml_f_bias])), row(ml_norm_w),
        w_out.astype(BF16), row(norm2_w), w_route.astype(BF16),
        e16, e2, tril, seq, sge, meq, mge,
    ]


def _route_kernel(lg_ref, bias_ref, striu_ref, lpos_ref, gate_ref, cnt_ref):
    tr = lg_ref.shape[1]
    lg = lg_ref[...] + bias_ref[...]
    el = lg[0:N_EXPERTS]
    gl = lg[N_EXPERTS:N_EXPERTS + 8]
    gmax = jnp.max(gl, axis=0, keepdims=True)
    pg = 1.0 / jnp.sum(jnp.exp(gl - gmax), axis=0, keepdims=True)
    grow = lax.broadcasted_iota(jnp.int32, gl.shape, 0)
    gsel = jnp.min(jnp.where(gl == gmax, grow, 8), axis=0, keepdims=True)
    erow = lax.broadcasted_iota(jnp.int32, el.shape, 0)
    m1 = jnp.where((erow >> 3) == gsel, el, NEG_INF)
    v1 = jnp.max(m1, axis=0, keepdims=True)
    i1 = jnp.min(jnp.where(m1 == v1, erow, N_EXPERTS), axis=0, keepdims=True)
    m2 = jnp.where(erow == i1, NEG_INF, m1)
    v2 = jnp.max(m2, axis=0, keepdims=True)
    i2 = jnp.min(jnp.where(m2 == v2, erow, N_EXPERTS), axis=0, keepdims=True)
    e2 = jnp.exp(v2 - v1)
    g1 = pg / (1.0 + e2)
    g2 = g1 * e2
    hit1 = erow == i1
    hit2 = erow == i2
    oh = jnp.where(hit1, 1.0, jnp.where(hit2, 1.0, 0.0))
    earlier = _dot(oh.astype(BF16), striu_ref[...])
    cnt = jnp.sum(oh, axis=1, keepdims=True)
    seg = jnp.floor((cnt + (ROW_ALIGN - 1)) * (1.0 / ROW_ALIGN)) * ROW_ALIGN
    l1 = jnp.sum(jnp.where(hit1, earlier, jnp.where(erow < i1, seg, 0.0)), axis=0, keepdims=True)
    l2 = jnp.sum(jnp.where(hit2, earlier, jnp.where(erow < i2, seg, 0.0)), axis=0, keepdims=True)
    lpos_ref[...] = jnp.concatenate([l1.astype(jnp.int32), l2.astype(jnp.int32), jnp.zeros((6, tr), jnp.int32)], axis=0)
    gate_ref[...] = jnp.concatenate([g1, g2, jnp.zeros((6, tr), F32)], axis=0)
    cnt_ref[...] = jnp.broadcast_to(cnt, cnt_ref.shape)


def _route_call(lg_t, bias_col, tr):
    t = lg_t.shape[1]
    striu = jnp.asarray(np.triu(np.ones((tr, tr), np.float32), 1), BF16)
    return pl.pallas_call(
        _route_kernel,
        grid=(t // tr,),
        in_specs=[pl.BlockSpec((LOGIT_ROWS, tr), lambda i: (0, i)),
                  pl.BlockSpec((LOGIT_ROWS, 1), lambda i: (0, 0)),
                  pl.BlockSpec((tr, tr), lambda i: (0, 0))],
        out_specs=[pl.BlockSpec((8, tr), lambda i: (0, i)),
                   pl.BlockSpec((8, tr), lambda i: (0, i)),
                   pl.BlockSpec((None, N_EXPERTS, LANES), lambda i: (i, 0, 0))],
        out_shape=[jax.ShapeDtypeStruct((8, t), jnp.int32),
                   jax.ShapeDtypeStruct((8, t), F32),
                   jax.ShapeDtypeStruct((t // tr, N_EXPERTS, LANES), F32)],
        compiler_params=pltpu.CompilerParams(dimension_semantics=("arbitrary",)),
        name="route",
    )(lg_t, bias_col, striu)


def _run_copies(runs_ref, copy_run, max_piece, filler_far, enable=None):
    for e in range(N_EXPERTS + 1):
        dst = runs_ref[0, e] if e < N_EXPERTS else filler_far
        src = runs_ref[1, e]
        n = runs_ref[2, e] if enable is None else jnp.where(enable, runs_ref[2, e], 0)
        p = max_piece
        while p >= ROW_ALIGN:
            done = n & ~(2 * p - 1)

            @pl.when((n & p) != 0)
            def _(p=p, done=done, dst=dst, src=src):
                copy_run(pl.multiple_of(dst + done, ROW_ALIGN), pl.multiple_of(src + done, ROW_ALIGN), p)
            p //= 2


def _by_parity(body):
    def kernel(*refs, **kw):
        for parity in range(2):
            @pl.when(pl.program_id(0) % 2 == parity)
            def _(parity=parity):
                body(*refs, slot=parity, **kw)
    return kernel


def _dispatch_step(runs_ref, prev_runs_ref, lpos_ref, h2_ref, xs_ref, loc, sem, *, max_spare, slot):
    tr = h2_ref.shape[0]
    nloc = loc.shape[1]
    step = pl.program_id(0)
    last = pl.num_programs(0) - 1
    used = runs_ref[0, N_EXPERTS]

    def wait_block(s):
        pltpu.make_async_copy(loc.at[s], xs_ref.at[pl.ds(0, nloc)], sem.at[s]).wait()

    def send_block(table, s, enable=None):
        def copy_run(dst, src, n):
            pltpu.make_async_copy(loc.at[s, pl.ds(src, n)], xs_ref.at[pl.ds(dst, n)], sem.at[s]).start()
        _run_copies(table, copy_run, tr, used + s * (nloc - TOP_K * tr), enable)

    @pl.when(step > 1)
    def _():
        wait_block(slot)

    send_block(prev_runs_ref, 1 - slot, enable=step > 0)

    jrow = lax.broadcasted_iota(jnp.int32, (nloc, tr), 0)
    perm = jnp.where(jrow == lpos_ref[0:1, :], 1.0, jnp.where(jrow == lpos_ref[1:2, :], 1.0, 0.0)).astype(BF16)
    loc[slot] = _pack_rows(_dot(perm, h2_ref[...]), rounded=True)

    @pl.when(step == last)
    def _():
        send_block(runs_ref, slot)

        @pl.when(step > 0)
        def _():
            wait_block(1 - slot)
        wait_block(slot)
        loc[slot, 0:tr, :] = jnp.zeros((tr, loc.shape[2]), loc.dtype)
        spare = xs_ref.shape[0] - used

        def zero_copy(off, n):
            return pltpu.make_async_copy(loc.at[slot, pl.ds(0, n)],
                                         xs_ref.at[pl.ds(pl.multiple_of(used + off, ROW_ALIGN), n)], sem.at[slot])

        def pieces(do):
            for i in range(max_spare // tr):
                @pl.when((i + 1) * tr <= spare)
                def _(i=i):
                    do(zero_copy(i * tr, tr))
            p = tr // 2
            while p >= ROW_ALIGN:
                @pl.when((spare & p) != 0)
                def _(p=p):
                    do(zero_copy(spare & ~(2 * p - 1), p))
                p //= 2

        pieces(lambda c: c.start())
        pieces(lambda c: c.wait())


def _dispatch_call(runs, lpos, h2, tr, n_rows):
    t = h2.shape[0]
    return pl.pallas_call(
        functools.partial(_by_parity(_dispatch_step), max_spare=n_rows - t * TOP_K),
        grid=(t // tr,),
        in_specs=[pl.BlockSpec((None, 3, N_EXPERTS + 1), lambda i: (i, 0, 0), memory_space=pltpu.SMEM),
                  pl.BlockSpec((None, 3, N_EXPERTS + 1), lambda i: (jnp.maximum(i - 1, 0), 0, 0), memory_space=pltpu.SMEM),
                  pl.BlockSpec((8, tr), lambda i: (0, i)),
                  pl.BlockSpec((tr, D_MODEL), lambda i: (i, 0))],
        out_specs=pl.BlockSpec(memory_space=pl.ANY),
        out_shape=jax.ShapeDtypeStruct((n_rows, D_MODEL // 2), jnp.uint32),
        scratch_shapes=[pltpu.VMEM((2, _local_rows(tr), D_MODEL // 2), jnp.uint32), pltpu.SemaphoreType.DMA((2,))],
        compiler_params=pltpu.CompilerParams(dimension_semantics=("arbitrary",), vmem_limit_bytes=V7X_VMEM_LIMIT),
        name="dispatch",
    )(runs, runs, lpos, h2)


def _experts_kernel(tile_ref, exp_ref, starts_ref, xs_ref, wg_ref, wu_ref, wd_ref, ys_ref):
    i = pl.program_id(0)
    tm = xs_ref.shape[0]
    e = exp_ref[i]
    tile = tile_ref[i]
    is_first = jnp.logical_or(i == 0, tile_ref[jnp.maximum(i - 1, 0)] != tile)

    def expert_rows(other):
        x = _unpack_rows(xs_ref[...])
        act = (_silu(_dot(x, wg_ref[...].astype(BF16))) * _dot(x, wu_ref[...].astype(BF16))).astype(BF16)
        y = _pack_rows(_dot(act, wd_ref[...].astype(BF16)), rounded=False)
        rows = tile * tm + lax.broadcasted_iota(jnp.int32, (tm, 1), 0)
        return jnp.where((rows >= starts_ref[e]) & (rows < starts_ref[e + 1]), y, other)

    @pl.when(jnp.logical_and(e < N_EXPERTS, is_first))
    def _():
        ys_ref[...] = expert_rows(jnp.zeros_like(ys_ref))

    @pl.when(jnp.logical_and(e < N_EXPERTS, jnp.logical_not(is_first)))
    def _():
        ys_ref[...] = expert_rows(ys_ref[...])

    @pl.when(e == N_EXPERTS)
    def _():
        ys_ref[...] = jnp.zeros_like(ys_ref)


def _experts_call(item_tile, item_exp, starts_ext, xs, wg, wu, wd, tm):
    n = xs.shape[0]
    n_items = item_tile.shape[0]
    w_idx = lambda i, tile, ex, st: (jnp.minimum(ex[i], N_EXPERTS - 1), 0, 0)
    grid_spec = pltpu.PrefetchScalarGridSpec(
        num_scalar_prefetch=3,
        grid=(n_items,),
        in_specs=[pl.BlockSpec((tm, D_MODEL // 2), lambda i, tile, ex, st: (tile[i], 0)),
                  pl.BlockSpec((None, D_MODEL, D_EXPERT), w_idx),
                  pl.BlockSpec((None, D_MODEL, D_EXPERT), w_idx),
                  pl.BlockSpec((None, D_EXPERT, D_MODEL), w_idx)],
        out_specs=pl.BlockSpec((tm, D_MODEL // 2), lambda i, tile, ex, st: (tile[i], 0)),
    )
    return pl.pallas_call(
        _experts_kernel,
        grid_spec=grid_spec,
        out_shape=jax.ShapeDtypeStruct((n, D_MODEL // 2), jnp.uint32),
        compiler_params=pltpu.CompilerParams(
            dimension_semantics=("arbitrary",), vmem_limit_bytes=V7X_VMEM_LIMIT),
        name="experts",
    )(item_tile, item_exp, starts_ext, xs, wg, wu, wd)


def _local_rows(tr):
    return TOP_K * tr + N_EXPERTS * ROW_ALIGN


def _expert_items(counts, n_rows, tm):
    n_tiles = n_rows // tm
    n_items = n_tiles + N_EXPERTS - 1
    starts = jnp.concatenate([jnp.zeros((1,), jnp.int32), jnp.cumsum(counts).astype(jnp.int32)])
    lo = starts[:-1] // tm
    hi = (starts[1:] - 1) // tm
    per = jnp.where(counts > 0, hi - lo + 1, 0)
    cum = jnp.cumsum(per)
    idx = jnp.arange(n_items, dtype=jnp.int32)
    e = jnp.sum(idx[:, None] >= cum[None, :], axis=1).astype(jnp.int32)
    valid = idx < cum[-1]
    e_c = jnp.minimum(e, N_EXPERTS - 1)
    tile = lo[e_c] + idx - (cum[e_c] - per[e_c])
    spare_tile = -(-starts[-1] // tm) + idx - cum[-1]
    item_tile = jnp.where(valid, tile, jnp.minimum(spare_tile, n_tiles - 1)).astype(jnp.int32)
    item_exp = jnp.where(valid, e_c, jnp.where(spare_tile < n_tiles, N_EXPERTS, N_EXPERTS + 1)).astype(jnp.int32)
    starts_ext = jnp.concatenate([starts, starts[-1:], starts[-1:]])
    return item_tile, item_exp, starts, starts_ext


def _combine_step(runs_ref, next_runs_ref, lpos_ref, gate_ref, x1_ref, nf_ref, ys_ref, out_ref, loc, sem, *, slot):
    tr = x1_ref.shape[0]
    nloc = loc.shape[1]
    step = pl.program_id(0)

    def gather(table, s, enable=None):
        def copy_run(dst, src, n):
            pltpu.make_async_copy(ys_ref.at[pl.ds(dst, n)], loc.at[s, pl.ds(src, n)], sem.at[s]).start()
        _run_copies(table, copy_run, tr, 0, enable)

    @pl.when(step == 0)
    def _():
        gather(runs_ref, slot)

    jrow = lax.broadcasted_iota(jnp.int32, (nloc, tr), 0)
    sel = jnp.where(jrow == lpos_ref[0:1, :], gate_ref[0:1, :],
                    jnp.where(jrow == lpos_ref[1:2, :], gate_ref[1:2, :], 0.0)).astype(BF16)
    pltpu.make_async_copy(ys_ref.at[pl.ds(0, nloc)], loc.at[slot], sem.at[slot]).wait()

    gather(next_runs_ref, 1 - slot, enable=step + 1 < pl.num_programs(0))

    y = _dot_tn(sel, _unpack_rows(loc[slot]))
    out_ref[...] = _rms(x1_ref[...] + y, nf_ref[...])


def _combine_call(runs, lpos, gates, x1, nf_w, ys, tr):
    t = x1.shape[0]
    return pl.pallas_call(
        _by_parity(_combine_step),
        grid=(t // tr,),
        in_specs=[pl.BlockSpec((None, 3, N_EXPERTS + 1), lambda i: (i, 0, 0), memory_space=pltpu.SMEM),
                  pl.BlockSpec((None, 3, N_EXPERTS + 1), lambda i: (jnp.minimum(i + 1, t // tr - 1), 0, 0),
                               memory_space=pltpu.SMEM),
                  pl.BlockSpec((8, tr), lambda i: (0, i)),
                  pl.BlockSpec((8, tr), lambda i: (0, i)),
                  pl.BlockSpec((tr, D_MODEL), lambda i: (i, 0)),
                  pl.BlockSpec((1, D_MODEL), lambda i: (0, 0)),
                  pl.BlockSpec(memory_space=pl.ANY)],
        out_specs=pl.BlockSpec((tr, D_MODEL), lambda i: (i, 0)),
        out_shape=jax.ShapeDtypeStruct((t, D_MODEL), F32),
        scratch_shapes=[pltpu.VMEM((2, _local_rows(tr), D_MODEL // 2), jnp.uint32), pltpu.SemaphoreType.DMA((2,))],
        compiler_params=pltpu.CompilerParams(dimension_semantics=("arbitrary",), vmem_limit_bytes=V7X_VMEM_LIMIT),
        name="combine",
    )(runs, runs, lpos, gates, x1, nf_w, ys)


SEQ_TILE = 4 * CHUNK
TOKEN_TILE = 512
ROW_TILE = 512


def kernel(x, norm1_w, w_in, ssd_conv_w, ssd_conv_b, ssd_dt_bias, ssd_a_log, ssd_d, ssd_norm_w, ml_conv_w, ml_conv_b, ml_i_bias, ml_f_bias, ml_norm_w, w_out, norm2_w, router_g_w, router_g_b, router_e_w, router_e_b, exp_w_gate, exp_w_up, exp_w_down, norm_f_w):
    bsz, seq, d = x.shape
    t = bsz * seq
    tl, tr, tm = SEQ_TILE, TOKEN_TILE, ROW_TILE
    assert d == D_MODEL and norm1_w.shape[0] == 1 and seq % tl == 0 and t % tr == 0

    consts = _mixer_consts(norm1_w[0], w_in[0], ssd_conv_w[0], ssd_conv_b[0], ssd_dt_bias[0], ssd_a_log[0],
                           ssd_d[0], ssd_norm_w[0], ml_conv_w[0], ml_conv_b[0], ml_i_bias[0], ml_f_bias[0],
                           ml_norm_w[0], w_out[0], norm2_w[0], router_g_w[0], router_e_w[0])
    x1, h2, lg_t = _mixer_call(x, consts, tl)
    x1 = x1.reshape(t, D_MODEL)
    h2 = h2.reshape(t, D_MODEL)

    bias_col = jnp.concatenate([router_e_b[0], router_g_b[0],
                                jnp.full((LOGIT_ROWS - N_EXPERTS - MOE_GROUPS,), STAB_INIT, F32)]).reshape(-1, 1)
    lpos, gates, cnt = _route_call(lg_t, bias_col.astype(F32), tr)

    seg = (cnt[:, :, 0].astype(jnp.int32) + (ROW_ALIGN - 1)) // ROW_ALIGN * ROW_ALIGN
    n_tok_tiles = t // tr
    n_rows = -(-(t * TOP_K + n_tok_tiles * N_EXPERTS * ROW_ALIGN + _local_rows(tr)) // tm) * tm
    item_tile, item_exp, starts, starts_ext = _expert_items(jnp.sum(seg, axis=0), n_rows, tm)
    run_dst = starts[None, :N_EXPERTS] + jnp.cumsum(seg, axis=0) - seg
    run_src = jnp.cumsum(seg, axis=1) - seg
    used = jnp.sum(seg, axis=1, keepdims=True)
    fill = jnp.concatenate([jnp.full_like(used, starts[N_EXPERTS]), used, _local_rows(tr) - used], axis=1)[:, :, None]
    runs = jnp.concatenate([jnp.stack([run_dst, run_src, seg], axis=1), fill], axis=2).astype(jnp.int32)

    xs = _dispatch_call(runs, lpos, h2, tr, n_rows)
    ys = _experts_call(item_tile, item_exp, starts_ext, xs, exp_w_gate[0], exp_w_up[0], exp_w_down[0], tm)
    out = _combine_call(runs, lpos, gates, x1, norm_f_w.reshape(1, -1).astype(F32), ys, tr)
    return out.reshape(bsz, seq, D_MODEL)
```

```python
import functools

import jax
import jax.numpy as jnp
import numpy as np
from jax import lax
from jax.experimental import pallas as pl
from jax.experimental.pallas import tpu as pltpu

D_MODEL = 1024
CHUNK = 64
SSD_WIDTH = 1024
SSD_HEAD_DIM = 64
SSD_HEADS = 16
SSD_GROUPS = 2
SSD_STATE = 128
SSD_CONV = 4
SSD_XBC = SSD_WIDTH + 2 * SSD_GROUPS * SSD_STATE
ML_WIDTH = 1024
ML_HEADS = 8
ML_HEAD_DIM = 128
ML_CONV = 4
D_MIX = SSD_WIDTH + ML_WIDTH
MOE_GROUPS = 4
EXPERTS_PER_GROUP = 8
N_EXPERTS = 32
TOP_K = 2
D_EXPERT = 512
EPS = 1e-6
STAB_INIT = -1e30

LANES = 128
GATE_COLS = 2 * LANES
ROW_ALIGN = 8
CONV_PAD = 8
LOGIT_ROWS = 64
V7X_VMEM_LIMIT = 56 * 1024 * 1024

F32 = jnp.float32
BF16 = jnp.bfloat16
NEG_INF = float("-inf")


def _dot(a, b, precision=None):
    return jnp.dot(a, b, preferred_element_type=F32, precision=precision)


def _dot_nt(a, b):
    return lax.dot_general(a, b, (((1,), (1,)), ((), ())), preferred_element_type=F32)


def _dot_tn(a, b):
    return lax.dot_general(a, b, (((0,), (0,)), ((), ())), preferred_element_type=F32)


def _split3(x, axis):
    hi = x.astype(BF16)
    r1 = x - hi.astype(F32)
    mid = r1.astype(BF16)
    lo = (r1 - mid.astype(F32)).astype(BF16)
    return jnp.concatenate([hi, mid, lo], axis=axis)


def _pack_rows(x, rounded):
    if not rounded:
        x = x.astype(BF16).astype(F32)
    bits = lax.bitcast_convert_type(x, jnp.uint32)
    half = x.shape[1] // 2
    return bits[:, half:] | (bits[:, :half] >> 16)


def _unpack_rows(p):
    lo = lax.bitcast_convert_type(p << 16, F32)
    hi = lax.bitcast_convert_type(p & jnp.uint32(0xFFFF0000), F32)
    return jnp.concatenate([lo, hi], axis=1).astype(BF16)


def _silu(x):
    return x * jax.nn.sigmoid(x)


def _softplus(x):
    return jnp.maximum(x, 0.0) + jnp.log1p(jnp.exp(-jnp.abs(x)))


def _rms(x, w):
    return x * lax.rsqrt(jnp.mean(x * x, axis=-1, keepdims=True) + EPS) * w


def _mixer_kernel(*refs, tl, nl):
    for parity in range(2):
        @pl.when(pl.program_id(0) % 2 == parity)
        def _(parity=parity):
            _mixer_step(*refs, tl=tl, nl=nl, slot_a=parity)


def _mixer_step(x_ref, xc_ref, n1_ref, wz_ref, wxbc_ref, wqk_ref, wv_ref, wo_ref, wg_ref,
                  scw_ref, scb_ref, dtb_ref, alog_ref, dskip_ref, snw_ref,
                  mcw_ref, mcb_ref, gb_ref, mnw_ref, wout_ref, n2_ref, wr_ref,
                  e16_ref, e2_ref, tril_ref, seq_ref, sge_ref, meq_ref, mge_ref,
                  x1_ref, h2_ref, lg_ref,
                  xbc_buf, qk_buf, z_buf, v_buf, o_buf, g_buf, xbc_c, qk_c, mix_buf,
                  xbc_tail, qk_tail, sst, mst, mm, h_buf, *, tl, nl, slot_a):
    nchunk = tl // CHUNK
    step = pl.program_id(0)
    slot_b = 1 - slot_a
    slot_c = slot_a

    @pl.when(step == 0)
    def _():
        xbc_buf[...] = jnp.zeros_like(xbc_buf)
        qk_buf[...] = jnp.zeros_like(qk_buf)
        z_buf[...] = jnp.zeros_like(z_buf)
        v_buf[...] = jnp.zeros_like(v_buf)
        o_buf[...] = jnp.zeros_like(o_buf)
        g_buf[...] = jnp.zeros_like(g_buf)
        mix_buf[...] = jnp.zeros_like(mix_buf)
        xbc_tail[...] = jnp.zeros_like(xbc_tail)
        qk_tail[...] = jnp.zeros_like(qk_tail)
        sst[...] = jnp.zeros_like(sst)
        mst[...] = jnp.zeros_like(mst)
        mm[...] = jnp.full(mm.shape, STAB_INIT, F32)

    h_buf[...] = _rms(x_ref[...], n1_ref[...]).astype(BF16)
    nblk = 2 * LANES
    tasks = []

    def proj_task(w_ref, store):
        for b0 in range(0, w_ref.shape[1], nblk):
            tasks.append(lambda b0=b0: store(slice(b0, b0 + nblk), _dot(h_buf[...], w_ref[:, b0:b0 + nblk])))

    def out_task(cols):
        x1_ref[:, cols] = xc_ref[:, cols] + _dot(mix_buf[slot_c], wout_ref[:, cols])

    def route_task():
        h2 = _rms(x1_ref[...], n2_ref[...])
        h2 = h2.astype(BF16)
        h2_ref[...] = h2
        lg_ref[...] = _dot_nt(wr_ref[...], h2)

    for b0 in range(0, D_MODEL, nblk):
        tasks.append(lambda b0=b0: out_task(slice(b0, b0 + nblk)))
    tasks.append(route_task)

    def put(buf, r0=0):
        def store(cols, val):
            buf[slot_a, r0:r0 + tl, cols] = val
        return store

    def put_tiles(buf):
        def store(cols, val):
            for k in range(nblk // LANES):
                buf[slot_a, cols.start // LANES + k, CONV_PAD:CONV_PAD + tl, :] = val[:, k * LANES:(k + 1) * LANES]
        return store

    proj_task(wz_ref, put(z_buf))
    proj_task(wg_ref, put(g_buf))
    proj_task(wxbc_ref, put_tiles(xbc_buf))
    proj_task(wqk_ref, put_tiles(qk_buf))
    proj_task(wv_ref, put(v_buf))
    proj_task(wo_ref, put(o_buf))
    per_chunk = -(-len(tasks) // nchunk)

    def run_task(c, k):
        idx = c * per_chunk + k
        if idx < len(tasks):
            tasks[idx]()

    seq_start = (step + nl - 1) % nl == 0
    xbc_buf[slot_b, :, 0:CONV_PAD, :] = jnp.where(seq_start, 0.0, xbc_tail[...])
    qk_buf[slot_b, :, 0:CONV_PAD, :] = jnp.where(seq_start, 0.0, qk_tail[...])
    sst[...] = jnp.where(seq_start, 0.0, sst[...])
    mst[...] = jnp.where(seq_start, 0.0, mst[...])
    mm[...] = jnp.where(seq_start, STAB_INIT, mm[...])

    first = CONV_PAD - (SSD_CONV - 1)
    xbc_tail[...] = xbc_buf[slot_b, :, tl:tl + CONV_PAD, :]
    qk_tail[...] = qk_buf[slot_b, :, tl:tl + CONV_PAD, :]

    def conv_silu(buf, w_ref, b_ref, out, r0):
        for t in range(buf.shape[1]):
            cols = slice(t * LANES, (t + 1) * LANES)
            acc = b_ref[:, cols] + w_ref[0:1, cols] * buf[slot_b, t, r0 + first:r0 + first + CHUNK, :]
            for j in range(1, w_ref.shape[0]):
                acc = acc + w_ref[j:j + 1, cols] * buf[slot_b, t, r0 + first + j:r0 + first + j + CHUNK, :]
            out[r0:r0 + CHUNK, cols] = _silu(acc)

    def convs(c):
        run_task(c, 0)
        conv_silu(xbc_buf, scw_ref, scb_ref, xbc_c, c * CHUNK)
        run_task(c, 1)
        conv_silu(qk_buf, mcw_ref, mcb_ref, qk_c, c * CHUNK)
        run_task(c, 2)

    for c in range(nchunk):
        convs(c)
        rows = slice(c * CHUNK, (c + 1) * CHUNK)
        tril3 = tril_ref[...]

        xs = xbc_c[rows, 0:SSD_WIDTH]
        bm = xbc_c[rows, SSD_WIDTH:SSD_WIDTH + 2 * SSD_STATE].astype(BF16)
        cm = xbc_c[rows, SSD_WIDTH + 2 * SSD_STATE:SSD_XBC].astype(BF16)
        dt = _softplus(g_buf[slot_b, rows, 0:SSD_HEADS] + dtb_ref[...])
        a_cs = _dot(tril3, _split3(dt * (-jnp.exp(alog_ref[...])), 0))
        dt_e = _dot(_split3(dt, 1), e16_ref[...])
        a_col = _dot(_split3(a_cs, 1), e16_ref[...])
        a_row = jnp.sum(a_col * seq_ref[...], axis=0, keepdims=True)
        lmat = jnp.exp(a_col - a_row + sge_ref[...])
        cb = jnp.concatenate(
            [_dot_nt(cm[:, g * SSD_STATE:(g + 1) * SSD_STATE],
                     jnp.concatenate([bm[:, g * SSD_STATE:(g + 1) * SSD_STATE]] * (SSD_HEADS // SSD_GROUPS), axis=0))
             for g in range(SSD_GROUPS)], axis=1)
        m_all = (cb * lmat).astype(BF16)
        run_task(c, 3)
        xdt = xs * dt_e
        lane = lax.broadcasted_iota(jnp.int32, (CHUNK, LANES), 1)
        y_parts = []
        for j in range(SSD_HEADS // 2):
            xp = xdt[:, j * LANES:(j + 1) * LANES]
            xbd = jnp.concatenate([jnp.where(lane < SSD_HEAD_DIM, xp, 0.0),
                                   jnp.where(lane >= SSD_HEAD_DIM, xp, 0.0)], axis=0).astype(BF16)
            y_parts.append(_dot(m_all[:, j * LANES:(j + 1) * LANES], xbd))
        y = jnp.concatenate(y_parts, axis=1)
        half = SSD_WIDTH // SSD_GROUPS
        y_int = jnp.concatenate(
            [_dot(cm[:, g * SSD_STATE:(g + 1) * SSD_STATE], sst[g].astype(BF16)) for g in range(SSD_GROUPS)], axis=1)
        y = y + y_int * jnp.exp(a_col)
        a_last = a_col[CHUNK - 1:CHUNK, :]
        xd_b = (xdt * jnp.exp(a_last - a_col)).astype(BF16)
        st_scale = jnp.exp(a_last)
        for g in range(SSD_GROUPS):
            sst[g] = sst[g] * st_scale[:, g * half:(g + 1) * half] + _dot_tn(
                bm[:, g * SSD_STATE:(g + 1) * SSD_STATE], xd_b[:, g * half:(g + 1) * half])
        y = (y + dskip_ref[...] * xs) * _silu(z_buf[slot_b, rows, :])
        for g in range(SSD_GROUPS):
            yg = y[:, g * half:(g + 1) * half]
            yg = yg * lax.rsqrt(jnp.mean(yg * yg, axis=-1, keepdims=True) + EPS) * snw_ref[:, g * half:(g + 1) * half]
            mix_buf[slot_b, rows, g * half:(g + 1) * half] = yg.astype(BF16)

        run_task(c, 4)

        gi = g_buf[slot_b, rows, LANES:LANES + 2 * ML_HEADS] + gb_ref[...]
        col16 = lax.broadcasted_iota(jnp.int32, (CHUNK, 2 * ML_HEADS), 1)
        gi = jnp.where(col16 < ML_HEADS, gi, jnp.minimum(gi, 0.0) - jnp.log1p(jnp.exp(-jnp.abs(gi))))
        gi = jnp.where(col16 < ML_HEADS, gi, _dot(tril3, _split3(gi, 0)))
        gi_e = _dot(_split3(gi, 1), e2_ref[...])
        ig_e = gi_e[:, 0:ML_WIDTH]
        b_col = gi_e[:, ML_WIDTH:2 * ML_WIDTH]
        c_row = jnp.sum((ig_e - b_col) * meq_ref[...], axis=0, keepdims=True)
        dmat = b_col + c_row + mge_ref[...]
        heads = range(ML_HEADS)
        hs = lambda arr, hd: arr[:, hd * LANES:(hd + 1) * LANES]
        ones_blk = jnp.ones((CHUNK, LANES), BF16)
        m_prev = mm[...]
        m_inter = b_col + m_prev
        q = [qk_c[rows, hd * LANES:(hd + 1) * LANES].astype(BF16) for hd in heads]
        kf = [qk_c[rows, ML_WIDTH + hd * LANES:ML_WIDTH + (hd + 1) * LANES] * (ML_HEAD_DIM ** -0.5) for hd in heads]
        v_aug = [jnp.concatenate([v_buf[slot_b, rows, hd * LANES:(hd + 1) * LANES].astype(BF16), ones_blk], axis=1)
                 for hd in heads]
        s = [_dot_nt(q[hd], kf[hd].astype(BF16)) for hd in heads]
        qc = [_dot(q[hd], mst[hd].astype(BF16)) for hd in heads]
        run_task(c, 5)
        m_t = jnp.maximum(m_inter, jnp.concatenate(
            [jnp.broadcast_to(jnp.max(hs(dmat, hd), axis=-1, keepdims=True), (CHUNK, LANES)) for hd in heads], axis=1))
        w = jnp.exp(dmat - m_t)
        inter = jnp.exp(m_inter - m_t)
        den_floor = jnp.exp(-m_t)
        nd = [_dot((s[hd] * hs(w, hd)[:, 0:CHUNK]).astype(BF16), v_aug[hd])
              + qc[hd] * jnp.concatenate([hs(inter, hd)] * 2, axis=1) for hd in heads]
        run_task(c, 6)
        hh = [nd[hd][:, 0:LANES] / jnp.maximum(jnp.abs(nd[hd][:, LANES:2 * LANES]), hs(den_floor, hd)) for hd in heads]
        hc = [hh[hd] - jnp.mean(hh[hd], axis=-1, keepdims=True) for hd in heads]
        var = [jnp.mean(hc[hd] * hc[hd], axis=-1, keepdims=True) for hd in heads]
        for hd in heads:
            hn = hc[hd] * lax.rsqrt(var[hd] + EPS) * mnw_ref[:, hd * LANES:(hd + 1) * LANES]
            mix_buf[slot_b, rows, SSD_WIDTH + hd * LANES:SSD_WIDTH + (hd + 1) * LANES] = (
                jax.nn.sigmoid(o_buf[slot_b, rows, hd * LANES:(hd + 1) * LANES]) * hn).astype(BF16)
        run_task(c, 7)
        g_row = b_col[CHUNK - 1:CHUNK, :]
        a = g_row - b_col + ig_e
        m_new = jnp.maximum(g_row + m_prev, jnp.max(a, axis=0, keepdims=True))
        wk = jnp.exp(a - m_new)
        cs = jnp.exp(g_row + m_prev - m_new)
        for hd in heads:
            mst[hd] = mst[hd] * jnp.concatenate([hs(cs, hd)] * 2, axis=1) + _dot_tn(
                (kf[hd] * hs(wk, hd)).astype(BF16), v_aug[hd])
        mm[...] = m_new


def _const_spec(shape):
    nd = len(shape)
    return pl.BlockSpec(shape, lambda s: (0,) * nd, pipeline_mode=pl.Buffered(1))


def _mixer_call(x, consts, tl):
    bsz, seq, _ = x.shape
    nl = seq // tl
    n_tiles = bsz * nl
    x_tiles = x.reshape(n_tiles, tl, D_MODEL)
    tile_in = lambda s: (jnp.minimum(s, n_tiles - 1), 0, 0)
    tile_out = lambda s: (jnp.maximum(s - 2, 0), 0, 0)
    in_specs = ([pl.BlockSpec((None, tl, D_MODEL), tile_in), pl.BlockSpec((None, tl, D_MODEL), tile_out)]
                + [_const_spec(c.shape) for c in consts])
    out_specs = [pl.BlockSpec((None, tl, D_MODEL), tile_out), pl.BlockSpec((None, tl, D_MODEL), tile_out),
                 pl.BlockSpec((LOGIT_ROWS, tl), lambda s: (0, jnp.maximum(s - 2, 0)))]
    out_shape = [jax.ShapeDtypeStruct((n_tiles, tl, D_MODEL), F32),
                 jax.ShapeDtypeStruct((n_tiles, tl, D_MODEL), BF16),
                 jax.ShapeDtypeStruct((LOGIT_ROWS, bsz * seq), F32)]
    scratch = [
        pltpu.VMEM((2, SSD_XBC // LANES, tl + CONV_PAD, LANES), F32),
        pltpu.VMEM((2, 2 * ML_WIDTH // LANES, tl + CONV_PAD, LANES), F32),
        pltpu.VMEM((2, tl, SSD_WIDTH), F32),
        pltpu.VMEM((2, tl, ML_WIDTH), F32),
        pltpu.VMEM((2, tl, ML_WIDTH), F32),
        pltpu.VMEM((2, tl, GATE_COLS), F32),
        pltpu.VMEM((tl, SSD_XBC), F32),
        pltpu.VMEM((tl, 2 * ML_WIDTH), F32),
        pltpu.VMEM((2, tl, D_MIX), BF16),
        pltpu.VMEM((SSD_XBC // LANES, CONV_PAD, LANES), F32),
        pltpu.VMEM((2 * ML_WIDTH // LANES, CONV_PAD, LANES), F32),
        pltpu.VMEM((SSD_GROUPS, SSD_STATE, SSD_WIDTH // SSD_GROUPS), F32),
        pltpu.VMEM((ML_HEADS, ML_HEAD_DIM, 2 * LANES), F32),
        pltpu.VMEM((1, ML_WIDTH), F32),
        pltpu.VMEM((tl, D_MODEL), BF16),
    ]
    return pl.pallas_call(
        functools.partial(_mixer_kernel, tl=tl, nl=nl),
        grid=(n_tiles + 2,),
        in_specs=in_specs,
        out_specs=out_specs,
        out_shape=out_shape,
        scratch_shapes=scratch,
        compiler_params=pltpu.CompilerParams(
            dimension_semantics=("arbitrary",), vmem_limit_bytes=V7X_VMEM_LIMIT),
        name="mixer",
    )(x_tiles, x_tiles, *consts)


def _mixer_consts(norm1_w, w_in, ssd_conv_w, ssd_conv_b, ssd_dt_bias, ssd_a_log, ssd_d, ssd_norm_w,
                  ml_conv_w, ml_conv_b, ml_i_bias, ml_f_bias, ml_norm_w, w_out, norm2_w,
                  router_g_w, router_e_w):
    o1 = SSD_WIDTH
    o2 = o1 + SSD_XBC
    o3 = o2 + SSD_HEADS
    o4 = o3 + 2 * ML_WIDTH
    o5 = o4 + ML_WIDTH
    o6 = o5 + ML_WIDTH
    o7 = o6 + ML_HEADS
    zpad = lambda n: jnp.zeros((D_MODEL, n), F32)
    w_gate = jnp.concatenate([w_in[:, o2:o3], zpad(LANES - SSD_HEADS),
                              w_in[:, o6:o7], w_in[:, o7:], zpad(LANES - 2 * ML_HEADS)], axis=1)
    row = lambda v: v.reshape(1, -1).astype(F32)
    w_route = jnp.concatenate([router_e_w, router_g_w,
                               jnp.zeros((D_MODEL, LOGIT_ROWS - N_EXPERTS - MOE_GROUPS), F32)], axis=1).T

    c1024 = np.arange(SSD_WIDTH)
    r64 = np.arange(CHUNK)[:, None]
    e16 = np.tile(c1024[None, :] // SSD_HEAD_DIM == np.arange(SSD_HEADS)[:, None], (3, 1))
    e2 = np.tile(np.arange(2 * ML_WIDTH)[None, :] // LANES == np.arange(2 * ML_HEADS)[:, None], (3, 1))
    tril = np.tile(np.arange(CHUNK)[None, :] <= r64, (1, 3))
    s_pos = (c1024 % SSD_HEAD_DIM)[None, :]
    m_pos = (c1024 % LANES)[None, :]
    as_bf16 = lambda m: jnp.asarray(m.astype(np.float32), BF16)
    as_f32 = lambda m: jnp.asarray(m.astype(np.float32))
    e16, e2, tril = as_bf16(e16), as_bf16(e2), as_bf16(tril)
    causal = lambda keep: jnp.asarray(np.where(keep, 0.0, -np.inf).astype(np.float32))
    seq, sge, meq, mge = as_f32(r64 == s_pos), causal(r64 >= s_pos), as_f32(r64 == m_pos), causal(r64 >= m_pos)
    return [
        row(norm1_w),
        w_in[:, :o1].astype(BF16), w_in[:, o1:o2].astype(BF16), w_in[:, o3:o4].astype(BF16),
        w_in[:, o4:o5].astype(BF16), w_in[:, o5:o6].astype(BF16), w_gate.astype(BF16),
        ssd_conv_w.astype(F32), row(ssd_conv_b), row(ssd_dt_bias),
        row(ssd_a_log), row(jnp.repeat(ssd_d, SSD_HEAD_DIM)), row(ssd_norm_w),
        ml_conv_w.astype(F32), row(ml_conv_b), row(jnp.concatenate([ml_i_bias, ml_f_bias])), row(ml_norm_w),
        w_out.astype(BF16), row(norm2_w), w_route.astype(BF16),
        e16, e2, tril, seq, sge, meq, mge,
    ]


def _route_kernel(lg_ref, bias_ref, striu_ref, lpos_ref, gate_ref, cnt_ref):
    tr = lg_ref.shape[1]
    lg = lg_ref[...] + bias_ref[...]
    el = lg[0:N_EXPERTS]
    gl = lg[N_EXPERTS:N_EXPERTS + 8]
    gmax = jnp.max(gl, axis=0, keepdims=True)
    pg = 1.0 / jnp.sum(jnp.exp(gl - gmax), axis=0, keepdims=True)
    grow = lax.broadcasted_iota(jnp.int32, gl.shape, 0)
    gsel = jnp.min(jnp.where(gl == gmax, grow, 8), axis=0, keepdims=True)
    erow = lax.broadcasted_iota(jnp.int32, el.shape, 0)
    m1 = jnp.where((erow >> 3) == gsel, el, NEG_INF)
    v1 = jnp.max(m1, axis=0, keepdims=True)
    i1 = jnp.min(jnp.where(m1 == v1, erow, N_EXPERTS), axis=0, keepdims=True)
    m2 = jnp.where(erow == i1, NEG_INF, m1)
    v2 = jnp.max(m2, axis=0, keepdims=True)
    i2 = jnp.min(jnp.where(m2 == v2, erow, N_EXPERTS), axis=0, keepdims=True)
    e2 = jnp.exp(v2 - v1)
    g1 = pg / (1.0 + e2)
    g2 = g1 * e2
    hit1 = erow == i1
    hit2 = erow == i2
    oh = jnp.where(hit1, 1.0, jnp.where(hit2, 1.0, 0.0))
    earlier = _dot(oh.astype(BF16), striu_ref[...])
    cnt = jnp.sum(oh, axis=1, keepdims=True)
    seg = jnp.floor((cnt + (ROW_ALIGN - 1)) * (1.0 / ROW_ALIGN)) * ROW_ALIGN
    l1 = jnp.sum(jnp.where(hit1, earlier, jnp.where(erow < i1, seg, 0.0)), axis=0, keepdims=True)
    l2 = jnp.sum(jnp.where(hit2, earlier, jnp.where(erow < i2, seg, 0.0)), axis=0, keepdims=True)
    lpos_ref[...] = jnp.concatenate([l1.astype(jnp.int32), l2.astype(jnp.int32), jnp.zeros((6, tr), jnp.int32)], axis=0)
    gate_ref[...] = jnp.concatenate([g1, g2, jnp.zeros((6, tr), F32)], axis=0)
    cnt_ref[...] = jnp.broadcast_to(cnt, cnt_ref.shape)


def _route_call(lg_t, bias_col, tr):
    t = lg_t.shape[1]
    striu = jnp.asarray(np.triu(np.ones((tr, tr), np.float32), 1), BF16)
    return pl.pallas_call(
        _route_kernel,
        grid=(t // tr,),
        in_specs=[pl.BlockSpec((LOGIT_ROWS, tr), lambda i: (0, i)),
                  pl.BlockSpec((LOGIT_ROWS, 1), lambda i: (0, 0)),
                  pl.BlockSpec((tr, tr), lambda i: (0, 0))],
        out_specs=[pl.BlockSpec((8, tr), lambda i: (0, i)),
                   pl.BlockSpec((8, tr), lambda i: (0, i)),
                   pl.BlockSpec((None, N_EXPERTS, LANES), lambda i: (i, 0, 0))],
        out_shape=[jax.ShapeDtypeStruct((8, t), jnp.int32),
                   jax.ShapeDtypeStruct((8, t), F32),
                   jax.ShapeDtypeStruct((t // tr, N_EXPERTS, LANES), F32)],
        compiler_params=pltpu.CompilerParams(dimension_semantics=("arbitrary",)),
        name="route",
    )(lg_t, bias_col, striu)


def _run_copies(runs_ref, copy_run, max_piece, filler_far, enable=None):
    for e in range(N_EXPERTS + 1):
        dst = runs_ref[0, e] if e < N_EXPERTS else filler_far
        src = runs_ref[1, e]
        n = runs_ref[2, e] if enable is None else jnp.where(enable, runs_ref[2, e], 0)
        p = max_piece
        while p >= ROW_ALIGN:
            done = n & ~(2 * p - 1)

            @pl.when((n & p) != 0)
            def _(p=p, done=done, dst=dst, src=src):
                copy_run(pl.multiple_of(dst + done, ROW_ALIGN), pl.multiple_of(src + done, ROW_ALIGN), p)
            p //= 2


def _by_parity(body):
    def kernel(*refs, **kw):
        for parity in range(2):
            @pl.when(pl.program_id(0) % 2 == parity)
            def _(parity=parity):
                body(*refs, slot=parity, **kw)
    return kernel


def _dispatch_step(runs_ref, prev_runs_ref, lpos_ref, h2_ref, xs_ref, loc, sem, *, max_spare, slot):
    tr = h2_ref.shape[0]
    nloc = loc.shape[1]
    step = pl.program_id(0)
    last = pl.num_programs(0) - 1
    used = runs_ref[0, N_EXPERTS]

    def wait_block(s):
        pltpu.make_async_copy(loc.at[s], xs_ref.at[pl.ds(0, nloc)], sem.at[s]).wait()

    def send_block(table, s, enable=None):
        def copy_run(dst, src, n):
            pltpu.make_async_copy(loc.at[s, pl.ds(src, n)], xs_ref.at[pl.ds(dst, n)], sem.at[s]).start()
        _run_copies(table, copy_run, tr, used + s * (nloc - TOP_K * tr), enable)

    @pl.when(step > 1)
    def _():
        wait_block(slot)

    send_block(prev_runs_ref, 1 - slot, enable=step > 0)

    jrow = lax.broadcasted_iota(jnp.int32, (nloc, tr), 0)
    perm = jnp.where(jrow == lpos_ref[0:1, :], 1.0, jnp.where(jrow == lpos_ref[1:2, :], 1.0, 0.0)).astype(BF16)
    loc[slot] = _pack_rows(_dot(perm, h2_ref[...]), rounded=True)

    @pl.when(step == last)
    def _():
        send_block(runs_ref, slot)

        @pl.when(step > 0)
        def _():
            wait_block(1 - slot)
        wait_block(slot)
        loc[slot, 0:tr, :] = jnp.zeros((tr, loc.shape[2]), loc.dtype)
        spare = xs_ref.shape[0] - used

        def zero_copy(off, n):
            return pltpu.make_async_copy(loc.at[slot, pl.ds(0, n)],
                                         xs_ref.at[pl.ds(pl.multiple_of(used + off, ROW_ALIGN), n)], sem.at[slot])

        def pieces(do):
            for i in range(max_spare // tr):
                @pl.when((i + 1) * tr <= spare)
                def _(i=i):
                    do(zero_copy(i * tr, tr))
            p = tr // 2
            while p >= ROW_ALIGN:
                @pl.when((spare & p) != 0)
                def _(p=p):
                    do(zero_copy(spare & ~(2 * p - 1), p))
                p //= 2

        pieces(lambda c: c.start())
        pieces(lambda c: c.wait())


def _dispatch_call(runs, lpos, h2, tr, n_rows):
    t = h2.shape[0]
    return pl.pallas_call(
        functools.partial(_by_parity(_dispatch_step), max_spare=n_rows - t * TOP_K),
        grid=(t // tr,),
        in_specs=[pl.BlockSpec((None, 3, N_EXPERTS + 1), lambda i: (i, 0, 0), memory_space=pltpu.SMEM),
                  pl.BlockSpec((None, 3, N_EXPERTS + 1), lambda i: (jnp.maximum(i - 1, 0), 0, 0), memory_space=pltpu.SMEM),
                  pl.BlockSpec((8, tr), lambda i: (0, i)),
                  pl.BlockSpec((tr, D_MODEL), lambda i: (i, 0))],
        out_specs=pl.BlockSpec(memory_space=pl.ANY),
        out_shape=jax.ShapeDtypeStruct((n_rows, D_MODEL // 2), jnp.uint32),
        scratch_shapes=[pltpu.VMEM((2, _local_rows(tr), D_MODEL // 2), jnp.uint32), pltpu.SemaphoreType.DMA((2,))],
        compiler_params=pltpu.CompilerParams(dimension_semantics=("arbitrary",), vmem_limit_bytes=V7X_VMEM_LIMIT),
        name="dispatch",
    )(runs, runs, lpos, h2)


def _experts_kernel(tile_ref, exp_ref, starts_ref, xs_ref, wg_ref, wu_ref, wd_ref, ys_ref, wg_b, wu_b, wd_b):
    i = pl.program_id(0)
    tm = xs_ref.shape[0]
    e = exp_ref[i]
    tile = tile_ref[i]
    is_first = jnp.logical_or(i == 0, tile_ref[jnp.maximum(i - 1, 0)] != tile)

    @pl.when(jnp.logical_and(e < N_EXPERTS, jnp.logical_or(i == 0, exp_ref[jnp.maximum(i - 1, 0)] != e)))
    def _():
        wg_b[...] = wg_ref[...].astype(BF16)
        wu_b[...] = wu_ref[...].astype(BF16)
        wd_b[...] = wd_ref[...].astype(BF16)

    def expert_rows(other):
        x = _unpack_rows(xs_ref[...])
        act = (_silu(_dot(x, wg_b[...])) * _dot(x, wu_b[...])).astype(BF16)
        y = _pack_rows(_dot(act, wd_b[...]), rounded=False)
        rows = tile * tm + lax.broadcasted_iota(jnp.int32, (tm, 1), 0)
        return jnp.where((rows >= starts_ref[e]) & (rows < starts_ref[e + 1]), y, other)

    @pl.when(jnp.logical_and(e < N_EXPERTS, is_first))
    def _():
        ys_ref[...] = expert_rows(jnp.zeros_like(ys_ref))

    @pl.when(jnp.logical_and(e < N_EXPERTS, jnp.logical_not(is_first)))
    def _():
        ys_ref[...] = expert_rows(ys_ref[...])

    @pl.when(e == N_EXPERTS)
    def _():
        ys_ref[...] = jnp.zeros_like(ys_ref)


def _experts_call(item_tile, item_exp, starts_ext, xs, wg, wu, wd, tm):
    n = xs.shape[0]
    n_items = item_tile.shape[0]
    w_idx = lambda i, tile, ex, st: (jnp.minimum(ex[i], N_EXPERTS - 1), 0, 0)
    grid_spec = pltpu.PrefetchScalarGridSpec(
        num_scalar_prefetch=3,
        grid=(n_items,),
        in_specs=[pl.BlockSpec((tm, D_MODEL // 2), lambda i, tile, ex, st: (tile[i], 0)),
                  pl.BlockSpec((None, D_MODEL, D_EXPERT), w_idx),
                  pl.BlockSpec((None, D_MODEL, D_EXPERT), w_idx),
                  pl.BlockSpec((None, D_EXPERT, D_MODEL), w_idx)],
        out_specs=pl.BlockSpec((tm, D_MODEL // 2), lambda i, tile, ex, st: (tile[i], 0)),
        scratch_shapes=[pltpu.VMEM((D_MODEL, D_EXPERT), BF16), pltpu.VMEM((D_MODEL, D_EXPERT), BF16),
                        pltpu.VMEM((D_EXPERT, D_MODEL), BF16)],
    )
    return pl.pallas_call(
        _experts_kernel,
        grid_spec=grid_spec,
        out_shape=jax.ShapeDtypeStruct((n, D_MODEL // 2), jnp.uint32),
        compiler_params=pltpu.CompilerParams(
            dimension_semantics=("arbitrary",), vmem_limit_bytes=V7X_VMEM_LIMIT),
        name="experts",
    )(item_tile, item_exp, starts_ext, xs, wg, wu, wd)


def _local_rows(tr):
    return TOP_K * tr + N_EXPERTS * ROW_ALIGN


def _expert_items(counts, n_rows, tm):
    n_tiles = n_rows // tm
    n_items = n_tiles + N_EXPERTS - 1
    starts = jnp.concatenate([jnp.zeros((1,), jnp.int32), jnp.cumsum(counts).astype(jnp.int32)])
    lo = starts[:-1] // tm
    hi = (starts[1:] - 1) // tm
    per = jnp.where(counts > 0, hi - lo + 1, 0)
    cum = jnp.cumsum(per)
    idx = jnp.arange(n_items, dtype=jnp.int32)
    e = jnp.sum(idx[:, None] >= cum[None, :], axis=1).astype(jnp.int32)
    valid = idx < cum[-1]
    e_c = jnp.minimum(e, N_EXPERTS - 1)
    first_item = jnp.sum(jnp.where(e_c[:, None] == jnp.arange(N_EXPERTS)[None, :], (lo - (cum - per))[None, :], 0), axis=1)
    tile = first_item + idx
    spare_tile = -(-starts[-1] // tm) + idx - cum[-1]
    item_tile = jnp.where(valid, tile, jnp.minimum(spare_tile, n_tiles - 1)).astype(jnp.int32)
    item_exp = jnp.where(valid, e_c, jnp.where(spare_tile < n_tiles, N_EXPERTS, N_EXPERTS + 1)).astype(jnp.int32)
    starts_ext = jnp.concatenate([starts, starts[-1:], starts[-1:]])
    return item_tile, item_exp, starts, starts_ext


def _combine_step(runs_ref, next_runs_ref, lpos_ref, gate_ref, x1_ref, nf_ref, ys_ref, out_ref, loc, sem, *, slot):
    tr = x1_ref.shape[0]
    nloc = loc.shape[1]
    step = pl.program_id(0)

    def gather(table, s, enable=None):
        def copy_run(dst, src, n):
            pltpu.make_async_copy(ys_ref.at[pl.ds(dst, n)], loc.at[s, pl.ds(src, n)], sem.at[s]).start()
        _run_copies(table, copy_run, tr, 0, enable)

    @pl.when(step == 0)
    def _():
        gather(runs_ref, slot)

    jrow = lax.broadcasted_iota(jnp.int32, (nloc, tr), 0)
    sel = jnp.where(jrow == lpos_ref[0:1, :], gate_ref[0:1, :],
                    jnp.where(jrow == lpos_ref[1:2, :], gate_ref[1:2, :], 0.0)).astype(BF16)
    pltpu.make_async_copy(ys_ref.at[pl.ds(0, nloc)], loc.at[slot], sem.at[slot]).wait()

    gather(next_runs_ref, 1 - slot, enable=step + 1 < pl.num_programs(0))

    y = _dot_tn(sel, _unpack_rows(loc[slot]))
    out_ref[...] = _rms(x1_ref[...] + y, nf_ref[...])


def _combine_call(runs, lpos, gates, x1, nf_w, ys, tr):
    t = x1.shape[0]
    return pl.pallas_call(
        _by_parity(_combine_step),
        grid=(t // tr,),
        in_specs=[pl.BlockSpec((None, 3, N_EXPERTS + 1), lambda i: (i, 0, 0), memory_space=pltpu.SMEM),
                  pl.BlockSpec((None, 3, N_EXPERTS + 1), lambda i: (jnp.minimum(i + 1, t // tr - 1), 0, 0),
                               memory_space=pltpu.SMEM),
                  pl.BlockSpec((8, tr), lambda i: (0, i)),
                  pl.BlockSpec((8, tr), lambda i: (0, i)),
                  pl.BlockSpec((tr, D_MODEL), lambda i: (i, 0)),
                  pl.BlockSpec((1, D_MODEL), lambda i: (0, 0)),
                  pl.BlockSpec(memory_space=pl.ANY)],
        out_specs=pl.BlockSpec((tr, D_MODEL), lambda i: (i, 0)),
        out_shape=jax.ShapeDtypeStruct((t, D_MODEL), F32),
        scratch_shapes=[pltpu.VMEM((2, _local_rows(tr), D_MODEL // 2), jnp.uint32), pltpu.SemaphoreType.DMA((2,))],
        compiler_params=pltpu.CompilerParams(dimension_semantics=("arbitrary",), vmem_limit_bytes=V7X_VMEM_LIMIT),
        name="combine",
    )(runs, runs, lpos, gates, x1, nf_w, ys)


SEQ_TILE = 4 * CHUNK
TOKEN_TILE = 512
ROW_TILE = 512


def kernel(x, norm1_w, w_in, ssd_conv_w, ssd_conv_b, ssd_dt_bias, ssd_a_log, ssd_d, ssd_norm_w, ml_conv_w, ml_conv_b, ml_i_bias, ml_f_bias, ml_norm_w, w_out, norm2_w, router_g_w, router_g_b, router_e_w, router_e_b, exp_w_gate, exp_w_up, exp_w_down, norm_f_w):
    bsz, seq, d = x.shape
    t = bsz * seq
    tl, tr, tm = SEQ_TILE, TOKEN_TILE, ROW_TILE
    assert d == D_MODEL and norm1_w.shape[0] == 1 and seq % tl == 0 and t % tr == 0

    consts = _mixer_consts(norm1_w[0], w_in[0], ssd_conv_w[0], ssd_conv_b[0], ssd_dt_bias[0], ssd_a_log[0],
                           ssd_d[0], ssd_norm_w[0], ml_conv_w[0], ml_conv_b[0], ml_i_bias[0], ml_f_bias[0],
                           ml_norm_w[0], w_out[0], norm2_w[0], router_g_w[0], router_e_w[0])
    x1, h2, lg_t = _mixer_call(x, consts, tl)
    x1 = x1.reshape(t, D_MODEL)
    h2 = h2.reshape(t, D_MODEL)

    bias_col = jnp.concatenate([router_e_b[0], router_g_b[0],
                                jnp.full((LOGIT_ROWS - N_EXPERTS - MOE_GROUPS,), STAB_INIT, F32)]).reshape(-1, 1)
    lpos, gates, cnt = _route_call(lg_t, bias_col.astype(F32), tr)

    seg = (cnt[:, :, 0].astype(jnp.int32) + (ROW_ALIGN - 1)) // ROW_ALIGN * ROW_ALIGN
    n_tok_tiles = t // tr
    n_rows = -(-(t * TOP_K + n_tok_tiles * N_EXPERTS * ROW_ALIGN + _local_rows(tr)) // tm) * tm
    item_tile, item_exp, starts, starts_ext = _expert_items(jnp.sum(seg, axis=0), n_rows, tm)
    run_dst = starts[None, :N_EXPERTS] + jnp.cumsum(seg, axis=0) - seg
    run_src = jnp.cumsum(seg, axis=1) - seg
    used = jnp.sum(seg, axis=1, keepdims=True)
    fill = jnp.concatenate([jnp.full_like(used, starts[N_EXPERTS]), used, _local_rows(tr) - used], axis=1)[:, :, None]
    runs = jnp.concatenate([jnp.stack([run_dst, run_src, seg], axis=1), fill], axis=2).astype(jnp.int32)

    xs = _dispatch_call(runs, lpos, h2, tr, n_rows)
    ys = _experts_call(item_tile, item_exp, starts_ext, xs, exp_w_gate[0], exp_w_up[0], exp_w_down[0], tm)
    out = _combine_call(runs, lpos, gates, x1, norm_f_w.reshape(1, -1).astype(F32), ys, tr)
    return out.reshape(bsz, seq, D_MODEL)
```

```python
import functools

import jax
import jax.numpy as jnp
import numpy as np
from jax import lax
from jax.experimental import pallas as pl
from jax.experimental.pallas import tpu as pltpu

D_MODEL = 1024
CHUNK = 64
SSD_WIDTH = 1024
SSD_HEAD_DIM = 64
SSD_HEADS = 16
SSD_GROUPS = 2
SSD_STATE = 128
SSD_CONV = 4
SSD_XBC = SSD_WIDTH + 2 * SSD_GROUPS * SSD_STATE
ML_WIDTH = 1024
ML_HEADS = 8
ML_HEAD_DIM = 128
ML_CONV = 4
D_MIX = SSD_WIDTH + ML_WIDTH
MOE_GROUPS = 4
EXPERTS_PER_GROUP = 8
N_EXPERTS = 32
TOP_K = 2
D_EXPERT = 512
EPS = 1e-6
STAB_INIT = -1e30

LANES = 128
GATE_COLS = 2 * LANES
ROW_ALIGN = 8
CONV_PAD = 8
LOGIT_ROWS = 64
V7X_VMEM_LIMIT = 56 * 1024 * 1024

F32 = jnp.float32
BF16 = jnp.bfloat16
NEG_INF = float("-inf")


def _dot(a, b):
    return jnp.dot(a, b, preferred_element_type=F32)


def _dot_nt(a, b):
    return lax.dot_general(a, b, (((1,), (1,)), ((), ())), preferred_element_type=F32)


def _dot_tn(a, b):
    return lax.dot_general(a, b, (((0,), (0,)), ((), ())), preferred_element_type=F32)


def _split3(x, axis):
    hi = x.astype(BF16)
    r1 = x - hi.astype(F32)
    mid = r1.astype(BF16)
    lo = (r1 - mid.astype(F32)).astype(BF16)
    return jnp.concatenate([hi, mid, lo], axis=axis)


def _pack_rows(x, rounded):
    if not rounded:
        x = x.astype(BF16).astype(F32)
    bits = lax.bitcast_convert_type(x, jnp.uint32)
    half = x.shape[1] // 2
    return bits[:, half:] | (bits[:, :half] >> 16)


def _unpack_rows(p):
    lo = lax.bitcast_convert_type(p << 16, F32)
    hi = lax.bitcast_convert_type(p & jnp.uint32(0xFFFF0000), F32)
    return jnp.concatenate([lo, hi], axis=1).astype(BF16)


def _silu(x):
    return x * jax.nn.sigmoid(x)


def _softplus(x):
    return jnp.maximum(x, 0.0) + jnp.log1p(jnp.exp(-jnp.abs(x)))


def _rms(x, w):
    return x * lax.rsqrt(jnp.mean(x * x, axis=-1, keepdims=True) + EPS) * w


def _mixer_kernel(*refs, tl, nl):
    for parity in range(2):
        @pl.when(pl.program_id(0) % 2 == parity)
        def _(parity=parity):
            _mixer_step(*refs, tl=tl, nl=nl, slot_a=parity)


def _mixer_step(x_ref, xc_ref, n1_ref, wz_ref, wxbc_ref, wqk_ref, wv_ref, wo_ref, wg_ref,
                  scw_ref, scb_ref, dtb_ref, alog_ref, dskip_ref, snw_ref,
                  mcw_ref, mcb_ref, gb_ref, mnw_ref, wout_ref, n2_ref, wr_ref,
                  e16_ref, e2_ref, tril_ref, seq_ref, sge_ref, meq_ref, mge_ref,
                  x1_ref, h2_ref, lg_ref,
                  xbc_buf, qk_buf, z_buf, v_buf, o_buf, g_buf, xbc_c, qk_c, mix_buf,
                  xbc_tail, qk_tail, sst, mst, mm, h_buf, *, tl, nl, slot_a):
    nchunk = tl // CHUNK
    step = pl.program_id(0)
    slot_b = 1 - slot_a
    slot_c = slot_a

    @pl.when(step == 0)
    def _():
        xbc_buf[...] = jnp.zeros_like(xbc_buf)
        qk_buf[...] = jnp.zeros_like(qk_buf)
        z_buf[...] = jnp.zeros_like(z_buf)
        v_buf[...] = jnp.zeros_like(v_buf)
        o_buf[...] = jnp.zeros_like(o_buf)
        g_buf[...] = jnp.zeros_like(g_buf)
        mix_buf[...] = jnp.zeros_like(mix_buf)
        xbc_tail[...] = jnp.zeros_like(xbc_tail)
        qk_tail[...] = jnp.zeros_like(qk_tail)
        sst[...] = jnp.zeros_like(sst)
        mst[...] = jnp.zeros_like(mst)
        mm[...] = jnp.full(mm.shape, STAB_INIT, F32)

    h_buf[...] = _rms(x_ref[...], n1_ref[...]).astype(BF16)
    nblk = 2 * LANES
    tasks = []

    def proj_task(w_ref, store):
        for b0 in range(0, w_ref.shape[1], nblk):
            tasks.append(lambda b0=b0: store(slice(b0, b0 + nblk), _dot(h_buf[...], w_ref[:, b0:b0 + nblk])))

    def out_task(cols):
        x1_ref[:, cols] = xc_ref[:, cols] + _dot(mix_buf[slot_c], wout_ref[:, cols])

    def route_task():
        h2 = _rms(x1_ref[...], n2_ref[...])
        h2 = h2.astype(BF16)
        h2_ref[...] = h2
        lg_ref[...] = _dot_nt(wr_ref[...], h2)

    for b0 in range(0, D_MODEL, nblk):
        tasks.append(lambda b0=b0: out_task(slice(b0, b0 + nblk)))
    tasks.append(route_task)

    def put(buf, r0=0):
        def store(cols, val):
            buf[slot_a, r0:r0 + tl, cols] = val
        return store

    def put_tiles(buf):
        def store(cols, val):
            for k in range(nblk // LANES):
                buf[slot_a, cols.start // LANES + k, CONV_PAD:CONV_PAD + tl, :] = val[:, k * LANES:(k + 1) * LANES]
        return store

    proj_task(wz_ref, put(z_buf))
    proj_task(wg_ref, put(g_buf))
    proj_task(wxbc_ref, put_tiles(xbc_buf))
    proj_task(wqk_ref, put_tiles(qk_buf))
    proj_task(wv_ref, put(v_buf))
    proj_task(wo_ref, put(o_buf))
    per_chunk = -(-len(tasks) // nchunk)
    assert per_chunk <= 8

    def run_task(c, k):
        idx = c * per_chunk + k
        if idx < len(tasks):
            tasks[idx]()

    seq_start = (step + nl - 1) % nl == 0
    xbc_buf[slot_b, :, 0:CONV_PAD, :] = jnp.where(seq_start, 0.0, xbc_tail[...])
    qk_buf[slot_b, :, 0:CONV_PAD, :] = jnp.where(seq_start, 0.0, qk_tail[...])
    sst[...] = jnp.where(seq_start, 0.0, sst[...])
    mst[...] = jnp.where(seq_start, 0.0, mst[...])
    mm[...] = jnp.where(seq_start, STAB_INIT, mm[...])

    first = CONV_PAD - (SSD_CONV - 1)
    xbc_tail[...] = xbc_buf[slot_b, :, tl:tl + CONV_PAD, :]
    qk_tail[...] = qk_buf[slot_b, :, tl:tl + CONV_PAD, :]

    def conv_silu(buf, w_ref, b_ref, out, r0):
        for t in range(buf.shape[1]):
            cols = slice(t * LANES, (t + 1) * LANES)
            acc = b_ref[:, cols] + w_ref[0:1, cols] * buf[slot_b, t, r0 + first:r0 + first + CHUNK, :]
            for j in range(1, w_ref.shape[0]):
                acc = acc + w_ref[j:j + 1, cols] * buf[slot_b, t, r0 + first + j:r0 + first + j + CHUNK, :]
            out[r0:r0 + CHUNK, cols] = _silu(acc)

    def convs(c):
        run_task(c, 0)
        conv_silu(xbc_buf, scw_ref, scb_ref, xbc_c, c * CHUNK)
        run_task(c, 1)
        conv_silu(qk_buf, mcw_ref, mcb_ref, qk_c, c * CHUNK)
        run_task(c, 2)

    for c in range(nchunk):
        convs(c)
        rows = slice(c * CHUNK, (c + 1) * CHUNK)
        tril3 = tril_ref[...]

        xs = xbc_c[rows, 0:SSD_WIDTH]
        bm = xbc_c[rows, SSD_WIDTH:SSD_WIDTH + 2 * SSD_STATE].astype(BF16)
        cm = xbc_c[rows, SSD_WIDTH + 2 * SSD_STATE:SSD_XBC].astype(BF16)
        dt = _softplus(g_buf[slot_b, rows, 0:SSD_HEADS] + dtb_ref[...])
        a_cs = _dot(tril3, _split3(dt * (-jnp.exp(alog_ref[...])), 0))
        dt_e = _dot(_split3(dt, 1), e16_ref[...])
        a_col = _dot(_split3(a_cs, 1), e16_ref[...])
        a_row = jnp.sum(a_col * seq_ref[...], axis=0, keepdims=True)
        lmat = jnp.exp(a_col - a_row + sge_ref[...])
        cb = jnp.concatenate(
            [_dot_nt(cm[:, g * SSD_STATE:(g + 1) * SSD_STATE],
                     jnp.concatenate([bm[:, g * SSD_STATE:(g + 1) * SSD_STATE]] * (SSD_HEADS // SSD_GROUPS), axis=0))
             for g in range(SSD_GROUPS)], axis=1)
        m_all = (cb * lmat).astype(BF16)
        run_task(c, 3)
        xdt = xs * dt_e
        lane = lax.broadcasted_iota(jnp.int32, (CHUNK, LANES), 1)
        y_parts = []
        for j in range(SSD_HEADS // 2):
            xp = xdt[:, j * LANES:(j + 1) * LANES]
            xbd = jnp.concatenate([jnp.where(lane < SSD_HEAD_DIM, xp, 0.0),
                                   jnp.where(lane >= SSD_HEAD_DIM, xp, 0.0)], axis=0).astype(BF16)
            y_parts.append(_dot(m_all[:, j * LANES:(j + 1) * LANES], xbd))
        y = jnp.concatenate(y_parts, axis=1)
        half = SSD_WIDTH // SSD_GROUPS
        y_int = jnp.concatenate(
            [_dot(cm[:, g * SSD_STATE:(g + 1) * SSD_STATE], sst[g].astype(BF16)) for g in range(SSD_GROUPS)], axis=1)
        y = y + y_int * jnp.exp(a_col)
        a_last = a_col[CHUNK - 1:CHUNK, :]
        xd_b = (xdt * jnp.exp(a_last - a_col)).astype(BF16)
        st_scale = jnp.exp(a_last)
        for g in range(SSD_GROUPS):
            sst[g] = sst[g] * st_scale[:, g * half:(g + 1) * half] + _dot_tn(
                bm[:, g * SSD_STATE:(g + 1) * SSD_STATE], xd_b[:, g * half:(g + 1) * half])
        y = (y + dskip_ref[...] * xs) * _silu(z_buf[slot_b, rows, :])
        for g in range(SSD_GROUPS):
            yg = y[:, g * half:(g + 1) * half]
            yg = yg * lax.rsqrt(jnp.mean(yg * yg, axis=-1, keepdims=True) + EPS) * snw_ref[:, g * half:(g + 1) * half]
            mix_buf[slot_b, rows, g * half:(g + 1) * half] = yg.astype(BF16)

        run_task(c, 4)

        gi = g_buf[slot_b, rows, LANES:LANES + 2 * ML_HEADS] + gb_ref[...]
        col16 = lax.broadcasted_iota(jnp.int32, (CHUNK, 2 * ML_HEADS), 1)
        gi = jnp.where(col16 < ML_HEADS, gi, jnp.minimum(gi, 0.0) - jnp.log1p(jnp.exp(-jnp.abs(gi))))
        gi = jnp.where(col16 < ML_HEADS, gi, _dot(tril3, _split3(gi, 0)))
        gi_e = _dot(_split3(gi, 1), e2_ref[...])
        ig_e = gi_e[:, 0:ML_WIDTH]
        b_col = gi_e[:, ML_WIDTH:2 * ML_WIDTH]
        c_row = jnp.sum((ig_e - b_col) * meq_ref[...], axis=0, keepdims=True)
        dmat = b_col + c_row + mge_ref[...]
        heads = range(ML_HEADS)
        hs = lambda arr, hd: arr[:, hd * LANES:(hd + 1) * LANES]
        ones_blk = jnp.ones((CHUNK, LANES), BF16)
        m_prev = mm[...]
        m_inter = b_col + m_prev
        q = [qk_c[rows, hd * LANES:(hd + 1) * LANES].astype(BF16) for hd in heads]
        kf = [qk_c[rows, ML_WIDTH + hd * LANES:ML_WIDTH + (hd + 1) * LANES] * (ML_HEAD_DIM ** -0.5) for hd in heads]
        v_aug = [jnp.concatenate([v_buf[slot_b, rows, hd * LANES:(hd + 1) * LANES].astype(BF16), ones_blk], axis=1)
                 for hd in heads]
        s = [_dot_nt(q[hd], kf[hd].astype(BF16)) for hd in heads]
        qc = [_dot(q[hd], mst[hd].astype(BF16)) for hd in heads]
        run_task(c, 5)
        m_t = jnp.maximum(m_inter, jnp.concatenate(
            [jnp.broadcast_to(jnp.max(hs(dmat, hd), axis=-1, keepdims=True), (CHUNK, LANES)) for hd in heads], axis=1))
        w = jnp.exp(dmat - m_t)
        inter = jnp.exp(m_inter - m_t)
        den_floor = jnp.exp(-m_t)
        nd = [_dot((s[hd] * hs(w, hd)[:, 0:CHUNK]).astype(BF16), v_aug[hd])
              + qc[hd] * jnp.concatenate([hs(inter, hd)] * 2, axis=1) for hd in heads]
        run_task(c, 6)
        hh = [nd[hd][:, 0:LANES] / jnp.maximum(jnp.abs(nd[hd][:, LANES:2 * LANES]), hs(den_floor, hd)) for hd in heads]
        hc = [hh[hd] - jnp.mean(hh[hd], axis=-1, keepdims=True) for hd in heads]
        var = [jnp.mean(hc[hd] * hc[hd], axis=-1, keepdims=True) for hd in heads]
        for hd in heads:
            hn = hc[hd] * lax.rsqrt(var[hd] + EPS) * mnw_ref[:, hd * LANES:(hd + 1) * LANES]
            mix_buf[slot_b, rows, SSD_WIDTH + hd * LANES:SSD_WIDTH + (hd + 1) * LANES] = (
                jax.nn.sigmoid(o_buf[slot_b, rows, hd * LANES:(hd + 1) * LANES]) * hn).astype(BF16)
        run_task(c, 7)
        g_row = b_col[CHUNK - 1:CHUNK, :]
        a = g_row - b_col + ig_e
        m_new = jnp.maximum(g_row + m_prev, jnp.max(a, axis=0, keepdims=True))
        wk = jnp.exp(a - m_new)
        cs = jnp.exp(g_row + m_prev - m_new)
        for hd in heads:
            mst[hd] = mst[hd] * jnp.concatenate([hs(cs, hd)] * 2, axis=1) + _dot_tn(
                (kf[hd] * hs(wk, hd)).astype(BF16), v_aug[hd])
        mm[...] = m_new


def _const_spec(shape):
    nd = len(shape)
    return pl.BlockSpec(shape, lambda s: (0,) * nd, pipeline_mode=pl.Buffered(1))


def _mixer_call(x, consts, tl):
    bsz, seq, _ = x.shape
    nl = seq // tl
    n_tiles = bsz * nl
    x_tiles = x.reshape(n_tiles, tl, D_MODEL)
    tile_in = lambda s: (jnp.minimum(s, n_tiles - 1), 0, 0)
    tile_out = lambda s: (jnp.maximum(s - 2, 0), 0, 0)
    in_specs = ([pl.BlockSpec((None, tl, D_MODEL), tile_in), pl.BlockSpec((None, tl, D_MODEL), tile_out)]
                + [_const_spec(c.shape) for c in consts])
    out_specs = [pl.BlockSpec((None, tl, D_MODEL), tile_out), pl.BlockSpec((None, tl, D_MODEL), tile_out),
                 pl.BlockSpec((LOGIT_ROWS, tl), lambda s: (0, jnp.maximum(s - 2, 0)))]
    out_shape = [jax.ShapeDtypeStruct((n_tiles, tl, D_MODEL), F32),
                 jax.ShapeDtypeStruct((n_tiles, tl, D_MODEL), BF16),
                 jax.ShapeDtypeStruct((LOGIT_ROWS, bsz * seq), F32)]
    scratch = [
        pltpu.VMEM((2, SSD_XBC // LANES, tl + CONV_PAD, LANES), F32),
        pltpu.VMEM((2, 2 * ML_WIDTH // LANES, tl + CONV_PAD, LANES), F32),
        pltpu.VMEM((2, tl, SSD_WIDTH), F32),
        pltpu.VMEM((2, tl, ML_WIDTH), F32),
        pltpu.VMEM((2, tl, ML_WIDTH), F32),
        pltpu.VMEM((2, tl, GATE_COLS), F32),
        pltpu.VMEM((tl, SSD_XBC), F32),
        pltpu.VMEM((tl, 2 * ML_WIDTH), F32),
        pltpu.VMEM((2, tl, D_MIX), BF16),
        pltpu.VMEM((SSD_XBC // LANES, CONV_PAD, LANES), F32),
        pltpu.VMEM((2 * ML_WIDTH // LANES, CONV_PAD, LANES), F32),
        pltpu.VMEM((SSD_GROUPS, SSD_STATE, SSD_WIDTH // SSD_GROUPS), F32),
        pltpu.VMEM((ML_HEADS, ML_HEAD_DIM, 2 * LANES), F32),
        pltpu.VMEM((1, ML_WIDTH), F32),
        pltpu.VMEM((tl, D_MODEL), BF16),
    ]
    return pl.pallas_call(
        functools.partial(_mixer_kernel, tl=tl, nl=nl),
        grid=(n_tiles + 2,),
        in_specs=in_specs,
        out_specs=out_specs,
        out_shape=out_shape,
        scratch_shapes=scratch,
        compiler_params=pltpu.CompilerParams(
            dimension_semantics=("arbitrary",), vmem_limit_bytes=V7X_VMEM_LIMIT),
        name="mixer",
    )(x_tiles, x_tiles, *consts)


def _mixer_consts(norm1_w, w_in, ssd_conv_w, ssd_conv_b, ssd_dt_bias, ssd_a_log, ssd_d, ssd_norm_w,
                  ml_conv_w, ml_conv_b, ml_i_bias, ml_f_bias, ml_norm_w, w_out, norm2_w,
                  router_g_w, router_e_w):
    o1 = SSD_WIDTH
    o2 = o1 + SSD_XBC
    o3 = o2 + SSD_HEADS
    o4 = o3 + 2 * ML_WIDTH
    o5 = o4 + ML_WIDTH
    o6 = o5 + ML_WIDTH
    o7 = o6 + ML_HEADS
    zpad = lambda n: jnp.zeros((D_MODEL, n), F32)
    w_gate = jnp.concatenate([w_in[:, o2:o3], zpad(LANES - SSD_HEADS),
                              w_in[:, o6:o7], w_in[:, o7:], zpad(LANES - 2 * ML_HEADS)], axis=1)
    row = lambda v: v.reshape(1, -1).astype(F32)
    w_route = jnp.concatenate([router_e_w, router_g_w,
                               jnp.zeros((D_MODEL, LOGIT_ROWS - N_EXPERTS - MOE_GROUPS), F32)], axis=1).T

    c1024 = np.arange(SSD_WIDTH)
    r64 = np.arange(CHUNK)[:, None]
    e16 = np.tile(c1024[None, :] // SSD_HEAD_DIM == np.arange(SSD_HEADS)[:, None], (3, 1))
    e2 = np.tile(np.arange(2 * ML_WIDTH)[None, :] // LANES == np.arange(2 * ML_HEADS)[:, None], (3, 1))
    tril = np.tile(np.arange(CHUNK)[None, :] <= r64, (1, 3))
    s_pos = (c1024 % SSD_HEAD_DIM)[None, :]
    m_pos = (c1024 % LANES)[None, :]
    as_bf16 = lambda m: jnp.asarray(m.astype(np.float32), BF16)
    as_f32 = lambda m: jnp.asarray(m.astype(np.float32))
    e16, e2, tril = as_bf16(e16), as_bf16(e2), as_bf16(tril)
    causal = lambda keep: jnp.asarray(np.where(keep, 0.0, -np.inf).astype(np.float32))
    seq, sge, meq, mge = as_f32(r64 == s_pos), causal(r64 >= s_pos), as_f32(r64 == m_pos), causal(r64 >= m_pos)
    return [
        row(norm1_w),
        w_in[:, :o1].astype(BF16), w_in[:, o1:o2].astype(BF16), w_in[:, o3:o4].astype(BF16),
        w_in[:, o4:o5].astype(BF16), w_in[:, o5:o6].astype(BF16), w_gate.astype(BF16),
        ssd_conv_w.astype(F32), row(ssd_conv_b), row(ssd_dt_bias),
        row(ssd_a_log), row(jnp.repeat(ssd_d, SSD_HEAD_DIM)), row(ssd_norm_w),
        ml_conv_w.astype(F32), row(ml_conv_b), row(jnp.concatenate([ml_i_bias, ml_f_bias])), row(ml_norm_w),
        w_out.astype(BF16), row(norm2_w), w_route.astype(BF16),
        e16, e2, tril, seq, sge, meq, mge,
    ]


def _route_kernel(lg_ref, bias_ref, striu_ref, lpos_ref, gate_ref, cnt_ref):
    tr = lg_ref.shape[1]
    lg = lg_ref[...] + bias_ref[...]
    el = lg[0:N_EXPERTS]
    gl = lg[N_EXPERTS:N_EXPERTS + 8]
    gmax = jnp.max(gl, axis=0, keepdims=True)
    pg = 1.0 / jnp.sum(jnp.exp(gl - gmax), axis=0, keepdims=True)
    grow = lax.broadcasted_iota(jnp.int32, gl.shape, 0)
    gsel = jnp.min(jnp.where(gl == gmax, grow, 8), axis=0, keepdims=True)
    erow = lax.broadcasted_iota(jnp.int32, el.shape, 0)
    m1 = jnp.where((erow >> 3) == gsel, el, NEG_INF)
    v1 = jnp.max(m1, axis=0, keepdims=True)
    i1 = jnp.min(jnp.where(m1 == v1, erow, N_EXPERTS), axis=0, keepdims=True)
    m2 = jnp.where(erow == i1, NEG_INF, m1)
    v2 = jnp.max(m2, axis=0, keepdims=True)
    i2 = jnp.min(jnp.where(m2 == v2, erow, N_EXPERTS), axis=0, keepdims=True)
    e2 = jnp.exp(v2 - v1)
    g1 = pg / (1.0 + e2)
    g2 = g1 * e2
    hit1 = erow == i1
    hit2 = erow == i2
    oh = jnp.where(hit1, 1.0, jnp.where(hit2, 1.0, 0.0))
    earlier = _dot(oh.astype(BF16), striu_ref[...])
    cnt = jnp.sum(oh, axis=1, keepdims=True)
    seg = jnp.floor((cnt + (ROW_ALIGN - 1)) * (1.0 / ROW_ALIGN)) * ROW_ALIGN
    l1 = jnp.sum(jnp.where(hit1, earlier, jnp.where(erow < i1, seg, 0.0)), axis=0, keepdims=True)
    l2 = jnp.sum(jnp.where(hit2, earlier, jnp.where(erow < i2, seg, 0.0)), axis=0, keepdims=True)
    lpos_ref[...] = jnp.concatenate([l1.astype(jnp.int32), l2.astype(jnp.int32), jnp.zeros((6, tr), jnp.int32)], axis=0)
    gate_ref[...] = jnp.concatenate([g1, g2, jnp.zeros((6, tr), F32)], axis=0)
    cnt_ref[...] = jnp.broadcast_to(cnt, cnt_ref.shape)


def _route_call(lg_t, bias_col, tr):
    t = lg_t.shape[1]
    striu = jnp.asarray(np.triu(np.ones((tr, tr), np.float32), 1), BF16)
    return pl.pallas_call(
        _route_kernel,
        grid=(t // tr,),
        in_specs=[pl.BlockSpec((LOGIT_ROWS, tr), lambda i: (0, i)),
                  pl.BlockSpec((LOGIT_ROWS, 1), lambda i: (0, 0)),
                  pl.BlockSpec((tr, tr), lambda i: (0, 0))],
        out_specs=[pl.BlockSpec((8, tr), lambda i: (0, i)),
                   pl.BlockSpec((8, tr), lambda i: (0, i)),
                   pl.BlockSpec((None, N_EXPERTS, LANES), lambda i: (i, 0, 0))],
        out_shape=[jax.ShapeDtypeStruct((8, t), jnp.int32),
                   jax.ShapeDtypeStruct((8, t), F32),
                   jax.ShapeDtypeStruct((t // tr, N_EXPERTS, LANES), F32)],
        compiler_params=pltpu.CompilerParams(dimension_semantics=("arbitrary",)),
        name="route",
    )(lg_t, bias_col, striu)


def _run_copies(runs_ref, copy_run, max_piece, filler_far, enable=None):
    for e in range(N_EXPERTS + 1):
        dst = runs_ref[0, e] if e < N_EXPERTS else filler_far
        src = runs_ref[1, e]
        n = runs_ref[2, e] if enable is None else jnp.where(enable, runs_ref[2, e], 0)
        p = max_piece
        while p >= ROW_ALIGN:
            done = n & ~(2 * p - 1)

            @pl.when((n & p) != 0)
            def _(p=p, done=done, dst=dst, src=src):
                copy_run(pl.multiple_of(dst + done, ROW_ALIGN), pl.multiple_of(src + done, ROW_ALIGN), p)
            p //= 2


def _by_parity(body):
    def kernel(*refs, **kw):
        for parity in range(2):
            @pl.when(pl.program_id(0) % 2 == parity)
            def _(parity=parity):
                body(*refs, slot=parity, **kw)
    return kernel


def _dispatch_step(runs_ref, prev_runs_ref, lpos_ref, h2_ref, xs_ref, loc, sem, *, max_spare, slot):
    tr = h2_ref.shape[0]
    nloc = loc.shape[1]
    step = pl.program_id(0)
    last = pl.num_programs(0) - 1
    used = runs_ref[0, N_EXPERTS]

    def wait_block(s):
        pltpu.make_async_copy(loc.at[s], xs_ref.at[pl.ds(0, nloc)], sem.at[s]).wait()

    def send_block(table, s, enable=None):
        def copy_run(dst, src, n):
            pltpu.make_async_copy(loc.at[s, pl.ds(src, n)], xs_ref.at[pl.ds(dst, n)], sem.at[s]).start()
        _run_copies(table, copy_run, tr, used + s * (nloc - TOP_K * tr), enable)

    @pl.when(step > 1)
    def _():
        wait_block(slot)

    send_block(prev_runs_ref, 1 - slot, enable=step > 0)

    jrow = lax.broadcasted_iota(jnp.int32, (nloc, tr), 0)
    perm = jnp.where(jrow == lpos_ref[0:1, :], 1.0, jnp.where(jrow == lpos_ref[1:2, :], 1.0, 0.0)).astype(BF16)
    loc[slot] = _pack_rows(_dot(perm, h2_ref[...]), rounded=True)

    @pl.when(step == last)
    def _():
        send_block(runs_ref, slot)

        @pl.when(step > 0)
        def _():
            wait_block(1 - slot)
        wait_block(slot)
        loc[slot, 0:tr, :] = jnp.zeros((tr, loc.shape[2]), loc.dtype)
        spare = xs_ref.shape[0] - used

        def zero_copy(off, n):
            return pltpu.make_async_copy(loc.at[slot, pl.ds(0, n)],
                                         xs_ref.at[pl.ds(pl.multiple_of(used + off, ROW_ALIGN), n)], sem.at[slot])

        def pieces(do):
            for i in range(max_spare // tr):
                @pl.when((i + 1) * tr <= spare)
                def _(i=i):
                    do(zero_copy(i * tr, tr))
            p = tr // 2
            while p >= ROW_ALIGN:
                @pl.when((spare & p) != 0)
                def _(p=p):
                    do(zero_copy(spare & ~(2 * p - 1), p))
                p //= 2

        pieces(lambda c: c.start())
        pieces(lambda c: c.wait())


def _dispatch_call(runs, lpos, h2, tr, n_rows):
    t = h2.shape[0]
    return pl.pallas_call(
        functools.partial(_by_parity(_dispatch_step), max_spare=n_rows - t * TOP_K),
        grid=(t // tr,),
        in_specs=[pl.BlockSpec((None, 3, N_EXPERTS + 1), lambda i: (i, 0, 0), memory_space=pltpu.SMEM),
                  pl.BlockSpec((None, 3, N_EXPERTS + 1), lambda i: (jnp.maximum(i - 1, 0), 0, 0), memory_space=pltpu.SMEM),
                  pl.BlockSpec((8, tr), lambda i: (0, i)),
                  pl.BlockSpec((tr, D_MODEL), lambda i: (i, 0))],
        out_specs=pl.BlockSpec(memory_space=pl.ANY),
        out_shape=jax.ShapeDtypeStruct((n_rows, D_MODEL // 2), jnp.uint32),
        scratch_shapes=[pltpu.VMEM((2, _local_rows(tr), D_MODEL // 2), jnp.uint32), pltpu.SemaphoreType.DMA((2,))],
        compiler_params=pltpu.CompilerParams(dimension_semantics=("arbitrary",), vmem_limit_bytes=V7X_VMEM_LIMIT),
        name="dispatch",
    )(runs, runs, lpos, h2)


def _experts_kernel(tile_ref, exp_ref, starts_ref, xs_ref, wg_ref, wu_ref, wd_ref, ys_ref, wg_b, wu_b, wd_b):
    i = pl.program_id(0)
    tm = xs_ref.shape[0]
    e = exp_ref[i]
    tile = tile_ref[i]
    is_first = jnp.logical_or(i == 0, tile_ref[jnp.maximum(i - 1, 0)] != tile)

    @pl.when(jnp.logical_and(e < N_EXPERTS, jnp.logical_or(i == 0, exp_ref[jnp.maximum(i - 1, 0)] != e)))
    def _():
        wg_b[...] = wg_ref[...].astype(BF16)
        wu_b[...] = wu_ref[...].astype(BF16)
        wd_b[...] = wd_ref[...].astype(BF16)

    def expert_rows(other):
        x = _unpack_rows(xs_ref[...])
        act = (_silu(_dot(x, wg_b[...])) * _dot(x, wu_b[...])).astype(BF16)
        y = _pack_rows(_dot(act, wd_b[...]), rounded=False)
        rows = tile * tm + lax.broadcasted_iota(jnp.int32, (tm, 1), 0)
        return jnp.where((rows >= starts_ref[e]) & (rows < starts_ref[e + 1]), y, other)

    @pl.when(jnp.logical_and(e < N_EXPERTS, is_first))
    def _():
        ys_ref[...] = expert_rows(jnp.zeros_like(ys_ref))

    @pl.when(jnp.logical_and(e < N_EXPERTS, jnp.logical_not(is_first)))
    def _():
        ys_ref[...] = expert_rows(ys_ref[...])

    @pl.when(e == N_EXPERTS)
    def _():
        ys_ref[...] = jnp.zeros_like(ys_ref)


def _experts_call(item_tile, item_exp, starts_ext, xs, wg, wu, wd, tm):
    n = xs.shape[0]
    n_items = item_tile.shape[0]
    w_idx = lambda i, tile, ex, st: (jnp.minimum(ex[i], N_EXPERTS - 1), 0, 0)
    grid_spec = pltpu.PrefetchScalarGridSpec(
        num_scalar_prefetch=3,
        grid=(n_items,),
        in_specs=[pl.BlockSpec((tm, D_MODEL // 2), lambda i, tile, ex, st: (tile[i], 0)),
                  pl.BlockSpec((None, D_MODEL, D_EXPERT), w_idx),
                  pl.BlockSpec((None, D_MODEL, D_EXPERT), w_idx),
                  pl.BlockSpec((None, D_EXPERT, D_MODEL), w_idx)],
        out_specs=pl.BlockSpec((tm, D_MODEL // 2), lambda i, tile, ex, st: (tile[i], 0)),
        scratch_shapes=[pltpu.VMEM((D_MODEL, D_EXPERT), BF16), pltpu.VMEM((D_MODEL, D_EXPERT), BF16),
                        pltpu.VMEM((D_EXPERT, D_MODEL), BF16)],
    )
    return pl.pallas_call(
        _experts_kernel,
        grid_spec=grid_spec,
        out_shape=jax.ShapeDtypeStruct((n, D_MODEL // 2), jnp.uint32),
        compiler_params=pltpu.CompilerParams(
            dimension_semantics=("arbitrary",), vmem_limit_bytes=V7X_VMEM_LIMIT),
        name="experts",
    )(item_tile, item_exp, starts_ext, xs, wg, wu, wd)


def _local_rows(tr):
    return TOP_K * tr + N_EXPERTS * ROW_ALIGN


def _expert_items(counts, n_rows, tm):
    n_tiles = n_rows // tm
    n_items = n_tiles + N_EXPERTS - 1
    starts = jnp.concatenate([jnp.zeros((1,), jnp.int32), jnp.cumsum(counts).astype(jnp.int32)])
    lo = starts[:-1] // tm
    hi = (starts[1:] - 1) // tm
    per = jnp.where(counts > 0, hi - lo + 1, 0)
    cum = jnp.cumsum(per)
    idx = jnp.arange(n_items, dtype=jnp.int32)
    e = jnp.sum(idx[:, None] >= cum[None, :], axis=1).astype(jnp.int32)
    valid = idx < cum[-1]
    e_c = jnp.minimum(e, N_EXPERTS - 1)
    first_item = jnp.sum(jnp.where(e_c[:, None] == jnp.arange(N_EXPERTS)[None, :], (lo - (cum - per))[None, :], 0), axis=1)
    tile = first_item + idx
    spare_tile = -(-starts[-1] // tm) + idx - cum[-1]
    item_tile = jnp.where(valid, tile, jnp.minimum(spare_tile, n_tiles - 1)).astype(jnp.int32)
    item_exp = jnp.where(valid, e_c, jnp.where(spare_tile < n_tiles, N_EXPERTS, N_EXPERTS + 1)).astype(jnp.int32)
    starts_ext = jnp.concatenate([starts, starts[-1:], starts[-1:]])
    return item_tile, item_exp, starts, starts_ext


def _combine_step(runs_ref, next_runs_ref, lpos_ref, gate_ref, next_lpos_ref, next_gate_ref, x1_ref, nf_ref, ys_ref,
                  out_ref, loc, sel_buf, sem, *, slot):
    tr = x1_ref.shape[0]
    nloc = loc.shape[1]
    step = pl.program_id(0)

    def gather(table, s, enable=None):
        def copy_run(dst, src, n):
            pltpu.make_async_copy(ys_ref.at[pl.ds(dst, n)], loc.at[s, pl.ds(src, n)], sem.at[s]).start()
        _run_copies(table, copy_run, tr, 0, enable)

    def build_sel(pos_ref, g_ref, s):
        jrow = lax.broadcasted_iota(jnp.int32, (nloc, tr), 0)
        sel_buf[s] = jnp.where(jrow == pos_ref[0:1, :], g_ref[0:1, :],
                               jnp.where(jrow == pos_ref[1:2, :], g_ref[1:2, :], 0.0)).astype(BF16)

    @pl.when(step == 0)
    def _():
        gather(runs_ref, slot)
        build_sel(lpos_ref, gate_ref, slot)

    pltpu.make_async_copy(ys_ref.at[pl.ds(0, nloc)], loc.at[slot], sem.at[slot]).wait()

    gather(next_runs_ref, 1 - slot, enable=step + 1 < pl.num_programs(0))
    y = _dot_tn(sel_buf[slot], _unpack_rows(loc[slot]))
    build_sel(next_lpos_ref, next_gate_ref, 1 - slot)
    out_ref[...] = _rms(x1_ref[...] + y, nf_ref[...])


def _combine_call(runs, lpos, gates, x1, nf_w, ys, tr):
    t = x1.shape[0]
    return pl.pallas_call(
        _by_parity(_combine_step),
        grid=(t // tr,),
        in_specs=[pl.BlockSpec((None, 3, N_EXPERTS + 1), lambda i: (i, 0, 0), memory_space=pltpu.SMEM),
                  pl.BlockSpec((None, 3, N_EXPERTS + 1), lambda i: (jnp.minimum(i + 1, t // tr - 1), 0, 0),
                               memory_space=pltpu.SMEM),
                  pl.BlockSpec((8, tr), lambda i: (0, i)),
                  pl.BlockSpec((8, tr), lambda i: (0, i)),
                  pl.BlockSpec((8, tr), lambda i: (0, jnp.minimum(i + 1, t // tr - 1))),
                  pl.BlockSpec((8, tr), lambda i: (0, jnp.minimum(i + 1, t // tr - 1))),
                  pl.BlockSpec((tr, D_MODEL), lambda i: (i, 0)),
                  pl.BlockSpec((1, D_MODEL), lambda i: (0, 0)),
                  pl.BlockSpec(memory_space=pl.ANY)],
        out_specs=pl.BlockSpec((tr, D_MODEL), lambda i: (i, 0)),
        out_shape=jax.ShapeDtypeStruct((t, D_MODEL), F32),
        scratch_shapes=[pltpu.VMEM((2, _local_rows(tr), D_MODEL // 2), jnp.uint32),
                        pltpu.VMEM((2, _local_rows(tr), tr), BF16), pltpu.SemaphoreType.DMA((2,))],
        compiler_params=pltpu.CompilerParams(dimension_semantics=("arbitrary",), vmem_limit_bytes=V7X_VMEM_LIMIT),
        name="combine",
    )(runs, runs, lpos, gates, lpos, gates, x1, nf_w, ys)


SEQ_TILE = 4 * CHUNK
TOKEN_TILE = 512
ROW_TILE = 512


def kernel(x, norm1_w, w_in, ssd_conv_w, ssd_conv_b, ssd_dt_bias, ssd_a_log, ssd_d, ssd_norm_w, ml_conv_w, ml_conv_b, ml_i_bias, ml_f_bias, ml_norm_w, w_out, norm2_w, router_g_w, router_g_b, router_e_w, router_e_b, exp_w_gate, exp_w_up, exp_w_down, norm_f_w):
    bsz, seq, d = x.shape
    t = bsz * seq
    tl, tr, tm = SEQ_TILE, TOKEN_TILE, ROW_TILE
    assert d == D_MODEL and norm1_w.shape[0] == 1 and seq % tl == 0 and t % tr == 0

    consts = _mixer_consts(norm1_w[0], w_in[0], ssd_conv_w[0], ssd_conv_b[0], ssd_dt_bias[0], ssd_a_log[0],
                           ssd_d[0], ssd_norm_w[0], ml_conv_w[0], ml_conv_b[0], ml_i_bias[0], ml_f_bias[0],
                           ml_norm_w[0], w_out[0], norm2_w[0], router_g_w[0], router_e_w[0])
    x1, h2, lg_t = _mixer_call(x, consts, tl)
    x1 = x1.reshape(t, D_MODEL)
    h2 = h2.reshape(t, D_MODEL)

    bias_col = jnp.concatenate([router_e_b[0], router_g_b[0],
                                jnp.full((LOGIT_ROWS - N_EXPERTS - MOE_GROUPS,), STAB_INIT, F32)]).reshape(-1, 1)
    lpos, gates, cnt = _route_call(lg_t, bias_col.astype(F32), tr)

    seg = (cnt[:, :, 0].astype(jnp.int32) + (ROW_ALIGN - 1)) // ROW_ALIGN * ROW_ALIGN
    n_tok_tiles = t // tr
    n_rows = -(-(t * TOP_K + n_tok_tiles * N_EXPERTS * ROW_ALIGN + _local_rows(tr)) // tm) * tm
    item_tile, item_exp, starts, starts_ext = _expert_items(jnp.sum(seg, axis=0), n_rows, tm)
    run_dst = starts[None, :N_EXPERTS] + jnp.cumsum(seg, axis=0) - seg
    run_src = jnp.cumsum(seg, axis=1) - seg
    used = jnp.sum(seg, axis=1, keepdims=True)
    fill = jnp.concatenate([jnp.full_like(used, starts[N_EXPERTS]), used, _local_rows(tr) - used], axis=1)[:, :, None]
    runs = jnp.concatenate([jnp.stack([run_dst, run_src, seg], axis=1), fill], axis=2).astype(jnp.int32)

    xs = _dispatch_call(runs, lpos, h2, tr, n_rows)
    ys = _experts_call(item_tile, item_exp, starts_ext, xs, exp_w_gate[0], exp_w_up[0], exp_w_down[0], tm)
    out = _combine_call(runs, lpos, gates, x1, norm_f_w.reshape(1, -1).astype(F32), ys, tr)
    return out.reshape(bsz, seq, D_MODEL)
```

```python
import functools

import jax
import jax.numpy as jnp
import numpy as np
from jax import lax
from jax.experimental import pallas as pl
from jax.experimental.pallas import tpu as pltpu

D_MODEL = 1024
CHUNK = 64
SSD_WIDTH = 1024
SSD_HEAD_DIM = 64
SSD_HEADS = 16
SSD_GROUPS = 2
SSD_STATE = 128
SSD_CONV = 4
SSD_XBC = SSD_WIDTH + 2 * SSD_GROUPS * SSD_STATE
ML_WIDTH = 1024
ML_HEADS = 8
ML_HEAD_DIM = 128
ML_CONV = 4
D_MIX = SSD_WIDTH + ML_WIDTH
MOE_GROUPS = 4
EXPERTS_PER_GROUP = 8
N_EXPERTS = 32
TOP_K = 2
D_EXPERT = 512
EPS = 1e-6
STAB_INIT = -1e30

LANES = 128
GATE_COLS = 2 * LANES
ROW_ALIGN = 8
CONV_PAD = 8
LOGIT_ROWS = 64
V7X_VMEM_LIMIT = 56 * 1024 * 1024

F32 = jnp.float32
BF16 = jnp.bfloat16
NEG_INF = float("-inf")


def _dot(a, b):
    return jnp.dot(a, b, preferred_element_type=F32)


def _dot_nt(a, b):
    return lax.dot_general(a, b, (((1,), (1,)), ((), ())), preferred_element_type=F32)


def _dot_tn(a, b):
    return lax.dot_general(a, b, (((0,), (0,)), ((), ())), preferred_element_type=F32)


def _split3(x, axis):
    hi = x.astype(BF16)
    r1 = x - hi.astype(F32)
    mid = r1.astype(BF16)
    lo = (r1 - mid.astype(F32)).astype(BF16)
    return jnp.concatenate([hi, mid, lo], axis=axis)


def _pack_rows(x, rounded):
    if not rounded:
        x = x.astype(BF16).astype(F32)
    bits = lax.bitcast_convert_type(x, jnp.uint32)
    half = x.shape[1] // 2
    return bits[:, half:] | (bits[:, :half] >> 16)


def _unpack_rows(p):
    lo = lax.bitcast_convert_type(p << 16, F32)
    hi = lax.bitcast_convert_type(p & jnp.uint32(0xFFFF0000), F32)
    return jnp.concatenate([lo, hi], axis=1).astype(BF16)


def _silu(x):
    return x * jax.nn.sigmoid(x)


def _softplus(x):
    return jnp.maximum(x, 0.0) + jnp.log1p(jnp.exp(-jnp.abs(x)))


def _rms(x, w):
    return x * lax.rsqrt(jnp.mean(x * x, axis=-1, keepdims=True) + EPS) * w


def _mixer_kernel(*refs, tl, nl):
    for parity in range(2):
        @pl.when(pl.program_id(0) % 2 == parity)
        def _(parity=parity):
            _mixer_step(*refs, tl=tl, nl=nl, slot_a=parity)


def _mixer_step(x_ref, xc_ref, n1_ref, wz_ref, wxbc_ref, wqk_ref, wv_ref, wo_ref, wg_ref,
                  scw_ref, scb_ref, dtb_ref, alog_ref, dskip_ref, snw_ref,
                  mcw_ref, mcb_ref, gb_ref, mnw_ref, wout_ref, n2_ref, wr_ref,
                  e16_ref, e2_ref, tril_ref, seq_ref, sge_ref, meq_ref, mge_ref,
                  x1_ref, h2_ref, lg_ref,
                  xbc_buf, qk_buf, z_buf, v_buf, o_buf, g_buf, xbc_c, qk_c, mix_buf,
                  xbc_tail, qk_tail, sst, mst, mm, h_buf, *, tl, nl, slot_a):
    nchunk = tl // CHUNK
    step = pl.program_id(0)
    slot_b = 1 - slot_a
    slot_c = slot_a

    @pl.when(step == 0)
    def _():
        xbc_buf[...] = jnp.zeros_like(xbc_buf)
        qk_buf[...] = jnp.zeros_like(qk_buf)
        z_buf[...] = jnp.zeros_like(z_buf)
        v_buf[...] = jnp.zeros_like(v_buf)
        o_buf[...] = jnp.zeros_like(o_buf)
        g_buf[...] = jnp.zeros_like(g_buf)
        mix_buf[...] = jnp.zeros_like(mix_buf)
        xbc_tail[...] = jnp.zeros_like(xbc_tail)
        qk_tail[...] = jnp.zeros_like(qk_tail)
        sst[...] = jnp.zeros_like(sst)
        mst[...] = jnp.zeros_like(mst)
        mm[...] = jnp.full(mm.shape, STAB_INIT, F32)

    h_buf[...] = _rms(x_ref[...], n1_ref[...]).astype(BF16)
    nblk = 2 * LANES
    tasks = []

    def proj_task(w_ref, store):
        for b0 in range(0, w_ref.shape[1], nblk):
            tasks.append(lambda b0=b0: store(slice(b0, b0 + nblk), _dot(h_buf[...], w_ref[:, b0:b0 + nblk])))

    def out_task(cols):
        x1_ref[:, cols] = xc_ref[:, cols] + _dot(mix_buf[slot_c], wout_ref[:, cols])

    def route_task():
        h2 = _rms(x1_ref[...], n2_ref[...])
        h2 = h2.astype(BF16)
        h2_ref[...] = h2
        lg_ref[...] = _dot_nt(wr_ref[...], h2)

    for b0 in range(0, D_MODEL, nblk):
        tasks.append(lambda b0=b0: out_task(slice(b0, b0 + nblk)))
    tasks.append(route_task)

    def put(buf, r0=0):
        def store(cols, val):
            buf[slot_a, r0:r0 + tl, cols] = val
        return store

    def put_tiles(buf):
        def store(cols, val):
            for k in range(nblk // LANES):
                buf[slot_a, cols.start // LANES + k, CONV_PAD:CONV_PAD + tl, :] = val[:, k * LANES:(k + 1) * LANES]
        return store

    proj_task(wz_ref, put(z_buf))
    proj_task(wg_ref, put(g_buf))
    proj_task(wxbc_ref, put_tiles(xbc_buf))
    proj_task(wqk_ref, put_tiles(qk_buf))
    proj_task(wv_ref, put(v_buf))
    proj_task(wo_ref, put(o_buf))
    per_chunk = -(-len(tasks) // nchunk)
    assert per_chunk <= 8

    def run_task(c, k):
        idx = c * per_chunk + k
        if idx < len(tasks):
            tasks[idx]()

    seq_start = (step + nl - 1) % nl == 0
    xbc_buf[slot_b, :, 0:CONV_PAD, :] = jnp.where(seq_start, 0.0, xbc_tail[...])
    qk_buf[slot_b, :, 0:CONV_PAD, :] = jnp.where(seq_start, 0.0, qk_tail[...])
    sst[...] = jnp.where(seq_start, 0.0, sst[...])
    mst[...] = jnp.where(seq_start, 0.0, mst[...])
    mm[...] = jnp.where(seq_start, STAB_INIT, mm[...])

    first = CONV_PAD - (SSD_CONV - 1)
    xbc_tail[...] = xbc_buf[slot_b, :, tl:tl + CONV_PAD, :]
    qk_tail[...] = qk_buf[slot_b, :, tl:tl + CONV_PAD, :]

    def conv_silu(buf, w_ref, b_ref, out, r0):
        for t in range(buf.shape[1]):
            cols = slice(t * LANES, (t + 1) * LANES)
            acc = b_ref[:, cols] + w_ref[0:1, cols] * buf[slot_b, t, r0 + first:r0 + first + CHUNK, :]
            for j in range(1, w_ref.shape[0]):
                acc = acc + w_ref[j:j + 1, cols] * buf[slot_b, t, r0 + first + j:r0 + first + j + CHUNK, :]
            out[r0:r0 + CHUNK, cols] = _silu(acc)

    def convs(c):
        run_task(c, 0)
        conv_silu(xbc_buf, scw_ref, scb_ref, xbc_c, c * CHUNK)
        run_task(c, 1)
        conv_silu(qk_buf, mcw_ref, mcb_ref, qk_c, c * CHUNK)
        run_task(c, 2)

    for c in range(nchunk):
        convs(c)
        rows = slice(c * CHUNK, (c + 1) * CHUNK)
        tril3 = tril_ref[...]

        xs = xbc_c[rows, 0:SSD_WIDTH]
        bm = xbc_c[rows, SSD_WIDTH:SSD_WIDTH + 2 * SSD_STATE].astype(BF16)
        cm = xbc_c[rows, SSD_WIDTH + 2 * SSD_STATE:SSD_XBC].astype(BF16)
        dt = _softplus(g_buf[slot_b, rows, 0:SSD_HEADS] + dtb_ref[...])
        a_cs = _dot(tril3, _split3(dt * (-jnp.exp(alog_ref[...])), 0))
        dt_e = _dot(_split3(dt, 1), e16_ref[...])
        a_col = _dot(_split3(a_cs, 1), e16_ref[...])
        a_row = jnp.sum(a_col * seq_ref[...], axis=0, keepdims=True)
        lmat = jnp.exp(a_col - a_row + sge_ref[...])
        cb = jnp.concatenate(
            [_dot_nt(cm[:, g * SSD_STATE:(g + 1) * SSD_STATE],
                     jnp.concatenate([bm[:, g * SSD_STATE:(g + 1) * SSD_STATE]] * (SSD_HEADS // SSD_GROUPS), axis=0))
             for g in range(SSD_GROUPS)], axis=1)
        m_all = (cb * lmat).astype(BF16)
        run_task(c, 3)
        xdt = xs * dt_e
        lane = lax.broadcasted_iota(jnp.int32, (CHUNK, LANES), 1)
        y_parts = []
        for j in range(SSD_HEADS // 2):
            xp = xdt[:, j * LANES:(j + 1) * LANES]
            xbd = jnp.concatenate([jnp.where(lane < SSD_HEAD_DIM, xp, 0.0),
                                   jnp.where(lane >= SSD_HEAD_DIM, xp, 0.0)], axis=0).astype(BF16)
            y_parts.append(_dot(m_all[:, j * LANES:(j + 1) * LANES], xbd))
        y = jnp.concatenate(y_parts, axis=1)
        half = SSD_WIDTH // SSD_GROUPS
        y_int = jnp.concatenate(
            [_dot(cm[:, g * SSD_STATE:(g + 1) * SSD_STATE], sst[g].astype(BF16)) for g in range(SSD_GROUPS)], axis=1)
        y = y + y_int * jnp.exp(a_col)
        a_last = a_col[CHUNK - 1:CHUNK, :]
        xd_b = (xdt * jnp.exp(a_last - a_col)).astype(BF16)
        st_scale = jnp.exp(a_last)
        for g in range(SSD_GROUPS):
            sst[g] = sst[g] * st_scale[:, g * half:(g + 1) * half] + _dot_tn(
                bm[:, g * SSD_STATE:(g + 1) * SSD_STATE], xd_b[:, g * half:(g + 1) * half])
        y = (y + dskip_ref[...] * xs) * _silu(z_buf[slot_b, rows, :])
        for g in range(SSD_GROUPS):
            yg = y[:, g * half:(g + 1) * half]
            yg = yg * lax.rsqrt(jnp.mean(yg * yg, axis=-1, keepdims=True) + EPS) * snw_ref[:, g * half:(g + 1) * half]
            mix_buf[slot_b, rows, g * half:(g + 1) * half] = yg.astype(BF16)

        run_task(c, 4)

        gi = g_buf[slot_b, rows, LANES:LANES + 2 * ML_HEADS] + gb_ref[...]
        col16 = lax.broadcasted_iota(jnp.int32, (CHUNK, 2 * ML_HEADS), 1)
        gi = jnp.where(col16 < ML_HEADS, gi, jnp.minimum(gi, 0.0) - jnp.log1p(jnp.exp(-jnp.abs(gi))))
        gi = jnp.where(col16 < ML_HEADS, gi, _dot(tril3, _split3(gi, 0)))
        gi_e = _dot(_split3(gi, 1), e2_ref[...])
        ig_e = gi_e[:, 0:ML_WIDTH]
        b_col = gi_e[:, ML_WIDTH:2 * ML_WIDTH]
        c_row = jnp.sum((ig_e - b_col) * meq_ref[...], axis=0, keepdims=True)
        dmat = b_col + c_row + mge_ref[...]
        heads = range(ML_HEADS)
        hs = lambda arr, hd: arr[:, hd * LANES:(hd + 1) * LANES]
        ones_blk = jnp.ones((CHUNK, LANES), BF16)
        m_prev = mm[...]
        m_inter = b_col + m_prev
        q = [qk_c[rows, hd * LANES:(hd + 1) * LANES].astype(BF16) for hd in heads]
        kf = [qk_c[rows, ML_WIDTH + hd * LANES:ML_WIDTH + (hd + 1) * LANES] * (ML_HEAD_DIM ** -0.5) for hd in heads]
        v_aug = [jnp.concatenate([v_buf[slot_b, rows, hd * LANES:(hd + 1) * LANES].astype(BF16), ones_blk], axis=1)
                 for hd in heads]
        s = [_dot_nt(q[hd], kf[hd].astype(BF16)) for hd in heads]
        qc = [_dot(q[hd], mst[hd].astype(BF16)) for hd in heads]
        run_task(c, 5)
        m_t = jnp.maximum(m_inter, jnp.concatenate(
            [jnp.broadcast_to(jnp.max(hs(dmat, hd), axis=-1, keepdims=True), (CHUNK, LANES)) for hd in heads], axis=1))
        w = jnp.exp(dmat - m_t)
        inter = jnp.exp(m_inter - m_t)
        den_floor = jnp.exp(-m_t)
        nd = [_dot((s[hd] * hs(w, hd)[:, 0:CHUNK]).astype(BF16), v_aug[hd])
              + qc[hd] * jnp.concatenate([hs(inter, hd)] * 2, axis=1) for hd in heads]
        run_task(c, 6)
        hh = [nd[hd][:, 0:LANES] / jnp.maximum(jnp.abs(nd[hd][:, LANES:2 * LANES]), hs(den_floor, hd)) for hd in heads]
        hc = [hh[hd] - jnp.mean(hh[hd], axis=-1, keepdims=True) for hd in heads]
        var = [jnp.mean(hc[hd] * hc[hd], axis=-1, keepdims=True) for hd in heads]
        for hd in heads:
            hn = hc[hd] * lax.rsqrt(var[hd] + EPS) * mnw_ref[:, hd * LANES:(hd + 1) * LANES]
            mix_buf[slot_b, rows, SSD_WIDTH + hd * LANES:SSD_WIDTH + (hd + 1) * LANES] = (
                jax.nn.sigmoid(o_buf[slot_b, rows, hd * LANES:(hd + 1) * LANES]) * hn).astype(BF16)
        run_task(c, 7)
        g_row = b_col[CHUNK - 1:CHUNK, :]
        a = g_row - b_col + ig_e
        m_new = jnp.maximum(g_row + m_prev, jnp.max(a, axis=0, keepdims=True))
        wk = jnp.exp(a - m_new)
        cs = jnp.exp(g_row + m_prev - m_new)
        for hd in heads:
            mst[hd] = mst[hd] * jnp.concatenate([hs(cs, hd)] * 2, axis=1) + _dot_tn(
                (kf[hd] * hs(wk, hd)).astype(BF16), v_aug[hd])
        mm[...] = m_new


def _const_spec(shape):
    nd = len(shape)
    return pl.BlockSpec(shape, lambda s: (0,) * nd, pipeline_mode=pl.Buffered(1))


def _mixer_call(x, consts, tl):
    bsz, seq, _ = x.shape
    nl = seq // tl
    n_tiles = bsz * nl
    x_tiles = x.reshape(n_tiles, tl, D_MODEL)
    tile_in = lambda s: (jnp.minimum(s, n_tiles - 1), 0, 0)
    tile_out = lambda s: (jnp.maximum(s - 2, 0), 0, 0)
    in_specs = ([pl.BlockSpec((None, tl, D_MODEL), tile_in), pl.BlockSpec((None, tl, D_MODEL), tile_out)]
                + [_const_spec(c.shape) for c in consts])
    out_specs = [pl.BlockSpec((None, tl, D_MODEL), tile_out), pl.BlockSpec((None, tl, D_MODEL), tile_out),
                 pl.BlockSpec((LOGIT_ROWS, tl), lambda s: (0, jnp.maximum(s - 2, 0)))]
    out_shape = [jax.ShapeDtypeStruct((n_tiles, tl, D_MODEL), F32),
                 jax.ShapeDtypeStruct((n_tiles, tl, D_MODEL), BF16),
                 jax.ShapeDtypeStruct((LOGIT_ROWS, bsz * seq), F32)]
    scratch = [
        pltpu.VMEM((2, SSD_XBC // LANES, tl + CONV_PAD, LANES), F32),
        pltpu.VMEM((2, 2 * ML_WIDTH // LANES, tl + CONV_PAD, LANES), F32),
        pltpu.VMEM((2, tl, SSD_WIDTH), F32),
        pltpu.VMEM((2, tl, ML_WIDTH), F32),
        pltpu.VMEM((2, tl, ML_WIDTH), F32),
        pltpu.VMEM((2, tl, GATE_COLS), F32),
        pltpu.VMEM((tl, SSD_XBC), F32),
        pltpu.VMEM((tl, 2 * ML_WIDTH), F32),
        pltpu.VMEM((2, tl, D_MIX), BF16),
        pltpu.VMEM((SSD_XBC // LANES, CONV_PAD, LANES), F32),
        pltpu.VMEM((2 * ML_WIDTH // LANES, CONV_PAD, LANES), F32),
        pltpu.VMEM((SSD_GROUPS, SSD_STATE, SSD_WIDTH // SSD_GROUPS), F32),
        pltpu.VMEM((ML_HEADS, ML_HEAD_DIM, 2 * LANES), F32),
        pltpu.VMEM((1, ML_WIDTH), F32),
        pltpu.VMEM((tl, D_MODEL), BF16),
    ]
    return pl.pallas_call(
        functools.partial(_mixer_kernel, tl=tl, nl=nl),
        grid=(n_tiles + 2,),
        in_specs=in_specs,
        out_specs=out_specs,
        out_shape=out_shape,
        scratch_shapes=scratch,
        compiler_params=pltpu.CompilerParams(
            dimension_semantics=("arbitrary",), vmem_limit_bytes=V7X_VMEM_LIMIT),
        name="mixer",
    )(x_tiles, x_tiles, *consts)


def _mixer_consts(norm1_w, w_in, ssd_conv_w, ssd_conv_b, ssd_dt_bias, ssd_a_log, ssd_d, ssd_norm_w,
                  ml_conv_w, ml_conv_b, ml_i_bias, ml_f_bias, ml_norm_w, w_out, norm2_w,
                  router_g_w, router_e_w):
    o1 = SSD_WIDTH
    o2 = o1 + SSD_XBC
    o3 = o2 + SSD_HEADS
    o4 = o3 + 2 * ML_WIDTH
    o5 = o4 + ML_WIDTH
    o6 = o5 + ML_WIDTH
    o7 = o6 + ML_HEADS
    zpad = lambda n: jnp.zeros((D_MODEL, n), F32)
    w_gate = jnp.concatenate([w_in[:, o2:o3], zpad(LANES - SSD_HEADS),
                              w_in[:, o6:o7], w_in[:, o7:], zpad(LANES - 2 * ML_HEADS)], axis=1)
    row = lambda v: v.reshape(1, -1).astype(F32)
    w_route = jnp.concatenate([router_e_w, router_g_w,
                               jnp.zeros((D_MODEL, LOGIT_ROWS - N_EXPERTS - MOE_GROUPS), F32)], axis=1).T

    c1024 = np.arange(SSD_WIDTH)
    r64 = np.arange(CHUNK)[:, None]
    e16 = np.tile(c1024[None, :] // SSD_HEAD_DIM == np.arange(SSD_HEADS)[:, None], (3, 1))
    e2 = np.tile(np.arange(2 * ML_WIDTH)[None, :] // LANES == np.arange(2 * ML_HEADS)[:, None], (3, 1))
    tril = np.tile(np.arange(CHUNK)[None, :] <= r64, (1, 3))
    s_pos = (c1024 % SSD_HEAD_DIM)[None, :]
    m_pos = (c1024 % LANES)[None, :]
    as_bf16 = lambda m: jnp.asarray(m.astype(np.float32), BF16)
    as_f32 = lambda m: jnp.asarray(m.astype(np.float32))
    e16, e2, tril = as_bf16(e16), as_bf16(e2), as_bf16(tril)
    causal = lambda keep: jnp.asarray(np.where(keep, 0.0, -np.inf).astype(np.float32))
    seq, sge, meq, mge = as_f32(r64 == s_pos), causal(r64 >= s_pos), as_f32(r64 == m_pos), causal(r64 >= m_pos)
    return [
        row(norm1_w),
        w_in[:, :o1].astype(BF16), w_in[:, o1:o2].astype(BF16), w_in[:, o3:o4].astype(BF16),
        w_in[:, o4:o5].astype(BF16), w_in[:, o5:o6].astype(BF16), w_gate.astype(BF16),
        ssd_conv_w.astype(F32), row(ssd_conv_b), row(ssd_dt_bias),
        row(ssd_a_log), row(jnp.repeat(ssd_d, SSD_HEAD_DIM)), row(ssd_norm_w),
        ml_conv_w.astype(F32), row(ml_conv_b), row(jnp.concatenate([ml_i_bias, ml_f_bias])), row(ml_norm_w),
        w_out.astype(BF16), row(norm2_w), w_route.astype(BF16),
        e16, e2, tril, seq, sge, meq, mge,
    ]


def _route_kernel(lg_ref, bias_ref, striu_ref, lpos_ref, gate_ref, cnt_ref):
    tr = lg_ref.shape[1]
    lg = lg_ref[...] + bias_ref[...]
    el = lg[0:N_EXPERTS]
    gl = lg[N_EXPERTS:N_EXPERTS + 8]
    gmax = jnp.max(gl, axis=0, keepdims=True)
    pg = 1.0 / jnp.sum(jnp.exp(gl - gmax), axis=0, keepdims=True)
    grow = lax.broadcasted_iota(jnp.int32, gl.shape, 0)
    gsel = jnp.min(jnp.where(gl == gmax, grow, 8), axis=0, keepdims=True)
    erow = lax.broadcasted_iota(jnp.int32, el.shape, 0)
    m1 = jnp.where((erow >> 3) == gsel, el, NEG_INF)
    v1 = jnp.max(m1, axis=0, keepdims=True)
    i1 = jnp.min(jnp.where(m1 == v1, erow, N_EXPERTS), axis=0, keepdims=True)
    m2 = jnp.where(erow == i1, NEG_INF, m1)
    v2 = jnp.max(m2, axis=0, keepdims=True)
    i2 = jnp.min(jnp.where(m2 == v2, erow, N_EXPERTS), axis=0, keepdims=True)
    e2 = jnp.exp(v2 - v1)
    g1 = pg / (1.0 + e2)
    g2 = g1 * e2
    hit1 = erow == i1
    hit2 = erow == i2
    oh = jnp.where(hit1, 1.0, jnp.where(hit2, 1.0, 0.0))
    earlier = _dot(oh.astype(BF16), striu_ref[...])
    cnt = jnp.sum(oh, axis=1, keepdims=True)
    seg = jnp.floor((cnt + (ROW_ALIGN - 1)) * (1.0 / ROW_ALIGN)) * ROW_ALIGN
    l1 = jnp.sum(jnp.where(hit1, earlier, jnp.where(erow < i1, seg, 0.0)), axis=0, keepdims=True)
    l2 = jnp.sum(jnp.where(hit2, earlier, jnp.where(erow < i2, seg, 0.0)), axis=0, keepdims=True)
    lpos_ref[...] = jnp.concatenate([l1.astype(jnp.int32), l2.astype(jnp.int32), jnp.zeros((6, tr), jnp.int32)], axis=0)
    gate_ref[...] = jnp.concatenate([g1, g2, jnp.zeros((6, tr), F32)], axis=0)
    cnt_ref[...] = jnp.broadcast_to(cnt, cnt_ref.shape)


def _route_call(lg_t, bias_col, tr):
    t = lg_t.shape[1]
    striu = jnp.asarray(np.triu(np.ones((tr, tr), np.float32), 1), BF16)
    return pl.pallas_call(
        _route_kernel,
        grid=(t // tr,),
        in_specs=[pl.BlockSpec((LOGIT_ROWS, tr), lambda i: (0, i)),
                  pl.BlockSpec((LOGIT_ROWS, 1), lambda i: (0, 0)),
                  pl.BlockSpec((tr, tr), lambda i: (0, 0))],
        out_specs=[pl.BlockSpec((8, tr), lambda i: (0, i)),
                   pl.BlockSpec((8, tr), lambda i: (0, i)),
                   pl.BlockSpec((None, N_EXPERTS, LANES), lambda i: (i, 0, 0))],
        out_shape=[jax.ShapeDtypeStruct((8, t), jnp.int32),
                   jax.ShapeDtypeStruct((8, t), F32),
                   jax.ShapeDtypeStruct((t // tr, N_EXPERTS, LANES), F32)],
        compiler_params=pltpu.CompilerParams(dimension_semantics=("arbitrary",)),
        name="route",
    )(lg_t, bias_col, striu)


def _run_copies(runs_ref, copy_run, max_piece, filler_far, enable=None):
    for e in range(N_EXPERTS + 1):
        dst = runs_ref[0, e] if e < N_EXPERTS else filler_far
        src = runs_ref[1, e]
        n = runs_ref[2, e] if enable is None else jnp.where(enable, runs_ref[2, e], 0)
        p = max_piece
        while p >= ROW_ALIGN:
            done = n & ~(2 * p - 1)

            @pl.when((n & p) != 0)
            def _(p=p, done=done, dst=dst, src=src):
                copy_run(pl.multiple_of(dst + done, ROW_ALIGN), pl.multiple_of(src + done, ROW_ALIGN), p)
            p //= 2


def _dispatch_kernel(runs_ref, prev_runs_ref, lpos_ref, h2_ref, xs_ref, loc, sem, *, max_spare):
    tr = h2_ref.shape[0] // 2
    nloc = loc.shape[1]
    step = pl.program_id(0)
    last = pl.num_programs(0) - 1
    used = runs_ref[0, 0, N_EXPERTS]

    def wait_block(s):
        pltpu.make_async_copy(loc.at[s], xs_ref.at[pl.ds(0, nloc)], sem.at[s]).wait()

    def send_block(table, s, enable=None):
        def copy_run(dst, src, n):
            pltpu.make_async_copy(loc.at[s, pl.ds(src, n)], xs_ref.at[pl.ds(dst, n)], sem.at[s]).start()
        _run_copies(table, copy_run, tr, used + s * (nloc - TOP_K * tr), enable)

    def sort_tile(s):
        cols = slice(s * tr, (s + 1) * tr)
        jrow = lax.broadcasted_iota(jnp.int32, (nloc, tr), 0)
        perm = jnp.where(jrow == lpos_ref[0:1, cols], 1.0, jnp.where(jrow == lpos_ref[1:2, cols], 1.0, 0.0)).astype(BF16)
        loc[s] = _pack_rows(_dot(perm, h2_ref[cols, :]), rounded=True)

    @pl.when(step > 0)
    def _():
        wait_block(0)
    send_block(prev_runs_ref.at[1], 1, enable=step > 0)
    sort_tile(0)

    @pl.when(step > 0)
    def _():
        wait_block(1)
    send_block(runs_ref.at[0], 0)
    sort_tile(1)

    @pl.when(step == last)
    def _():
        send_block(runs_ref.at[1], 1)
        wait_block(0)
        wait_block(1)
        loc[0, 0:tr, :] = jnp.zeros((tr, loc.shape[2]), loc.dtype)
        spare = xs_ref.shape[0] - used

        def zero_copy(off, n):
            return pltpu.make_async_copy(loc.at[0, pl.ds(0, n)],
                                         xs_ref.at[pl.ds(pl.multiple_of(used + off, ROW_ALIGN), n)], sem.at[0])

        def pieces(do):
            for i in range(max_spare // tr):
                @pl.when((i + 1) * tr <= spare)
                def _(i=i):
                    do(zero_copy(i * tr, tr))
            p = tr // 2
            while p >= ROW_ALIGN:
                @pl.when((spare & p) != 0)
                def _(p=p):
                    do(zero_copy(spare & ~(2 * p - 1), p))
                p //= 2

        pieces(lambda c: c.start())
        pieces(lambda c: c.wait())


def _dispatch_call(runs, lpos, h2, tr, n_rows):
    t = h2.shape[0]
    pairs = runs.reshape(t // (2 * tr), 2, 3, N_EXPERTS + 1)
    return pl.pallas_call(
        functools.partial(_dispatch_kernel, max_spare=n_rows - t * TOP_K),
        grid=(t // (2 * tr),),
        in_specs=[pl.BlockSpec((None, 2, 3, N_EXPERTS + 1), lambda i: (i, 0, 0, 0), memory_space=pltpu.SMEM),
                  pl.BlockSpec((None, 2, 3, N_EXPERTS + 1), lambda i: (jnp.maximum(i - 1, 0), 0, 0, 0),
                               memory_space=pltpu.SMEM),
                  pl.BlockSpec((8, 2 * tr), lambda i: (0, i)),
                  pl.BlockSpec((2 * tr, D_MODEL), lambda i: (i, 0))],
        out_specs=pl.BlockSpec(memory_space=pl.ANY),
        out_shape=jax.ShapeDtypeStruct((n_rows, D_MODEL // 2), jnp.uint32),
        scratch_shapes=[pltpu.VMEM((2, _local_rows(tr), D_MODEL // 2), jnp.uint32), pltpu.SemaphoreType.DMA((2,))],
        compiler_params=pltpu.CompilerParams(dimension_semantics=("arbitrary",), vmem_limit_bytes=V7X_VMEM_LIMIT),
        name="dispatch",
    )(pairs, pairs, lpos, h2)


def _experts_kernel(tile_ref, exp_ref, starts_ref, xs_ref, wg_ref, wu_ref, wd_ref, ys_ref, wg_b, wu_b, wd_b):
    i = pl.program_id(0)
    tm = xs_ref.shape[0]
    e = exp_ref[i]
    tile = tile_ref[i]
    is_first = jnp.logical_or(i == 0, tile_ref[jnp.maximum(i - 1, 0)] != tile)

    @pl.when(jnp.logical_and(e < N_EXPERTS, jnp.logical_or(i == 0, exp_ref[jnp.maximum(i - 1, 0)] != e)))
    def _():
        wg_b[...] = wg_ref[...].astype(BF16)
        wu_b[...] = wu_ref[...].astype(BF16)
        wd_b[...] = wd_ref[...].astype(BF16)

    def expert_rows(other):
        x = _unpack_rows(xs_ref[...])
        act = (_silu(_dot(x, wg_b[...])) * _dot(x, wu_b[...])).astype(BF16)
        y = _pack_rows(_dot(act, wd_b[...]), rounded=False)
        rows = tile * tm + lax.broadcasted_iota(jnp.int32, (tm, 1), 0)
        return jnp.where((rows >= starts_ref[e]) & (rows < starts_ref[e + 1]), y, other)

    @pl.when(jnp.logical_and(e < N_EXPERTS, is_first))
    def _():
        ys_ref[...] = expert_rows(jnp.zeros_like(ys_ref))

    @pl.when(jnp.logical_and(e < N_EXPERTS, jnp.logical_not(is_first)))
    def _():
        ys_ref[...] = expert_rows(ys_ref[...])

    @pl.when(e == N_EXPERTS)
    def _():
        ys_ref[...] = jnp.zeros_like(ys_ref)


def _experts_call(item_tile, item_exp, starts_ext, xs, wg, wu, wd, tm):
    n = xs.shape[0]
    n_items = item_tile.shape[0]
    w_idx = lambda i, tile, ex, st: (jnp.minimum(ex[i], N_EXPERTS - 1), 0, 0)
    grid_spec = pltpu.PrefetchScalarGridSpec(
        num_scalar_prefetch=3,
        grid=(n_items,),
        in_specs=[pl.BlockSpec((tm, D_MODEL // 2), lambda i, tile, ex, st: (tile[i], 0)),
                  pl.BlockSpec((None, D_MODEL, D_EXPERT), w_idx),
                  pl.BlockSpec((None, D_MODEL, D_EXPERT), w_idx),
                  pl.BlockSpec((None, D_EXPERT, D_MODEL), w_idx)],
        out_specs=pl.BlockSpec((tm, D_MODEL // 2), lambda i, tile, ex, st: (tile[i], 0)),
        scratch_shapes=[pltpu.VMEM((D_MODEL, D_EXPERT), BF16), pltpu.VMEM((D_MODEL, D_EXPERT), BF16),
                        pltpu.VMEM((D_EXPERT, D_MODEL), BF16)],
    )
    return pl.pallas_call(
        _experts_kernel,
        grid_spec=grid_spec,
        out_shape=jax.ShapeDtypeStruct((n, D_MODEL // 2), jnp.uint32),
        compiler_params=pltpu.CompilerParams(
            dimension_semantics=("arbitrary",), vmem_limit_bytes=V7X_VMEM_LIMIT),
        name="experts",
    )(item_tile, item_exp, starts_ext, xs, wg, wu, wd)


def _local_rows(tr):
    return TOP_K * tr + N_EXPERTS * ROW_ALIGN


def _expert_items(counts, n_rows, tm):
    n_tiles = n_rows // tm
    n_items = n_tiles + N_EXPERTS - 1
    starts = jnp.concatenate([jnp.zeros((1,), jnp.int32), jnp.cumsum(counts).astype(jnp.int32)])
    lo = starts[:-1] // tm
    hi = (starts[1:] - 1) // tm
    per = jnp.where(counts > 0, hi - lo + 1, 0)
    cum = jnp.cumsum(per)
    idx = jnp.arange(n_items, dtype=jnp.int32)
    e = jnp.sum(idx[:, None] >= cum[None, :], axis=1).astype(jnp.int32)
    valid = idx < cum[-1]
    e_c = jnp.minimum(e, N_EXPERTS - 1)
    first_item = jnp.sum(jnp.where(e_c[:, None] == jnp.arange(N_EXPERTS)[None, :], (lo - (cum - per))[None, :], 0), axis=1)
    tile = first_item + idx
    spare_tile = -(-starts[-1] // tm) + idx - cum[-1]
    item_tile = jnp.where(valid, tile, jnp.minimum(spare_tile, n_tiles - 1)).astype(jnp.int32)
    item_exp = jnp.where(valid, e_c, jnp.where(spare_tile < n_tiles, N_EXPERTS, N_EXPERTS + 1)).astype(jnp.int32)
    starts_ext = jnp.concatenate([starts, starts[-1:], starts[-1:]])
    return item_tile, item_exp, starts, starts_ext


def _combine_kernel(runs_ref, next_runs_ref, lpos_ref, gate_ref, next_lpos_ref, next_gate_ref, x1_ref, nf_ref, ys_ref,
                    out_ref, loc, sel_buf, sem):
    tr = x1_ref.shape[0] // 2
    nloc = loc.shape[1]
    step = pl.program_id(0)
    more = step + 1 < pl.num_programs(0)

    def gather(table, s, enable=None):
        def copy_run(dst, src, n):
            pltpu.make_async_copy(ys_ref.at[pl.ds(dst, n)], loc.at[s, pl.ds(src, n)], sem.at[s]).start()
        _run_copies(table, copy_run, tr, 0, enable)

    def build_sel(pos_ref, g_ref, cols, s):
        jrow = lax.broadcasted_iota(jnp.int32, (nloc, tr), 0)
        sel_buf[s] = jnp.where(jrow == pos_ref[0:1, cols], g_ref[0:1, cols],
                               jnp.where(jrow == pos_ref[1:2, cols], g_ref[1:2, cols], 0.0)).astype(BF16)

    def finish(s):
        rows = slice(s * tr, (s + 1) * tr)
        y = _dot_tn(sel_buf[s], _unpack_rows(loc[s]))
        out_ref[rows, :] = _rms(x1_ref[rows, :] + y, nf_ref[...])

    first, second = slice(0, tr), slice(tr, 2 * tr)

    @pl.when(step == 0)
    def _():
        gather(runs_ref.at[0], 0)
        build_sel(lpos_ref, gate_ref, first, 0)

    pltpu.make_async_copy(ys_ref.at[pl.ds(0, nloc)], loc.at[0], sem.at[0]).wait()
    gather(runs_ref.at[1], 1)
    finish(0)
    build_sel(lpos_ref, gate_ref, second, 1)

    pltpu.make_async_copy(ys_ref.at[pl.ds(0, nloc)], loc.at[1], sem.at[1]).wait()
    gather(next_runs_ref.at[0], 0, enable=more)
    finish(1)
    build_sel(next_lpos_ref, next_gate_ref, first, 0)


def _combine_call(runs, lpos, gates, x1, nf_w, ys, tr):
    t = x1.shape[0]
    n_steps = t // (2 * tr)
    pairs = runs.reshape(n_steps, 2, 3, N_EXPERTS + 1)
    nxt = lambda i: jnp.minimum(i + 1, n_steps - 1)
    return pl.pallas_call(
        _combine_kernel,
        grid=(n_steps,),
        in_specs=[pl.BlockSpec((None, 2, 3, N_EXPERTS + 1), lambda i: (i, 0, 0, 0), memory_space=pltpu.SMEM),
                  pl.BlockSpec((None, 2, 3, N_EXPERTS + 1), lambda i: (nxt(i), 0, 0, 0), memory_space=pltpu.SMEM),
                  pl.BlockSpec((8, 2 * tr), lambda i: (0, i)),
                  pl.BlockSpec((8, 2 * tr), lambda i: (0, i)),
                  pl.BlockSpec((8, 2 * tr), lambda i: (0, nxt(i))),
                  pl.BlockSpec((8, 2 * tr), lambda i: (0, nxt(i))),
                  pl.BlockSpec((2 * tr, D_MODEL), lambda i: (i, 0)),
                  pl.BlockSpec((1, D_MODEL), lambda i: (0, 0)),
                  pl.BlockSpec(memory_space=pl.ANY)],
        out_specs=pl.BlockSpec((2 * tr, D_MODEL), lambda i: (i, 0)),
        out_shape=jax.ShapeDtypeStruct((t, D_MODEL), F32),
        scratch_shapes=[pltpu.VMEM((2, _local_rows(tr), D_MODEL // 2), jnp.uint32),
                        pltpu.VMEM((2, _local_rows(tr), tr), BF16), pltpu.SemaphoreType.DMA((2,))],
        compiler_params=pltpu.CompilerParams(dimension_semantics=("arbitrary",), vmem_limit_bytes=V7X_VMEM_LIMIT),
        name="combine",
    )(pairs, pairs, lpos, gates, lpos, gates, x1, nf_w, ys)


SEQ_TILE = 4 * CHUNK
TOKEN_TILE = 512
ROW_TILE = 512


def kernel(x, norm1_w, w_in, ssd_conv_w, ssd_conv_b, ssd_dt_bias, ssd_a_log, ssd_d, ssd_norm_w, ml_conv_w, ml_conv_b, ml_i_bias, ml_f_bias, ml_norm_w, w_out, norm2_w, router_g_w, router_g_b, router_e_w, router_e_b, exp_w_gate, exp_w_up, exp_w_down, norm_f_w):
    bsz, seq, d = x.shape
    t = bsz * seq
    tl, tr, tm = SEQ_TILE, TOKEN_TILE, ROW_TILE
    assert d == D_MODEL and norm1_w.shape[0] == 1 and seq % tl == 0 and t % (2 * tr) == 0

    consts = _mixer_consts(norm1_w[0], w_in[0], ssd_conv_w[0], ssd_conv_b[0], ssd_dt_bias[0], ssd_a_log[0],
                           ssd_d[0], ssd_norm_w[0], ml_conv_w[0], ml_conv_b[0], ml_i_bias[0], ml_f_bias[0],
                           ml_norm_w[0], w_out[0], norm2_w[0], router_g_w[0], router_e_w[0])
    x1, h2, lg_t = _mixer_call(x, consts, tl)
    x1 = x1.reshape(t, D_MODEL)
    h2 = h2.reshape(t, D_MODEL)

    bias_col = jnp.concatenate([router_e_b[0], router_g_b[0],
                                jnp.full((LOGIT_ROWS - N_EXPERTS - MOE_GROUPS,), STAB_INIT, F32)]).reshape(-1, 1)
    lpos, gates, cnt = _route_call(lg_t, bias_col.astype(F32), tr)

    seg = (cnt[:, :, 0].astype(jnp.int32) + (ROW_ALIGN - 1)) // ROW_ALIGN * ROW_ALIGN
    n_tok_tiles = t // tr
    n_rows = -(-(t * TOP_K + n_tok_tiles * N_EXPERTS * ROW_ALIGN + _local_rows(tr)) // tm) * tm
    item_tile, item_exp, starts, starts_ext = _expert_items(jnp.sum(seg, axis=0), n_rows, tm)
    run_dst = starts[None, :N_EXPERTS] + jnp.cumsum(seg, axis=0) - seg
    run_src = jnp.cumsum(seg, axis=1) - seg
    used = jnp.sum(seg, axis=1, keepdims=True)
    fill = jnp.concatenate([jnp.full_like(used, starts[N_EXPERTS]), used, _local_rows(tr) - used], axis=1)[:, :, None]
    runs = jnp.concatenate([jnp.stack([run_dst, run_src, seg], axis=1), fill], axis=2).astype(jnp.int32)

    xs = _dispatch_call(runs, lpos, h2, tr, n_rows)
    ys = _experts_call(item_tile, item_exp, starts_ext, xs, exp_w_gate[0], exp_w_up[0], exp_w_down[0], tm)
    out = _combine_call(runs, lpos, gates, x1, norm_f_w.reshape(1, -1).astype(F32), ys, tr)
    return out.reshape(bsz, seq, D_MODEL)
```

```python
import functools

import jax
import jax.numpy as jnp
import numpy as np
from jax import lax
from jax.experimental import pallas as pl
from jax.experimental.pallas import tpu as pltpu

D_MODEL = 1024
CHUNK = 64
SSD_WIDTH = 1024
SSD_HEAD_DIM = 64
SSD_HEADS = 16
SSD_GROUPS = 2
SSD_STATE = 128
SSD_CONV = 4
SSD_XBC = SSD_WIDTH + 2 * SSD_GROUPS * SSD_STATE
ML_WIDTH = 1024
ML_HEADS = 8
ML_HEAD_DIM = 128
ML_CONV = 4
D_MIX = SSD_WIDTH + ML_WIDTH
MOE_GROUPS = 4
EXPERTS_PER_GROUP = 8
N_EXPERTS = 32
TOP_K = 2
D_EXPERT = 512
EPS = 1e-6
STAB_INIT = -1e30

LANES = 128
GATE_COLS = 2 * LANES
ROW_ALIGN = 8
CONV_PAD = 8
LOGIT_ROWS = 64
V7X_VMEM_LIMIT = 56 * 1024 * 1024

F32 = jnp.float32
BF16 = jnp.bfloat16
NEG_INF = float("-inf")


def _dot(a, b):
    return jnp.dot(a, b, preferred_element_type=F32)


def _dot_nt(a, b):
    return lax.dot_general(a, b, (((1,), (1,)), ((), ())), preferred_element_type=F32)


def _dot_tn(a, b):
    return lax.dot_general(a, b, (((0,), (0,)), ((), ())), preferred_element_type=F32)


def _split3(x, axis):
    hi = x.astype(BF16)
    r1 = x - hi.astype(F32)
    mid = r1.astype(BF16)
    lo = (r1 - mid.astype(F32)).astype(BF16)
    return jnp.concatenate([hi, mid, lo], axis=axis)


def _pack_rows(x, rounded):
    if not rounded:
        x = x.astype(BF16).astype(F32)
    bits = lax.bitcast_convert_type(x, jnp.uint32)
    half = x.shape[1] // 2
    return bits[:, half:] | (bits[:, :half] >> 16)


def _unpack_rows(p):
    lo = lax.bitcast_convert_type(p << 16, F32)
    hi = lax.bitcast_convert_type(p & jnp.uint32(0xFFFF0000), F32)
    return jnp.concatenate([lo, hi], axis=1).astype(BF16)


def _silu(x):
    return x * jax.nn.sigmoid(x)


def _softplus(x):
    return jnp.maximum(x, 0.0) + jnp.log1p(jnp.exp(-jnp.abs(x)))


def _rms(x, w):
    return x * lax.rsqrt(jnp.mean(x * x, axis=-1, keepdims=True) + EPS) * w


def _mixer_kernel(*refs, tl, nl):
    for parity in range(2):
        @pl.when(pl.program_id(0) % 2 == parity)
        def _(parity=parity):
            _mixer_step(*refs, tl=tl, nl=nl, slot_a=parity)


def _mixer_step(x_ref, xc_ref, n1_ref, wz_ref, wxbc_ref, wqk_ref, wv_ref, wo_ref, wg_ref,
                  scw_ref, scb_ref, dtb_ref, alog_ref, dskip_ref, snw_ref,
                  mcw_ref, mcb_ref, gb_ref, mnw_ref, wout_ref, n2_ref, wr_ref,
                  e16_ref, e2_ref, tril_ref, seq_ref, sge_ref, meq_ref, mge_ref,
                  x1_ref, h2_ref, lg_ref,
                  xbc_buf, qk_buf, z_buf, v_buf, o_buf, g_buf, xbc_c, qk_c, mix_buf,
                  xbc_tail, qk_tail, sst, mst, mm, h_buf, *, tl, nl, slot_a):
    nchunk = tl // CHUNK
    step = pl.program_id(0)
    slot_b = 1 - slot_a
    slot_c = slot_a

    @pl.when(step == 0)
    def _():
        xbc_buf[...] = jnp.zeros_like(xbc_buf)
        qk_buf[...] = jnp.zeros_like(qk_buf)
        z_buf[...] = jnp.zeros_like(z_buf)
        v_buf[...] = jnp.zeros_like(v_buf)
        o_buf[...] = jnp.zeros_like(o_buf)
        g_buf[...] = jnp.zeros_like(g_buf)
        mix_buf[...] = jnp.zeros_like(mix_buf)
        xbc_tail[...] = jnp.zeros_like(xbc_tail)
        qk_tail[...] = jnp.zeros_like(qk_tail)
        sst[...] = jnp.zeros_like(sst)
        mst[...] = jnp.zeros_like(mst)
        mm[...] = jnp.full(mm.shape, STAB_INIT, F32)

    h_buf[...] = _rms(x_ref[...], n1_ref[...]).astype(BF16)
    nblk = 2 * LANES
    tasks = []

    def proj_task(w_ref, store):
        for b0 in range(0, w_ref.shape[1], nblk):
            tasks.append(lambda b0=b0: store(slice(b0, b0 + nblk), _dot(h_buf[...], w_ref[:, b0:b0 + nblk])))

    def out_task(cols):
        x1_ref[:, cols] = xc_ref[:, cols] + _dot(mix_buf[slot_c], wout_ref[:, cols])

    def route_task():
        h2 = _rms(x1_ref[...], n2_ref[...])
        h2 = h2.astype(BF16)
        h2_ref[...] = h2
        lg_ref[...] = _dot_nt(wr_ref[...], h2)

    for b0 in range(0, D_MODEL, nblk):
        tasks.append(lambda b0=b0: out_task(slice(b0, b0 + nblk)))
    tasks.append(route_task)

    def put(buf, r0=0):
        def store(cols, val):
            buf[slot_a, r0:r0 + tl, cols] = val
        return store

    def put_tiles(buf):
        def store(cols, val):
            for k in range(nblk // LANES):
                buf[slot_a, cols.start // LANES + k, CONV_PAD:CONV_PAD + tl, :] = val[:, k * LANES:(k + 1) * LANES]
        return store

    proj_task(wz_ref, put(z_buf))
    proj_task(wg_ref, put(g_buf))
    proj_task(wxbc_ref, put_tiles(xbc_buf))
    proj_task(wqk_ref, put_tiles(qk_buf))
    proj_task(wv_ref, put(v_buf))
    proj_task(wo_ref, put(o_buf))
    per_chunk = -(-len(tasks) // nchunk)
    assert per_chunk <= 8

    def run_task(c, k):
        idx = c * per_chunk + k
        if idx < len(tasks):
            tasks[idx]()

    seq_start = (step + nl - 1) % nl == 0
    xbc_buf[slot_b, :, 0:CONV_PAD, :] = jnp.where(seq_start, 0.0, xbc_tail[...])
    qk_buf[slot_b, :, 0:CONV_PAD, :] = jnp.where(seq_start, 0.0, qk_tail[...])
    sst[...] = jnp.where(seq_start, 0.0, sst[...])
    mst[...] = jnp.where(seq_start, 0.0, mst[...])
    mm[...] = jnp.where(seq_start, STAB_INIT, mm[...])

    first = CONV_PAD - (SSD_CONV - 1)
    xbc_tail[...] = xbc_buf[slot_b, :, tl:tl + CONV_PAD, :]
    qk_tail[...] = qk_buf[slot_b, :, tl:tl + CONV_PAD, :]

    def conv_silu(buf, w_ref, b_ref, out, r0):
        for t in range(buf.shape[1]):
            cols = slice(t * LANES, (t + 1) * LANES)
            acc = b_ref[:, cols] + w_ref[0:1, cols] * buf[slot_b, t, r0 + first:r0 + first + CHUNK, :]
            for j in range(1, w_ref.shape[0]):
                acc = acc + w_ref[j:j + 1, cols] * buf[slot_b, t, r0 + first + j:r0 + first + j + CHUNK, :]
            out[r0:r0 + CHUNK, cols] = _silu(acc)

    def convs(c):
        run_task(c, 0)
        conv_silu(xbc_buf, scw_ref, scb_ref, xbc_c, c * CHUNK)
        run_task(c, 1)
        conv_silu(qk_buf, mcw_ref, mcb_ref, qk_c, c * CHUNK)
        run_task(c, 2)

    for c in range(nchunk):
        convs(c)
        rows = slice(c * CHUNK, (c + 1) * CHUNK)
        tril3 = tril_ref[...]

        xs = xbc_c[rows, 0:SSD_WIDTH]
        bm = xbc_c[rows, SSD_WIDTH:SSD_WIDTH + 2 * SSD_STATE].astype(BF16)
        cm = xbc_c[rows, SSD_WIDTH + 2 * SSD_STATE:SSD_XBC].astype(BF16)
        dt = _softplus(g_buf[slot_b, rows, 0:SSD_HEADS] + dtb_ref[...])
        a_cs = _dot(tril3, _split3(dt * (-jnp.exp(alog_ref[...])), 0))
        dt_e = _dot(_split3(dt, 1), e16_ref[...])
        a_col = _dot(_split3(a_cs, 1), e16_ref[...])
        a_row = jnp.sum(a_col * seq_ref[...], axis=0, keepdims=True)
        lmat = jnp.exp(a_col - a_row + sge_ref[...])
        cb = jnp.concatenate(
            [_dot_nt(cm[:, g * SSD_STATE:(g + 1) * SSD_STATE],
                     jnp.concatenate([bm[:, g * SSD_STATE:(g + 1) * SSD_STATE]] * (SSD_HEADS // SSD_GROUPS), axis=0))
             for g in range(SSD_GROUPS)], axis=1)
        m_all = (cb * lmat).astype(BF16)
        run_task(c, 3)
        xdt = xs * dt_e
        lane = lax.broadcasted_iota(jnp.int32, (CHUNK, LANES), 1)
        y_parts = []
        for j in range(SSD_HEADS // 2):
            xp = xdt[:, j * LANES:(j + 1) * LANES]
            xbd = jnp.concatenate([jnp.where(lane < SSD_HEAD_DIM, xp, 0.0),
                                   jnp.where(lane >= SSD_HEAD_DIM, xp, 0.0)], axis=0).astype(BF16)
            y_parts.append(_dot(m_all[:, j * LANES:(j + 1) * LANES], xbd))
        y = jnp.concatenate(y_parts, axis=1)
        half = SSD_WIDTH // SSD_GROUPS
        y_int = jnp.concatenate(
            [_dot(cm[:, g * SSD_STATE:(g + 1) * SSD_STATE], sst[g].astype(BF16)) for g in range(SSD_GROUPS)], axis=1)
        y = y + y_int * jnp.exp(a_col)
        a_last = a_col[CHUNK - 1:CHUNK, :]
        xd_b = (xdt * jnp.exp(a_last - a_col)).astype(BF16)
        st_scale = jnp.exp(a_last)
        for g in range(SSD_GROUPS):
            sst[g] = sst[g] * st_scale[:, g * half:(g + 1) * half] + _dot_tn(
                bm[:, g * SSD_STATE:(g + 1) * SSD_STATE], xd_b[:, g * half:(g + 1) * half])
        y = (y + dskip_ref[...] * xs) * _silu(z_buf[slot_b, rows, :])
        for g in range(SSD_GROUPS):
            yg = y[:, g * half:(g + 1) * half]
            yg = yg * lax.rsqrt(jnp.mean(yg * yg, axis=-1, keepdims=True) + EPS) * snw_ref[:, g * half:(g + 1) * half]
            mix_buf[slot_b, rows, g * half:(g + 1) * half] = yg.astype(BF16)

        run_task(c, 4)

        gi = g_buf[slot_b, rows, LANES:LANES + 2 * ML_HEADS] + gb_ref[...]
        col16 = lax.broadcasted_iota(jnp.int32, (CHUNK, 2 * ML_HEADS), 1)
        gi = jnp.where(col16 < ML_HEADS, gi, jnp.minimum(gi, 0.0) - jnp.log1p(jnp.exp(-jnp.abs(gi))))
        gi = jnp.where(col16 < ML_HEADS, gi, _dot(tril3, _split3(gi, 0)))
        gi_e = _dot(_split3(gi, 1), e2_ref[...])
        ig_e = gi_e[:, 0:ML_WIDTH]
        b_col = gi_e[:, ML_WIDTH:2 * ML_WIDTH]
        c_row = jnp.sum((ig_e - b_col) * meq_ref[...], axis=0, keepdims=True)
        dmat = b_col + c_row + mge_ref[...]
        heads = range(ML_HEADS)
        hs = lambda arr, hd: arr[:, hd * LANES:(hd + 1) * LANES]
        ones_blk = jnp.ones((CHUNK, LANES), BF16)
        m_prev = mm[...]
        m_inter = b_col + m_prev
        q = [qk_c[rows, hd * LANES:(hd + 1) * LANES].astype(BF16) for hd in heads]
        kf = [qk_c[rows, ML_WIDTH + hd * LANES:ML_WIDTH + (hd + 1) * LANES] * (ML_HEAD_DIM ** -0.5) for hd in heads]
        v_aug = [jnp.concatenate([v_buf[slot_b, rows, hd * LANES:(hd + 1) * LANES].astype(BF16), ones_blk], axis=1)
                 for hd in heads]
        s = [_dot_nt(q[hd], kf[hd].astype(BF16)) for hd in heads]
        qc = [_dot(q[hd], mst[hd].astype(BF16)) for hd in heads]
        run_task(c, 5)
        m_t = jnp.maximum(m_inter, jnp.concatenate(
            [jnp.broadcast_to(jnp.max(hs(dmat, hd), axis=-1, keepdims=True), (CHUNK, LANES)) for hd in heads], axis=1))
        w = jnp.exp(dmat - m_t)
        inter = jnp.exp(m_inter - m_t)
        den_floor = jnp.exp(-m_t)
        nd = [_dot((s[hd] * hs(w, hd)[:, 0:CHUNK]).astype(BF16), v_aug[hd])
              + qc[hd] * jnp.concatenate([hs(inter, hd)] * 2, axis=1) for hd in heads]
        run_task(c, 6)
        hh = [nd[hd][:, 0:LANES] / jnp.maximum(jnp.abs(nd[hd][:, LANES:2 * LANES]), hs(den_floor, hd)) for hd in heads]
        hc = [hh[hd] - jnp.mean(hh[hd], axis=-1, keepdims=True) for hd in heads]
        var = [jnp.mean(hc[hd] * hc[hd], axis=-1, keepdims=True) for hd in heads]
        for hd in heads:
            hn = hc[hd] * lax.rsqrt(var[hd] + EPS) * mnw_ref[:, hd * LANES:(hd + 1) * LANES]
            mix_buf[slot_b, rows, SSD_WIDTH + hd * LANES:SSD_WIDTH + (hd + 1) * LANES] = (
                jax.nn.sigmoid(o_buf[slot_b, rows, hd * LANES:(hd + 1) * LANES]) * hn).astype(BF16)
        run_task(c, 7)
        g_row = b_col[CHUNK - 1:CHUNK, :]
        a = g_row - b_col + ig_e
        m_new = jnp.maximum(g_row + m_prev, jnp.max(a, axis=0, keepdims=True))
        wk = jnp.exp(a - m_new)
        cs = jnp.exp(g_row + m_prev - m_new)
        for hd in heads:
            mst[hd] = mst[hd] * jnp.concatenate([hs(cs, hd)] * 2, axis=1) + _dot_tn(
                (kf[hd] * hs(wk, hd)).astype(BF16), v_aug[hd])
        mm[...] = m_new


def _const_spec(shape):
    nd = len(shape)
    return pl.BlockSpec(shape, lambda s: (0,) * nd, pipeline_mode=pl.Buffered(1))


def _mixer_call(x, consts, tl):
    bsz, seq, _ = x.shape
    nl = seq // tl
    n_tiles = bsz * nl
    x_tiles = x.reshape(n_tiles, tl, D_MODEL)
    tile_in = lambda s: (jnp.minimum(s, n_tiles - 1), 0, 0)
    tile_out = lambda s: (jnp.maximum(s - 2, 0), 0, 0)
    in_specs = ([pl.BlockSpec((None, tl, D_MODEL), tile_in), pl.BlockSpec((None, tl, D_MODEL), tile_out)]
                + [_const_spec(c.shape) for c in consts])
    out_specs = [pl.BlockSpec((None, tl, D_MODEL), tile_out), pl.BlockSpec((None, tl, D_MODEL), tile_out),
                 pl.BlockSpec((LOGIT_ROWS, tl), lambda s: (0, jnp.maximum(s - 2, 0)))]
    out_shape = [jax.ShapeDtypeStruct((n_tiles, tl, D_MODEL), F32),
                 jax.ShapeDtypeStruct((n_tiles, tl, D_MODEL), BF16),
                 jax.ShapeDtypeStruct((LOGIT_ROWS, bsz * seq), F32)]
    scratch = [
        pltpu.VMEM((2, SSD_XBC // LANES, tl + CONV_PAD, LANES), F32),
        pltpu.VMEM((2, 2 * ML_WIDTH // LANES, tl + CONV_PAD, LANES), F32),
        pltpu.VMEM((2, tl, SSD_WIDTH), F32),
        pltpu.VMEM((2, tl, ML_WIDTH), F32),
        pltpu.VMEM((2, tl, ML_WIDTH), F32),
        pltpu.VMEM((2, tl, GATE_COLS), F32),
        pltpu.VMEM((tl, SSD_XBC), F32),
        pltpu.VMEM((tl, 2 * ML_WIDTH), F32),
        pltpu.VMEM((2, tl, D_MIX), BF16),
        pltpu.VMEM((SSD_XBC // LANES, CONV_PAD, LANES), F32),
        pltpu.VMEM((2 * ML_WIDTH // LANES, CONV_PAD, LANES), F32),
        pltpu.VMEM((SSD_GROUPS, SSD_STATE, SSD_WIDTH // SSD_GROUPS), F32),
        pltpu.VMEM((ML_HEADS, ML_HEAD_DIM, 2 * LANES), F32),
        pltpu.VMEM((1, ML_WIDTH), F32),
        pltpu.VMEM((tl, D_MODEL), BF16),
    ]
    return pl.pallas_call(
        functools.partial(_mixer_kernel, tl=tl, nl=nl),
        grid=(n_tiles + 2,),
        in_specs=in_specs,
        out_specs=out_specs,
        out_shape=out_shape,
        scratch_shapes=scratch,
        compiler_params=pltpu.CompilerParams(
            dimension_semantics=("arbitrary",), vmem_limit_bytes=V7X_VMEM_LIMIT),
        name="mixer",
    )(x_tiles, x_tiles, *consts)


def _mixer_consts(norm1_w, w_in, ssd_conv_w, ssd_conv_b, ssd_dt_bias, ssd_a_log, ssd_d, ssd_norm_w,
                  ml_conv_w, ml_conv_b, ml_i_bias, ml_f_bias, ml_norm_w, w_out, norm2_w,
                  router_g_w, router_e_w):
    o1 = SSD_WIDTH
    o2 = o1 + SSD_XBC
    o3 = o2 + SSD_HEADS
    o4 = o3 + 2 * ML_WIDTH
    o5 = o4 + ML_WIDTH
    o6 = o5 + ML_WIDTH
    o7 = o6 + ML_HEADS
    zpad = lambda n: jnp.zeros((D_MODEL, n), F32)
    w_gate = jnp.concatenate([w_in[:, o2:o3], zpad(LANES - SSD_HEADS),
                              w_in[:, o6:o7], w_in[:, o7:], zpad(LANES - 2 * ML_HEADS)], axis=1)
    row = lambda v: v.reshape(1, -1).astype(F32)
    w_route = jnp.concatenate([router_e_w, router_g_w,
                               jnp.zeros((D_MODEL, LOGIT_ROWS - N_EXPERTS - MOE_GROUPS), F32)], axis=1).T

    c1024 = np.arange(SSD_WIDTH)
    r64 = np.arange(CHUNK)[:, None]
    e16 = np.tile(c1024[None, :] // SSD_HEAD_DIM == np.arange(SSD_HEADS)[:, None], (3, 1))
    e2 = np.tile(np.arange(2 * ML_WIDTH)[None, :] // LANES == np.arange(2 * ML_HEADS)[:, None], (3, 1))
    tril = np.tile(np.arange(CHUNK)[None, :] <= r64, (1, 3))
    s_pos = (c1024 % SSD_HEAD_DIM)[None, :]
    m_pos = (c1024 % LANES)[None, :]
    as_bf16 = lambda m: jnp.asarray(m.astype(np.float32), BF16)
    as_f32 = lambda m: jnp.asarray(m.astype(np.float32))
    e16, e2, tril = as_bf16(e16), as_bf16(e2), as_bf16(tril)
    causal = lambda keep: jnp.asarray(np.where(keep, 0.0, -np.inf).astype(np.float32))
    seq, sge, meq, mge = as_f32(r64 == s_pos), causal(r64 >= s_pos), as_f32(r64 == m_pos), causal(r64 >= m_pos)
    return [
        row(norm1_w),
        w_in[:, :o1].astype(BF16), w_in[:, o1:o2].astype(BF16), w_in[:, o3:o4].astype(BF16),
        w_in[:, o4:o5].astype(BF16), w_in[:, o5:o6].astype(BF16), w_gate.astype(BF16),
        ssd_conv_w.astype(F32), row(ssd_conv_b), row(ssd_dt_bias),
        row(ssd_a_log), row(jnp.repeat(ssd_d, SSD_HEAD_DIM)), row(ssd_norm_w),
        ml_conv_w.astype(F32), row(ml_conv_b), row(jnp.concatenate([ml_i_bias, ml_f_bias])), row(ml_norm_w),
        w_out.astype(BF16), row(norm2_w), w_route.astype(BF16),
        e16, e2, tril, seq, sge, meq, mge,
    ]


def _route_kernel(lg_ref, bias_ref, striu_ref, lpos_ref, gate_ref, cnt_ref):
    tr = lg_ref.shape[1]
    lg = lg_ref[...] + bias_ref[...]
    el = lg[0:N_EXPERTS]
    gl = lg[N_EXPERTS:N_EXPERTS + 8]
    gmax = jnp.max(gl, axis=0, keepdims=True)
    pg = 1.0 / jnp.sum(jnp.exp(gl - gmax), axis=0, keepdims=True)
    grow = lax.broadcasted_iota(jnp.int32, gl.shape, 0)
    gsel = jnp.min(jnp.where(gl == gmax, grow, 8), axis=0, keepdims=True)
    erow = lax.broadcasted_iota(jnp.int32, el.shape, 0)
    m1 = jnp.where((erow >> 3) == gsel, el, NEG_INF)
    v1 = jnp.max(m1, axis=0, keepdims=True)
    i1 = jnp.min(jnp.where(m1 == v1, erow, N_EXPERTS), axis=0, keepdims=True)
    m2 = jnp.where(erow == i1, NEG_INF, m1)
    v2 = jnp.max(m2, axis=0, keepdims=True)
    i2 = jnp.min(jnp.where(m2 == v2, erow, N_EXPERTS), axis=0, keepdims=True)
    e2 = jnp.exp(v2 - v1)
    g1 = pg / (1.0 + e2)
    g2 = g1 * e2
    hit1 = erow == i1
    hit2 = erow == i2
    oh = jnp.where(hit1, 1.0, jnp.where(hit2, 1.0, 0.0))
    earlier = _dot(oh.astype(BF16), striu_ref[...])
    cnt = jnp.sum(oh, axis=1, keepdims=True)
    seg = jnp.floor((cnt + (ROW_ALIGN - 1)) * (1.0 / ROW_ALIGN)) * ROW_ALIGN
    l1 = jnp.sum(jnp.where(hit1, earlier, jnp.where(erow < i1, seg, 0.0)), axis=0, keepdims=True)
    l2 = jnp.sum(jnp.where(hit2, earlier, jnp.where(erow < i2, seg, 0.0)), axis=0, keepdims=True)
    lpos_ref[...] = jnp.concatenate([l1.astype(jnp.int32), l2.astype(jnp.int32), jnp.zeros((6, tr), jnp.int32)], axis=0)
    gate_ref[...] = jnp.concatenate([g1, g2, jnp.zeros((6, tr), F32)], axis=0)
    cnt_ref[...] = jnp.broadcast_to(cnt, cnt_ref.shape)


def _route_call(lg_t, bias_col, tr):
    t = lg_t.shape[1]
    striu = jnp.asarray(np.triu(np.ones((tr, tr), np.float32), 1), BF16)
    return pl.pallas_call(
        _route_kernel,
        grid=(t // tr,),
        in_specs=[pl.BlockSpec((LOGIT_ROWS, tr), lambda i: (0, i)),
                  pl.BlockSpec((LOGIT_ROWS, 1), lambda i: (0, 0)),
                  pl.BlockSpec((tr, tr), lambda i: (0, 0))],
        out_specs=[pl.BlockSpec((8, tr), lambda i: (0, i)),
                   pl.BlockSpec((8, tr), lambda i: (0, i)),
                   pl.BlockSpec((None, N_EXPERTS, LANES), lambda i: (i, 0, 0))],
        out_shape=[jax.ShapeDtypeStruct((8, t), jnp.int32),
                   jax.ShapeDtypeStruct((8, t), F32),
                   jax.ShapeDtypeStruct((t // tr, N_EXPERTS, LANES), F32)],
        compiler_params=pltpu.CompilerParams(dimension_semantics=("arbitrary",)),
        name="route",
    )(lg_t, bias_col, striu)


def _run_copies(runs_ref, copy_run, max_piece, filler_far, enable=None):
    for e in range(N_EXPERTS + 1):
        dst = runs_ref[0, e] if e < N_EXPERTS else filler_far
        src = runs_ref[1, e]
        n = runs_ref[2, e] if enable is None else jnp.where(enable, runs_ref[2, e], 0)
        p = max_piece
        while p >= ROW_ALIGN:
            done = n & ~(2 * p - 1)

            @pl.when((n & p) != 0)
            def _(p=p, done=done, dst=dst, src=src):
                copy_run(pl.multiple_of(dst + done, ROW_ALIGN), pl.multiple_of(src + done, ROW_ALIGN), p)
            p //= 2


def _dispatch_kernel(runs_ref, prev_runs_ref, lpos_ref, h2_ref, xs_ref, loc, sem, *, max_spare):
    tr = h2_ref.shape[0] // 2
    nloc = loc.shape[1]
    step = pl.program_id(0)
    last = pl.num_programs(0) - 1
    used = runs_ref[0, 0, N_EXPERTS]

    def wait_block(s):
        pltpu.make_async_copy(loc.at[s], xs_ref.at[pl.ds(0, nloc)], sem.at[s]).wait()

    def send_block(table, s, enable=None):
        def copy_run(dst, src, n):
            pltpu.make_async_copy(loc.at[s, pl.ds(src, n)], xs_ref.at[pl.ds(dst, n)], sem.at[s]).start()
        _run_copies(table, copy_run, tr, used + s * (nloc - TOP_K * tr), enable)

    def sort_tile(s):
        cols = slice(s * tr, (s + 1) * tr)
        jrow = lax.broadcasted_iota(jnp.int32, (nloc, tr), 0)
        perm = jnp.where(jrow == lpos_ref[0:1, cols], 1.0, jnp.where(jrow == lpos_ref[1:2, cols], 1.0, 0.0)).astype(BF16)
        loc[s] = _pack_rows(_dot(perm, h2_ref[cols, :]), rounded=True)

    @pl.when(step > 0)
    def _():
        wait_block(0)
    send_block(prev_runs_ref.at[1], 1, enable=step > 0)
    sort_tile(0)

    @pl.when(step > 0)
    def _():
        wait_block(1)
    send_block(runs_ref.at[0], 0)
    sort_tile(1)

    @pl.when(step == last)
    def _():
        send_block(runs_ref.at[1], 1)
        wait_block(0)
        wait_block(1)
        loc[0, 0:tr, :] = jnp.zeros((tr, loc.shape[2]), loc.dtype)
        spare = xs_ref.shape[0] - used

        def zero_copy(off, n):
            return pltpu.make_async_copy(loc.at[0, pl.ds(0, n)],
                                         xs_ref.at[pl.ds(pl.multiple_of(used + off, ROW_ALIGN), n)], sem.at[0])

        def pieces(do):
            for i in range(max_spare // tr):
                @pl.when((i + 1) * tr <= spare)
                def _(i=i):
                    do(zero_copy(i * tr, tr))
            p = tr // 2
            while p >= ROW_ALIGN:
                @pl.when((spare & p) != 0)
                def _(p=p):
                    do(zero_copy(spare & ~(2 * p - 1), p))
                p //= 2

        pieces(lambda c: c.start())
        pieces(lambda c: c.wait())


def _dispatch_call(runs, lpos, h2, tr, n_rows):
    t = h2.shape[0]
    pairs = runs.reshape(t // (2 * tr), 2, 3, N_EXPERTS + 1)
    return pl.pallas_call(
        functools.partial(_dispatch_kernel, max_spare=n_rows - t * TOP_K),
        grid=(t // (2 * tr),),
        in_specs=[pl.BlockSpec((None, 2, 3, N_EXPERTS + 1), lambda i: (i, 0, 0, 0), memory_space=pltpu.SMEM),
                  pl.BlockSpec((None, 2, 3, N_EXPERTS + 1), lambda i: (jnp.maximum(i - 1, 0), 0, 0, 0),
                               memory_space=pltpu.SMEM),
                  pl.BlockSpec((8, 2 * tr), lambda i: (0, i)),
                  pl.BlockSpec((2 * tr, D_MODEL), lambda i: (i, 0))],
        out_specs=pl.BlockSpec(memory_space=pl.ANY),
        out_shape=jax.ShapeDtypeStruct((n_rows, D_MODEL // 2), jnp.uint32),
        scratch_shapes=[pltpu.VMEM((2, _local_rows(tr), D_MODEL // 2), jnp.uint32), pltpu.SemaphoreType.DMA((2,))],
        compiler_params=pltpu.CompilerParams(dimension_semantics=("arbitrary",), vmem_limit_bytes=V7X_VMEM_LIMIT),
        name="dispatch",
    )(pairs, pairs, lpos, h2)


def _experts_kernel(tile_ref, exp_ref, starts_ref, xs_ref, wg_ref, wu_ref, wd_ref, ys_ref, wg_b, wu_b, wd_b):
    i = pl.program_id(0)
    tm = xs_ref.shape[0]
    e = exp_ref[i]
    tile = tile_ref[i]
    is_first = jnp.logical_or(i == 0, tile_ref[jnp.maximum(i - 1, 0)] != tile)

    @pl.when(jnp.logical_and(e < N_EXPERTS, jnp.logical_or(i == 0, exp_ref[jnp.maximum(i - 1, 0)] != e)))
    def _():
        wg_b[...] = wg_ref[...].astype(BF16)
        wu_b[...] = wu_ref[...].astype(BF16)
        wd_b[...] = wd_ref[...].astype(BF16)

    def expert_rows(other):
        x = _unpack_rows(xs_ref[...])
        act = (_silu(_dot(x, wg_b[...])) * _dot(x, wu_b[...])).astype(BF16)
        y = _pack_rows(_dot(act, wd_b[...]), rounded=False)
        rows = tile * tm + lax.broadcasted_iota(jnp.int32, (tm, 1), 0)
        return jnp.where((rows >= starts_ref[e]) & (rows < starts_ref[e + 1]), y, other)

    @pl.when(jnp.logical_and(e < N_EXPERTS, is_first))
    def _():
        ys_ref[...] = expert_rows(jnp.zeros_like(ys_ref))

    @pl.when(jnp.logical_and(e < N_EXPERTS, jnp.logical_not(is_first)))
    def _():
        ys_ref[...] = expert_rows(ys_ref[...])

    @pl.when(e == N_EXPERTS)
    def _():
        ys_ref[...] = jnp.zeros_like(ys_ref)


def _experts_call(item_tile, item_exp, starts_ext, xs, wg, wu, wd, tm):
    n = xs.shape[0]
    n_items = item_tile.shape[0]
    w_idx = lambda i, tile, ex, st: (jnp.minimum(ex[i], N_EXPERTS - 1), 0, 0)
    grid_spec = pltpu.PrefetchScalarGridSpec(
        num_scalar_prefetch=3,
        grid=(n_items,),
        in_specs=[pl.BlockSpec((tm, D_MODEL // 2), lambda i, tile, ex, st: (tile[i], 0)),
                  pl.BlockSpec((None, D_MODEL, D_EXPERT), w_idx),
                  pl.BlockSpec((None, D_MODEL, D_EXPERT), w_idx),
                  pl.BlockSpec((None, D_EXPERT, D_MODEL), w_idx)],
        out_specs=pl.BlockSpec((tm, D_MODEL // 2), lambda i, tile, ex, st: (tile[i], 0)),
        scratch_shapes=[pltpu.VMEM((D_MODEL, D_EXPERT), BF16), pltpu.VMEM((D_MODEL, D_EXPERT), BF16),
                        pltpu.VMEM((D_EXPERT, D_MODEL), BF16)],
    )
    return pl.pallas_call(
        _experts_kernel,
        grid_spec=grid_spec,
        out_shape=jax.ShapeDtypeStruct((n, D_MODEL // 2), jnp.uint32),
        compiler_params=pltpu.CompilerParams(
            dimension_semantics=("arbitrary",), vmem_limit_bytes=V7X_VMEM_LIMIT),
        name="experts",
    )(item_tile, item_exp, starts_ext, xs, wg, wu, wd)


def _local_rows(tr):
    return TOP_K * tr + N_EXPERTS * ROW_ALIGN


def _expert_items(counts, n_rows, tm):
    n_tiles = n_rows // tm
    n_items = n_tiles + N_EXPERTS - 1
    starts = jnp.concatenate([jnp.zeros((1,), jnp.int32), jnp.cumsum(counts).astype(jnp.int32)])
    lo = starts[:-1] // tm
    hi = (starts[1:] - 1) // tm
    per = jnp.where(counts > 0, hi - lo + 1, 0)
    cum = jnp.cumsum(per)
    idx = jnp.arange(n_items, dtype=jnp.int32)
    e = jnp.sum(idx[:, None] >= cum[None, :], axis=1).astype(jnp.int32)
    valid = idx < cum[-1]
    e_c = jnp.minimum(e, N_EXPERTS - 1)
    first_item = jnp.sum(jnp.where(e_c[:, None] == jnp.arange(N_EXPERTS)[None, :], (lo - (cum - per))[None, :], 0), axis=1)
    tile = first_item + idx
    spare_tile = -(-starts[-1] // tm) + idx - cum[-1]
    item_tile = jnp.where(valid, tile, jnp.minimum(spare_tile, n_tiles - 1)).astype(jnp.int32)
    item_exp = jnp.where(valid, e_c, jnp.where(spare_tile < n_tiles, N_EXPERTS, N_EXPERTS + 1)).astype(jnp.int32)
    starts_ext = jnp.concatenate([starts, starts[-1:], starts[-1:]])
    return item_tile, item_exp, starts, starts_ext


def _by_parity(body):
    def kernel(*refs, **kw):
        for parity in range(2):
            @pl.when(pl.program_id(0) % 2 == parity)
            def _(parity=parity):
                body(*refs, slot=parity, **kw)
    return kernel


def _combine_step(runs_ref, next_runs_ref, lpos_ref, gate_ref, next_lpos_ref, next_gate_ref, x1_ref, nf_ref, ys_ref,
                  out_ref, loc, sel_buf, sem, *, slot):
    tr = x1_ref.shape[0]
    nloc = loc.shape[1]
    step = pl.program_id(0)

    def gather(table, s, enable=None):
        def copy_run(dst, src, n):
            pltpu.make_async_copy(ys_ref.at[pl.ds(dst, n)], loc.at[s, pl.ds(src, n)], sem.at[s]).start()
        _run_copies(table, copy_run, tr, 0, enable)

    def build_sel(pos_ref, g_ref, s):
        jrow = lax.broadcasted_iota(jnp.int32, (nloc, tr), 0)
        sel_buf[s] = jnp.where(jrow == pos_ref[0:1, :], g_ref[0:1, :],
                               jnp.where(jrow == pos_ref[1:2, :], g_ref[1:2, :], 0.0)).astype(BF16)

    @pl.when(step == 0)
    def _():
        gather(runs_ref, slot)
        build_sel(lpos_ref, gate_ref, slot)

    pltpu.make_async_copy(ys_ref.at[pl.ds(0, nloc)], loc.at[slot], sem.at[slot]).wait()

    gather(next_runs_ref, 1 - slot, enable=step + 1 < pl.num_programs(0))
    y = _dot_tn(sel_buf[slot], _unpack_rows(loc[slot]))
    build_sel(next_lpos_ref, next_gate_ref, 1 - slot)
    out_ref[...] = _rms(x1_ref[...] + y, nf_ref[...])


def _combine_call(runs, lpos, gates, x1, nf_w, ys, tr):
    t = x1.shape[0]
    return pl.pallas_call(
        _by_parity(_combine_step),
        grid=(t // tr,),
        in_specs=[pl.BlockSpec((None, 3, N_EXPERTS + 1), lambda i: (i, 0, 0), memory_space=pltpu.SMEM),
                  pl.BlockSpec((None, 3, N_EXPERTS + 1), lambda i: (jnp.minimum(i + 1, t // tr - 1), 0, 0),
                               memory_space=pltpu.SMEM),
                  pl.BlockSpec((8, tr), lambda i: (0, i)),
                  pl.BlockSpec((8, tr), lambda i: (0, i)),
                  pl.BlockSpec((8, tr), lambda i: (0, jnp.minimum(i + 1, t // tr - 1))),
                  pl.BlockSpec((8, tr), lambda i: (0, jnp.minimum(i + 1, t // tr - 1))),
                  pl.BlockSpec((tr, D_MODEL), lambda i: (i, 0)),
                  pl.BlockSpec((1, D_MODEL), lambda i: (0, 0)),
                  pl.BlockSpec(memory_space=pl.ANY)],
        out_specs=pl.BlockSpec((tr, D_MODEL), lambda i: (i, 0)),
        out_shape=jax.ShapeDtypeStruct((t, D_MODEL), F32),
        scratch_shapes=[pltpu.VMEM((2, _local_rows(tr), D_MODEL // 2), jnp.uint32),
                        pltpu.VMEM((2, _local_rows(tr), tr), BF16), pltpu.SemaphoreType.DMA((2,))],
        compiler_params=pltpu.CompilerParams(dimension_semantics=("arbitrary",), vmem_limit_bytes=V7X_VMEM_LIMIT),
        name="combine",
    )(runs, runs, lpos, gates, lpos, gates, x1, nf_w, ys)


SEQ_TILE = 4 * CHUNK
TOKEN_TILE = 512
ROW_TILE = 512


def kernel(x, norm1_w, w_in, ssd_conv_w, ssd_conv_b, ssd_dt_bias, ssd_a_log, ssd_d, ssd_norm_w, ml_conv_w, ml_conv_b, ml_i_bias, ml_f_bias, ml_norm_w, w_out, norm2_w, router_g_w, router_g_b, router_e_w, router_e_b, exp_w_gate, exp_w_up, exp_w_down, norm_f_w):
    bsz, seq, d = x.shape
    t = bsz * seq
    tl, tr, tm = SEQ_TILE, TOKEN_TILE, ROW_TILE
    assert d == D_MODEL and norm1_w.shape[0] == 1 and seq % tl == 0 and t % (2 * tr) == 0

    consts = _mixer_consts(norm1_w[0], w_in[0], ssd_conv_w[0], ssd_conv_b[0], ssd_dt_bias[0], ssd_a_log[0],
                           ssd_d[0], ssd_norm_w[0], ml_conv_w[0], ml_conv_b[0], ml_i_bias[0], ml_f_bias[0],
                           ml_norm_w[0], w_out[0], norm2_w[0], router_g_w[0], router_e_w[0])
    x1, h2, lg_t = _mixer_call(x, consts, tl)
    x1 = x1.reshape(t, D_MODEL)
    h2 = h2.reshape(t, D_MODEL)

    bias_col = jnp.concatenate([router_e_b[0], router_g_b[0],
                                jnp.full((LOGIT_ROWS - N_EXPERTS - MOE_GROUPS,), STAB_INIT, F32)]).reshape(-1, 1)
    lpos, gates, cnt = _route_call(lg_t, bias_col.astype(F32), tr)

    seg = (cnt[:, :, 0].astype(jnp.int32) + (ROW_ALIGN - 1)) // ROW_ALIGN * ROW_ALIGN
    n_tok_tiles = t // tr
    n_rows = -(-(t * TOP_K + n_tok_tiles * N_EXPERTS * ROW_ALIGN + _local_rows(tr)) // tm) * tm
    item_tile, item_exp, starts, starts_ext = _expert_items(jnp.sum(seg, axis=0), n_rows, tm)
    run_dst = starts[None, :N_EXPERTS] + jnp.cumsum(seg, axis=0) - seg
    run_src = jnp.cumsum(seg, axis=1) - seg
    used = jnp.sum(seg, axis=1, keepdims=True)
    fill = jnp.concatenate([jnp.full_like(used, starts[N_EXPERTS]), used, _local_rows(tr) - used], axis=1)[:, :, None]
    runs = jnp.concatenate([jnp.stack([run_dst, run_src, seg], axis=1), fill], axis=2).astype(jnp.int32)

    xs = _dispatch_call(runs, lpos, h2, tr, n_rows)
    ys = _experts_call(item_tile, item_exp, starts_ext, xs, exp_w_gate[0], exp_w_up[0], exp_w_down[0], tm)
    out = _combine_call(runs, lpos, gates, x1, norm_f_w.reshape(1, -1).astype(F32), ys, tr)
    return out.reshape(bsz, seq, D_MODEL)
```

```python
import functools

import jax
import jax.numpy as jnp
import numpy as np
from jax import lax
from jax.experimental import pallas as pl
from jax.experimental.pallas import tpu as pltpu

D_MODEL = 1024
CHUNK = 64
SSD_WIDTH = 1024
SSD_HEAD_DIM = 64
SSD_HEADS = 16
SSD_GROUPS = 2
SSD_STATE = 128
SSD_CONV = 4
SSD_XBC = SSD_WIDTH + 2 * SSD_GROUPS * SSD_STATE
ML_WIDTH = 1024
ML_HEADS = 8
ML_HEAD_DIM = 128
ML_CONV = 4
D_MIX = SSD_WIDTH + ML_WIDTH
MOE_GROUPS = 4
EXPERTS_PER_GROUP = 8
N_EXPERTS = 32
TOP_K = 2
D_EXPERT = 512
EPS = 1e-6
STAB_INIT = -1e30

LANES = 128
GATE_COLS = 2 * LANES
ROW_ALIGN = 8
CONV_PAD = 8
LOGIT_ROWS = 64
V7X_VMEM_LIMIT = 60 * 1024 * 1024

F32 = jnp.float32
BF16 = jnp.bfloat16
NEG_INF = float("-inf")


def _dot(a, b):
    return jnp.dot(a, b, preferred_element_type=F32)


def _dot_nt(a, b):
    return lax.dot_general(a, b, (((1,), (1,)), ((), ())), preferred_element_type=F32)


def _dot_tn(a, b):
    return lax.dot_general(a, b, (((0,), (0,)), ((), ())), preferred_element_type=F32)


def _split3(x, axis):
    hi = x.astype(BF16)
    r1 = x - hi.astype(F32)
    mid = r1.astype(BF16)
    lo = (r1 - mid.astype(F32)).astype(BF16)
    return jnp.concatenate([hi, mid, lo], axis=axis)


def _pack_rows(x, rounded):
    if not rounded:
        x = x.astype(BF16).astype(F32)
    bits = lax.bitcast_convert_type(x, jnp.uint32)
    half = x.shape[1] // 2
    return bits[:, half:] | (bits[:, :half] >> 16)


def _unpack_rows(p):
    lo = lax.bitcast_convert_type(p << 16, F32)
    hi = lax.bitcast_convert_type(p & jnp.uint32(0xFFFF0000), F32)
    return jnp.concatenate([lo, hi], axis=1).astype(BF16)


def _silu(x):
    return x * jax.nn.sigmoid(x)


def _softplus(x):
    return jnp.maximum(x, 0.0) + jnp.log1p(jnp.exp(-jnp.abs(x)))


def _rms(x, w):
    return x * lax.rsqrt(jnp.mean(x * x, axis=-1, keepdims=True) + EPS) * w


def _mixer_kernel(x_ref, xc_ref, *refs, tl, nl):
    n_in = len(refs) - 3 - 15
    consts, (x1_ref, h2_ref, lg_ref), scratch = refs[:n_in], refs[n_in:n_in + 3], refs[n_in + 3:]
    for half in range(2):
        rows = pl.ds(half * tl, tl)
        _mixer_step(x_ref.at[rows], xc_ref.at[rows], *consts, x1_ref.at[rows], h2_ref.at[rows],
                    lg_ref.at[:, rows], *scratch, tl=tl, nl=nl, slot_a=half, step=2 * pl.program_id(0) + half)


def _mixer_step(x_ref, xc_ref, n1_ref, wz_ref, wxbc_ref, wqk_ref, wv_ref, wo_ref, wg_ref,
                  scw_ref, scb_ref, dtb_ref, alog_ref, dskip_ref, snw_ref,
                  mcw_ref, mcb_ref, gb_ref, mnw_ref, wout_ref, n2_ref, wr_ref,
                  e16_ref, e2_ref, tril_ref, seq_ref, sge_ref, meq_ref, mge_ref,
                  x1_ref, h2_ref, lg_ref,
                  xbc_buf, qk_buf, z_buf, v_buf, o_buf, g_buf, xbc_c, qk_c, mix_buf,
                  xbc_tail, qk_tail, sst, mst, mm, h_buf, *, tl, nl, slot_a, step):
    nchunk = tl // CHUNK
    slot_b = 1 - slot_a
    slot_c = slot_a

    @pl.when(step == 0)
    def _():
        xbc_buf[...] = jnp.zeros_like(xbc_buf)
        qk_buf[...] = jnp.zeros_like(qk_buf)
        z_buf[...] = jnp.zeros_like(z_buf)
        v_buf[...] = jnp.zeros_like(v_buf)
        o_buf[...] = jnp.zeros_like(o_buf)
        g_buf[...] = jnp.zeros_like(g_buf)
        mix_buf[...] = jnp.zeros_like(mix_buf)
        xbc_tail[...] = jnp.zeros_like(xbc_tail)
        qk_tail[...] = jnp.zeros_like(qk_tail)
        sst[...] = jnp.zeros_like(sst)
        mst[...] = jnp.zeros_like(mst)
        mm[...] = jnp.full(mm.shape, STAB_INIT, F32)

    h_buf[...] = _rms(x_ref[...], n1_ref[...]).astype(BF16)
    nblk = 2 * LANES
    tasks = []

    def proj_task(w_ref, store):
        for b0 in range(0, w_ref.shape[1], nblk):
            tasks.append(lambda b0=b0: store(slice(b0, b0 + nblk), _dot(h_buf[...], w_ref[:, b0:b0 + nblk])))

    def out_task(cols):
        x1_ref[:, cols] = xc_ref[:, cols] + _dot(mix_buf[slot_c], wout_ref[:, cols])

    def route_task():
        h2 = _rms(x1_ref[...], n2_ref[...])
        h2 = h2.astype(BF16)
        h2_ref[...] = h2
        lg_ref[...] = _dot_nt(wr_ref[...], h2)

    for b0 in range(0, D_MODEL, nblk):
        tasks.append(lambda b0=b0: out_task(slice(b0, b0 + nblk)))
    tasks.append(route_task)

    def put(buf):
        def store(cols, val):
            buf[slot_a, :, cols] = val.astype(buf.dtype)
        return store

    def put_tiles(buf):
        def store(cols, val):
            for k in range(nblk // LANES):
                buf[slot_a, cols.start // LANES + k, CONV_PAD:CONV_PAD + tl, :] = val[:, k * LANES:(k + 1) * LANES]
        return store

    proj_task(wz_ref, put(z_buf))
    proj_task(wg_ref, put(g_buf))
    proj_task(wxbc_ref, put_tiles(xbc_buf))
    proj_task(wqk_ref, put_tiles(qk_buf))
    proj_task(wv_ref, put(v_buf))
    proj_task(wo_ref, put(o_buf))
    per_chunk = -(-len(tasks) // nchunk)
    assert per_chunk <= 8

    def run_task(c, k):
        idx = c * per_chunk + k
        if idx < len(tasks):
            tasks[idx]()

    seq_start = (step + nl - 1) % nl == 0
    xbc_buf[slot_b, :, 0:CONV_PAD, :] = jnp.where(seq_start, 0.0, xbc_tail[...])
    qk_buf[slot_b, :, 0:CONV_PAD, :] = jnp.where(seq_start, 0.0, qk_tail[...])
    sst[...] = jnp.where(seq_start, 0.0, sst[...])
    mst[...] = jnp.where(seq_start, 0.0, mst[...])
    mm[...] = jnp.where(seq_start, STAB_INIT, mm[...])

    first = CONV_PAD - (SSD_CONV - 1)
    xbc_tail[...] = xbc_buf[slot_b, :, tl:tl + CONV_PAD, :]
    qk_tail[...] = qk_buf[slot_b, :, tl:tl + CONV_PAD, :]

    def conv_silu(buf, w_ref, b_ref, out, r0):
        for t in range(buf.shape[1]):
            cols = slice(t * LANES, (t + 1) * LANES)
            acc = b_ref[:, cols] + w_ref[0:1, cols] * buf[slot_b, t, r0 + first:r0 + first + CHUNK, :]
            for j in range(1, w_ref.shape[0]):
                acc = acc + w_ref[j:j + 1, cols] * buf[slot_b, t, r0 + first + j:r0 + first + j + CHUNK, :]
            out[r0:r0 + CHUNK, cols] = _silu(acc)

    def convs(c):
        run_task(c, 0)
        conv_silu(xbc_buf, scw_ref, scb_ref, xbc_c, c * CHUNK)
        run_task(c, 1)
        conv_silu(qk_buf, mcw_ref, mcb_ref, qk_c, c * CHUNK)
        run_task(c, 2)

    for c in range(nchunk):
        convs(c)
        rows = slice(c * CHUNK, (c + 1) * CHUNK)
        tril3 = tril_ref[...]

        xs = xbc_c[rows, 0:SSD_WIDTH]
        bm = xbc_c[rows, SSD_WIDTH:SSD_WIDTH + 2 * SSD_STATE].astype(BF16)
        cm = xbc_c[rows, SSD_WIDTH + 2 * SSD_STATE:SSD_XBC].astype(BF16)
        dt = _softplus(g_buf[slot_b, rows, 0:SSD_HEADS] + dtb_ref[...])
        a_cs = _dot(tril3, _split3(dt * (-jnp.exp(alog_ref[...])), 0))
        dt_e = _dot(_split3(dt, 1), e16_ref[...])
        a_col = _dot(_split3(a_cs, 1), e16_ref[...])
        a_row = jnp.sum(a_col * seq_ref[...], axis=0, keepdims=True)
        lmat = jnp.exp(a_col - a_row + sge_ref[...])
        cb = jnp.concatenate(
            [_dot_nt(cm[:, g * SSD_STATE:(g + 1) * SSD_STATE],
                     jnp.concatenate([bm[:, g * SSD_STATE:(g + 1) * SSD_STATE]] * (SSD_HEADS // SSD_GROUPS), axis=0))
             for g in range(SSD_GROUPS)], axis=1)
        m_all = (cb * lmat).astype(BF16)
        run_task(c, 3)
        xdt = xs * dt_e
        lane = lax.broadcasted_iota(jnp.int32, (CHUNK, LANES), 1)
        y_parts = []
        for j in range(SSD_HEADS // 2):
            xp = xdt[:, j * LANES:(j + 1) * LANES]
            xbd = jnp.concatenate([jnp.where(lane < SSD_HEAD_DIM, xp, 0.0),
                                   jnp.where(lane >= SSD_HEAD_DIM, xp, 0.0)], axis=0).astype(BF16)
            y_parts.append(_dot(m_all[:, j * LANES:(j + 1) * LANES], xbd))
        y = jnp.concatenate(y_parts, axis=1)
        half = SSD_WIDTH // SSD_GROUPS
        y_int = jnp.concatenate(
            [_dot(cm[:, g * SSD_STATE:(g + 1) * SSD_STATE], sst[g].astype(BF16)) for g in range(SSD_GROUPS)], axis=1)
        y = y + y_int * jnp.exp(a_col)
        a_last = a_col[CHUNK - 1:CHUNK, :]
        xd_b = (xdt * jnp.exp(a_last - a_col)).astype(BF16)
        st_scale = jnp.exp(a_last)
        for g in range(SSD_GROUPS):
            sst[g] = sst[g] * st_scale[:, g * half:(g + 1) * half] + _dot_tn(
                bm[:, g * SSD_STATE:(g + 1) * SSD_STATE], xd_b[:, g * half:(g + 1) * half])
        y = (y + dskip_ref[...] * xs) * _silu(z_buf[slot_b, rows, :])
        for g in range(SSD_GROUPS):
            yg = y[:, g * half:(g + 1) * half]
            yg = yg * lax.rsqrt(jnp.mean(yg * yg, axis=-1, keepdims=True) + EPS) * snw_ref[:, g * half:(g + 1) * half]
            mix_buf[slot_b, rows, g * half:(g + 1) * half] = yg.astype(BF16)

        run_task(c, 4)

        gi = g_buf[slot_b, rows, LANES:LANES + 2 * ML_HEADS] + gb_ref[...]
        col16 = lax.broadcasted_iota(jnp.int32, (CHUNK, 2 * ML_HEADS), 1)
        gi = jnp.where(col16 < ML_HEADS, gi, jnp.minimum(gi, 0.0) - jnp.log1p(jnp.exp(-jnp.abs(gi))))
        gi = jnp.where(col16 < ML_HEADS, gi, _dot(tril3, _split3(gi, 0)))
        gi_e = _dot(_split3(gi, 1), e2_ref[...])
        ig_e = gi_e[:, 0:ML_WIDTH]
        b_col = gi_e[:, ML_WIDTH:2 * ML_WIDTH]
        c_row = jnp.sum((ig_e - b_col) * meq_ref[...], axis=0, keepdims=True)
        dmat = b_col + c_row + mge_ref[...]
        heads = range(ML_HEADS)
        hs = lambda arr, hd: arr[:, hd * LANES:(hd + 1) * LANES]
        ones_blk = jnp.ones((CHUNK, LANES), BF16)
        m_prev = mm[...]
        m_inter = b_col + m_prev
        q = [qk_c[rows, hd * LANES:(hd + 1) * LANES].astype(BF16) for hd in heads]
        kf = [qk_c[rows, ML_WIDTH + hd * LANES:ML_WIDTH + (hd + 1) * LANES] * (ML_HEAD_DIM ** -0.5) for hd in heads]
        v_aug = [jnp.concatenate([v_buf[slot_b, rows, hd * LANES:(hd + 1) * LANES], ones_blk], axis=1)
                 for hd in heads]
        s = [_dot_nt(q[hd], kf[hd].astype(BF16)) for hd in heads]
        qc = [_dot(q[hd], mst[hd].astype(BF16)) for hd in heads]
        run_task(c, 5)
        m_t = jnp.maximum(m_inter, jnp.concatenate(
            [jnp.broadcast_to(jnp.max(hs(dmat, hd), axis=-1, keepdims=True), (CHUNK, LANES)) for hd in heads], axis=1))
        w = jnp.exp(dmat - m_t)
        inter = jnp.exp(m_inter - m_t)
        den_floor = jnp.exp(-m_t)
        nd = [_dot((s[hd] * hs(w, hd)[:, 0:CHUNK]).astype(BF16), v_aug[hd])
              + qc[hd] * jnp.concatenate([hs(inter, hd)] * 2, axis=1) for hd in heads]
        run_task(c, 6)
        hh = [nd[hd][:, 0:LANES] / jnp.maximum(jnp.abs(nd[hd][:, LANES:2 * LANES]), hs(den_floor, hd)) for hd in heads]
        hc = [hh[hd] - jnp.mean(hh[hd], axis=-1, keepdims=True) for hd in heads]
        var = [jnp.mean(hc[hd] * hc[hd], axis=-1, keepdims=True) for hd in heads]
        for hd in heads:
            hn = hc[hd] * lax.rsqrt(var[hd] + EPS) * mnw_ref[:, hd * LANES:(hd + 1) * LANES]
            mix_buf[slot_b, rows, SSD_WIDTH + hd * LANES:SSD_WIDTH + (hd + 1) * LANES] = (
                jax.nn.sigmoid(o_buf[slot_b, rows, hd * LANES:(hd + 1) * LANES]) * hn).astype(BF16)
        run_task(c, 7)
        g_row = b_col[CHUNK - 1:CHUNK, :]
        a = g_row - b_col + ig_e
        m_new = jnp.maximum(g_row + m_prev, jnp.max(a, axis=0, keepdims=True))
        wk = jnp.exp(a - m_new)
        cs = jnp.exp(g_row + m_prev - m_new)
        for hd in heads:
            mst[hd] = mst[hd] * jnp.concatenate([hs(cs, hd)] * 2, axis=1) + _dot_tn(
                (kf[hd] * hs(wk, hd)).astype(BF16), v_aug[hd])
        mm[...] = m_new


def _const_spec(shape):
    nd = len(shape)
    return pl.BlockSpec(shape, lambda s: (0,) * nd, pipeline_mode=pl.Buffered(1))


def _mixer_call(x, consts, tl):
    bsz, seq, _ = x.shape
    nl = seq // tl
    n_pairs = bsz * nl // 2
    x_pairs = x.reshape(n_pairs, 2 * tl, D_MODEL)
    pair_in = lambda i: (jnp.minimum(i, n_pairs - 1), 0, 0)
    pair_out = lambda i: (jnp.maximum(i - 1, 0), 0, 0)
    in_specs = ([pl.BlockSpec((None, 2 * tl, D_MODEL), pair_in), pl.BlockSpec((None, 2 * tl, D_MODEL), pair_out)]
                + [_const_spec(c.shape) for c in consts])
    out_specs = [pl.BlockSpec((None, 2 * tl, D_MODEL), pair_out), pl.BlockSpec((None, 2 * tl, D_MODEL), pair_out),
                 pl.BlockSpec((LOGIT_ROWS, 2 * tl), lambda i: (0, jnp.maximum(i - 1, 0)))]
    out_shape = [jax.ShapeDtypeStruct((n_pairs, 2 * tl, D_MODEL), F32),
                 jax.ShapeDtypeStruct((n_pairs, 2 * tl, D_MODEL), BF16),
                 jax.ShapeDtypeStruct((LOGIT_ROWS, bsz * seq), F32)]
    scratch = [
        pltpu.VMEM((2, SSD_XBC // LANES, tl + CONV_PAD, LANES), F32),
        pltpu.VMEM((2, 2 * ML_WIDTH // LANES, tl + CONV_PAD, LANES), F32),
        pltpu.VMEM((2, tl, SSD_WIDTH), F32),
        pltpu.VMEM((2, tl, ML_WIDTH), BF16),
        pltpu.VMEM((2, tl, ML_WIDTH), F32),
        pltpu.VMEM((2, tl, GATE_COLS), F32),
        pltpu.VMEM((tl, SSD_XBC), F32),
        pltpu.VMEM((tl, 2 * ML_WIDTH), F32),
        pltpu.VMEM((2, tl, D_MIX), BF16),
        pltpu.VMEM((SSD_XBC // LANES, CONV_PAD, LANES), F32),
        pltpu.VMEM((2 * ML_WIDTH // LANES, CONV_PAD, LANES), F32),
        pltpu.VMEM((SSD_GROUPS, SSD_STATE, SSD_WIDTH // SSD_GROUPS), F32),
        pltpu.VMEM((ML_HEADS, ML_HEAD_DIM, 2 * LANES), F32),
        pltpu.VMEM((1, ML_WIDTH), F32),
        pltpu.VMEM((tl, D_MODEL), BF16),
    ]
    return pl.pallas_call(
        functools.partial(_mixer_kernel, tl=tl, nl=nl),
        grid=(n_pairs + 1,),
        in_specs=in_specs,
        out_specs=out_specs,
        out_shape=out_shape,
        scratch_shapes=scratch,
        compiler_params=pltpu.CompilerParams(
            dimension_semantics=("arbitrary",), vmem_limit_bytes=V7X_VMEM_LIMIT),
        name="mixer",
    )(x_pairs, x_pairs, *consts)


def _mixer_consts(norm1_w, w_in, ssd_conv_w, ssd_conv_b, ssd_dt_bias, ssd_a_log, ssd_d, ssd_norm_w,
                  ml_conv_w, ml_conv_b, ml_i_bias, ml_f_bias, ml_norm_w, w_out, norm2_w,
                  router_g_w, router_e_w):
    o1 = SSD_WIDTH
    o2 = o1 + SSD_XBC
    o3 = o2 + SSD_HEADS
    o4 = o3 + 2 * ML_WIDTH
    o5 = o4 + ML_WIDTH
    o6 = o5 + ML_WIDTH
    o7 = o6 + ML_HEADS
    zpad = lambda n: jnp.zeros((D_MODEL, n), F32)
    w_gate = jnp.concatenate([w_in[:, o2:o3], zpad(LANES - SSD_HEADS),
                              w_in[:, o6:o7], w_in[:, o7:], zpad(LANES - 2 * ML_HEADS)], axis=1)
    row = lambda v: v.reshape(1, -1).astype(F32)
    w_route = jnp.concatenate([router_e_w, router_g_w,
                               jnp.zeros((D_MODEL, LOGIT_ROWS - N_EXPERTS - MOE_GROUPS), F32)], axis=1).T

    c1024 = np.arange(SSD_WIDTH)
    r64 = np.arange(CHUNK)[:, None]
    e16 = np.tile(c1024[None, :] // SSD_HEAD_DIM == np.arange(SSD_HEADS)[:, None], (3, 1))
    e2 = np.tile(np.arange(2 * ML_WIDTH)[None, :] // LANES == np.arange(2 * ML_HEADS)[:, None], (3, 1))
    tril = np.tile(np.arange(CHUNK)[None, :] <= r64, (1, 3))
    s_pos = (c1024 % SSD_HEAD_DIM)[None, :]
    m_pos = (c1024 % LANES)[None, :]
    as_bf16 = lambda m: jnp.asarray(m.astype(np.float32), BF16)
    as_f32 = lambda m: jnp.asarray(m.astype(np.float32))
    e16, e2, tril = as_bf16(e16), as_bf16(e2), as_bf16(tril)
    causal = lambda keep: jnp.asarray(np.where(keep, 0.0, -np.inf).astype(np.float32))
    seq, sge, meq, mge = as_f32(r64 == s_pos), causal(r64 >= s_pos), as_f32(r64 == m_pos), causal(r64 >= m_pos)
    return [
        row(norm1_w),
        w_in[:, :o1].astype(BF16), w_in[:, o1:o2].astype(BF16), w_in[:, o3:o4].astype(BF16),
        w_in[:, o4:o5].astype(BF16), w_in[:, o5:o6].astype(BF16), w_gate.astype(BF16),
        ssd_conv_w.astype(F32), row(ssd_conv_b), row(ssd_dt_bias),
        row(ssd_a_log), row(jnp.repeat(ssd_d, SSD_HEAD_DIM)), row(ssd_norm_w),
        ml_conv_w.astype(F32), row(ml_conv_b), row(jnp.concatenate([ml_i_bias, ml_f_bias])), row(ml_norm_w),
        w_out.astype(BF16), row(norm2_w), w_route.astype(BF16),
        e16, e2, tril, seq, sge, meq, mge,
    ]


def _route_kernel(lg_ref, bias_ref, striu_ref, lpos_ref, gate_ref, cnt_ref):
    tr = lg_ref.shape[1]
    lg = lg_ref[...] + bias_ref[...]
    el = lg[0:N_EXPERTS]
    gl = lg[N_EXPERTS:N_EXPERTS + 8]
    gmax = jnp.max(gl, axis=0, keepdims=True)
    pg = 1.0 / jnp.sum(jnp.exp(gl - gmax), axis=0, keepdims=True)
    grow = lax.broadcasted_iota(jnp.int32, gl.shape, 0)
    gsel = jnp.min(jnp.where(gl == gmax, grow, 8), axis=0, keepdims=True)
    erow = lax.broadcasted_iota(jnp.int32, el.shape, 0)
    m1 = jnp.where((erow >> 3) == gsel, el, NEG_INF)
    v1 = jnp.max(m1, axis=0, keepdims=True)
    i1 = jnp.min(jnp.where(m1 == v1, erow, N_EXPERTS), axis=0, keepdims=True)
    m2 = jnp.where(erow == i1, NEG_INF, m1)
    v2 = jnp.max(m2, axis=0, keepdims=True)
    i2 = jnp.min(jnp.where(m2 == v2, erow, N_EXPERTS), axis=0, keepdims=True)
    e2 = jnp.exp(v2 - v1)
    g1 = pg / (1.0 + e2)
    g2 = g1 * e2
    hit1 = erow == i1
    hit2 = erow == i2
    oh = jnp.where(hit1, 1.0, jnp.where(hit2, 1.0, 0.0))
    earlier = _dot(oh.astype(BF16), striu_ref[...])
    cnt = jnp.sum(oh, axis=1, keepdims=True)
    seg = jnp.floor((cnt + (ROW_ALIGN - 1)) * (1.0 / ROW_ALIGN)) * ROW_ALIGN
    l1 = jnp.sum(jnp.where(hit1, earlier, jnp.where(erow < i1, seg, 0.0)), axis=0, keepdims=True)
    l2 = jnp.sum(jnp.where(hit2, earlier, jnp.where(erow < i2, seg, 0.0)), axis=0, keepdims=True)
    lpos_ref[...] = jnp.concatenate([l1.astype(jnp.int32), l2.astype(jnp.int32), jnp.zeros((6, tr), jnp.int32)], axis=0)
    gate_ref[...] = jnp.concatenate([g1, g2, jnp.zeros((6, tr), F32)], axis=0)
    cnt_ref[...] = jnp.broadcast_to(cnt, cnt_ref.shape)


def _route_call(lg_t, bias_col, tr):
    t = lg_t.shape[1]
    striu = jnp.asarray(np.triu(np.ones((tr, tr), np.float32), 1), BF16)
    return pl.pallas_call(
        _route_kernel,
        grid=(t // tr,),
        in_specs=[pl.BlockSpec((LOGIT_ROWS, tr), lambda i: (0, i)),
                  pl.BlockSpec((LOGIT_ROWS, 1), lambda i: (0, 0)),
                  pl.BlockSpec((tr, tr), lambda i: (0, 0))],
        out_specs=[pl.BlockSpec((8, tr), lambda i: (0, i)),
                   pl.BlockSpec((8, tr), lambda i: (0, i)),
                   pl.BlockSpec((None, N_EXPERTS, LANES), lambda i: (i, 0, 0))],
        out_shape=[jax.ShapeDtypeStruct((8, t), jnp.int32),
                   jax.ShapeDtypeStruct((8, t), F32),
                   jax.ShapeDtypeStruct((t // tr, N_EXPERTS, LANES), F32)],
        compiler_params=pltpu.CompilerParams(dimension_semantics=("arbitrary",)),
        name="route",
    )(lg_t, bias_col, striu)


def _run_copies(runs_ref, copy_run, max_piece, filler_far, enable=None):
    for e in range(N_EXPERTS + 1):
        dst = runs_ref[0, e] if e < N_EXPERTS else filler_far
        src = runs_ref[1, e]
        n = runs_ref[2, e] if enable is None else jnp.where(enable, runs_ref[2, e], 0)
        p = max_piece
        while p >= ROW_ALIGN:
            done = n & ~(2 * p - 1)

            @pl.when((n & p) != 0)
            def _(p=p, done=done, dst=dst, src=src):
                copy_run(pl.multiple_of(dst + done, ROW_ALIGN), pl.multiple_of(src + done, ROW_ALIGN), p)
            p //= 2


def _dispatch_kernel(runs_ref, prev_runs_ref, lpos_ref, h2_ref, xs_ref, loc, sem, *, max_spare):
    tr = h2_ref.shape[0] // 2
    nloc = loc.shape[1]
    step = pl.program_id(0)
    last = pl.num_programs(0) - 1
    used = runs_ref[0, 0, N_EXPERTS]

    def wait_block(s):
        pltpu.make_async_copy(loc.at[s], xs_ref.at[pl.ds(0, nloc)], sem.at[s]).wait()

    def send_block(table, s, enable=None):
        def copy_run(dst, src, n):
            pltpu.make_async_copy(loc.at[s, pl.ds(src, n)], xs_ref.at[pl.ds(dst, n)], sem.at[s]).start()
        _run_copies(table, copy_run, tr, used + s * (nloc - TOP_K * tr), enable)

    def sort_tile(s):
        cols = slice(s * tr, (s + 1) * tr)
        jrow = lax.broadcasted_iota(jnp.int32, (nloc, tr), 0)
        perm = jnp.where(jrow == lpos_ref[0:1, cols], 1.0, jnp.where(jrow == lpos_ref[1:2, cols], 1.0, 0.0)).astype(BF16)
        loc[s] = _pack_rows(_dot(perm, h2_ref[cols, :]), rounded=True)

    @pl.when(step > 0)
    def _():
        wait_block(0)
    send_block(prev_runs_ref.at[1], 1, enable=step > 0)
    sort_tile(0)

    @pl.when(step > 0)
    def _():
        wait_block(1)
    send_block(runs_ref.at[0], 0)
    sort_tile(1)

    @pl.when(step == last)
    def _():
        send_block(runs_ref.at[1], 1)
        wait_block(0)
        wait_block(1)
        loc[0, 0:tr, :] = jnp.zeros((tr, loc.shape[2]), loc.dtype)
        spare = xs_ref.shape[0] - used

        def zero_copy(off, n):
            return pltpu.make_async_copy(loc.at[0, pl.ds(0, n)],
                                         xs_ref.at[pl.ds(pl.multiple_of(used + off, ROW_ALIGN), n)], sem.at[0])

        def pieces(do):
            for i in range(max_spare // tr):
                @pl.when((i + 1) * tr <= spare)
                def _(i=i):
                    do(zero_copy(i * tr, tr))
            p = tr // 2
            while p >= ROW_ALIGN:
                @pl.when((spare & p) != 0)
                def _(p=p):
                    do(zero_copy(spare & ~(2 * p - 1), p))
                p //= 2

        pieces(lambda c: c.start())
        pieces(lambda c: c.wait())


def _dispatch_call(runs, lpos, h2, tr, n_rows):
    t = h2.shape[0]
    pairs = runs.reshape(t // (2 * tr), 2, 3, N_EXPERTS + 1)
    return pl.pallas_call(
        functools.partial(_dispatch_kernel, max_spare=n_rows - t * TOP_K),
        grid=(t // (2 * tr),),
        in_specs=[pl.BlockSpec((None, 2, 3, N_EXPERTS + 1), lambda i: (i, 0, 0, 0), memory_space=pltpu.SMEM),
                  pl.BlockSpec((None, 2, 3, N_EXPERTS + 1), lambda i: (jnp.maximum(i - 1, 0), 0, 0, 0),
                               memory_space=pltpu.SMEM),
                  pl.BlockSpec((8, 2 * tr), lambda i: (0, i)),
                  pl.BlockSpec((2 * tr, D_MODEL), lambda i: (i, 0))],
        out_specs=pl.BlockSpec(memory_space=pl.ANY),
        out_shape=jax.ShapeDtypeStruct((n_rows, D_MODEL // 2), jnp.uint32),
        scratch_shapes=[pltpu.VMEM((2, _local_rows(tr), D_MODEL // 2), jnp.uint32), pltpu.SemaphoreType.DMA((2,))],
        compiler_params=pltpu.CompilerParams(dimension_semantics=("arbitrary",), vmem_limit_bytes=V7X_VMEM_LIMIT),
        name="dispatch",
    )(pairs, pairs, lpos, h2)


def _experts_kernel(tile_ref, exp_ref, starts_ref, xs_ref, wg_ref, wu_ref, wd_ref, ys_ref, wg_b, wu_b, wd_b):
    i = pl.program_id(0)
    tm = xs_ref.shape[0]
    e = exp_ref[i]
    tile = tile_ref[i]
    is_first = jnp.logical_or(i == 0, tile_ref[jnp.maximum(i - 1, 0)] != tile)

    @pl.when(jnp.logical_and(e < N_EXPERTS, jnp.logical_or(i == 0, exp_ref[jnp.maximum(i - 1, 0)] != e)))
    def _():
        wg_b[...] = wg_ref[...].astype(BF16)
        wu_b[...] = wu_ref[...].astype(BF16)
        wd_b[...] = wd_ref[...].astype(BF16)

    def expert_rows(other):
        x = _unpack_rows(xs_ref[...])
        act = (_silu(_dot(x, wg_b[...])) * _dot(x, wu_b[...])).astype(BF16)
        y = _pack_rows(_dot(act, wd_b[...]), rounded=False)
        rows = tile * tm + lax.broadcasted_iota(jnp.int32, (tm, 1), 0)
        return jnp.where((rows >= starts_ref[e]) & (rows < starts_ref[e + 1]), y, other)

    @pl.when(jnp.logical_and(e < N_EXPERTS, is_first))
    def _():
        ys_ref[...] = expert_rows(jnp.zeros_like(ys_ref))

    @pl.when(jnp.logical_and(e < N_EXPERTS, jnp.logical_not(is_first)))
    def _():
        ys_ref[...] = expert_rows(ys_ref[...])

    @pl.when(e == N_EXPERTS)
    def _():
        ys_ref[...] = jnp.zeros_like(ys_ref)


def _experts_call(item_tile, item_exp, starts_ext, xs, wg, wu, wd, tm):
    n = xs.shape[0]
    n_items = item_tile.shape[0]
    w_idx = lambda i, tile, ex, st: (jnp.minimum(ex[i], N_EXPERTS - 1), 0, 0)
    grid_spec = pltpu.PrefetchScalarGridSpec(
        num_scalar_prefetch=3,
        grid=(n_items,),
        in_specs=[pl.BlockSpec((tm, D_MODEL // 2), lambda i, tile, ex, st: (tile[i], 0)),
                  pl.BlockSpec((None, D_MODEL, D_EXPERT), w_idx),
                  pl.BlockSpec((None, D_MODEL, D_EXPERT), w_idx),
                  pl.BlockSpec((None, D_EXPERT, D_MODEL), w_idx)],
        out_specs=pl.BlockSpec((tm, D_MODEL // 2), lambda i, tile, ex, st: (tile[i], 0)),
        scratch_shapes=[pltpu.VMEM((D_MODEL, D_EXPERT), BF16), pltpu.VMEM((D_MODEL, D_EXPERT), BF16),
                        pltpu.VMEM((D_EXPERT, D_MODEL), BF16)],
    )
    return pl.pallas_call(
        _experts_kernel,
        grid_spec=grid_spec,
        out_shape=jax.ShapeDtypeStruct((n, D_MODEL // 2), jnp.uint32),
        compiler_params=pltpu.CompilerParams(
            dimension_semantics=("arbitrary",), vmem_limit_bytes=V7X_VMEM_LIMIT),
        name="experts",
    )(item_tile, item_exp, starts_ext, xs, wg, wu, wd)


def _local_rows(tr):
    return TOP_K * tr + N_EXPERTS * ROW_ALIGN


def _expert_items(counts, n_rows, tm):
    n_tiles = n_rows // tm
    n_items = n_tiles + N_EXPERTS - 1
    starts = jnp.concatenate([jnp.zeros((1,), jnp.int32), jnp.cumsum(counts).astype(jnp.int32)])
    lo = starts[:-1] // tm
    hi = (starts[1:] - 1) // tm
    per = jnp.where(counts > 0, hi - lo + 1, 0)
    cum = jnp.cumsum(per)
    idx = jnp.arange(n_items, dtype=jnp.int32)
    e = jnp.sum(idx[:, None] >= cum[None, :], axis=1).astype(jnp.int32)
    valid = idx < cum[-1]
    e_c = jnp.minimum(e, N_EXPERTS - 1)
    first_item = jnp.sum(jnp.where(e_c[:, None] == jnp.arange(N_EXPERTS)[None, :], (lo - (cum - per))[None, :], 0), axis=1)
    tile = first_item + idx
    spare_tile = -(-starts[-1] // tm) + idx - cum[-1]
    item_tile = jnp.where(valid, tile, jnp.minimum(spare_tile, n_tiles - 1)).astype(jnp.int32)
    item_exp = jnp.where(valid, e_c, jnp.where(spare_tile < n_tiles, N_EXPERTS, N_EXPERTS + 1)).astype(jnp.int32)
    starts_ext = jnp.concatenate([starts, starts[-1:], starts[-1:]])
    return item_tile, item_exp, starts, starts_ext


def _by_parity(body):
    def kernel(*refs, **kw):
        for parity in range(2):
            @pl.when(pl.program_id(0) % 2 == parity)
            def _(parity=parity):
                body(*refs, slot=parity, **kw)
    return kernel


def _combine_step(runs_ref, next_runs_ref, lpos_ref, gate_ref, next_lpos_ref, next_gate_ref, x1_ref, nf_ref, ys_ref,
                  out_ref, loc, sel_buf, sem, *, slot):
    tr = x1_ref.shape[0]
    nloc = loc.shape[1]
    step = pl.program_id(0)

    def gather(table, s, enable=None):
        def copy_run(dst, src, n):
            pltpu.make_async_copy(ys_ref.at[pl.ds(dst, n)], loc.at[s, pl.ds(src, n)], sem.at[s]).start()
        _run_copies(table, copy_run, tr, 0, enable)

    def build_sel(pos_ref, g_ref, s):
        jrow = lax.broadcasted_iota(jnp.int32, (nloc, tr), 0)
        sel_buf[s] = jnp.where(jrow == pos_ref[0:1, :], g_ref[0:1, :],
                               jnp.where(jrow == pos_ref[1:2, :], g_ref[1:2, :], 0.0)).astype(BF16)

    @pl.when(step == 0)
    def _():
        gather(runs_ref, slot)
        build_sel(lpos_ref, gate_ref, slot)

    pltpu.make_async_copy(ys_ref.at[pl.ds(0, nloc)], loc.at[slot], sem.at[slot]).wait()

    gather(next_runs_ref, 1 - slot, enable=step + 1 < pl.num_programs(0))
    y = _dot_tn(sel_buf[slot], _unpack_rows(loc[slot]))
    build_sel(next_lpos_ref, next_gate_ref, 1 - slot)
    out_ref[...] = _rms(x1_ref[...] + y, nf_ref[...])


def _combine_call(runs, lpos, gates, x1, nf_w, ys, tr):
    t = x1.shape[0]
    return pl.pallas_call(
        _by_parity(_combine_step),
        grid=(t // tr,),
        in_specs=[pl.BlockSpec((None, 3, N_EXPERTS + 1), lambda i: (i, 0, 0), memory_space=pltpu.SMEM),
                  pl.BlockSpec((None, 3, N_EXPERTS + 1), lambda i: (jnp.minimum(i + 1, t // tr - 1), 0, 0),
                               memory_space=pltpu.SMEM),
                  pl.BlockSpec((8, tr), lambda i: (0, i)),
                  pl.BlockSpec((8, tr), lambda i: (0, i)),
                  pl.BlockSpec((8, tr), lambda i: (0, jnp.minimum(i + 1, t // tr - 1))),
                  pl.BlockSpec((8, tr), lambda i: (0, jnp.minimum(i + 1, t // tr - 1))),
                  pl.BlockSpec((tr, D_MODEL), lambda i: (i, 0)),
                  pl.BlockSpec((1, D_MODEL), lambda i: (0, 0)),
                  pl.BlockSpec(memory_space=pl.ANY)],
        out_specs=pl.BlockSpec((tr, D_MODEL), lambda i: (i, 0)),
        out_shape=jax.ShapeDtypeStruct((t, D_MODEL), F32),
        scratch_shapes=[pltpu.VMEM((2, _local_rows(tr), D_MODEL // 2), jnp.uint32),
                        pltpu.VMEM((2, _local_rows(tr), tr), BF16), pltpu.SemaphoreType.DMA((2,))],
        compiler_params=pltpu.CompilerParams(dimension_semantics=("arbitrary",), vmem_limit_bytes=V7X_VMEM_LIMIT),
        name="combine",
    )(runs, runs, lpos, gates, lpos, gates, x1, nf_w, ys)


SEQ_TILE = 4 * CHUNK
TOKEN_TILE = 512
ROW_TILE = 512


def kernel(x, norm1_w, w_in, ssd_conv_w, ssd_conv_b, ssd_dt_bias, ssd_a_log, ssd_d, ssd_norm_w, ml_conv_w, ml_conv_b, ml_i_bias, ml_f_bias, ml_norm_w, w_out, norm2_w, router_g_w, router_g_b, router_e_w, router_e_b, exp_w_gate, exp_w_up, exp_w_down, norm_f_w):
    bsz, seq, d = x.shape
    t = bsz * seq
    tl, tr, tm = SEQ_TILE, TOKEN_TILE, ROW_TILE
    assert d == D_MODEL and norm1_w.shape[0] == 1 and seq % tl == 0 and t % (2 * tl) == 0 and t % (2 * tr) == 0

    consts = _mixer_consts(norm1_w[0], w_in[0], ssd_conv_w[0], ssd_conv_b[0], ssd_dt_bias[0], ssd_a_log[0],
                           ssd_d[0], ssd_norm_w[0], ml_conv_w[0], ml_conv_b[0], ml_i_bias[0], ml_f_bias[0],
                           ml_norm_w[0], w_out[0], norm2_w[0], router_g_w[0], router_e_w[0])
    x1, h2, lg_t = _mixer_call(x, consts, tl)
    x1 = x1.reshape(t, D_MODEL)
    h2 = h2.reshape(t, D_MODEL)

    bias_col = jnp.concatenate([router_e_b[0], router_g_b[0],
                                jnp.full((LOGIT_ROWS - N_EXPERTS - MOE_GROUPS,), STAB_INIT, F32)]).reshape(-1, 1)
    lpos, gates, cnt = _route_call(lg_t, bias_col.astype(F32), tr)

    seg = (cnt[:, :, 0].astype(jnp.int32) + (ROW_ALIGN - 1)) // ROW_ALIGN * ROW_ALIGN
    n_tok_tiles = t // tr
    n_rows = -(-(t * TOP_K + n_tok_tiles * N_EXPERTS * ROW_ALIGN + _local_rows(tr)) // tm) * tm
    item_tile, item_exp, starts, starts_ext = _expert_items(jnp.sum(seg, axis=0), n_rows, tm)
    run_dst = starts[None, :N_EXPERTS] + jnp.cumsum(seg, axis=0) - seg
    run_src = jnp.cumsum(seg, axis=1) - seg
    used = jnp.sum(seg, axis=1, keepdims=True)
    fill = jnp.concatenate([jnp.full_like(used, starts[N_EXPERTS]), used, _local_rows(tr) - used], axis=1)[:, :, None]
    runs = jnp.concatenate([jnp.stack([run_dst, run_src, seg], axis=1), fill], axis=2).astype(jnp.int32)

    xs = _dispatch_call(runs, lpos, h2, tr, n_rows)
    ys = _experts_call(item_tile, item_exp, starts_ext, xs, exp_w_gate[0], exp_w_up[0], exp_w_down[0], tm)
    out = _combine_call(runs, lpos, gates, x1, norm_f_w.reshape(1, -1).astype(F32), ys, tr)
    return out.reshape(bsz, seq, D_MODEL)
```

```python
import functools

import jax
import jax.numpy as jnp
import numpy as np
from jax import lax
from jax.experimental import pallas as pl
from jax.experimental.pallas import tpu as pltpu

D_MODEL = 1024
CHUNK = 64
SSD_WIDTH = 1024
SSD_HEAD_DIM = 64
SSD_HEADS = 16
SSD_GROUPS = 2
SSD_STATE = 128
SSD_CONV = 4
SSD_XBC = SSD_WIDTH + 2 * SSD_GROUPS * SSD_STATE
ML_WIDTH = 1024
ML_HEADS = 8
ML_HEAD_DIM = 128
ML_CONV = 4
D_MIX = SSD_WIDTH + ML_WIDTH
MOE_GROUPS = 4
EXPERTS_PER_GROUP = 8
N_EXPERTS = 32
TOP_K = 2
D_EXPERT = 512
EPS = 1e-6
STAB_INIT = -1e30

LANES = 128
GATE_COLS = 2 * LANES
ROW_ALIGN = 8
CONV_PAD = 8
LOGIT_ROWS = 64
V7X_VMEM_LIMIT = 60 * 1024 * 1024

F32 = jnp.float32
BF16 = jnp.bfloat16
NEG_INF = float("-inf")


def _dot(a, b):
    return jnp.dot(a, b, preferred_element_type=F32)


def _dot_nt(a, b):
    return lax.dot_general(a, b, (((1,), (1,)), ((), ())), preferred_element_type=F32)


def _dot_tn(a, b):
    return lax.dot_general(a, b, (((0,), (0,)), ((), ())), preferred_element_type=F32)


def _split3(x, axis):
    hi = x.astype(BF16)
    r1 = x - hi.astype(F32)
    mid = r1.astype(BF16)
    lo = (r1 - mid.astype(F32)).astype(BF16)
    return jnp.concatenate([hi, mid, lo], axis=axis)


def _pack_rows(x, rounded):
    if not rounded:
        x = x.astype(BF16).astype(F32)
    bits = lax.bitcast_convert_type(x, jnp.uint32)
    half = x.shape[1] // 2
    return bits[:, half:] | (bits[:, :half] >> 16)


def _unpack_rows(p):
    lo = lax.bitcast_convert_type(p << 16, F32)
    hi = lax.bitcast_convert_type(p & jnp.uint32(0xFFFF0000), F32)
    return jnp.concatenate([lo, hi], axis=1).astype(BF16)


def _silu(x):
    return x * jax.nn.sigmoid(x)


def _softplus(x):
    return jnp.maximum(x, 0.0) + jnp.log1p(jnp.exp(-jnp.abs(x)))


def _rms(x, w):
    return x * lax.rsqrt(jnp.mean(x * x, axis=-1, keepdims=True) + EPS) * w


def _mixer_kernel(x_ref, xc_ref, *refs, tl, nl):
    n_in = len(refs) - 3 - 15
    consts, (x1_ref, h2_ref, lg_ref), scratch = refs[:n_in], refs[n_in:n_in + 3], refs[n_in + 3:]
    for half in range(2):
        rows = pl.ds(half * tl, tl)
        _mixer_step(x_ref.at[rows], xc_ref.at[rows], *consts, x1_ref.at[rows], h2_ref.at[rows],
                    lg_ref.at[:, rows], *scratch, tl=tl, nl=nl, slot_a=half, step=2 * pl.program_id(0) + half)


def _mixer_step(x_ref, xc_ref, n1_ref, wz_ref, wxbc_ref, wqk_ref, wv_ref, wo_ref, wg_ref,
                  scw_ref, scb_ref, dtb_ref, alog_ref, dskip_ref, snw_ref,
                  mcw_ref, mcb_ref, gb_ref, mnw_ref, wout_ref, n2_ref, wr_ref,
                  e16_ref, e2_ref, tril_ref, seq_ref, sge_ref, meq_ref, mge_ref,
                  x1_ref, h2_ref, lg_ref,
                  xbc_buf, qk_buf, z_buf, v_buf, o_buf, g_buf, xbc_c, qk_c, mix_buf,
                  xbc_tail, qk_tail, sst, mst, mm, h_buf, *, tl, nl, slot_a, step):
    nchunk = tl // CHUNK
    slot_b = 1 - slot_a
    slot_c = slot_a

    @pl.when(step == 0)
    def _():
        xbc_buf[...] = jnp.zeros_like(xbc_buf)
        qk_buf[...] = jnp.zeros_like(qk_buf)
        z_buf[...] = jnp.zeros_like(z_buf)
        v_buf[...] = jnp.zeros_like(v_buf)
        o_buf[...] = jnp.zeros_like(o_buf)
        g_buf[...] = jnp.zeros_like(g_buf)
        mix_buf[...] = jnp.zeros_like(mix_buf)
        xbc_tail[...] = jnp.zeros_like(xbc_tail)
        qk_tail[...] = jnp.zeros_like(qk_tail)
        sst[...] = jnp.zeros_like(sst)
        mst[...] = jnp.zeros_like(mst)
        mm[...] = jnp.full(mm.shape, STAB_INIT, F32)

    h_buf[...] = _rms(x_ref[...], n1_ref[...]).astype(BF16)
    nblk = 2 * LANES
    tasks = []

    def proj_task(w_ref, store):
        for b0 in range(0, w_ref.shape[1], nblk):
            tasks.append(lambda b0=b0: store(slice(b0, b0 + nblk), _dot(h_buf[...], w_ref[:, b0:b0 + nblk])))

    def out_task(cols):
        x1_ref[:, cols] = xc_ref[:, cols] + _dot(mix_buf[slot_c], wout_ref[:, cols])

    def route_task():
        h2 = _rms(x1_ref[...], n2_ref[...])
        h2 = h2.astype(BF16)
        h2_ref[...] = h2
        lg_ref[...] = _dot_nt(wr_ref[...], h2)

    for b0 in range(0, D_MODEL, nblk):
        tasks.append(lambda b0=b0: out_task(slice(b0, b0 + nblk)))
    tasks.append(route_task)

    def put(buf):
        def store(cols, val):
            buf[slot_a, :, cols] = val.astype(buf.dtype)
        return store

    def put_tiles(buf):
        def store(cols, val):
            for k in range(nblk // LANES):
                buf[slot_a, cols.start // LANES + k, CONV_PAD:CONV_PAD + tl, :] = val[:, k * LANES:(k + 1) * LANES]
        return store

    proj_task(wz_ref, put(z_buf))
    proj_task(wg_ref, put(g_buf))
    proj_task(wxbc_ref, put_tiles(xbc_buf))
    proj_task(wqk_ref, put_tiles(qk_buf))
    proj_task(wv_ref, put(v_buf))
    proj_task(wo_ref, put(o_buf))
    per_chunk = -(-len(tasks) // nchunk)
    assert per_chunk <= 8

    def run_task(c, k):
        idx = c * per_chunk + k
        if idx < len(tasks):
            tasks[idx]()

    seq_start = (step + nl - 1) % nl == 0
    xbc_buf[slot_b, :, 0:CONV_PAD, :] = jnp.where(seq_start, 0.0, xbc_tail[...])
    qk_buf[slot_b, :, 0:CONV_PAD, :] = jnp.where(seq_start, 0.0, qk_tail[...])
    sst[...] = jnp.where(seq_start, 0.0, sst[...])
    mst[...] = jnp.where(seq_start, 0.0, mst[...])
    mm[...] = jnp.where(seq_start, STAB_INIT, mm[...])

    first = CONV_PAD - (SSD_CONV - 1)
    xbc_tail[...] = xbc_buf[slot_b, :, tl:tl + CONV_PAD, :]
    qk_tail[...] = qk_buf[slot_b, :, tl:tl + CONV_PAD, :]

    def conv_silu(buf, w_ref, b_ref, out, r0):
        for t in range(buf.shape[1]):
            cols = slice(t * LANES, (t + 1) * LANES)
            acc = b_ref[:, cols] + w_ref[0:1, cols] * buf[slot_b, t, r0 + first:r0 + first + CHUNK, :]
            for j in range(1, w_ref.shape[0]):
                acc = acc + w_ref[j:j + 1, cols] * buf[slot_b, t, r0 + first + j:r0 + first + j + CHUNK, :]
            out[r0:r0 + CHUNK, cols] = _silu(acc)

    def convs(c):
        run_task(c, 0)
        conv_silu(xbc_buf, scw_ref, scb_ref, xbc_c, c * CHUNK)
        run_task(c, 1)
        conv_silu(qk_buf, mcw_ref, mcb_ref, qk_c, c * CHUNK)
        run_task(c, 2)

    for c in range(nchunk):
        convs(c)
        rows = slice(c * CHUNK, (c + 1) * CHUNK)
        tril3 = tril_ref[...]

        xs = xbc_c[rows, 0:SSD_WIDTH]
        bm = xbc_c[rows, SSD_WIDTH:SSD_WIDTH + 2 * SSD_STATE].astype(BF16)
        cm = xbc_c[rows, SSD_WIDTH + 2 * SSD_STATE:SSD_XBC].astype(BF16)
        dt = _softplus(g_buf[slot_b, rows, 0:SSD_HEADS] + dtb_ref[...])
        a_cs = _dot(tril3, _split3(dt * (-jnp.exp(alog_ref[...])), 0))
        dt_e = _dot(_split3(dt, 1), e16_ref[...])
        a_col = _dot(_split3(a_cs, 1), e16_ref[...])
        a_row = jnp.sum(a_col * seq_ref[...], axis=0, keepdims=True)
        lmat = jnp.exp(a_col - a_row + sge_ref[...])
        cb = jnp.concatenate(
            [_dot_nt(cm[:, g * SSD_STATE:(g + 1) * SSD_STATE],
                     jnp.concatenate([bm[:, g * SSD_STATE:(g + 1) * SSD_STATE]] * (SSD_HEADS // SSD_GROUPS), axis=0))
             for g in range(SSD_GROUPS)], axis=1)
        m_all = (cb * lmat).astype(BF16)
        run_task(c, 3)
        xdt = xs * dt_e
        lane = lax.broadcasted_iota(jnp.int32, (CHUNK, LANES), 1)
        y_parts = []
        for j in range(SSD_HEADS // 2):
            xp = xdt[:, j * LANES:(j + 1) * LANES]
            xbd = jnp.concatenate([jnp.where(lane < SSD_HEAD_DIM, xp, 0.0),
                                   jnp.where(lane >= SSD_HEAD_DIM, xp, 0.0)], axis=0).astype(BF16)
            y_parts.append(_dot(m_all[:, j * LANES:(j + 1) * LANES], xbd))
        y = jnp.concatenate(y_parts, axis=1)
        half = SSD_WIDTH // SSD_GROUPS
        y_int = jnp.concatenate(
            [_dot(cm[:, g * SSD_STATE:(g + 1) * SSD_STATE], sst[g].astype(BF16)) for g in range(SSD_GROUPS)], axis=1)
        y = y + y_int * jnp.exp(a_col)
        a_last = a_col[CHUNK - 1:CHUNK, :]
        xd_b = (xdt * jnp.exp(a_last - a_col)).astype(BF16)
        st_scale = jnp.exp(a_last)
        for g in range(SSD_GROUPS):
            sst[g] = sst[g] * st_scale[:, g * half:(g + 1) * half] + _dot_tn(
                bm[:, g * SSD_STATE:(g + 1) * SSD_STATE], xd_b[:, g * half:(g + 1) * half])
        y = (y + dskip_ref[...] * xs) * _silu(z_buf[slot_b, rows, :])
        for g in range(SSD_GROUPS):
            yg = y[:, g * half:(g + 1) * half]
            yg = yg * lax.rsqrt(jnp.mean(yg * yg, axis=-1, keepdims=True) + EPS) * snw_ref[:, g * half:(g + 1) * half]
            mix_buf[slot_b, rows, g * half:(g + 1) * half] = yg.astype(BF16)

        run_task(c, 4)

        gi = g_buf[slot_b, rows, LANES:LANES + 2 * ML_HEADS] + gb_ref[...]
        col16 = lax.broadcasted_iota(jnp.int32, (CHUNK, 2 * ML_HEADS), 1)
        gi = jnp.where(col16 < ML_HEADS, gi, jnp.minimum(gi, 0.0) - jnp.log1p(jnp.exp(-jnp.abs(gi))))
        gi = jnp.where(col16 < ML_HEADS, gi, _dot(tril3, _split3(gi, 0)))
        gi_e = _dot(_split3(gi, 1), e2_ref[...])
        ig_e = gi_e[:, 0:ML_WIDTH]
        b_col = gi_e[:, ML_WIDTH:2 * ML_WIDTH]
        c_row = jnp.sum((ig_e - b_col) * meq_ref[...], axis=0, keepdims=True)
        dmat = b_col + c_row + mge_ref[...]
        heads = range(ML_HEADS)
        hs = lambda arr, hd: arr[:, hd * LANES:(hd + 1) * LANES]
        ones_blk = jnp.ones((CHUNK, LANES), BF16)
        m_prev = mm[...]
        m_inter = b_col + m_prev
        q = [qk_c[rows, hd * LANES:(hd + 1) * LANES].astype(BF16) for hd in heads]
        kf = [qk_c[rows, ML_WIDTH + hd * LANES:ML_WIDTH + (hd + 1) * LANES] * (ML_HEAD_DIM ** -0.5) for hd in heads]
        v_aug = [jnp.concatenate([v_buf[slot_b, rows, hd * LANES:(hd + 1) * LANES], ones_blk], axis=1)
                 for hd in heads]
        s = [_dot_nt(q[hd], kf[hd].astype(BF16)) for hd in heads]
        qc = [_dot(q[hd], mst[hd].astype(BF16)) for hd in heads]
        run_task(c, 5)
        m_t = jnp.maximum(m_inter, jnp.concatenate(
            [jnp.broadcast_to(jnp.max(hs(dmat, hd), axis=-1, keepdims=True), (CHUNK, LANES)) for hd in heads], axis=1))
        w = jnp.exp(dmat - m_t)
        inter = jnp.exp(m_inter - m_t)
        den_floor = jnp.exp(-m_t)
        nd = [_dot((s[hd] * hs(w, hd)[:, 0:CHUNK]).astype(BF16), v_aug[hd])
              + qc[hd] * jnp.concatenate([hs(inter, hd)] * 2, axis=1) for hd in heads]
        run_task(c, 6)
        hh = [nd[hd][:, 0:LANES] / jnp.maximum(jnp.abs(nd[hd][:, LANES:2 * LANES]), hs(den_floor, hd)) for hd in heads]
        hc = [hh[hd] - jnp.mean(hh[hd], axis=-1, keepdims=True) for hd in heads]
        var = [jnp.mean(hc[hd] * hc[hd], axis=-1, keepdims=True) for hd in heads]
        for hd in heads:
            hn = hc[hd] * lax.rsqrt(var[hd] + EPS) * mnw_ref[:, hd * LANES:(hd + 1) * LANES]
            mix_buf[slot_b, rows, SSD_WIDTH + hd * LANES:SSD_WIDTH + (hd + 1) * LANES] = (
                jax.nn.sigmoid(o_buf[slot_b, rows, hd * LANES:(hd + 1) * LANES]) * hn).astype(BF16)
        run_task(c, 7)
        g_row = b_col[CHUNK - 1:CHUNK, :]
        a = g_row - b_col + ig_e
        m_new = jnp.maximum(g_row + m_prev, jnp.max(a, axis=0, keepdims=True))
        wk = jnp.exp(a - m_new)
        cs = jnp.exp(g_row + m_prev - m_new)
        for hd in heads:
            mst[hd] = mst[hd] * jnp.concatenate([hs(cs, hd)] * 2, axis=1) + _dot_tn(
                (kf[hd] * hs(wk, hd)).astype(BF16), v_aug[hd])
        mm[...] = m_new


def _const_spec(shape):
    nd = len(shape)
    return pl.BlockSpec(shape, lambda s: (0,) * nd, pipeline_mode=pl.Buffered(1))


def _mixer_call(x, consts, tl):
    bsz, seq, _ = x.shape
    nl = seq // tl
    n_pairs = bsz * nl // 2
    x_pairs = x.reshape(n_pairs, 2 * tl, D_MODEL)
    pair_in = lambda i: (jnp.minimum(i, n_pairs - 1), 0, 0)
    pair_out = lambda i: (jnp.maximum(i - 1, 0), 0, 0)
    in_specs = ([pl.BlockSpec((None, 2 * tl, D_MODEL), pair_in), pl.BlockSpec((None, 2 * tl, D_MODEL), pair_out)]
                + [_const_spec(c.shape) for c in consts])
    out_specs = [pl.BlockSpec((None, 2 * tl, D_MODEL), pair_out), pl.BlockSpec((None, 2 * tl, D_MODEL), pair_out),
                 pl.BlockSpec((LOGIT_ROWS, 2 * tl), lambda i: (0, jnp.maximum(i - 1, 0)))]
    out_shape = [jax.ShapeDtypeStruct((n_pairs, 2 * tl, D_MODEL), F32),
                 jax.ShapeDtypeStruct((n_pairs, 2 * tl, D_MODEL), BF16),
                 jax.ShapeDtypeStruct((LOGIT_ROWS, bsz * seq), F32)]
    scratch = [
        pltpu.VMEM((2, SSD_XBC // LANES, tl + CONV_PAD, LANES), F32),
        pltpu.VMEM((2, 2 * ML_WIDTH // LANES, tl + CONV_PAD, LANES), F32),
        pltpu.VMEM((2, tl, SSD_WIDTH), F32),
        pltpu.VMEM((2, tl, ML_WIDTH), BF16),
        pltpu.VMEM((2, tl, ML_WIDTH), F32),
        pltpu.VMEM((2, tl, GATE_COLS), F32),
        pltpu.VMEM((tl, SSD_XBC), F32),
        pltpu.VMEM((tl, 2 * ML_WIDTH), F32),
        pltpu.VMEM((2, tl, D_MIX), BF16),
        pltpu.VMEM((SSD_XBC // LANES, CONV_PAD, LANES), F32),
        pltpu.VMEM((2 * ML_WIDTH // LANES, CONV_PAD, LANES), F32),
        pltpu.VMEM((SSD_GROUPS, SSD_STATE, SSD_WIDTH // SSD_GROUPS), F32),
        pltpu.VMEM((ML_HEADS, ML_HEAD_DIM, 2 * LANES), F32),
        pltpu.VMEM((1, ML_WIDTH), F32),
        pltpu.VMEM((tl, D_MODEL), BF16),
    ]
    return pl.pallas_call(
        functools.partial(_mixer_kernel, tl=tl, nl=nl),
        grid=(n_pairs + 1,),
        in_specs=in_specs,
        out_specs=out_specs,
        out_shape=out_shape,
        scratch_shapes=scratch,
        compiler_params=pltpu.CompilerParams(
            dimension_semantics=("arbitrary",), vmem_limit_bytes=V7X_VMEM_LIMIT),
        name="mixer",
    )(x_pairs, x_pairs, *consts)


def _mixer_consts(norm1_w, w_in, ssd_conv_w, ssd_conv_b, ssd_dt_bias, ssd_a_log, ssd_d, ssd_norm_w,
                  ml_conv_w, ml_conv_b, ml_i_bias, ml_f_bias, ml_norm_w, w_out, norm2_w,
                  router_g_w, router_e_w):
    o1 = SSD_WIDTH
    o2 = o1 + SSD_XBC
    o3 = o2 + SSD_HEADS
    o4 = o3 + 2 * ML_WIDTH
    o5 = o4 + ML_WIDTH
    o6 = o5 + ML_WIDTH
    o7 = o6 + ML_HEADS
    zpad = lambda n: jnp.zeros((D_MODEL, n), F32)
    w_gate = jnp.concatenate([w_in[:, o2:o3], zpad(LANES - SSD_HEADS),
                              w_in[:, o6:o7], w_in[:, o7:], zpad(LANES - 2 * ML_HEADS)], axis=1)
    row = lambda v: v.reshape(1, -1).astype(F32)
    w_route = jnp.concatenate([router_e_w, router_g_w,
                               jnp.zeros((D_MODEL, LOGIT_ROWS - N_EXPERTS - MOE_GROUPS), F32)], axis=1).T

    c1024 = np.arange(SSD_WIDTH)
    r64 = np.arange(CHUNK)[:, None]
    e16 = np.tile(c1024[None, :] // SSD_HEAD_DIM == np.arange(SSD_HEADS)[:, None], (3, 1))
    e2 = np.tile(np.arange(2 * ML_WIDTH)[None, :] // LANES == np.arange(2 * ML_HEADS)[:, None], (3, 1))
    tril = np.tile(np.arange(CHUNK)[None, :] <= r64, (1, 3))
    s_pos = (c1024 % SSD_HEAD_DIM)[None, :]
    m_pos = (c1024 % LANES)[None, :]
    as_bf16 = lambda m: jnp.asarray(m.astype(np.float32), BF16)
    as_f32 = lambda m: jnp.asarray(m.astype(np.float32))
    e16, e2, tril = as_bf16(e16), as_bf16(e2), as_bf16(tril)
    causal = lambda keep: jnp.asarray(np.where(keep, 0.0, -np.inf).astype(np.float32))
    seq, sge, meq, mge = as_f32(r64 == s_pos), causal(r64 >= s_pos), as_f32(r64 == m_pos), causal(r64 >= m_pos)
    return [
        row(norm1_w),
        w_in[:, :o1].astype(BF16), w_in[:, o1:o2].astype(BF16), w_in[:, o3:o4].astype(BF16),
        w_in[:, o4:o5].astype(BF16), w_in[:, o5:o6].astype(BF16), w_gate.astype(BF16),
        ssd_conv_w.astype(F32), row(ssd_conv_b), row(ssd_dt_bias),
        row(ssd_a_log), row(jnp.repeat(ssd_d, SSD_HEAD_DIM)), row(ssd_norm_w),
        ml_conv_w.astype(F32), row(ml_conv_b), row(jnp.concatenate([ml_i_bias, ml_f_bias])), row(ml_norm_w),
        w_out.astype(BF16), row(norm2_w), w_route.astype(BF16),
        e16, e2, tril, seq, sge, meq, mge,
    ]


def _route_kernel(lg_ref, bias_ref, striu_ref, lpos_ref, gate_ref, cnt_ref):
    tr = striu_ref.shape[0]
    for half in range(lg_ref.shape[1] // tr):
        _route_tile(lg_ref[:, half * tr:(half + 1) * tr] + bias_ref[...], striu_ref,
                    lpos_ref.at[:, pl.ds(half * tr, tr)], gate_ref.at[:, pl.ds(half * tr, tr)], cnt_ref.at[half])


def _route_tile(lg, striu_ref, lpos_ref, gate_ref, cnt_ref):
    tr = lg.shape[1]
    el = lg[0:N_EXPERTS]
    gl = lg[N_EXPERTS:N_EXPERTS + 8]
    gmax = jnp.max(gl, axis=0, keepdims=True)
    pg = 1.0 / jnp.sum(jnp.exp(gl - gmax), axis=0, keepdims=True)
    grow = lax.broadcasted_iota(jnp.int32, gl.shape, 0)
    gsel = jnp.min(jnp.where(gl == gmax, grow, 8), axis=0, keepdims=True)
    erow = lax.broadcasted_iota(jnp.int32, el.shape, 0)
    m1 = jnp.where((erow >> 3) == gsel, el, NEG_INF)
    v1 = jnp.max(m1, axis=0, keepdims=True)
    i1 = jnp.min(jnp.where(m1 == v1, erow, N_EXPERTS), axis=0, keepdims=True)
    m2 = jnp.where(erow == i1, NEG_INF, m1)
    v2 = jnp.max(m2, axis=0, keepdims=True)
    i2 = jnp.min(jnp.where(m2 == v2, erow, N_EXPERTS), axis=0, keepdims=True)
    e2 = jnp.exp(v2 - v1)
    g1 = pg / (1.0 + e2)
    g2 = g1 * e2
    hit1 = erow == i1
    hit2 = erow == i2
    oh = jnp.where(hit1, 1.0, jnp.where(hit2, 1.0, 0.0))
    earlier = _dot(oh.astype(BF16), striu_ref[...])
    cnt = jnp.sum(oh, axis=1, keepdims=True)
    seg = jnp.floor((cnt + (ROW_ALIGN - 1)) * (1.0 / ROW_ALIGN)) * ROW_ALIGN
    l1 = jnp.sum(jnp.where(hit1, earlier, jnp.where(erow < i1, seg, 0.0)), axis=0, keepdims=True)
    l2 = jnp.sum(jnp.where(hit2, earlier, jnp.where(erow < i2, seg, 0.0)), axis=0, keepdims=True)
    lpos_ref[...] = jnp.concatenate([l1.astype(jnp.int32), l2.astype(jnp.int32), jnp.zeros((6, tr), jnp.int32)], axis=0)
    gate_ref[...] = jnp.concatenate([g1, g2, jnp.zeros((6, tr), F32)], axis=0)
    cnt_ref[...] = jnp.broadcast_to(cnt, cnt_ref.shape)


def _route_call(lg_t, bias_col, tr):
    t = lg_t.shape[1]
    striu = jnp.asarray(np.triu(np.ones((tr, tr), np.float32), 1), BF16)
    return pl.pallas_call(
        _route_kernel,
        grid=(t // (2 * tr),),
        in_specs=[pl.BlockSpec((LOGIT_ROWS, 2 * tr), lambda i: (0, i)),
                  pl.BlockSpec((LOGIT_ROWS, 1), lambda i: (0, 0)),
                  pl.BlockSpec((tr, tr), lambda i: (0, 0))],
        out_specs=[pl.BlockSpec((8, 2 * tr), lambda i: (0, i)),
                   pl.BlockSpec((8, 2 * tr), lambda i: (0, i)),
                   pl.BlockSpec((2, N_EXPERTS, LANES), lambda i: (i, 0, 0))],
        out_shape=[jax.ShapeDtypeStruct((8, t), jnp.int32),
                   jax.ShapeDtypeStruct((8, t), F32),
                   jax.ShapeDtypeStruct((t // tr, N_EXPERTS, LANES), F32)],
        compiler_params=pltpu.CompilerParams(dimension_semantics=("arbitrary",)),
        name="route",
    )(lg_t, bias_col, striu)


def _run_copies(runs_ref, copy_run, max_piece, filler_far, enable=None):
    for e in range(N_EXPERTS + 1):
        dst = runs_ref[0, e] if e < N_EXPERTS else filler_far
        src = runs_ref[1, e]
        n = runs_ref[2, e] if enable is None else jnp.where(enable, runs_ref[2, e], 0)
        p = max_piece
        while p >= ROW_ALIGN:
            done = n & ~(2 * p - 1)

            @pl.when((n & p) != 0)
            def _(p=p, done=done, dst=dst, src=src):
                copy_run(pl.multiple_of(dst + done, ROW_ALIGN), pl.multiple_of(src + done, ROW_ALIGN), p)
            p //= 2


def _dispatch_kernel(runs_ref, prev_runs_ref, lpos_ref, h2_ref, xs_ref, loc, sem, *, max_spare):
    tr = h2_ref.shape[0] // 2
    nloc = loc.shape[1]
    step = pl.program_id(0)
    last = pl.num_programs(0) - 1
    used = runs_ref[0, 0, N_EXPERTS]

    def wait_block(s):
        pltpu.make_async_copy(loc.at[s], xs_ref.at[pl.ds(0, nloc)], sem.at[s]).wait()

    def send_block(table, s, enable=None):
        def copy_run(dst, src, n):
            pltpu.make_async_copy(loc.at[s, pl.ds(src, n)], xs_ref.at[pl.ds(dst, n)], sem.at[s]).start()
        _run_copies(table, copy_run, tr, used + s * (nloc - TOP_K * tr), enable)

    def sort_tile(s):
        cols = slice(s * tr, (s + 1) * tr)
        jrow = lax.broadcasted_iota(jnp.int32, (nloc, tr), 0)
        perm = jnp.where(jrow == lpos_ref[0:1, cols], 1.0, jnp.where(jrow == lpos_ref[1:2, cols], 1.0, 0.0)).astype(BF16)
        loc[s] = _pack_rows(_dot(perm, h2_ref[cols, :]), rounded=True)

    @pl.when(step > 0)
    def _():
        wait_block(0)
    send_block(prev_runs_ref.at[1], 1, enable=step > 0)
    sort_tile(0)

    @pl.when(step > 0)
    def _():
        wait_block(1)
    send_block(runs_ref.at[0], 0)
    sort_tile(1)

    @pl.when(step == last)
    def _():
        send_block(runs_ref.at[1], 1)
        wait_block(0)
        wait_block(1)
        loc[0, 0:tr, :] = jnp.zeros((tr, loc.shape[2]), loc.dtype)
        spare = xs_ref.shape[0] - used

        def zero_copy(off, n):
            return pltpu.make_async_copy(loc.at[0, pl.ds(0, n)],
                                         xs_ref.at[pl.ds(pl.multiple_of(used + off, ROW_ALIGN), n)], sem.at[0])

        def pieces(do):
            for i in range(max_spare // tr):
                @pl.when((i + 1) * tr <= spare)
                def _(i=i):
                    do(zero_copy(i * tr, tr))
            p = tr // 2
            while p >= ROW_ALIGN:
                @pl.when((spare & p) != 0)
                def _(p=p):
                    do(zero_copy(spare & ~(2 * p - 1), p))
                p //= 2

        pieces(lambda c: c.start())
        pieces(lambda c: c.wait())


def _dispatch_call(runs, lpos, h2, tr, n_rows):
    t = h2.shape[0]
    pairs = runs.reshape(t // (2 * tr), 2, 3, N_EXPERTS + 1)
    return pl.pallas_call(
        functools.partial(_dispatch_kernel, max_spare=n_rows - t * TOP_K),
        grid=(t // (2 * tr),),
        in_specs=[pl.BlockSpec((None, 2, 3, N_EXPERTS + 1), lambda i: (i, 0, 0, 0), memory_space=pltpu.SMEM),
                  pl.BlockSpec((None, 2, 3, N_EXPERTS + 1), lambda i: (jnp.maximum(i - 1, 0), 0, 0, 0),
                               memory_space=pltpu.SMEM),
                  pl.BlockSpec((8, 2 * tr), lambda i: (0, i)),
                  pl.BlockSpec((2 * tr, D_MODEL), lambda i: (i, 0))],
        out_specs=pl.BlockSpec(memory_space=pl.ANY),
        out_shape=jax.ShapeDtypeStruct((n_rows, D_MODEL // 2), jnp.uint32),
        scratch_shapes=[pltpu.VMEM((2, _local_rows(tr), D_MODEL // 2), jnp.uint32), pltpu.SemaphoreType.DMA((2,))],
        compiler_params=pltpu.CompilerParams(dimension_semantics=("arbitrary",), vmem_limit_bytes=V7X_VMEM_LIMIT),
        name="dispatch",
    )(pairs, pairs, lpos, h2)


def _experts_kernel(tile_ref, exp_ref, starts_ref, xs_ref, wg_ref, wu_ref, wd_ref, ys_ref, wg_b, wu_b, wd_b):
    i = pl.program_id(0)
    tm = xs_ref.shape[0]
    e = exp_ref[i]
    tile = tile_ref[i]
    is_first = jnp.logical_or(i == 0, tile_ref[jnp.maximum(i - 1, 0)] != tile)

    @pl.when(jnp.logical_and(e < N_EXPERTS, jnp.logical_or(i == 0, exp_ref[jnp.maximum(i - 1, 0)] != e)))
    def _():
        wg_b[...] = wg_ref[...].astype(BF16)
        wu_b[...] = wu_ref[...].astype(BF16)
        wd_b[...] = wd_ref[...].astype(BF16)

    def expert_rows(other):
        x = _unpack_rows(xs_ref[...])
        act = (_silu(_dot(x, wg_b[...])) * _dot(x, wu_b[...])).astype(BF16)
        y = _pack_rows(_dot(act, wd_b[...]), rounded=False)
        rows = tile * tm + lax.broadcasted_iota(jnp.int32, (tm, 1), 0)
        return jnp.where((rows >= starts_ref[e]) & (rows < starts_ref[e + 1]), y, other)

    @pl.when(jnp.logical_and(e < N_EXPERTS, is_first))
    def _():
        ys_ref[...] = expert_rows(jnp.zeros_like(ys_ref))

    @pl.when(jnp.logical_and(e < N_EXPERTS, jnp.logical_not(is_first)))
    def _():
        ys_ref[...] = expert_rows(ys_ref[...])

    @pl.when(e == N_EXPERTS)
    def _():
        ys_ref[...] = jnp.zeros_like(ys_ref)


def _experts_call(item_tile, item_exp, starts_ext, xs, wg, wu, wd, tm):
    n = xs.shape[0]
    n_items = item_tile.shape[0]
    w_idx = lambda i, tile, ex, st: (jnp.minimum(ex[i], N_EXPERTS - 1), 0, 0)
    grid_spec = pltpu.PrefetchScalarGridSpec(
        num_scalar_prefetch=3,
        grid=(n_items,),
        in_specs=[pl.BlockSpec((tm, D_MODEL // 2), lambda i, tile, ex, st: (tile[i], 0)),
                  pl.BlockSpec((None, D_MODEL, D_EXPERT), w_idx),
                  pl.BlockSpec((None, D_MODEL, D_EXPERT), w_idx),
                  pl.BlockSpec((None, D_EXPERT, D_MODEL), w_idx)],
        out_specs=pl.BlockSpec((tm, D_MODEL // 2), lambda i, tile, ex, st: (tile[i], 0)),
        scratch_shapes=[pltpu.VMEM((D_MODEL, D_EXPERT), BF16), pltpu.VMEM((D_MODEL, D_EXPERT), BF16),
                        pltpu.VMEM((D_EXPERT, D_MODEL), BF16)],
    )
    return pl.pallas_call(
        _experts_kernel,
        grid_spec=grid_spec,
        out_shape=jax.ShapeDtypeStruct((n, D_MODEL // 2), jnp.uint32),
        compiler_params=pltpu.CompilerParams(
            dimension_semantics=("arbitrary",), vmem_limit_bytes=V7X_VMEM_LIMIT),
        name="experts",
    )(item_tile, item_exp, starts_ext, xs, wg, wu, wd)


def _local_rows(tr):
    return TOP_K * tr + N_EXPERTS * ROW_ALIGN


def _expert_items(counts, n_rows, tm):
    n_tiles = n_rows // tm
    n_items = n_tiles + N_EXPERTS - 1
    starts = jnp.concatenate([jnp.zeros((1,), jnp.int32), jnp.cumsum(counts).astype(jnp.int32)])
    lo = starts[:-1] // tm
    hi = (starts[1:] - 1) // tm
    per = jnp.where(counts > 0, hi - lo + 1, 0)
    cum = jnp.cumsum(per)
    idx = jnp.arange(n_items, dtype=jnp.int32)
    e = jnp.sum(idx[:, None] >= cum[None, :], axis=1).astype(jnp.int32)
    valid = idx < cum[-1]
    e_c = jnp.minimum(e, N_EXPERTS - 1)
    first_item = jnp.sum(jnp.where(e_c[:, None] == jnp.arange(N_EXPERTS)[None, :], (lo - (cum - per))[None, :], 0), axis=1)
    tile = first_item + idx
    spare_tile = -(-starts[-1] // tm) + idx - cum[-1]
    item_tile = jnp.where(valid, tile, jnp.minimum(spare_tile, n_tiles - 1)).astype(jnp.int32)
    item_exp = jnp.where(valid, e_c, jnp.where(spare_tile < n_tiles, N_EXPERTS, N_EXPERTS + 1)).astype(jnp.int32)
    starts_ext = jnp.concatenate([starts, starts[-1:], starts[-1:]])
    return item_tile, item_exp, starts, starts_ext


def _by_parity(body):
    def kernel(*refs, **kw):
        for parity in range(2):
            @pl.when(pl.program_id(0) % 2 == parity)
            def _(parity=parity):
                body(*refs, slot=parity, **kw)
    return kernel


def _combine_step(runs_ref, next_runs_ref, lpos_ref, gate_ref, next_lpos_ref, next_gate_ref, x1_ref, nf_ref, ys_ref,
                  out_ref, loc, sel_buf, sem, *, slot):
    tr = x1_ref.shape[0]
    nloc = loc.shape[1]
    step = pl.program_id(0)

    def gather(table, s, enable=None):
        def copy_run(dst, src, n):
            pltpu.make_async_copy(ys_ref.at[pl.ds(dst, n)], loc.at[s, pl.ds(src, n)], sem.at[s]).start()
        _run_copies(table, copy_run, tr, 0, enable)

    def build_sel(pos_ref, g_ref, s):
        jrow = lax.broadcasted_iota(jnp.int32, (nloc, tr), 0)
        sel_buf[s] = jnp.where(jrow == pos_ref[0:1, :], g_ref[0:1, :],
                               jnp.where(jrow == pos_ref[1:2, :], g_ref[1:2, :], 0.0)).astype(BF16)

    @pl.when(step == 0)
    def _():
        gather(runs_ref, slot)
        build_sel(lpos_ref, gate_ref, slot)

    pltpu.make_async_copy(ys_ref.at[pl.ds(0, nloc)], loc.at[slot], sem.at[slot]).wait()

    gather(next_runs_ref, 1 - slot, enable=step + 1 < pl.num_programs(0))
    y = _dot_tn(sel_buf[slot], _unpack_rows(loc[slot]))
    build_sel(next_lpos_ref, next_gate_ref, 1 - slot)
    out_ref[...] = _rms(x1_ref[...] + y, nf_ref[...])


def _combine_call(runs, lpos, gates, x1, nf_w, ys, tr):
    t = x1.shape[0]
    return pl.pallas_call(
        _by_parity(_combine_step),
        grid=(t // tr,),
        in_specs=[pl.BlockSpec((None, 3, N_EXPERTS + 1), lambda i: (i, 0, 0), memory_space=pltpu.SMEM),
                  pl.BlockSpec((None, 3, N_EXPERTS + 1), lambda i: (jnp.minimum(i + 1, t // tr - 1), 0, 0),
                               memory_space=pltpu.SMEM),
                  pl.BlockSpec((8, tr), lambda i: (0, i)),
                  pl.BlockSpec((8, tr), lambda i: (0, i)),
                  pl.BlockSpec((8, tr), lambda i: (0, jnp.minimum(i + 1, t // tr - 1))),
                  pl.BlockSpec((8, tr), lambda i: (0, jnp.minimum(i + 1, t // tr - 1))),
                  pl.BlockSpec((tr, D_MODEL), lambda i: (i, 0)),
                  pl.BlockSpec((1, D_MODEL), lambda i: (0, 0)),
                  pl.BlockSpec(memory_space=pl.ANY)],
        out_specs=pl.BlockSpec((tr, D_MODEL), lambda i: (i, 0)),
        out_shape=jax.ShapeDtypeStruct((t, D_MODEL), F32),
        scratch_shapes=[pltpu.VMEM((2, _local_rows(tr), D_MODEL // 2), jnp.uint32),
                        pltpu.VMEM((2, _local_rows(tr), tr), BF16), pltpu.SemaphoreType.DMA((2,))],
        compiler_params=pltpu.CompilerParams(dimension_semantics=("arbitrary",), vmem_limit_bytes=V7X_VMEM_LIMIT),
        name="combine",
    )(runs, runs, lpos, gates, lpos, gates, x1, nf_w, ys)


SEQ_TILE = 4 * CHUNK
TOKEN_TILE = 512
ROW_TILE = 512


def kernel(x, norm1_w, w_in, ssd_conv_w, ssd_conv_b, ssd_dt_bias, ssd_a_log, ssd_d, ssd_norm_w, ml_conv_w, ml_conv_b, ml_i_bias, ml_f_bias, ml_norm_w, w_out, norm2_w, router_g_w, router_g_b, router_e_w, router_e_b, exp_w_gate, exp_w_up, exp_w_down, norm_f_w):
    bsz, seq, d = x.shape
    t = bsz * seq
    tl, tr, tm = SEQ_TILE, TOKEN_TILE, ROW_TILE
    assert d == D_MODEL and norm1_w.shape[0] == 1 and seq % tl == 0 and t % (2 * tl) == 0 and t % (2 * tr) == 0

    consts = _mixer_consts(norm1_w[0], w_in[0], ssd_conv_w[0], ssd_conv_b[0], ssd_dt_bias[0], ssd_a_log[0],
                           ssd_d[0], ssd_norm_w[0], ml_conv_w[0], ml_conv_b[0], ml_i_bias[0], ml_f_bias[0],
                           ml_norm_w[0], w_out[0], norm2_w[0], router_g_w[0], router_e_w[0])
    x1, h2, lg_t = _mixer_call(x, consts, tl)
    x1 = x1.reshape(t, D_MODEL)
    h2 = h2.reshape(t, D_MODEL)

    bias_col = jnp.concatenate([router_e_b[0], router_g_b[0],
                                jnp.full((LOGIT_ROWS - N_EXPERTS - MOE_GROUPS,), STAB_INIT, F32)]).reshape(-1, 1)
    lpos, gates, cnt = _route_call(lg_t, bias_col.astype(F32), tr)

    seg = (cnt[:, :, 0].astype(jnp.int32) + (ROW_ALIGN - 1)) // ROW_ALIGN * ROW_ALIGN
    n_tok_tiles = t // tr
    n_rows = -(-(t * TOP_K + n_tok_tiles * N_EXPERTS * ROW_ALIGN + _local_rows(tr)) // tm) * tm
    item_tile, item_exp, starts, starts_ext = _expert_items(jnp.sum(seg, axis=0), n_rows, tm)
    run_dst = starts[None, :N_EXPERTS] + jnp.cumsum(seg, axis=0) - seg
    run_src = jnp.cumsum(seg, axis=1) - seg
    used = jnp.sum(seg, axis=1, keepdims=True)
    fill = jnp.concatenate([jnp.full_like(used, starts[N_EXPERTS]), used, _local_rows(tr) - used], axis=1)[:, :, None]
    runs = jnp.concatenate([jnp.stack([run_dst, run_src, seg], axis=1), fill], axis=2).astype(jnp.int32)

    xs = _dispatch_call(runs, lpos, h2, tr, n_rows)
    ys = _experts_call(item_tile, item_exp, starts_ext, xs, exp_w_gate[0], exp_w_up[0], exp_w_down[0], tm)
    out = _combine_call(runs, lpos, gates, x1, norm_f_w.reshape(1, -1).astype(F32), ys, tr)
    return out.reshape(bsz, seq, D_MODEL)
```

```python
import functools

import jax
import jax.numpy as jnp
import numpy as np
from jax import lax
from jax.experimental import pallas as pl
from jax.experimental.pallas import tpu as pltpu

D_MODEL = 1024
CHUNK = 64
SSD_WIDTH = 1024
SSD_HEAD_DIM = 64
SSD_HEADS = 16
SSD_GROUPS = 2
SSD_STATE = 128
SSD_CONV = 4
SSD_XBC = SSD_WIDTH + 2 * SSD_GROUPS * SSD_STATE
ML_WIDTH = 1024
ML_HEADS = 8
ML_HEAD_DIM = 128
ML_CONV = 4
D_MIX = SSD_WIDTH + ML_WIDTH
MOE_GROUPS = 4
EXPERTS_PER_GROUP = 8
N_EXPERTS = 32
TOP_K = 2
D_EXPERT = 512
EPS = 1e-6
STAB_INIT = -1e30

LANES = 128
GATE_COLS = 2 * LANES
ROW_ALIGN = 8
CONV_PAD = 8
LOGIT_ROWS = 64
V7X_VMEM_LIMIT = 60 * 1024 * 1024

F32 = jnp.float32
BF16 = jnp.bfloat16
NEG_INF = float("-inf")


def _dot(a, b):
    return jnp.dot(a, b, preferred_element_type=F32)


def _dot_nt(a, b):
    return lax.dot_general(a, b, (((1,), (1,)), ((), ())), preferred_element_type=F32)


def _dot_tn(a, b):
    return lax.dot_general(a, b, (((0,), (0,)), ((), ())), preferred_element_type=F32)


def _split3(x, axis):
    hi = x.astype(BF16)
    r1 = x - hi.astype(F32)
    mid = r1.astype(BF16)
    lo = (r1 - mid.astype(F32)).astype(BF16)
    return jnp.concatenate([hi, mid, lo], axis=axis)


def _pack_rows(x, rounded):
    if not rounded:
        x = x.astype(BF16).astype(F32)
    bits = lax.bitcast_convert_type(x, jnp.uint32)
    half = x.shape[1] // 2
    return bits[:, half:] | (bits[:, :half] >> 16)


def _unpack_rows(p):
    lo = lax.bitcast_convert_type(p << 16, F32)
    hi = lax.bitcast_convert_type(p & jnp.uint32(0xFFFF0000), F32)
    return jnp.concatenate([lo, hi], axis=1).astype(BF16)


def _silu(x):
    return x * jax.nn.sigmoid(x)


def _softplus(x):
    return jnp.maximum(x, 0.0) + jnp.log1p(jnp.exp(-jnp.abs(x)))


def _rms(x, w):
    return x * lax.rsqrt(jnp.mean(x * x, axis=-1, keepdims=True) + EPS) * w


def _mixer_kernel(x_ref, xc_ref, *refs, tl, nl):
    n_in = len(refs) - 3 - 15
    consts, (x1_ref, h2_ref, lg_ref), scratch = refs[:n_in], refs[n_in:n_in + 3], refs[n_in + 3:]
    for half in range(2):
        rows = pl.ds(half * tl, tl)
        _mixer_step(x_ref.at[rows], xc_ref.at[rows], *consts, x1_ref.at[rows], h2_ref.at[rows],
                    lg_ref.at[:, rows], *scratch, tl=tl, nl=nl, slot_a=half, step=2 * pl.program_id(0) + half)


def _mixer_step(x_ref, xc_ref, n1_ref, wz_ref, wxbc_ref, wqk_ref, wv_ref, wo_ref, wg_ref,
                  scw_ref, scb_ref, dtb_ref, alog_ref, dskip_ref, snw_ref,
                  mcw_ref, mcb_ref, gb_ref, mnw_ref, wout_ref, n2_ref, wr_ref,
                  e16_ref, e2_ref, tril_ref, seq_ref, sge_ref, meq_ref, mge_ref,
                  x1_ref, h2_ref, lg_ref,
                  xbc_buf, qk_buf, z_buf, v_buf, o_buf, g_buf, xbc_c, qk_c, mix_buf,
                  xbc_tail, qk_tail, sst, mst, mm, h_buf, *, tl, nl, slot_a, step):
    nchunk = tl // CHUNK
    slot_b = 1 - slot_a
    slot_c = slot_a

    @pl.when(step == 0)
    def _():
        xbc_buf[...] = jnp.zeros_like(xbc_buf)
        qk_buf[...] = jnp.zeros_like(qk_buf)
        z_buf[...] = jnp.zeros_like(z_buf)
        v_buf[...] = jnp.zeros_like(v_buf)
        o_buf[...] = jnp.zeros_like(o_buf)
        g_buf[...] = jnp.zeros_like(g_buf)
        mix_buf[...] = jnp.zeros_like(mix_buf)
        xbc_tail[...] = jnp.zeros_like(xbc_tail)
        qk_tail[...] = jnp.zeros_like(qk_tail)
        sst[...] = jnp.zeros_like(sst)
        mst[...] = jnp.zeros_like(mst)
        mm[...] = jnp.full(mm.shape, STAB_INIT, F32)

    h_buf[...] = _rms(x_ref[...], n1_ref[...]).astype(BF16)
    nblk = 2 * LANES
    tasks = []

    def proj_task(w_ref, store):
        for b0 in range(0, w_ref.shape[1], nblk):
            tasks.append(lambda b0=b0: store(slice(b0, b0 + nblk), _dot(h_buf[...], w_ref[:, b0:b0 + nblk])))

    def out_task(cols):
        x1_ref[:, cols] = xc_ref[:, cols] + _dot(mix_buf[slot_c], wout_ref[:, cols])

    def route_task():
        h2 = _rms(x1_ref[...], n2_ref[...])
        h2 = h2.astype(BF16)
        h2_ref[...] = h2
        lg_ref[...] = _dot_nt(wr_ref[...], h2)

    for b0 in range(0, D_MODEL, nblk):
        tasks.append(lambda b0=b0: out_task(slice(b0, b0 + nblk)))
    tasks.append(route_task)

    def put(buf):
        def store(cols, val):
            buf[slot_a, :, cols] = val.astype(buf.dtype)
        return store

    def put_tiles(buf):
        def store(cols, val):
            for k in range(nblk // LANES):
                buf[slot_a, cols.start // LANES + k, CONV_PAD:CONV_PAD + tl, :] = val[:, k * LANES:(k + 1) * LANES]
        return store

    proj_task(wz_ref, put(z_buf))
    proj_task(wg_ref, put(g_buf))
    proj_task(wxbc_ref, put_tiles(xbc_buf))
    proj_task(wqk_ref, put_tiles(qk_buf))
    proj_task(wv_ref, put(v_buf))
    proj_task(wo_ref, put(o_buf))
    per_chunk = -(-len(tasks) // nchunk)
    assert per_chunk <= 8

    def run_task(c, k):
        idx = c * per_chunk + k
        if idx < len(tasks):
            tasks[idx]()

    seq_start = (step + nl - 1) % nl == 0
    xbc_buf[slot_b, :, 0:CONV_PAD, :] = jnp.where(seq_start, 0.0, xbc_tail[...])
    qk_buf[slot_b, :, 0:CONV_PAD, :] = jnp.where(seq_start, 0.0, qk_tail[...])
    sst[...] = jnp.where(seq_start, 0.0, sst[...])
    mst[...] = jnp.where(seq_start, 0.0, mst[...])
    mm[...] = jnp.where(seq_start, STAB_INIT, mm[...])

    first = CONV_PAD - (SSD_CONV - 1)
    xbc_tail[...] = xbc_buf[slot_b, :, tl:tl + CONV_PAD, :]
    qk_tail[...] = qk_buf[slot_b, :, tl:tl + CONV_PAD, :]

    def conv_silu(buf, w_ref, b_ref, out, r0):
        for t in range(buf.shape[1]):
            cols = slice(t * LANES, (t + 1) * LANES)
            acc = b_ref[:, cols] + w_ref[0:1, cols] * buf[slot_b, t, r0 + first:r0 + first + CHUNK, :]
            for j in range(1, w_ref.shape[0]):
                acc = acc + w_ref[j:j + 1, cols] * buf[slot_b, t, r0 + first + j:r0 + first + j + CHUNK, :]
            out[r0:r0 + CHUNK, cols] = _silu(acc)

    def convs(c):
        run_task(c, 0)
        conv_silu(xbc_buf, scw_ref, scb_ref, xbc_c, c * CHUNK)
        run_task(c, 1)
        conv_silu(qk_buf, mcw_ref, mcb_ref, qk_c, c * CHUNK)
        run_task(c, 2)

    for c in range(nchunk):
        convs(c)
        rows = slice(c * CHUNK, (c + 1) * CHUNK)
        tril3 = tril_ref[...]

        xs = xbc_c[rows, 0:SSD_WIDTH]
        bm = xbc_c[rows, SSD_WIDTH:SSD_WIDTH + 2 * SSD_STATE].astype(BF16)
        cm = xbc_c[rows, SSD_WIDTH + 2 * SSD_STATE:SSD_XBC].astype(BF16)
        dt = _softplus(g_buf[slot_b, rows, 0:SSD_HEADS] + dtb_ref[...])
        a_cs = _dot(tril3, _split3(dt * (-jnp.exp(alog_ref[...])), 0))
        dt_e = _dot(_split3(dt, 1), e16_ref[...])
        a_col = _dot(_split3(a_cs, 1), e16_ref[...])
        a_row = jnp.sum(a_col * seq_ref[...], axis=0, keepdims=True)
        lmat = jnp.exp(a_col - a_row + sge_ref[...])
        cb = jnp.concatenate(
            [_dot_nt(cm[:, g * SSD_STATE:(g + 1) * SSD_STATE],
                     jnp.concatenate([bm[:, g * SSD_STATE:(g + 1) * SSD_STATE]] * (SSD_HEADS // SSD_GROUPS), axis=0))
             for g in range(SSD_GROUPS)], axis=1)
        m_all = (cb * lmat).astype(BF16)
        run_task(c, 3)
        xdt = xs * dt_e
        lane = lax.broadcasted_iota(jnp.int32, (CHUNK, LANES), 1)
        y_parts = []
        for j in range(SSD_HEADS // 2):
            xp = xdt[:, j * LANES:(j + 1) * LANES]
            xbd = jnp.concatenate([jnp.where(lane < SSD_HEAD_DIM, xp, 0.0),
                                   jnp.where(lane >= SSD_HEAD_DIM, xp, 0.0)], axis=0).astype(BF16)
            y_parts.append(_dot(m_all[:, j * LANES:(j + 1) * LANES], xbd))
        y = jnp.concatenate(y_parts, axis=1)
        half = SSD_WIDTH // SSD_GROUPS
        y_int = jnp.concatenate(
            [_dot(cm[:, g * SSD_STATE:(g + 1) * SSD_STATE], sst[g].astype(BF16)) for g in range(SSD_GROUPS)], axis=1)
        y = y + y_int * jnp.exp(a_col)
        a_last = a_col[CHUNK - 1:CHUNK, :]
        xd_b = (xdt * jnp.exp(a_last - a_col)).astype(BF16)
        st_scale = jnp.exp(a_last)
        for g in range(SSD_GROUPS):
            sst[g] = sst[g] * st_scale[:, g * half:(g + 1) * half] + _dot_tn(
                bm[:, g * SSD_STATE:(g + 1) * SSD_STATE], xd_b[:, g * half:(g + 1) * half])
        y = (y + dskip_ref[...] * xs) * _silu(z_buf[slot_b, rows, :])
        for g in range(SSD_GROUPS):
            yg = y[:, g * half:(g + 1) * half]
            yg = yg * lax.rsqrt(jnp.mean(yg * yg, axis=-1, keepdims=True) + EPS) * snw_ref[:, g * half:(g + 1) * half]
            mix_buf[slot_b, rows, g * half:(g + 1) * half] = yg.astype(BF16)

        run_task(c, 4)

        gi = g_buf[slot_b, rows, LANES:LANES + 2 * ML_HEADS] + gb_ref[...]
        col16 = lax.broadcasted_iota(jnp.int32, (CHUNK, 2 * ML_HEADS), 1)
        gi = jnp.where(col16 < ML_HEADS, gi, jnp.minimum(gi, 0.0) - jnp.log1p(jnp.exp(-jnp.abs(gi))))
        gi = jnp.where(col16 < ML_HEADS, gi, _dot(tril3, _split3(gi, 0)))
        gi_e = _dot(_split3(gi, 1), e2_ref[...])
        ig_e = gi_e[:, 0:ML_WIDTH]
        b_col = gi_e[:, ML_WIDTH:2 * ML_WIDTH]
        c_row = jnp.sum((ig_e - b_col) * meq_ref[...], axis=0, keepdims=True)
        dmat = b_col + c_row + mge_ref[...]
        heads = range(ML_HEADS)
        hs = lambda arr, hd: arr[:, hd * LANES:(hd + 1) * LANES]
        ones_blk = jnp.ones((CHUNK, LANES), BF16)
        m_prev = mm[...]
        m_inter = b_col + m_prev
        q = [qk_c[rows, hd * LANES:(hd + 1) * LANES].astype(BF16) for hd in heads]
        kf = [qk_c[rows, ML_WIDTH + hd * LANES:ML_WIDTH + (hd + 1) * LANES] * (ML_HEAD_DIM ** -0.5) for hd in heads]
        v_aug = [jnp.concatenate([v_buf[slot_b, rows, hd * LANES:(hd + 1) * LANES], ones_blk], axis=1)
                 for hd in heads]
        s = [_dot_nt(q[hd], kf[hd].astype(BF16)) for hd in heads]
        qc = [_dot(q[hd], mst[hd].astype(BF16)) for hd in heads]
        run_task(c, 5)
        m_t = jnp.maximum(m_inter, jnp.concatenate(
            [jnp.broadcast_to(jnp.max(hs(dmat, hd), axis=-1, keepdims=True), (CHUNK, LANES)) for hd in heads], axis=1))
        w = jnp.exp(dmat - m_t)
        inter = jnp.exp(m_inter - m_t)
        den_floor = jnp.exp(-m_t)
        nd = [_dot((s[hd] * hs(w, hd)[:, 0:CHUNK]).astype(BF16), v_aug[hd])
              + qc[hd] * jnp.concatenate([hs(inter, hd)] * 2, axis=1) for hd in heads]
        run_task(c, 6)
        hh = [nd[hd][:, 0:LANES] / jnp.maximum(jnp.abs(nd[hd][:, LANES:2 * LANES]), hs(den_floor, hd)) for hd in heads]
        hc = [hh[hd] - jnp.mean(hh[hd], axis=-1, keepdims=True) for hd in heads]
        var = [jnp.mean(hc[hd] * hc[hd], axis=-1, keepdims=True) for hd in heads]
        for hd in heads:
            hn = hc[hd] * lax.rsqrt(var[hd] + EPS) * mnw_ref[:, hd * LANES:(hd + 1) * LANES]
            mix_buf[slot_b, rows, SSD_WIDTH + hd * LANES:SSD_WIDTH + (hd + 1) * LANES] = (
                jax.nn.sigmoid(o_buf[slot_b, rows, hd * LANES:(hd + 1) * LANES]) * hn).astype(BF16)
        run_task(c, 7)
        g_row = b_col[CHUNK - 1:CHUNK, :]
        a = g_row - b_col + ig_e
        m_new = jnp.maximum(g_row + m_prev, jnp.max(a, axis=0, keepdims=True))
        wk = jnp.exp(a - m_new)
        cs = jnp.exp(g_row + m_prev - m_new)
        for hd in heads:
            mst[hd] = mst[hd] * jnp.concatenate([hs(cs, hd)] * 2, axis=1) + _dot_tn(
                (kf[hd] * hs(wk, hd)).astype(BF16), v_aug[hd])
        mm[...] = m_new


def _const_spec(shape):
    nd = len(shape)
    return pl.BlockSpec(shape, lambda s: (0,) * nd, pipeline_mode=pl.Buffered(1))


def _mixer_call(x, consts, tl):
    bsz, seq, _ = x.shape
    nl = seq // tl
    n_pairs = bsz * nl // 2
    x_pairs = x.reshape(n_pairs, 2 * tl, D_MODEL)
    pair_in = lambda i: (jnp.minimum(i, n_pairs - 1), 0, 0)
    pair_out = lambda i: (jnp.maximum(i - 1, 0), 0, 0)
    in_specs = ([pl.BlockSpec((None, 2 * tl, D_MODEL), pair_in), pl.BlockSpec((None, 2 * tl, D_MODEL), pair_out)]
                + [_const_spec(c.shape) for c in consts])
    out_specs = [pl.BlockSpec((None, 2 * tl, D_MODEL), pair_out), pl.BlockSpec((None, 2 * tl, D_MODEL), pair_out),
                 pl.BlockSpec((LOGIT_ROWS, 2 * tl), lambda i: (0, jnp.maximum(i - 1, 0)))]
    out_shape = [jax.ShapeDtypeStruct((n_pairs, 2 * tl, D_MODEL), F32),
                 jax.ShapeDtypeStruct((n_pairs, 2 * tl, D_MODEL), BF16),
                 jax.ShapeDtypeStruct((LOGIT_ROWS, bsz * seq), F32)]
    scratch = [
        pltpu.VMEM((2, SSD_XBC // LANES, tl + CONV_PAD, LANES), F32),
        pltpu.VMEM((2, 2 * ML_WIDTH // LANES, tl + CONV_PAD, LANES), F32),
        pltpu.VMEM((2, tl, SSD_WIDTH), F32),
        pltpu.VMEM((2, tl, ML_WIDTH), BF16),
        pltpu.VMEM((2, tl, ML_WIDTH), F32),
        pltpu.VMEM((2, tl, GATE_COLS), F32),
        pltpu.VMEM((tl, SSD_XBC), F32),
        pltpu.VMEM((tl, 2 * ML_WIDTH), F32),
        pltpu.VMEM((2, tl, D_MIX), BF16),
        pltpu.VMEM((SSD_XBC // LANES, CONV_PAD, LANES), F32),
        pltpu.VMEM((2 * ML_WIDTH // LANES, CONV_PAD, LANES), F32),
        pltpu.VMEM((SSD_GROUPS, SSD_STATE, SSD_WIDTH // SSD_GROUPS), F32),
        pltpu.VMEM((ML_HEADS, ML_HEAD_DIM, 2 * LANES), F32),
        pltpu.VMEM((1, ML_WIDTH), F32),
        pltpu.VMEM((tl, D_MODEL), BF16),
    ]
    return pl.pallas_call(
        functools.partial(_mixer_kernel, tl=tl, nl=nl),
        grid=(n_pairs + 1,),
        in_specs=in_specs,
        out_specs=out_specs,
        out_shape=out_shape,
        scratch_shapes=scratch,
        compiler_params=pltpu.CompilerParams(
            dimension_semantics=("arbitrary",), vmem_limit_bytes=V7X_VMEM_LIMIT),
        name="mixer",
    )(x_pairs, x_pairs, *consts)


def _mixer_consts(norm1_w, w_in, ssd_conv_w, ssd_conv_b, ssd_dt_bias, ssd_a_log, ssd_d, ssd_norm_w,
                  ml_conv_w, ml_conv_b, ml_i_bias, ml_f_bias, ml_norm_w, w_out, norm2_w,
                  router_g_w, router_e_w):
    o1 = SSD_WIDTH
    o2 = o1 + SSD_XBC
    o3 = o2 + SSD_HEADS
    o4 = o3 + 2 * ML_WIDTH
    o5 = o4 + ML_WIDTH
    o6 = o5 + ML_WIDTH
    o7 = o6 + ML_HEADS
    zpad = lambda n: jnp.zeros((D_MODEL, n), F32)
    w_gate = jnp.concatenate([w_in[:, o2:o3], zpad(LANES - SSD_HEADS),
                              w_in[:, o6:o7], w_in[:, o7:], zpad(LANES - 2 * ML_HEADS)], axis=1)
    row = lambda v: v.reshape(1, -1).astype(F32)
    w_route = jnp.concatenate([router_e_w, router_g_w,
                               jnp.zeros((D_MODEL, LOGIT_ROWS - N_EXPERTS - MOE_GROUPS), F32)], axis=1).T

    c1024 = np.arange(SSD_WIDTH)
    r64 = np.arange(CHUNK)[:, None]
    e16 = np.tile(c1024[None, :] // SSD_HEAD_DIM == np.arange(SSD_HEADS)[:, None], (3, 1))
    e2 = np.tile(np.arange(2 * ML_WIDTH)[None, :] // LANES == np.arange(2 * ML_HEADS)[:, None], (3, 1))
    tril = np.tile(np.arange(CHUNK)[None, :] <= r64, (1, 3))
    s_pos = (c1024 % SSD_HEAD_DIM)[None, :]
    m_pos = (c1024 % LANES)[None, :]
    as_bf16 = lambda m: jnp.asarray(m.astype(np.float32), BF16)
    as_f32 = lambda m: jnp.asarray(m.astype(np.float32))
    e16, e2, tril = as_bf16(e16), as_bf16(e2), as_bf16(tril)
    causal = lambda keep: jnp.asarray(np.where(keep, 0.0, -np.inf).astype(np.float32))
    seq, sge, meq, mge = as_f32(r64 == s_pos), causal(r64 >= s_pos), as_f32(r64 == m_pos), causal(r64 >= m_pos)
    return [
        row(norm1_w),
        w_in[:, :o1].astype(BF16), w_in[:, o1:o2].astype(BF16), w_in[:, o3:o4].astype(BF16),
        w_in[:, o4:o5].astype(BF16), w_in[:, o5:o6].astype(BF16), w_gate.astype(BF16),
        ssd_conv_w.astype(F32), row(ssd_conv_b), row(ssd_dt_bias),
        row(ssd_a_log), row(jnp.repeat(ssd_d, SSD_HEAD_DIM)), row(ssd_norm_w),
        ml_conv_w.astype(F32), row(ml_conv_b), row(jnp.concatenate([ml_i_bias, ml_f_bias])), row(ml_norm_w),
        w_out.astype(BF16), row(norm2_w), w_route.astype(BF16),
        e16, e2, tril, seq, sge, meq, mge,
    ]


def _route_kernel(lg_ref, bias_ref, striu_ref, lpos_ref, gate_ref, cnt_ref):
    tr = striu_ref.shape[0]
    for half in range(lg_ref.shape[1] // tr):
        _route_tile(lg_ref[:, half * tr:(half + 1) * tr] + bias_ref[...], striu_ref,
                    lpos_ref.at[:, pl.ds(half * tr, tr)], gate_ref.at[:, pl.ds(half * tr, tr)], cnt_ref.at[half])


def _route_tile(lg, striu_ref, lpos_ref, gate_ref, cnt_ref):
    tr = lg.shape[1]
    el = lg[0:N_EXPERTS]
    gl = lg[N_EXPERTS:N_EXPERTS + 8]
    gmax = jnp.max(gl, axis=0, keepdims=True)
    pg = 1.0 / jnp.sum(jnp.exp(gl - gmax), axis=0, keepdims=True)
    grow = lax.broadcasted_iota(jnp.int32, gl.shape, 0)
    gsel = jnp.min(jnp.where(gl == gmax, grow, 8), axis=0, keepdims=True)
    erow = lax.broadcasted_iota(jnp.int32, el.shape, 0)
    m1 = jnp.where((erow >> 3) == gsel, el, NEG_INF)
    v1 = jnp.max(m1, axis=0, keepdims=True)
    i1 = jnp.min(jnp.where(m1 == v1, erow, N_EXPERTS), axis=0, keepdims=True)
    m2 = jnp.where(erow == i1, NEG_INF, m1)
    v2 = jnp.max(m2, axis=0, keepdims=True)
    i2 = jnp.min(jnp.where(m2 == v2, erow, N_EXPERTS), axis=0, keepdims=True)
    e2 = jnp.exp(v2 - v1)
    g1 = pg / (1.0 + e2)
    g2 = g1 * e2
    hit1 = erow == i1
    hit2 = erow == i2
    oh = jnp.where(hit1, 1.0, jnp.where(hit2, 1.0, 0.0))
    earlier = _dot(oh.astype(BF16), striu_ref[...])
    cnt = jnp.sum(oh, axis=1, keepdims=True)
    seg = jnp.floor((cnt + (ROW_ALIGN - 1)) * (1.0 / ROW_ALIGN)) * ROW_ALIGN
    l1 = jnp.sum(jnp.where(hit1, earlier, jnp.where(erow < i1, seg, 0.0)), axis=0, keepdims=True)
    l2 = jnp.sum(jnp.where(hit2, earlier, jnp.where(erow < i2, seg, 0.0)), axis=0, keepdims=True)
    lpos_ref[...] = jnp.concatenate([l1.astype(jnp.int32), l2.astype(jnp.int32), jnp.zeros((6, tr), jnp.int32)], axis=0)
    gate_ref[...] = jnp.concatenate([g1, g2, jnp.zeros((6, tr), F32)], axis=0)
    cnt_ref[...] = jnp.broadcast_to(cnt, cnt_ref.shape)


def _route_call(lg_t, bias_col, tr):
    t = lg_t.shape[1]
    striu = jnp.asarray(np.triu(np.ones((tr, tr), np.float32), 1), BF16)
    return pl.pallas_call(
        _route_kernel,
        grid=(t // (ROUTE_TILES * tr),),
        in_specs=[pl.BlockSpec((LOGIT_ROWS, ROUTE_TILES * tr), lambda i: (0, i)),
                  pl.BlockSpec((LOGIT_ROWS, 1), lambda i: (0, 0)),
                  pl.BlockSpec((tr, tr), lambda i: (0, 0))],
        out_specs=[pl.BlockSpec((8, ROUTE_TILES * tr), lambda i: (0, i)),
                   pl.BlockSpec((8, ROUTE_TILES * tr), lambda i: (0, i)),
                   pl.BlockSpec((ROUTE_TILES, N_EXPERTS, LANES), lambda i: (i, 0, 0))],
        out_shape=[jax.ShapeDtypeStruct((8, t), jnp.int32),
                   jax.ShapeDtypeStruct((8, t), F32),
                   jax.ShapeDtypeStruct((t // tr, N_EXPERTS, LANES), F32)],
        compiler_params=pltpu.CompilerParams(dimension_semantics=("arbitrary",)),
        name="route",
    )(lg_t, bias_col, striu)


def _run_copies(runs_ref, copy_run, max_piece, filler_far, enable=None):
    for e in range(N_EXPERTS + 1):
        dst = runs_ref[0, e] if e < N_EXPERTS else filler_far
        src = runs_ref[1, e]
        n = runs_ref[2, e] if enable is None else jnp.where(enable, runs_ref[2, e], 0)
        p = max_piece
        while p >= ROW_ALIGN:
            done = n & ~(2 * p - 1)

            @pl.when((n & p) != 0)
            def _(p=p, done=done, dst=dst, src=src):
                copy_run(pl.multiple_of(dst + done, ROW_ALIGN), pl.multiple_of(src + done, ROW_ALIGN), p)
            p //= 2


def _dispatch_kernel(runs_ref, prev_runs_ref, lpos_ref, h2_ref, xs_ref, loc, sem, *, max_spare):
    n_tiles = runs_ref.shape[0]
    tr = h2_ref.shape[0] // n_tiles
    nloc = loc.shape[1]
    step = pl.program_id(0)
    last = pl.num_programs(0) - 1
    used = runs_ref[0, 0, N_EXPERTS]

    def wait_block(s):
        pltpu.make_async_copy(loc.at[s], xs_ref.at[pl.ds(0, nloc)], sem.at[s]).wait()

    def send_block(table, s, enable=None):
        def copy_run(dst, src, n):
            pltpu.make_async_copy(loc.at[s, pl.ds(src, n)], xs_ref.at[pl.ds(dst, n)], sem.at[s]).start()
        _run_copies(table, copy_run, tr, used + s * (nloc - TOP_K * tr), enable)

    def sort_tile(k, s):
        cols = slice(k * tr, (k + 1) * tr)
        jrow = lax.broadcasted_iota(jnp.int32, (nloc, tr), 0)
        perm = jnp.where(jrow == lpos_ref[0:1, cols], 1.0, jnp.where(jrow == lpos_ref[1:2, cols], 1.0, 0.0)).astype(BF16)
        loc[s] = _pack_rows(_dot(perm, h2_ref[cols, :]), rounded=True)

    for k in range(n_tiles):
        s = k % 2
        if k < 2:
            pl.when(step > 0)(functools.partial(wait_block, s))
        else:
            wait_block(s)
        if k == 0:
            send_block(prev_runs_ref.at[n_tiles - 1], 1, enable=step > 0)
        else:
            send_block(runs_ref.at[k - 1], 1 - s)
        sort_tile(k, s)

    @pl.when(step == last)
    def _():
        send_block(runs_ref.at[n_tiles - 1], 1)
        wait_block(0)
        wait_block(1)
        loc[0, 0:tr, :] = jnp.zeros((tr, loc.shape[2]), loc.dtype)
        spare = xs_ref.shape[0] - used

        def zero_copy(off, n):
            return pltpu.make_async_copy(loc.at[0, pl.ds(0, n)],
                                         xs_ref.at[pl.ds(pl.multiple_of(used + off, ROW_ALIGN), n)], sem.at[0])

        def pieces(do):
            for i in range(max_spare // tr):
                @pl.when((i + 1) * tr <= spare)
                def _(i=i):
                    do(zero_copy(i * tr, tr))
            p = tr // 2
            while p >= ROW_ALIGN:
                @pl.when((spare & p) != 0)
                def _(p=p):
                    do(zero_copy(spare & ~(2 * p - 1), p))
                p //= 2

        pieces(lambda c: c.start())
        pieces(lambda c: c.wait())


def _dispatch_call(runs, lpos, h2, tr, n_rows):
    t = h2.shape[0]
    k = DISPATCH_TILES
    groups = runs.reshape(t // (k * tr), k, 3, N_EXPERTS + 1)
    return pl.pallas_call(
        functools.partial(_dispatch_kernel, max_spare=n_rows - t * TOP_K),
        grid=(t // (k * tr),),
        in_specs=[pl.BlockSpec((None, k, 3, N_EXPERTS + 1), lambda i: (i, 0, 0, 0), memory_space=pltpu.SMEM),
                  pl.BlockSpec((None, k, 3, N_EXPERTS + 1), lambda i: (jnp.maximum(i - 1, 0), 0, 0, 0),
                               memory_space=pltpu.SMEM),
                  pl.BlockSpec((8, k * tr), lambda i: (0, i)),
                  pl.BlockSpec((k * tr, D_MODEL), lambda i: (i, 0))],
        out_specs=pl.BlockSpec(memory_space=pl.ANY),
        out_shape=jax.ShapeDtypeStruct((n_rows, D_MODEL // 2), jnp.uint32),
        scratch_shapes=[pltpu.VMEM((2, _local_rows(tr), D_MODEL // 2), jnp.uint32), pltpu.SemaphoreType.DMA((2,))],
        compiler_params=pltpu.CompilerParams(dimension_semantics=("arbitrary",), vmem_limit_bytes=V7X_VMEM_LIMIT),
        name="dispatch",
    )(groups, groups, lpos, h2)


def _experts_kernel(tile_ref, exp_ref, starts_ref, xs_ref, wg_ref, wu_ref, wd_ref, ys_ref, wg_b, wu_b, wd_b):
    i = pl.program_id(0)
    tm = xs_ref.shape[0]
    e = exp_ref[i]
    tile = tile_ref[i]
    is_first = jnp.logical_or(i == 0, tile_ref[jnp.maximum(i - 1, 0)] != tile)

    @pl.when(jnp.logical_and(e < N_EXPERTS, jnp.logical_or(i == 0, exp_ref[jnp.maximum(i - 1, 0)] != e)))
    def _():
        wg_b[...] = wg_ref[...].astype(BF16)
        wu_b[...] = wu_ref[...].astype(BF16)
        wd_b[...] = wd_ref[...].astype(BF16)

    def expert_rows(other):
        x = _unpack_rows(xs_ref[...])
        act = (_silu(_dot(x, wg_b[...])) * _dot(x, wu_b[...])).astype(BF16)
        y = _pack_rows(_dot(act, wd_b[...]), rounded=False)
        rows = tile * tm + lax.broadcasted_iota(jnp.int32, (tm, 1), 0)
        return jnp.where((rows >= starts_ref[e]) & (rows < starts_ref[e + 1]), y, other)

    @pl.when(jnp.logical_and(e < N_EXPERTS, is_first))
    def _():
        ys_ref[...] = expert_rows(jnp.zeros_like(ys_ref))

    @pl.when(jnp.logical_and(e < N_EXPERTS, jnp.logical_not(is_first)))
    def _():
        ys_ref[...] = expert_rows(ys_ref[...])

    @pl.when(e == N_EXPERTS)
    def _():
        ys_ref[...] = jnp.zeros_like(ys_ref)


def _experts_call(item_tile, item_exp, starts_ext, xs, wg, wu, wd, tm):
    n = xs.shape[0]
    n_items = item_tile.shape[0]
    w_idx = lambda i, tile, ex, st: (jnp.minimum(ex[i], N_EXPERTS - 1), 0, 0)
    grid_spec = pltpu.PrefetchScalarGridSpec(
        num_scalar_prefetch=3,
        grid=(n_items,),
        in_specs=[pl.BlockSpec((tm, D_MODEL // 2), lambda i, tile, ex, st: (tile[i], 0)),
                  pl.BlockSpec((None, D_MODEL, D_EXPERT), w_idx),
                  pl.BlockSpec((None, D_MODEL, D_EXPERT), w_idx),
                  pl.BlockSpec((None, D_EXPERT, D_MODEL), w_idx)],
        out_specs=pl.BlockSpec((tm, D_MODEL // 2), lambda i, tile, ex, st: (tile[i], 0)),
        scratch_shapes=[pltpu.VMEM((D_MODEL, D_EXPERT), BF16), pltpu.VMEM((D_MODEL, D_EXPERT), BF16),
                        pltpu.VMEM((D_EXPERT, D_MODEL), BF16)],
    )
    return pl.pallas_call(
        _experts_kernel,
        grid_spec=grid_spec,
        out_shape=jax.ShapeDtypeStruct((n, D_MODEL // 2), jnp.uint32),
        compiler_params=pltpu.CompilerParams(
            dimension_semantics=("arbitrary",), vmem_limit_bytes=V7X_VMEM_LIMIT),
        name="experts",
    )(item_tile, item_exp, starts_ext, xs, wg, wu, wd)


def _local_rows(tr):
    return TOP_K * tr + N_EXPERTS * ROW_ALIGN


def _expert_items(counts, n_rows, tm):
    n_tiles = n_rows // tm
    n_items = n_tiles + N_EXPERTS - 1
    starts = jnp.concatenate([jnp.zeros((1,), jnp.int32), jnp.cumsum(counts).astype(jnp.int32)])
    lo = starts[:-1] // tm
    hi = (starts[1:] - 1) // tm
    per = jnp.where(counts > 0, hi - lo + 1, 0)
    cum = jnp.cumsum(per)
    idx = jnp.arange(n_items, dtype=jnp.int32)
    e = jnp.sum(idx[:, None] >= cum[None, :], axis=1).astype(jnp.int32)
    valid = idx < cum[-1]
    e_c = jnp.minimum(e, N_EXPERTS - 1)
    first_item = jnp.sum(jnp.where(e_c[:, None] == jnp.arange(N_EXPERTS)[None, :], (lo - (cum - per))[None, :], 0), axis=1)
    tile = first_item + idx
    spare_tile = -(-starts[-1] // tm) + idx - cum[-1]
    item_tile = jnp.where(valid, tile, jnp.minimum(spare_tile, n_tiles - 1)).astype(jnp.int32)
    item_exp = jnp.where(valid, e_c, jnp.where(spare_tile < n_tiles, N_EXPERTS, N_EXPERTS + 1)).astype(jnp.int32)
    starts_ext = jnp.concatenate([starts, starts[-1:], starts[-1:]])
    return item_tile, item_exp, starts, starts_ext


def _by_parity(body):
    def kernel(*refs, **kw):
        for parity in range(2):
            @pl.when(pl.program_id(0) % 2 == parity)
            def _(parity=parity):
                body(*refs, slot=parity, **kw)
    return kernel


def _combine_step(runs_ref, next_runs_ref, lpos_ref, gate_ref, next_lpos_ref, next_gate_ref, x1_ref, nf_ref, ys_ref,
                  out_ref, loc, sel_buf, sem, *, slot):
    tr = x1_ref.shape[0]
    nloc = loc.shape[1]
    step = pl.program_id(0)

    def gather(table, s, enable=None):
        def copy_run(dst, src, n):
            pltpu.make_async_copy(ys_ref.at[pl.ds(dst, n)], loc.at[s, pl.ds(src, n)], sem.at[s]).start()
        _run_copies(table, copy_run, tr, 0, enable)

    def build_sel(pos_ref, g_ref, s):
        jrow = lax.broadcasted_iota(jnp.int32, (nloc, tr), 0)
        sel_buf[s] = jnp.where(jrow == pos_ref[0:1, :], g_ref[0:1, :],
                               jnp.where(jrow == pos_ref[1:2, :], g_ref[1:2, :], 0.0)).astype(BF16)

    @pl.when(step == 0)
    def _():
        gather(runs_ref, slot)
        build_sel(lpos_ref, gate_ref, slot)

    pltpu.make_async_copy(ys_ref.at[pl.ds(0, nloc)], loc.at[slot], sem.at[slot]).wait()

    gather(next_runs_ref, 1 - slot, enable=step + 1 < pl.num_programs(0))
    y = _dot_tn(sel_buf[slot], _unpack_rows(loc[slot]))
    build_sel(next_lpos_ref, next_gate_ref, 1 - slot)
    out_ref[...] = _rms(x1_ref[...] + y, nf_ref[...])


def _combine_call(runs, lpos, gates, x1, nf_w, ys, tr):
    t = x1.shape[0]
    return pl.pallas_call(
        _by_parity(_combine_step),
        grid=(t // tr,),
        in_specs=[pl.BlockSpec((None, 3, N_EXPERTS + 1), lambda i: (i, 0, 0), memory_space=pltpu.SMEM),
                  pl.BlockSpec((None, 3, N_EXPERTS + 1), lambda i: (jnp.minimum(i + 1, t // tr - 1), 0, 0),
                               memory_space=pltpu.SMEM),
                  pl.BlockSpec((8, tr), lambda i: (0, i)),
                  pl.BlockSpec((8, tr), lambda i: (0, i)),
                  pl.BlockSpec((8, tr), lambda i: (0, jnp.minimum(i + 1, t // tr - 1))),
                  pl.BlockSpec((8, tr), lambda i: (0, jnp.minimum(i + 1, t // tr - 1))),
                  pl.BlockSpec((tr, D_MODEL), lambda i: (i, 0)),
                  pl.BlockSpec((1, D_MODEL), lambda i: (0, 0)),
                  pl.BlockSpec(memory_space=pl.ANY)],
        out_specs=pl.BlockSpec((tr, D_MODEL), lambda i: (i, 0)),
        out_shape=jax.ShapeDtypeStruct((t, D_MODEL), F32),
        scratch_shapes=[pltpu.VMEM((2, _local_rows(tr), D_MODEL // 2), jnp.uint32),
                        pltpu.VMEM((2, _local_rows(tr), tr), BF16), pltpu.SemaphoreType.DMA((2,))],
        compiler_params=pltpu.CompilerParams(dimension_semantics=("arbitrary",), vmem_limit_bytes=V7X_VMEM_LIMIT),
        name="combine",
    )(runs, runs, lpos, gates, lpos, gates, x1, nf_w, ys)


SEQ_TILE = 4 * CHUNK
TOKEN_TILE = 512
ROW_TILE = 512
ROUTE_TILES = 4
DISPATCH_TILES = 4


def kernel(x, norm1_w, w_in, ssd_conv_w, ssd_conv_b, ssd_dt_bias, ssd_a_log, ssd_d, ssd_norm_w, ml_conv_w, ml_conv_b, ml_i_bias, ml_f_bias, ml_norm_w, w_out, norm2_w, router_g_w, router_g_b, router_e_w, router_e_b, exp_w_gate, exp_w_up, exp_w_down, norm_f_w):
    bsz, seq, d = x.shape
    t = bsz * seq
    tl, tr, tm = SEQ_TILE, TOKEN_TILE, ROW_TILE
    assert d == D_MODEL and norm1_w.shape[0] == 1 and seq % tl == 0 and t % (2 * tl) == 0 and t % (max(ROUTE_TILES, DISPATCH_TILES) * tr) == 0

    consts = _mixer_consts(norm1_w[0], w_in[0], ssd_conv_w[0], ssd_conv_b[0], ssd_dt_bias[0], ssd_a_log[0],
                           ssd_d[0], ssd_norm_w[0], ml_conv_w[0], ml_conv_b[0], ml_i_bias[0], ml_f_bias[0],
                           ml_norm_w[0], w_out[0], norm2_w[0], router_g_w[0], router_e_w[0])
    x1, h2, lg_t = _mixer_call(x, consts, tl)
    x1 = x1.reshape(t, D_MODEL)
    h2 = h2.reshape(t, D_MODEL)

    bias_col = jnp.concatenate([router_e_b[0], router_g_b[0],
                                jnp.full((LOGIT_ROWS - N_EXPERTS - MOE_GROUPS,), STAB_INIT, F32)]).reshape(-1, 1)
    lpos, gates, cnt = _route_call(lg_t, bias_col.astype(F32), tr)

    seg = (cnt[:, :, 0].astype(jnp.int32) + (ROW_ALIGN - 1)) // ROW_ALIGN * ROW_ALIGN
    n_tok_tiles = t // tr
    n_rows = -(-(t * TOP_K + n_tok_tiles * N_EXPERTS * ROW_ALIGN + _local_rows(tr)) // tm) * tm
    item_tile, item_exp, starts, starts_ext = _expert_items(jnp.sum(seg, axis=0), n_rows, tm)
    run_dst = starts[None, :N_EXPERTS] + jnp.cumsum(seg, axis=0) - seg
    run_src = jnp.cumsum(seg, axis=1) - seg
    used = jnp.sum(seg, axis=1, keepdims=True)
    fill = jnp.concatenate([jnp.full_like(used, starts[N_EXPERTS]), used, _local_rows(tr) - used], axis=1)[:, :, None]
    runs = jnp.concatenate([jnp.stack([run_dst, run_src, seg], axis=1), fill], axis=2).astype(jnp.int32)

    xs = _dispatch_call(runs, lpos, h2, tr, n_rows)
    ys = _experts_call(item_tile, item_exp, starts_ext, xs, exp_w_gate[0], exp_w_up[0], exp_w_down[0], tm)
    out = _combine_call(runs, lpos, gates, x1, norm_f_w.reshape(1, -1).astype(F32), ys, tr)
    return out.reshape(bsz, seq, D_MODEL)
```

```python
import functools

import jax
import jax.numpy as jnp
import numpy as np
from jax import lax
from jax.experimental import pallas as pl
from jax.experimental.pallas import tpu as pltpu

D_MODEL = 1024
CHUNK = 64
SSD_WIDTH = 1024
SSD_HEAD_DIM = 64
SSD_HEADS = 16
SSD_GROUPS = 2
SSD_STATE = 128
SSD_CONV = 4
SSD_XBC = SSD_WIDTH + 2 * SSD_GROUPS * SSD_STATE
ML_WIDTH = 1024
ML_HEADS = 8
ML_HEAD_DIM = 128
ML_CONV = 4
D_MIX = SSD_WIDTH + ML_WIDTH
MOE_GROUPS = 4
EXPERTS_PER_GROUP = 8
N_EXPERTS = 32
TOP_K = 2
D_EXPERT = 512
EPS = 1e-6
STAB_INIT = -1e30

LANES = 128
GATE_COLS = 2 * LANES
ROW_ALIGN = 8
CONV_PAD = 8
LOGIT_ROWS = 64
V7X_VMEM_LIMIT = 60 * 1024 * 1024

F32 = jnp.float32
BF16 = jnp.bfloat16
NEG_INF = float("-inf")


def _dot(a, b):
    return jnp.dot(a, b, preferred_element_type=F32)


def _dot_nt(a, b):
    return lax.dot_general(a, b, (((1,), (1,)), ((), ())), preferred_element_type=F32)


def _dot_tn(a, b):
    return lax.dot_general(a, b, (((0,), (0,)), ((), ())), preferred_element_type=F32)


def _split3(x, axis):
    hi = x.astype(BF16)
    r1 = x - hi.astype(F32)
    mid = r1.astype(BF16)
    lo = (r1 - mid.astype(F32)).astype(BF16)
    return jnp.concatenate([hi, mid, lo], axis=axis)


def _pack_rows(x, rounded):
    if not rounded:
        x = x.astype(BF16).astype(F32)
    bits = lax.bitcast_convert_type(x, jnp.uint32)
    half = x.shape[1] // 2
    return bits[:, half:] | (bits[:, :half] >> 16)


def _unpack_rows(p):
    lo = lax.bitcast_convert_type(p << 16, F32)
    hi = lax.bitcast_convert_type(p & jnp.uint32(0xFFFF0000), F32)
    return jnp.concatenate([lo, hi], axis=1).astype(BF16)


def _silu(x):
    return x * jax.nn.sigmoid(x)


def _softplus(x):
    return jnp.maximum(x, 0.0) + jnp.log1p(jnp.exp(-jnp.abs(x)))


def _rms(x, w):
    return x * lax.rsqrt(jnp.mean(x * x, axis=-1, keepdims=True) + EPS) * w


def _mixer_kernel(x_ref, xc_ref, *refs, tl, nl, n_scratch):
    n_in = len(refs) - 3 - n_scratch
    consts, (x1_ref, h2_ref, lg_ref), scratch = refs[:n_in], refs[n_in:n_in + 3], refs[n_in + 3:]
    for half in range(2):
        rows = pl.ds(half * tl, tl)
        _mixer_step(x_ref.at[rows], xc_ref.at[rows], *consts, x1_ref.at[rows], h2_ref.at[rows],
                    lg_ref.at[:, rows], *scratch, tl=tl, nl=nl, slot_a=half, step=2 * pl.program_id(0) + half)


def _mixer_step(x_ref, xc_ref, n1_ref, wz_ref, wxbc_ref, wqk_ref, wv_ref, wo_ref, wg_ref,
                  scw_ref, scb_ref, dtb_ref, alog_ref, dskip_ref, snw_ref,
                  mcw_ref, mcb_ref, gb_ref, mnw_ref, wout_ref, n2_ref, wr_ref,
                  e16_ref, e2_ref, tril_ref, seq_ref, sge_ref, meq_ref, mge_ref,
                  x1_ref, h2_ref, lg_ref,
                  xbc_buf, qk_buf, z_buf, v_buf, o_buf, g_buf, xbc_c, qk_c, mix_buf,
                  xbc_tail, qk_tail, sst, mst, mm, h_buf, *, tl, nl, slot_a, step):
    nchunk = tl // CHUNK
    slot_b = 1 - slot_a
    slot_c = slot_a

    @pl.when(step == 0)
    def _():
        xbc_buf[...] = jnp.zeros_like(xbc_buf)
        qk_buf[...] = jnp.zeros_like(qk_buf)
        z_buf[...] = jnp.zeros_like(z_buf)
        v_buf[...] = jnp.zeros_like(v_buf)
        o_buf[...] = jnp.zeros_like(o_buf)
        g_buf[...] = jnp.zeros_like(g_buf)
        mix_buf[...] = jnp.zeros_like(mix_buf)
        xbc_tail[...] = jnp.zeros_like(xbc_tail)
        qk_tail[...] = jnp.zeros_like(qk_tail)
        sst[...] = jnp.zeros_like(sst)
        mst[...] = jnp.zeros_like(mst)
        mm[...] = jnp.full(mm.shape, STAB_INIT, F32)

    h_buf[...] = _rms(x_ref[...], n1_ref[...]).astype(BF16)
    nblk = 2 * LANES
    tasks = []

    def proj_task(w_ref, store):
        for b0 in range(0, w_ref.shape[1], nblk):
            tasks.append(lambda b0=b0: store(slice(b0, b0 + nblk), _dot(h_buf[...], w_ref[:, b0:b0 + nblk])))

    def out_task(cols):
        x1_ref[:, cols] = xc_ref[:, cols] + _dot(mix_buf[slot_c], wout_ref[:, cols])

    def route_task():
        h2 = _rms(x1_ref[...], n2_ref[...])
        h2 = h2.astype(BF16)
        h2_ref[...] = h2
        lg_ref[...] = _dot_nt(wr_ref[...], h2)

    for b0 in range(0, D_MODEL, nblk):
        tasks.append(lambda b0=b0: out_task(slice(b0, b0 + nblk)))
    tasks.append(route_task)

    def put(buf):
        def store(cols, val):
            buf[slot_a, :, cols] = val.astype(buf.dtype)
        return store

    def put_tiles(buf):
        def store(cols, val):
            for k in range(nblk // LANES):
                buf[slot_a, cols.start // LANES + k, CONV_PAD:CONV_PAD + tl, :] = val[:, k * LANES:(k + 1) * LANES]
        return store

    proj_task(wz_ref, put(z_buf))
    proj_task(wg_ref, put(g_buf))
    proj_task(wxbc_ref, put_tiles(xbc_buf))
    proj_task(wqk_ref, put_tiles(qk_buf))
    proj_task(wv_ref, put(v_buf))
    proj_task(wo_ref, put(o_buf))
    per_chunk = -(-len(tasks) // nchunk)
    assert per_chunk <= 8

    def run_task(c, k):
        idx = c * per_chunk + k
        if idx < len(tasks):
            tasks[idx]()

    seq_start = (step + nl - 1) % nl == 0
    xbc_buf[slot_b, :, 0:CONV_PAD, :] = jnp.where(seq_start, 0.0, xbc_tail[...])
    qk_buf[slot_b, :, 0:CONV_PAD, :] = jnp.where(seq_start, 0.0, qk_tail[...])
    sst[...] = jnp.where(seq_start, 0.0, sst[...])
    mst[...] = jnp.where(seq_start, 0.0, mst[...])
    mm[...] = jnp.where(seq_start, STAB_INIT, mm[...])

    first = CONV_PAD - (SSD_CONV - 1)
    xbc_tail[...] = xbc_buf[slot_b, :, tl:tl + CONV_PAD, :]
    qk_tail[...] = qk_buf[slot_b, :, tl:tl + CONV_PAD, :]

    def conv_silu(buf, w_ref, b_ref, out, r0):
        for t in range(buf.shape[1]):
            cols = slice(t * LANES, (t + 1) * LANES)
            acc = b_ref[:, cols] + w_ref[0:1, cols] * buf[slot_b, t, r0 + first:r0 + first + CHUNK, :]
            for j in range(1, w_ref.shape[0]):
                acc = acc + w_ref[j:j + 1, cols] * buf[slot_b, t, r0 + first + j:r0 + first + j + CHUNK, :]
            out[r0:r0 + CHUNK, cols] = _silu(acc)

    def convs(c):
        run_task(c, 0)
        conv_silu(xbc_buf, scw_ref, scb_ref, xbc_c, c * CHUNK)
        run_task(c, 1)
        conv_silu(qk_buf, mcw_ref, mcb_ref, qk_c, c * CHUNK)
        run_task(c, 2)

    for c in range(nchunk):
        convs(c)
        rows = slice(c * CHUNK, (c + 1) * CHUNK)
        tril3 = tril_ref[...]

        xs = xbc_c[rows, 0:SSD_WIDTH]
        bm = xbc_c[rows, SSD_WIDTH:SSD_WIDTH + 2 * SSD_STATE].astype(BF16)
        cm = xbc_c[rows, SSD_WIDTH + 2 * SSD_STATE:SSD_XBC].astype(BF16)
        dt = _softplus(g_buf[slot_b, rows, 0:SSD_HEADS] + dtb_ref[...])
        a_cs = _dot(tril3, _split3(dt * (-jnp.exp(alog_ref[...])), 0))
        dt_e = _dot(_split3(dt, 1), e16_ref[...])
        a_col = _dot(_split3(a_cs, 1), e16_ref[...])
        a_row = jnp.sum(a_col * seq_ref[...], axis=0, keepdims=True)
        lmat = jnp.exp(a_col - a_row + sge_ref[...])
        cb = jnp.concatenate(
            [_dot_nt(cm[:, g * SSD_STATE:(g + 1) * SSD_STATE],
                     jnp.concatenate([bm[:, g * SSD_STATE:(g + 1) * SSD_STATE]] * (SSD_HEADS // SSD_GROUPS), axis=0))
             for g in range(SSD_GROUPS)], axis=1)
        m_all = (cb * lmat).astype(BF16)
        run_task(c, 3)
        xdt = xs * dt_e
        lane = lax.broadcasted_iota(jnp.int32, (CHUNK, LANES), 1)
        y_parts = []
        for j in range(SSD_HEADS // 2):
            xp = xdt[:, j * LANES:(j + 1) * LANES]
            xbd = jnp.concatenate([jnp.where(lane < SSD_HEAD_DIM, xp, 0.0),
                                   jnp.where(lane >= SSD_HEAD_DIM, xp, 0.0)], axis=0).astype(BF16)
            y_parts.append(_dot(m_all[:, j * LANES:(j + 1) * LANES], xbd))
        y = jnp.concatenate(y_parts, axis=1)
        half = SSD_WIDTH // SSD_GROUPS
        y_int = jnp.concatenate(
            [_dot(cm[:, g * SSD_STATE:(g + 1) * SSD_STATE], sst[g].astype(BF16)) for g in range(SSD_GROUPS)], axis=1)
        y = y + y_int * jnp.exp(a_col)
        a_last = a_col[CHUNK - 1:CHUNK, :]
        xd_b = (xdt * jnp.exp(a_last - a_col)).astype(BF16)
        st_scale = jnp.exp(a_last)
        for g in range(SSD_GROUPS):
            sst[g] = sst[g] * st_scale[:, g * half:(g + 1) * half] + _dot_tn(
                bm[:, g * SSD_STATE:(g + 1) * SSD_STATE], xd_b[:, g * half:(g + 1) * half])
        y = (y + dskip_ref[...] * xs) * _silu(z_buf[slot_b, rows, :])
        for g in range(SSD_GROUPS):
            yg = y[:, g * half:(g + 1) * half]
            yg = yg * lax.rsqrt(jnp.mean(yg * yg, axis=-1, keepdims=True) + EPS) * snw_ref[:, g * half:(g + 1) * half]
            mix_buf[slot_b, rows, g * half:(g + 1) * half] = yg.astype(BF16)

        run_task(c, 4)

        gi = g_buf[slot_b, rows, LANES:LANES + 2 * ML_HEADS] + gb_ref[...]
        col16 = lax.broadcasted_iota(jnp.int32, (CHUNK, 2 * ML_HEADS), 1)
        gi = jnp.where(col16 < ML_HEADS, gi, jnp.minimum(gi, 0.0) - jnp.log1p(jnp.exp(-jnp.abs(gi))))
        gi = jnp.where(col16 < ML_HEADS, gi, _dot(tril3, _split3(gi, 0)))
        gi_e = _dot(_split3(gi, 1), e2_ref[...])
        ig_e = gi_e[:, 0:ML_WIDTH]
        b_col = gi_e[:, ML_WIDTH:2 * ML_WIDTH]
        c_row = jnp.sum((ig_e - b_col) * meq_ref[...], axis=0, keepdims=True)
        dmat = b_col + c_row + mge_ref[...]
        heads = range(ML_HEADS)
        hs = lambda arr, hd: arr[:, hd * LANES:(hd + 1) * LANES]
        ones_blk = jnp.ones((CHUNK, LANES), BF16)
        m_prev = mm[...]
        m_inter = b_col + m_prev
        q = [qk_c[rows, hd * LANES:(hd + 1) * LANES].astype(BF16) for hd in heads]
        kf = [qk_c[rows, ML_WIDTH + hd * LANES:ML_WIDTH + (hd + 1) * LANES] * (ML_HEAD_DIM ** -0.5) for hd in heads]
        v_aug = [jnp.concatenate([v_buf[slot_b, rows, hd * LANES:(hd + 1) * LANES], ones_blk], axis=1)
                 for hd in heads]
        s = [_dot_nt(q[hd], kf[hd].astype(BF16)) for hd in heads]
        qc = [_dot(q[hd], mst[hd].astype(BF16)) for hd in heads]
        run_task(c, 5)
        m_t = jnp.maximum(m_inter, jnp.concatenate(
            [jnp.broadcast_to(jnp.max(hs(dmat, hd), axis=-1, keepdims=True), (CHUNK, LANES)) for hd in heads], axis=1))
        w = jnp.exp(dmat - m_t)
        inter = jnp.exp(m_inter - m_t)
        den_floor = jnp.exp(-m_t)
        nd = [_dot((s[hd] * hs(w, hd)[:, 0:CHUNK]).astype(BF16), v_aug[hd])
              + qc[hd] * jnp.concatenate([hs(inter, hd)] * 2, axis=1) for hd in heads]
        run_task(c, 6)
        hh = [nd[hd][:, 0:LANES] / jnp.maximum(jnp.abs(nd[hd][:, LANES:2 * LANES]), hs(den_floor, hd)) for hd in heads]
        hc = [hh[hd] - jnp.mean(hh[hd], axis=-1, keepdims=True) for hd in heads]
        var = [jnp.mean(hc[hd] * hc[hd], axis=-1, keepdims=True) for hd in heads]
        for hd in heads:
            hn = hc[hd] * lax.rsqrt(var[hd] + EPS) * mnw_ref[:, hd * LANES:(hd + 1) * LANES]
            mix_buf[slot_b, rows, SSD_WIDTH + hd * LANES:SSD_WIDTH + (hd + 1) * LANES] = (
                jax.nn.sigmoid(o_buf[slot_b, rows, hd * LANES:(hd + 1) * LANES]) * hn).astype(BF16)
        run_task(c, 7)
        g_row = b_col[CHUNK - 1:CHUNK, :]
        a = g_row - b_col + ig_e
        m_new = jnp.maximum(g_row + m_prev, jnp.max(a, axis=0, keepdims=True))
        wk = jnp.exp(a - m_new)
        cs = jnp.exp(g_row + m_prev - m_new)
        for hd in heads:
            mst[hd] = mst[hd] * jnp.concatenate([hs(cs, hd)] * 2, axis=1) + _dot_tn(
                (kf[hd] * hs(wk, hd)).astype(BF16), v_aug[hd])
        mm[...] = m_new


def _const_spec(shape):
    nd = len(shape)
    return pl.BlockSpec(shape, lambda s: (0,) * nd, pipeline_mode=pl.Buffered(1))


def _mixer_call(x, consts, tl):
    bsz, seq, _ = x.shape
    nl = seq // tl
    n_pairs = bsz * nl // 2
    x_pairs = x.reshape(n_pairs, 2 * tl, D_MODEL)
    pair_in = lambda i: (jnp.minimum(i, n_pairs - 1), 0, 0)
    pair_out = lambda i: (jnp.maximum(i - 1, 0), 0, 0)
    in_specs = ([pl.BlockSpec((None, 2 * tl, D_MODEL), pair_in), pl.BlockSpec((None, 2 * tl, D_MODEL), pair_out)]
                + [_const_spec(c.shape) for c in consts])
    out_specs = [pl.BlockSpec((None, 2 * tl, D_MODEL), pair_out), pl.BlockSpec((None, 2 * tl, D_MODEL), pair_out),
                 pl.BlockSpec((LOGIT_ROWS, 2 * tl), lambda i: (0, jnp.maximum(i - 1, 0)))]
    out_shape = [jax.ShapeDtypeStruct((n_pairs, 2 * tl, D_MODEL), F32),
                 jax.ShapeDtypeStruct((n_pairs, 2 * tl, D_MODEL), BF16),
                 jax.ShapeDtypeStruct((LOGIT_ROWS, bsz * seq), F32)]
    scratch = [
        pltpu.VMEM((2, SSD_XBC // LANES, tl + CONV_PAD, LANES), F32),
        pltpu.VMEM((2, 2 * ML_WIDTH // LANES, tl + CONV_PAD, LANES), F32),
        pltpu.VMEM((2, tl, SSD_WIDTH), F32),
        pltpu.VMEM((2, tl, ML_WIDTH), BF16),
        pltpu.VMEM((2, tl, ML_WIDTH), F32),
        pltpu.VMEM((2, tl, GATE_COLS), F32),
        pltpu.VMEM((tl, SSD_XBC), F32),
        pltpu.VMEM((tl, 2 * ML_WIDTH), F32),
        pltpu.VMEM((2, tl, D_MIX), BF16),
        pltpu.VMEM((SSD_XBC // LANES, CONV_PAD, LANES), F32),
        pltpu.VMEM((2 * ML_WIDTH // LANES, CONV_PAD, LANES), F32),
        pltpu.VMEM((SSD_GROUPS, SSD_STATE, SSD_WIDTH // SSD_GROUPS), F32),
        pltpu.VMEM((ML_HEADS, ML_HEAD_DIM, 2 * LANES), F32),
        pltpu.VMEM((1, ML_WIDTH), F32),
        pltpu.VMEM((tl, D_MODEL), BF16),
    ]
    return pl.pallas_call(
        functools.partial(_mixer_kernel, tl=tl, nl=nl, n_scratch=len(scratch)),
        grid=(n_pairs + 1,),
        in_specs=in_specs,
        out_specs=out_specs,
        out_shape=out_shape,
        scratch_shapes=scratch,
        compiler_params=pltpu.CompilerParams(
            dimension_semantics=("arbitrary",), vmem_limit_bytes=V7X_VMEM_LIMIT),
        name="mixer",
    )(x_pairs, x_pairs, *consts)


def _mixer_consts(norm1_w, w_in, ssd_conv_w, ssd_conv_b, ssd_dt_bias, ssd_a_log, ssd_d, ssd_norm_w,
                  ml_conv_w, ml_conv_b, ml_i_bias, ml_f_bias, ml_norm_w, w_out, norm2_w,
                  router_g_w, router_e_w):
    o1 = SSD_WIDTH
    o2 = o1 + SSD_XBC
    o3 = o2 + SSD_HEADS
    o4 = o3 + 2 * ML_WIDTH
    o5 = o4 + ML_WIDTH
    o6 = o5 + ML_WIDTH
    o7 = o6 + ML_HEADS
    zpad = lambda n: jnp.zeros((D_MODEL, n), F32)
    w_gate = jnp.concatenate([w_in[:, o2:o3], zpad(LANES - SSD_HEADS),
                              w_in[:, o6:o7], w_in[:, o7:], zpad(LANES - 2 * ML_HEADS)], axis=1)
    row = lambda v: v.reshape(1, -1).astype(F32)
    w_route = jnp.concatenate([router_e_w, router_g_w,
                               jnp.zeros((D_MODEL, LOGIT_ROWS - N_EXPERTS - MOE_GROUPS), F32)], axis=1).T

    c1024 = np.arange(SSD_WIDTH)
    r64 = np.arange(CHUNK)[:, None]
    e16 = np.tile(c1024[None, :] // SSD_HEAD_DIM == np.arange(SSD_HEADS)[:, None], (3, 1))
    e2 = np.tile(np.arange(2 * ML_WIDTH)[None, :] // LANES == np.arange(2 * ML_HEADS)[:, None], (3, 1))
    tril = np.tile(np.arange(CHUNK)[None, :] <= r64, (1, 3))
    s_pos = (c1024 % SSD_HEAD_DIM)[None, :]
    m_pos = (c1024 % LANES)[None, :]
    as_bf16 = lambda m: jnp.asarray(m.astype(np.float32), BF16)
    as_f32 = lambda m: jnp.asarray(m.astype(np.float32))
    e16, e2, tril = as_bf16(e16), as_bf16(e2), as_bf16(tril)
    causal = lambda keep: jnp.asarray(np.where(keep, 0.0, -np.inf).astype(np.float32))
    seq, sge, meq, mge = as_f32(r64 == s_pos), causal(r64 >= s_pos), as_f32(r64 == m_pos), causal(r64 >= m_pos)
    return [
        row(norm1_w),
        w_in[:, :o1].astype(BF16), w_in[:, o1:o2].astype(BF16), w_in[:, o3:o4].astype(BF16),
        w_in[:, o4:o5].astype(BF16), w_in[:, o5:o6].astype(BF16), w_gate.astype(BF16),
        ssd_conv_w.astype(F32), row(ssd_conv_b), row(ssd_dt_bias),
        row(ssd_a_log), row(jnp.repeat(ssd_d, SSD_HEAD_DIM)), row(ssd_norm_w),
        ml_conv_w.astype(F32), row(ml_conv_b), row(jnp.concatenate([ml_i_bias, ml_f_bias])), row(ml_norm_w),
        w_out.astype(BF16), row(norm2_w), w_route.astype(BF16),
        e16, e2, tril, seq, sge, meq, mge,
    ]


def _route_kernel(lg_ref, bias_ref, striu_ref, lpos_ref, gate_ref, cnt_ref):
    tr = striu_ref.shape[0]
    for half in range(lg_ref.shape[1] // tr):
        _route_tile(lg_ref[:, half * tr:(half + 1) * tr] + bias_ref[...], striu_ref,
                    lpos_ref.at[:, pl.ds(half * tr, tr)], gate_ref.at[:, pl.ds(half * tr, tr)], cnt_ref.at[half])


def _route_tile(lg, striu_ref, lpos_ref, gate_ref, cnt_ref):
    tr = lg.shape[1]
    el = lg[0:N_EXPERTS]
    gl = lg[N_EXPERTS:N_EXPERTS + 8]
    gmax = jnp.max(gl, axis=0, keepdims=True)
    pg = 1.0 / jnp.sum(jnp.exp(gl - gmax), axis=0, keepdims=True)
    grow = lax.broadcasted_iota(jnp.int32, gl.shape, 0)
    gsel = jnp.min(jnp.where(gl == gmax, grow, 8), axis=0, keepdims=True)
    erow = lax.broadcasted_iota(jnp.int32, el.shape, 0)
    m1 = jnp.where(erow // EXPERTS_PER_GROUP == gsel, el, NEG_INF)
    v1 = jnp.max(m1, axis=0, keepdims=True)
    i1 = jnp.min(jnp.where(m1 == v1, erow, N_EXPERTS), axis=0, keepdims=True)
    m2 = jnp.where(erow == i1, NEG_INF, m1)
    v2 = jnp.max(m2, axis=0, keepdims=True)
    i2 = jnp.min(jnp.where(m2 == v2, erow, N_EXPERTS), axis=0, keepdims=True)
    e2 = jnp.exp(v2 - v1)
    g1 = pg / (1.0 + e2)
    g2 = g1 * e2
    hit1 = erow == i1
    hit2 = erow == i2
    oh = jnp.where(hit1, 1.0, jnp.where(hit2, 1.0, 0.0))
    earlier = _dot(oh.astype(BF16), striu_ref[...])
    cnt = jnp.sum(oh, axis=1, keepdims=True)
    seg = jnp.floor((cnt + (ROW_ALIGN - 1)) * (1.0 / ROW_ALIGN)) * ROW_ALIGN
    l1 = jnp.sum(jnp.where(hit1, earlier, jnp.where(erow < i1, seg, 0.0)), axis=0, keepdims=True)
    l2 = jnp.sum(jnp.where(hit2, earlier, jnp.where(erow < i2, seg, 0.0)), axis=0, keepdims=True)
    lpos_ref[...] = jnp.concatenate([l1.astype(jnp.int32), l2.astype(jnp.int32), jnp.zeros((6, tr), jnp.int32)], axis=0)
    gate_ref[...] = jnp.concatenate([g1, g2, jnp.zeros((6, tr), F32)], axis=0)
    cnt_ref[...] = jnp.broadcast_to(cnt, cnt_ref.shape)


def _route_call(lg_t, bias_col, tr):
    t = lg_t.shape[1]
    striu = jnp.asarray(np.triu(np.ones((tr, tr), np.float32), 1), BF16)
    return pl.pallas_call(
        _route_kernel,
        grid=(t // (ROUTE_TILES * tr),),
        in_specs=[pl.BlockSpec((LOGIT_ROWS, ROUTE_TILES * tr), lambda i: (0, i)),
                  pl.BlockSpec((LOGIT_ROWS, 1), lambda i: (0, 0)),
                  pl.BlockSpec((tr, tr), lambda i: (0, 0))],
        out_specs=[pl.BlockSpec((8, ROUTE_TILES * tr), lambda i: (0, i)),
                   pl.BlockSpec((8, ROUTE_TILES * tr), lambda i: (0, i)),
                   pl.BlockSpec((ROUTE_TILES, N_EXPERTS, LANES), lambda i: (i, 0, 0))],
        out_shape=[jax.ShapeDtypeStruct((8, t), jnp.int32),
                   jax.ShapeDtypeStruct((8, t), F32),
                   jax.ShapeDtypeStruct((t // tr, N_EXPERTS, LANES), F32)],
        compiler_params=pltpu.CompilerParams(dimension_semantics=("arbitrary",)),
        name="route",
    )(lg_t, bias_col, striu)


def _run_copies(runs_ref, copy_run, max_piece, filler_far, enable=None):
    for e in range(N_EXPERTS + 1):
        dst = runs_ref[0, e] if e < N_EXPERTS else filler_far
        src = runs_ref[1, e]
        n = runs_ref[2, e] if enable is None else jnp.where(enable, runs_ref[2, e], 0)
        p = max_piece
        while p >= ROW_ALIGN:
            done = n & ~(2 * p - 1)

            @pl.when((n & p) != 0)
            def _(p=p, done=done, dst=dst, src=src):
                copy_run(pl.multiple_of(dst + done, ROW_ALIGN), pl.multiple_of(src + done, ROW_ALIGN), p)
            p //= 2


def _dispatch_kernel(runs_ref, prev_runs_ref, lpos_ref, h2_ref, xs_ref, loc, sem, *, max_spare):
    n_tiles = runs_ref.shape[0]
    tr = h2_ref.shape[0] // n_tiles
    nloc = loc.shape[1]
    step = pl.program_id(0)
    last = pl.num_programs(0) - 1
    used = runs_ref[0, 0, N_EXPERTS]

    def wait_block(s):
        pltpu.make_async_copy(loc.at[s], xs_ref.at[pl.ds(0, nloc)], sem.at[s]).wait()

    def send_block(table, s, enable=None):
        def copy_run(dst, src, n):
            pltpu.make_async_copy(loc.at[s, pl.ds(src, n)], xs_ref.at[pl.ds(dst, n)], sem.at[s]).start()
        _run_copies(table, copy_run, tr, used + s * (nloc - TOP_K * tr), enable)

    def sort_tile(k, s):
        cols = slice(k * tr, (k + 1) * tr)
        jrow = lax.broadcasted_iota(jnp.int32, (nloc, tr), 0)
        perm = jnp.where(jrow == lpos_ref[0:1, cols], 1.0, jnp.where(jrow == lpos_ref[1:2, cols], 1.0, 0.0)).astype(BF16)
        loc[s] = _pack_rows(_dot(perm, h2_ref[cols, :]), rounded=True)

    for k in range(n_tiles):
        s = k % 2
        if k < 2:
            pl.when(step > 0)(functools.partial(wait_block, s))
        else:
            wait_block(s)
        if k == 0:
            send_block(prev_runs_ref.at[n_tiles - 1], 1, enable=step > 0)
        else:
            send_block(runs_ref.at[k - 1], 1 - s)
        sort_tile(k, s)

    @pl.when(step == last)
    def _():
        send_block(runs_ref.at[n_tiles - 1], 1)
        wait_block(0)
        wait_block(1)
        loc[0, 0:tr, :] = jnp.zeros((tr, loc.shape[2]), loc.dtype)
        spare = xs_ref.shape[0] - used

        def zero_copy(off, n):
            return pltpu.make_async_copy(loc.at[0, pl.ds(0, n)],
                                         xs_ref.at[pl.ds(pl.multiple_of(used + off, ROW_ALIGN), n)], sem.at[0])

        def pieces(do):
            for i in range(max_spare // tr):
                @pl.when((i + 1) * tr <= spare)
                def _(i=i):
                    do(zero_copy(i * tr, tr))
            p = tr // 2
            while p >= ROW_ALIGN:
                @pl.when((spare & p) != 0)
                def _(p=p):
                    do(zero_copy(spare & ~(2 * p - 1), p))
                p //= 2

        pieces(lambda c: c.start())
        pieces(lambda c: c.wait())


def _dispatch_call(runs, lpos, h2, tr, n_rows):
    t = h2.shape[0]
    k = DISPATCH_TILES
    groups = runs.reshape(t // (k * tr), k, 3, N_EXPERTS + 1)
    return pl.pallas_call(
        functools.partial(_dispatch_kernel, max_spare=n_rows - t * TOP_K),
        grid=(t // (k * tr),),
        in_specs=[pl.BlockSpec((None, k, 3, N_EXPERTS + 1), lambda i: (i, 0, 0, 0), memory_space=pltpu.SMEM),
                  pl.BlockSpec((None, k, 3, N_EXPERTS + 1), lambda i: (jnp.maximum(i - 1, 0), 0, 0, 0),
                               memory_space=pltpu.SMEM),
                  pl.BlockSpec((8, k * tr), lambda i: (0, i)),
                  pl.BlockSpec((k * tr, D_MODEL), lambda i: (i, 0))],
        out_specs=pl.BlockSpec(memory_space=pl.ANY),
        out_shape=jax.ShapeDtypeStruct((n_rows, D_MODEL // 2), jnp.uint32),
        scratch_shapes=[pltpu.VMEM((2, _local_rows(tr), D_MODEL // 2), jnp.uint32), pltpu.SemaphoreType.DMA((2,))],
        compiler_params=pltpu.CompilerParams(dimension_semantics=("arbitrary",), vmem_limit_bytes=V7X_VMEM_LIMIT),
        name="dispatch",
    )(groups, groups, lpos, h2)


def _experts_kernel(tile_ref, exp_ref, starts_ref, xs_ref, wg_ref, wu_ref, wd_ref, ys_ref, wg_b, wu_b, wd_b):
    i = pl.program_id(0)
    tm = xs_ref.shape[0]
    e = exp_ref[i]
    tile = tile_ref[i]
    is_first = jnp.logical_or(i == 0, tile_ref[jnp.maximum(i - 1, 0)] != tile)

    @pl.when(jnp.logical_and(e < N_EXPERTS, jnp.logical_or(i == 0, exp_ref[jnp.maximum(i - 1, 0)] != e)))
    def _():
        wg_b[...] = wg_ref[...].astype(BF16)
        wu_b[...] = wu_ref[...].astype(BF16)
        wd_b[...] = wd_ref[...].astype(BF16)

    def expert_rows(other):
        x = _unpack_rows(xs_ref[...])
        act = (_silu(_dot(x, wg_b[...])) * _dot(x, wu_b[...])).astype(BF16)
        y = _pack_rows(_dot(act, wd_b[...]), rounded=False)
        rows = tile * tm + lax.broadcasted_iota(jnp.int32, (tm, 1), 0)
        return jnp.where((rows >= starts_ref[e]) & (rows < starts_ref[e + 1]), y, other)

    @pl.when(jnp.logical_and(e < N_EXPERTS, is_first))
    def _():
        ys_ref[...] = expert_rows(jnp.zeros_like(ys_ref))

    @pl.when(jnp.logical_and(e < N_EXPERTS, jnp.logical_not(is_first)))
    def _():
        ys_ref[...] = expert_rows(ys_ref[...])

    @pl.when(e == N_EXPERTS)
    def _():
        ys_ref[...] = jnp.zeros_like(ys_ref)


def _experts_call(item_tile, item_exp, starts_ext, xs, wg, wu, wd, tm):
    n = xs.shape[0]
    n_items = item_tile.shape[0]
    w_idx = lambda i, tile, ex, st: (jnp.minimum(ex[i], N_EXPERTS - 1), 0, 0)
    grid_spec = pltpu.PrefetchScalarGridSpec(
        num_scalar_prefetch=3,
        grid=(n_items,),
        in_specs=[pl.BlockSpec((tm, D_MODEL // 2), lambda i, tile, ex, st: (tile[i], 0)),
                  pl.BlockSpec((None, D_MODEL, D_EXPERT), w_idx),
                  pl.BlockSpec((None, D_MODEL, D_EXPERT), w_idx),
                  pl.BlockSpec((None, D_EXPERT, D_MODEL), w_idx)],
        out_specs=pl.BlockSpec((tm, D_MODEL // 2), lambda i, tile, ex, st: (tile[i], 0)),
        scratch_shapes=[pltpu.VMEM((D_MODEL, D_EXPERT), BF16), pltpu.VMEM((D_MODEL, D_EXPERT), BF16),
                        pltpu.VMEM((D_EXPERT, D_MODEL), BF16)],
    )
    return pl.pallas_call(
        _experts_kernel,
        grid_spec=grid_spec,
        out_shape=jax.ShapeDtypeStruct((n, D_MODEL // 2), jnp.uint32),
        compiler_params=pltpu.CompilerParams(
            dimension_semantics=("arbitrary",), vmem_limit_bytes=V7X_VMEM_LIMIT),
        name="experts",
    )(item_tile, item_exp, starts_ext, xs, wg, wu, wd)


def _local_rows(tr):
    return TOP_K * tr + N_EXPERTS * ROW_ALIGN


def _expert_items(counts, n_rows, tm):
    n_tiles = n_rows // tm
    n_items = n_tiles + N_EXPERTS - 1
    starts = jnp.concatenate([jnp.zeros((1,), jnp.int32), jnp.cumsum(counts).astype(jnp.int32)])
    lo = starts[:-1] // tm
    hi = (starts[1:] - 1) // tm
    per = jnp.where(counts > 0, hi - lo + 1, 0)
    cum = jnp.cumsum(per)
    idx = jnp.arange(n_items, dtype=jnp.int32)
    e = jnp.sum(idx[:, None] >= cum[None, :], axis=1).astype(jnp.int32)
    valid = idx < cum[-1]
    e_c = jnp.minimum(e, N_EXPERTS - 1)
    first_item = jnp.sum(jnp.where(e_c[:, None] == jnp.arange(N_EXPERTS)[None, :], (lo - (cum - per))[None, :], 0), axis=1)
    tile = first_item + idx
    spare_tile = -(-starts[-1] // tm) + idx - cum[-1]
    item_tile = jnp.where(valid, tile, jnp.minimum(spare_tile, n_tiles - 1)).astype(jnp.int32)
    item_exp = jnp.where(valid, e_c, jnp.where(spare_tile < n_tiles, N_EXPERTS, N_EXPERTS + 1)).astype(jnp.int32)
    starts_ext = jnp.concatenate([starts, starts[-1:], starts[-1:]])
    return item_tile, item_exp, starts, starts_ext


def _by_parity(body):
    def kernel(*refs, **kw):
        for parity in range(2):
            @pl.when(pl.program_id(0) % 2 == parity)
            def _(parity=parity):
                body(*refs, slot=parity, **kw)
    return kernel


def _combine_step(runs_ref, next_runs_ref, lpos_ref, gate_ref, next_lpos_ref, next_gate_ref, x1_ref, nf_ref, ys_ref,
                  out_ref, loc, sel_buf, sem, *, slot):
    tr = x1_ref.shape[0]
    nloc = loc.shape[1]
    step = pl.program_id(0)

    def gather(table, s, enable=None):
        def copy_run(dst, src, n):
            pltpu.make_async_copy(ys_ref.at[pl.ds(dst, n)], loc.at[s, pl.ds(src, n)], sem.at[s]).start()
        _run_copies(table, copy_run, tr, 0, enable)

    def build_sel(pos_ref, g_ref, s):
        jrow = lax.broadcasted_iota(jnp.int32, (nloc, tr), 0)
        sel_buf[s] = jnp.where(jrow == pos_ref[0:1, :], g_ref[0:1, :],
                               jnp.where(jrow == pos_ref[1:2, :], g_ref[1:2, :], 0.0)).astype(BF16)

    @pl.when(step == 0)
    def _():
        gather(runs_ref, slot)
        build_sel(lpos_ref, gate_ref, slot)

    pltpu.make_async_copy(ys_ref.at[pl.ds(0, nloc)], loc.at[slot], sem.at[slot]).wait()

    gather(next_runs_ref, 1 - slot, enable=step + 1 < pl.num_programs(0))
    y = _dot_tn(sel_buf[slot], _unpack_rows(loc[slot]))
    build_sel(next_lpos_ref, next_gate_ref, 1 - slot)
    out_ref[...] = _rms(x1_ref[...] + y, nf_ref[...])


def _combine_call(runs, lpos, gates, x1, nf_w, ys, tr):
    t = x1.shape[0]
    return pl.pallas_call(
        _by_parity(_combine_step),
        grid=(t // tr,),
        in_specs=[pl.BlockSpec((None, 3, N_EXPERTS + 1), lambda i: (i, 0, 0), memory_space=pltpu.SMEM),
                  pl.BlockSpec((None, 3, N_EXPERTS + 1), lambda i: (jnp.minimum(i + 1, t // tr - 1), 0, 0),
                               memory_space=pltpu.SMEM),
                  pl.BlockSpec((8, tr), lambda i: (0, i)),
                  pl.BlockSpec((8, tr), lambda i: (0, i)),
                  pl.BlockSpec((8, tr), lambda i: (0, jnp.minimum(i + 1, t // tr - 1))),
                  pl.BlockSpec((8, tr), lambda i: (0, jnp.minimum(i + 1, t // tr - 1))),
                  pl.BlockSpec((tr, D_MODEL), lambda i: (i, 0)),
                  pl.BlockSpec((1, D_MODEL), lambda i: (0, 0)),
                  pl.BlockSpec(memory_space=pl.ANY)],
        out_specs=pl.BlockSpec((tr, D_MODEL), lambda i: (i, 0)),
        out_shape=jax.ShapeDtypeStruct((t, D_MODEL), F32),
        scratch_shapes=[pltpu.VMEM((2, _local_rows(tr), D_MODEL // 2), jnp.uint32),
                        pltpu.VMEM((2, _local_rows(tr), tr), BF16), pltpu.SemaphoreType.DMA((2,))],
        compiler_params=pltpu.CompilerParams(dimension_semantics=("arbitrary",), vmem_limit_bytes=V7X_VMEM_LIMIT),
        name="combine",
    )(runs, runs, lpos, gates, lpos, gates, x1, nf_w, ys)


SEQ_TILE = 4 * CHUNK
TOKEN_TILE = 512
ROW_TILE = 512
ROUTE_TILES = 8
DISPATCH_TILES = 4


def kernel(x, norm1_w, w_in, ssd_conv_w, ssd_conv_b, ssd_dt_bias, ssd_a_log, ssd_d, ssd_norm_w, ml_conv_w, ml_conv_b, ml_i_bias, ml_f_bias, ml_norm_w, w_out, norm2_w, router_g_w, router_g_b, router_e_w, router_e_b, exp_w_gate, exp_w_up, exp_w_down, norm_f_w):
    bsz, seq, d = x.shape
    t = bsz * seq
    tl, tr, tm = SEQ_TILE, TOKEN_TILE, ROW_TILE
    assert d == D_MODEL and norm1_w.shape[0] == 1 and seq % tl == 0 and t % (2 * tl) == 0 and t % (max(ROUTE_TILES, DISPATCH_TILES) * tr) == 0

    consts = _mixer_consts(norm1_w[0], w_in[0], ssd_conv_w[0], ssd_conv_b[0], ssd_dt_bias[0], ssd_a_log[0],
                           ssd_d[0], ssd_norm_w[0], ml_conv_w[0], ml_conv_b[0], ml_i_bias[0], ml_f_bias[0],
                           ml_norm_w[0], w_out[0], norm2_w[0], router_g_w[0], router_e_w[0])
    x1, h2, lg_t = _mixer_call(x, consts, tl)
    x1 = x1.reshape(t, D_MODEL)
    h2 = h2.reshape(t, D_MODEL)

    bias_col = jnp.concatenate([router_e_b[0], router_g_b[0],
                                jnp.full((LOGIT_ROWS - N_EXPERTS - MOE_GROUPS,), STAB_INIT, F32)]).reshape(-1, 1)
    lpos, gates, cnt = _route_call(lg_t, bias_col.astype(F32), tr)

    seg = (cnt[:, :, 0].astype(jnp.int32) + (ROW_ALIGN - 1)) // ROW_ALIGN * ROW_ALIGN
    n_tok_tiles = t // tr
    n_rows = -(-(t * TOP_K + n_tok_tiles * N_EXPERTS * ROW_ALIGN + _local_rows(tr)) // tm) * tm
    item_tile, item_exp, starts, starts_ext = _expert_items(jnp.sum(seg, axis=0), n_rows, tm)
    run_dst = starts[None, :N_EXPERTS] + jnp.cumsum(seg, axis=0) - seg
    run_src = jnp.cumsum(seg, axis=1) - seg
    used = jnp.sum(seg, axis=1, keepdims=True)
    fill = jnp.concatenate([jnp.full_like(used, starts[N_EXPERTS]), used, _local_rows(tr) - used], axis=1)[:, :, None]
    runs = jnp.concatenate([jnp.stack([run_dst, run_src, seg], axis=1), fill], axis=2).astype(jnp.int32)

    xs = _dispatch_call(runs, lpos, h2, tr, n_rows)
    ys = _experts_call(item_tile, item_exp, starts_ext, xs, exp_w_gate[0], exp_w_up[0], exp_w_down[0], tm)
    out = _combine_call(runs, lpos, gates, x1, norm_f_w.reshape(1, -1).astype(F32), ys, tr)
    return out.reshape(bsz, seq, D_MODEL)
```

```python
import functools

import jax
import jax.numpy as jnp
import numpy as np
from jax import lax
from jax.experimental import pallas as pl
from jax.experimental.pallas import tpu as pltpu

D_MODEL = 1024
CHUNK = 64
SSD_WIDTH = 1024
SSD_HEAD_DIM = 64
SSD_HEADS = 16
SSD_GROUPS = 2
SSD_STATE = 128
SSD_CONV = 4
SSD_XBC = SSD_WIDTH + 2 * SSD_GROUPS * SSD_STATE
ML_WIDTH = 1024
ML_HEADS = 8
ML_HEAD_DIM = 128
ML_CONV = 4
D_MIX = SSD_WIDTH + ML_WIDTH
MOE_GROUPS = 4
EXPERTS_PER_GROUP = 8
N_EXPERTS = 32
TOP_K = 2
D_EXPERT = 512
EPS = 1e-6
STAB_INIT = -1e30

LANES = 128
GATE_COLS = 2 * LANES
ROW_ALIGN = 8
CONV_PAD = 8
LOGIT_ROWS = 64
V7X_VMEM_LIMIT = 60 * 1024 * 1024

F32 = jnp.float32
BF16 = jnp.bfloat16
NEG_INF = float("-inf")


def _dot(a, b):
    return jnp.dot(a, b, preferred_element_type=F32)


def _dot_nt(a, b):
    return lax.dot_general(a, b, (((1,), (1,)), ((), ())), preferred_element_type=F32)


def _dot_tn(a, b):
    return lax.dot_general(a, b, (((0,), (0,)), ((), ())), preferred_element_type=F32)


def _split3(x, axis):
    hi = x.astype(BF16)
    r1 = x - hi.astype(F32)
    mid = r1.astype(BF16)
    lo = (r1 - mid.astype(F32)).astype(BF16)
    return jnp.concatenate([hi, mid, lo], axis=axis)


def _pack_rows(x, rounded):
    if not rounded:
        x = x.astype(BF16).astype(F32)
    bits = lax.bitcast_convert_type(x, jnp.uint32)
    half = x.shape[1] // 2
    return bits[:, half:] | (bits[:, :half] >> 16)


def _unpack_rows(p):
    lo = lax.bitcast_convert_type(p << 16, F32)
    hi = lax.bitcast_convert_type(p & jnp.uint32(0xFFFF0000), F32)
    return jnp.concatenate([lo, hi], axis=1).astype(BF16)


def _silu(x):
    return x * jax.nn.sigmoid(x)


def _softplus(x):
    return jnp.maximum(x, 0.0) + jnp.log1p(jnp.exp(-jnp.abs(x)))


def _rms(x, w):
    return x * lax.rsqrt(jnp.mean(x * x, axis=-1, keepdims=True) + EPS) * w


def _mixer_kernel(x_ref, xc_ref, *refs, tl, nl):
    n_in = len(refs) - 3 - 15
    consts, (x1_ref, h2_ref, lg_ref), scratch = refs[:n_in], refs[n_in:n_in + 3], refs[n_in + 3:]
    for half in range(2):
        rows = pl.ds(half * tl, tl)
        _mixer_step(x_ref.at[rows], xc_ref.at[rows], *consts, x1_ref.at[rows], h2_ref.at[rows],
                    lg_ref.at[:, rows], *scratch, tl=tl, nl=nl, slot_a=half, step=2 * pl.program_id(0) + half)


def _mixer_step(x_ref, xc_ref, n1_ref, wz_ref, wxbc_ref, wqk_ref, wv_ref, wo_ref, wg_ref,
                  scw_ref, scb_ref, dtb_ref, alog_ref, dskip_ref, snw_ref,
                  mcw_ref, mcb_ref, gb_ref, mnw_ref, wout_ref, n2_ref, wr_ref,
                  e16_ref, e2_ref, tril_ref, seq_ref, sge_ref, meq_ref, mge_ref,
                  x1_ref, h2_ref, lg_ref,
                  xbc_buf, qk_buf, z_buf, v_buf, o_buf, g_buf, xbc_c, qk_c, mix_buf,
                  xbc_tail, qk_tail, sst, mst, mm, h_buf, *, tl, nl, slot_a, step):
    nchunk = tl // CHUNK
    slot_b = 1 - slot_a
    slot_c = slot_a

    @pl.when(step == 0)
    def _():
        xbc_buf[...] = jnp.zeros_like(xbc_buf)
        qk_buf[...] = jnp.zeros_like(qk_buf)
        z_buf[...] = jnp.zeros_like(z_buf)
        v_buf[...] = jnp.zeros_like(v_buf)
        o_buf[...] = jnp.zeros_like(o_buf)
        g_buf[...] = jnp.zeros_like(g_buf)
        mix_buf[...] = jnp.zeros_like(mix_buf)
        xbc_tail[...] = jnp.zeros_like(xbc_tail)
        qk_tail[...] = jnp.zeros_like(qk_tail)
        sst[...] = jnp.zeros_like(sst)
        mst[...] = jnp.zeros_like(mst)
        mm[...] = jnp.full(mm.shape, STAB_INIT, F32)

    h_buf[...] = _rms(x_ref[...], n1_ref[...]).astype(BF16)
    nblk = 2 * LANES
    tasks = []

    def proj_task(w_ref, store):
        for b0 in range(0, w_ref.shape[1], nblk):
            tasks.append(lambda b0=b0: store(slice(b0, b0 + nblk), _dot(h_buf[...], w_ref[:, b0:b0 + nblk])))

    def out_task(cols):
        x1_ref[:, cols] = xc_ref[:, cols] + _dot(mix_buf[slot_c], wout_ref[:, cols])

    def route_task():
        h2 = _rms(x1_ref[...], n2_ref[...])
        h2 = h2.astype(BF16)
        h2_ref[...] = h2
        lg_ref[...] = _dot_nt(wr_ref[...], h2)

    for b0 in range(0, D_MODEL, nblk):
        tasks.append(lambda b0=b0: out_task(slice(b0, b0 + nblk)))
    tasks.append(route_task)

    def put(buf):
        def store(cols, val):
            buf[slot_a, :, cols] = val.astype(buf.dtype)
        return store

    def put_tiles(buf):
        def store(cols, val):
            for k in range(nblk // LANES):
                buf[slot_a, cols.start // LANES + k, CONV_PAD:CONV_PAD + tl, :] = val[:, k * LANES:(k + 1) * LANES]
        return store

    proj_task(wz_ref, put(z_buf))
    proj_task(wg_ref, put(g_buf))
    proj_task(wxbc_ref, put_tiles(xbc_buf))
    proj_task(wqk_ref, put_tiles(qk_buf))
    proj_task(wv_ref, put(v_buf))
    proj_task(wo_ref, put(o_buf))
    per_chunk = -(-len(tasks) // nchunk)
    assert per_chunk <= 8

    def run_task(c, k):
        idx = c * per_chunk + k
        if idx < len(tasks):
            tasks[idx]()

    seq_start = (step + nl - 1) % nl == 0
    xbc_buf[slot_b, :, 0:CONV_PAD, :] = jnp.where(seq_start, 0.0, xbc_tail[...])
    qk_buf[slot_b, :, 0:CONV_PAD, :] = jnp.where(seq_start, 0.0, qk_tail[...])
    sst[...] = jnp.where(seq_start, 0.0, sst[...])
    mst[...] = jnp.where(seq_start, 0.0, mst[...])
    mm[...] = jnp.where(seq_start, STAB_INIT, mm[...])

    first = CONV_PAD - (SSD_CONV - 1)
    xbc_tail[...] = xbc_buf[slot_b, :, tl:tl + CONV_PAD, :]
    qk_tail[...] = qk_buf[slot_b, :, tl:tl + CONV_PAD, :]

    def conv_silu(buf, w_ref, b_ref, out, r0):
        for t in range(buf.shape[1]):
            cols = slice(t * LANES, (t + 1) * LANES)
            acc = b_ref[:, cols] + w_ref[0:1, cols] * buf[slot_b, t, r0 + first:r0 + first + CHUNK, :]
            for j in range(1, w_ref.shape[0]):
                acc = acc + w_ref[j:j + 1, cols] * buf[slot_b, t, r0 + first + j:r0 + first + j + CHUNK, :]
            out[r0:r0 + CHUNK, cols] = _silu(acc)

    def convs(c):
        run_task(c, 0)
        conv_silu(xbc_buf, scw_ref, scb_ref, xbc_c, c * CHUNK)
        run_task(c, 1)
        conv_silu(qk_buf, mcw_ref, mcb_ref, qk_c, c * CHUNK)
        run_task(c, 2)

    for c in range(nchunk):
        convs(c)
        rows = slice(c * CHUNK, (c + 1) * CHUNK)
        tril3 = tril_ref[...]

        xs = xbc_c[rows, 0:SSD_WIDTH]
        bm = xbc_c[rows, SSD_WIDTH:SSD_WIDTH + 2 * SSD_STATE].astype(BF16)
        cm = xbc_c[rows, SSD_WIDTH + 2 * SSD_STATE:SSD_XBC].astype(BF16)
        dt = _softplus(g_buf[slot_b, rows, 0:SSD_HEADS] + dtb_ref[...])
        a_cs = _dot(tril3, _split3(dt * (-jnp.exp(alog_ref[...])), 0))
        dt_e = _dot(_split3(dt, 1), e16_ref[...])
        a_col = _dot(_split3(a_cs, 1), e16_ref[...])
        a_row = jnp.sum(a_col * seq_ref[...], axis=0, keepdims=True)
        lmat = jnp.exp(a_col - a_row + sge_ref[...])
        cb = jnp.concatenate(
            [_dot_nt(cm[:, g * SSD_STATE:(g + 1) * SSD_STATE],
                     jnp.concatenate([bm[:, g * SSD_STATE:(g + 1) * SSD_STATE]] * (SSD_HEADS // SSD_GROUPS), axis=0))
             for g in range(SSD_GROUPS)], axis=1)
        m_all = (cb * lmat).astype(BF16)
        run_task(c, 3)
        xdt = xs * dt_e
        lane = lax.broadcasted_iota(jnp.int32, (CHUNK, LANES), 1)
        y_parts = []
        for j in range(SSD_HEADS // 2):
            xp = xdt[:, j * LANES:(j + 1) * LANES]
            xbd = jnp.concatenate([jnp.where(lane < SSD_HEAD_DIM, xp, 0.0),
                                   jnp.where(lane >= SSD_HEAD_DIM, xp, 0.0)], axis=0).astype(BF16)
            y_parts.append(_dot(m_all[:, j * LANES:(j + 1) * LANES], xbd))
        y = jnp.concatenate(y_parts, axis=1)
        half = SSD_WIDTH // SSD_GROUPS
        y_int = jnp.concatenate(
            [_dot(cm[:, g * SSD_STATE:(g + 1) * SSD_STATE], sst[g].astype(BF16)) for g in range(SSD_GROUPS)], axis=1)
        y = y + y_int * jnp.exp(a_col)
        a_last = a_col[CHUNK - 1:CHUNK, :]
        xd_b = (xdt * jnp.exp(a_last - a_col)).astype(BF16)
        st_scale = jnp.exp(a_last)
        for g in range(SSD_GROUPS):
            sst[g] = sst[g] * st_scale[:, g * half:(g + 1) * half] + _dot_tn(
                bm[:, g * SSD_STATE:(g + 1) * SSD_STATE], xd_b[:, g * half:(g + 1) * half])
        y = (y + dskip_ref[...] * xs) * _silu(z_buf[slot_b, rows, :])
        for g in range(SSD_GROUPS):
            yg = y[:, g * half:(g + 1) * half]
            yg = yg * lax.rsqrt(jnp.mean(yg * yg, axis=-1, keepdims=True) + EPS) * snw_ref[:, g * half:(g + 1) * half]
            mix_buf[slot_b, rows, g * half:(g + 1) * half] = yg.astype(BF16)

        run_task(c, 4)

        gi = g_buf[slot_b, rows, LANES:LANES + 2 * ML_HEADS] + gb_ref[...]
        col16 = lax.broadcasted_iota(jnp.int32, (CHUNK, 2 * ML_HEADS), 1)
        gi = jnp.where(col16 < ML_HEADS, gi, jnp.minimum(gi, 0.0) - jnp.log1p(jnp.exp(-jnp.abs(gi))))
        gi = jnp.where(col16 < ML_HEADS, gi, _dot(tril3, _split3(gi, 0)))
        gi_e = _dot(_split3(gi, 1), e2_ref[...])
        ig_e = gi_e[:, 0:ML_WIDTH]
        b_col = gi_e[:, ML_WIDTH:2 * ML_WIDTH]
        c_row = jnp.sum((ig_e - b_col) * meq_ref[...], axis=0, keepdims=True)
        dmat = b_col + c_row + mge_ref[...]
        heads = range(ML_HEADS)
        hs = lambda arr, hd: arr[:, hd * LANES:(hd + 1) * LANES]
        ones_blk = jnp.ones((CHUNK, LANES), BF16)
        m_prev = mm[...]
        m_inter = b_col + m_prev
        q = [qk_c[rows, hd * LANES:(hd + 1) * LANES].astype(BF16) for hd in heads]
        kf = [qk_c[rows, ML_WIDTH + hd * LANES:ML_WIDTH + (hd + 1) * LANES] * (ML_HEAD_DIM ** -0.5) for hd in heads]
        v_aug = [jnp.concatenate([v_buf[slot_b, rows, hd * LANES:(hd + 1) * LANES], ones_blk], axis=1)
                 for hd in heads]
        s = [_dot_nt(q[hd], kf[hd].astype(BF16)) for hd in heads]
        qc = [_dot(q[hd], mst[hd].astype(BF16)) for hd in heads]
        run_task(c, 5)
        m_t = jnp.maximum(m_inter, jnp.concatenate(
            [jnp.broadcast_to(jnp.max(hs(dmat, hd), axis=-1, keepdims=True), (CHUNK, LANES)) for hd in heads], axis=1))
        w = jnp.exp(dmat - m_t)
        inter = jnp.exp(m_inter - m_t)
        den_floor = jnp.exp(-m_t)
        nd = [_dot((s[hd] * hs(w, hd)[:, 0:CHUNK]).astype(BF16), v_aug[hd])
              + qc[hd] * jnp.concatenate([hs(inter, hd)] * 2, axis=1) for hd in heads]
        run_task(c, 6)
        hh = [nd[hd][:, 0:LANES] / jnp.maximum(jnp.abs(nd[hd][:, LANES:2 * LANES]), hs(den_floor, hd)) for hd in heads]
        hc = [hh[hd] - jnp.mean(hh[hd], axis=-1, keepdims=True) for hd in heads]
        var = [jnp.mean(hc[hd] * hc[hd], axis=-1, keepdims=True) for hd in heads]
        for hd in heads:
            hn = hc[hd] * lax.rsqrt(var[hd] + EPS) * mnw_ref[:, hd * LANES:(hd + 1) * LANES]
            mix_buf[slot_b, rows, SSD_WIDTH + hd * LANES:SSD_WIDTH + (hd + 1) * LANES] = (
                jax.nn.sigmoid(o_buf[slot_b, rows, hd * LANES:(hd + 1) * LANES]) * hn).astype(BF16)
        run_task(c, 7)
        g_row = b_col[CHUNK - 1:CHUNK, :]
        a = g_row - b_col + ig_e
        m_new = jnp.maximum(g_row + m_prev, jnp.max(a, axis=0, keepdims=True))
        wk = jnp.exp(a - m_new)
        cs = jnp.exp(g_row + m_prev - m_new)
        for hd in heads:
            mst[hd] = mst[hd] * jnp.concatenate([hs(cs, hd)] * 2, axis=1) + _dot_tn(
                (kf[hd] * hs(wk, hd)).astype(BF16), v_aug[hd])
        mm[...] = m_new


def _const_spec(shape):
    nd = len(shape)
    return pl.BlockSpec(shape, lambda s: (0,) * nd, pipeline_mode=pl.Buffered(1))


def _mixer_call(x, consts, tl):
    bsz, seq, _ = x.shape
    nl = seq // tl
    n_pairs = bsz * nl // 2
    x_pairs = x.reshape(n_pairs, 2 * tl, D_MODEL)
    pair_in = lambda i: (jnp.minimum(i, n_pairs - 1), 0, 0)
    pair_out = lambda i: (jnp.maximum(i - 1, 0), 0, 0)
    in_specs = ([pl.BlockSpec((None, 2 * tl, D_MODEL), pair_in), pl.BlockSpec((None, 2 * tl, D_MODEL), pair_out)]
                + [_const_spec(c.shape) for c in consts])
    out_specs = [pl.BlockSpec((None, 2 * tl, D_MODEL), pair_out), pl.BlockSpec((None, 2 * tl, D_MODEL), pair_out),
                 pl.BlockSpec((LOGIT_ROWS, 2 * tl), lambda i: (0, jnp.maximum(i - 1, 0)))]
    out_shape = [jax.ShapeDtypeStruct((n_pairs, 2 * tl, D_MODEL), F32),
                 jax.ShapeDtypeStruct((n_pairs, 2 * tl, D_MODEL), BF16),
                 jax.ShapeDtypeStruct((LOGIT_ROWS, bsz * seq), F32)]
    scratch = [
        pltpu.VMEM((2, SSD_XBC // LANES, tl + CONV_PAD, LANES), F32),
        pltpu.VMEM((2, 2 * ML_WIDTH // LANES, tl + CONV_PAD, LANES), F32),
        pltpu.VMEM((2, tl, SSD_WIDTH), F32),
        pltpu.VMEM((2, tl, ML_WIDTH), BF16),
        pltpu.VMEM((2, tl, ML_WIDTH), F32),
        pltpu.VMEM((2, tl, GATE_COLS), F32),
        pltpu.VMEM((tl, SSD_XBC), F32),
        pltpu.VMEM((tl, 2 * ML_WIDTH), F32),
        pltpu.VMEM((2, tl, D_MIX), BF16),
        pltpu.VMEM((SSD_XBC // LANES, CONV_PAD, LANES), F32),
        pltpu.VMEM((2 * ML_WIDTH // LANES, CONV_PAD, LANES), F32),
        pltpu.VMEM((SSD_GROUPS, SSD_STATE, SSD_WIDTH // SSD_GROUPS), F32),
        pltpu.VMEM((ML_HEADS, ML_HEAD_DIM, 2 * LANES), F32),
        pltpu.VMEM((1, ML_WIDTH), F32),
        pltpu.VMEM((tl, D_MODEL), BF16),
    ]
    return pl.pallas_call(
        functools.partial(_mixer_kernel, tl=tl, nl=nl),
        grid=(n_pairs + 1,),
        in_specs=in_specs,
        out_specs=out_specs,
        out_shape=out_shape,
        scratch_shapes=scratch,
        compiler_params=pltpu.CompilerParams(
            dimension_semantics=("arbitrary",), vmem_limit_bytes=V7X_VMEM_LIMIT),
        name="mixer",
    )(x_pairs, x_pairs, *consts)


def _mixer_consts(norm1_w, w_in, ssd_conv_w, ssd_conv_b, ssd_dt_bias, ssd_a_log, ssd_d, ssd_norm_w,
                  ml_conv_w, ml_conv_b, ml_i_bias, ml_f_bias, ml_norm_w, w_out, norm2_w,
                  router_g_w, router_e_w):
    o1 = SSD_WIDTH
    o2 = o1 + SSD_XBC
    o3 = o2 + SSD_HEADS
    o4 = o3 + 2 * ML_WIDTH
    o5 = o4 + ML_WIDTH
    o6 = o5 + ML_WIDTH
    o7 = o6 + ML_HEADS
    zpad = lambda n: jnp.zeros((D_MODEL, n), F32)
    w_gate = jnp.concatenate([w_in[:, o2:o3], zpad(LANES - SSD_HEADS),
                              w_in[:, o6:o7], w_in[:, o7:], zpad(LANES - 2 * ML_HEADS)], axis=1)
    row = lambda v: v.reshape(1, -1).astype(F32)
    w_route = jnp.concatenate([router_e_w, router_g_w,
                               jnp.zeros((D_MODEL, LOGIT_ROWS - N_EXPERTS - MOE_GROUPS), F32)], axis=1).T

    c1024 = np.arange(SSD_WIDTH)
    r64 = np.arange(CHUNK)[:, None]
    e16 = np.tile(c1024[None, :] // SSD_HEAD_DIM == np.arange(SSD_HEADS)[:, None], (3, 1))
    e2 = np.tile(np.arange(2 * ML_WIDTH)[None, :] // LANES == np.arange(2 * ML_HEADS)[:, None], (3, 1))
    tril = np.tile(np.arange(CHUNK)[None, :] <= r64, (1, 3))
    s_pos = (c1024 % SSD_HEAD_DIM)[None, :]
    m_pos = (c1024 % LANES)[None, :]
    as_bf16 = lambda m: jnp.asarray(m.astype(np.float32), BF16)
    as_f32 = lambda m: jnp.asarray(m.astype(np.float32))
    e16, e2, tril = as_bf16(e16), as_bf16(e2), as_bf16(tril)
    causal = lambda keep: jnp.asarray(np.where(keep, 0.0, -np.inf).astype(np.float32))
    seq, sge, meq, mge = as_f32(r64 == s_pos), causal(r64 >= s_pos), as_f32(r64 == m_pos), causal(r64 >= m_pos)
    return [
        row(norm1_w),
        w_in[:, :o1].astype(BF16), w_in[:, o1:o2].astype(BF16), w_in[:, o3:o4].astype(BF16),
        w_in[:, o4:o5].astype(BF16), w_in[:, o5:o6].astype(BF16), w_gate.astype(BF16),
        ssd_conv_w.astype(F32), row(ssd_conv_b), row(ssd_dt_bias),
        row(ssd_a_log), row(jnp.repeat(ssd_d, SSD_HEAD_DIM)), row(ssd_norm_w),
        ml_conv_w.astype(F32), row(ml_conv_b), row(jnp.concatenate([ml_i_bias, ml_f_bias])), row(ml_norm_w),
        w_out.astype(BF16), row(norm2_w), w_route.astype(BF16),
        e16, e2, tril, seq, sge, meq, mge,
    ]


def _route_kernel(lg_ref, bias_ref, striu_ref, lpos_ref, gate_ref, cnt_ref):
    tr = striu_ref.shape[0]
    for half in range(lg_ref.shape[1] // tr):
        _route_tile(lg_ref[:, half * tr:(half + 1) * tr] + bias_ref[...], striu_ref,
                    lpos_ref.at[:, pl.ds(half * tr, tr)], gate_ref.at[:, pl.ds(half * tr, tr)], cnt_ref.at[half])


def _route_tile(lg, striu_ref, lpos_ref, gate_ref, cnt_ref):
    tr = lg.shape[1]
    el = lg[0:N_EXPERTS]
    gl = lg[N_EXPERTS:N_EXPERTS + 8]
    gmax = jnp.max(gl, axis=0, keepdims=True)
    pg = 1.0 / jnp.sum(jnp.exp(gl - gmax), axis=0, keepdims=True)
    grow = lax.broadcasted_iota(jnp.int32, gl.shape, 0)
    gsel = jnp.min(jnp.where(gl == gmax, grow, 8), axis=0, keepdims=True)
    erow = lax.broadcasted_iota(jnp.int32, el.shape, 0)
    m1 = jnp.where((erow >> 3) == gsel, el, NEG_INF)
    v1 = jnp.max(m1, axis=0, keepdims=True)
    i1 = jnp.min(jnp.where(m1 == v1, erow, N_EXPERTS), axis=0, keepdims=True)
    m2 = jnp.where(erow == i1, NEG_INF, m1)
    v2 = jnp.max(m2, axis=0, keepdims=True)
    i2 = jnp.min(jnp.where(m2 == v2, erow, N_EXPERTS), axis=0, keepdims=True)
    e2 = jnp.exp(v2 - v1)
    g1 = pg / (1.0 + e2)
    g2 = g1 * e2
    hit1 = erow == i1
    hit2 = erow == i2
    oh = jnp.where(hit1, 1.0, jnp.where(hit2, 1.0, 0.0))
    earlier = _dot(oh.astype(BF16), striu_ref[...])
    cnt = jnp.sum(oh, axis=1, keepdims=True)
    seg = jnp.floor((cnt + (ROW_ALIGN - 1)) * (1.0 / ROW_ALIGN)) * ROW_ALIGN
    l1 = jnp.sum(jnp.where(hit1, earlier, jnp.where(erow < i1, seg, 0.0)), axis=0, keepdims=True)
    l2 = jnp.sum(jnp.where(hit2, earlier, jnp.where(erow < i2, seg, 0.0)), axis=0, keepdims=True)
    lpos_ref[...] = jnp.concatenate([l1.astype(jnp.int32), l2.astype(jnp.int32), jnp.zeros((6, tr), jnp.int32)], axis=0)
    gate_ref[...] = jnp.concatenate([g1, g2, jnp.zeros((6, tr), F32)], axis=0)
    cnt_ref[...] = jnp.broadcast_to(cnt, cnt_ref.shape)


def _route_call(lg_t, bias_col, tr):
    t = lg_t.shape[1]
    striu = jnp.asarray(np.triu(np.ones((tr, tr), np.float32), 1), BF16)
    return pl.pallas_call(
        _route_kernel,
        grid=(t // (ROUTE_TILES * tr),),
        in_specs=[pl.BlockSpec((LOGIT_ROWS, ROUTE_TILES * tr), lambda i: (0, i)),
                  pl.BlockSpec((LOGIT_ROWS, 1), lambda i: (0, 0)),
                  pl.BlockSpec((tr, tr), lambda i: (0, 0))],
        out_specs=[pl.BlockSpec((8, ROUTE_TILES * tr), lambda i: (0, i)),
                   pl.BlockSpec((8, ROUTE_TILES * tr), lambda i: (0, i)),
                   pl.BlockSpec((ROUTE_TILES, N_EXPERTS, LANES), lambda i: (i, 0, 0))],
        out_shape=[jax.ShapeDtypeStruct((8, t), jnp.int32),
                   jax.ShapeDtypeStruct((8, t), F32),
                   jax.ShapeDtypeStruct((t // tr, N_EXPERTS, LANES), F32)],
        compiler_params=pltpu.CompilerParams(dimension_semantics=("arbitrary",)),
        name="route",
    )(lg_t, bias_col, striu)


def _run_copies(runs_ref, copy_run, max_piece, filler_far, enable=None):
    for e in range(N_EXPERTS + 1):
        dst = runs_ref[0, e] if e < N_EXPERTS else filler_far
        src = runs_ref[1, e]
        n = runs_ref[2, e] if enable is None else jnp.where(enable, runs_ref[2, e], 0)
        p = max_piece
        level = 0
        while p >= ROW_ALIGN:
            done = n & ~(2 * p - 1)

            @pl.when((n & p) != 0)
            def _(p=p, done=done, dst=dst, src=src, prio=(e + level) % 2):
                copy_run(pl.multiple_of(dst + done, ROW_ALIGN), pl.multiple_of(src + done, ROW_ALIGN), p, prio)
            p //= 2
            level += 1


def _dispatch_kernel(runs_ref, prev_runs_ref, lpos_ref, h2_ref, xs_ref, loc, sem, *, max_spare):
    n_tiles = runs_ref.shape[0]
    tr = h2_ref.shape[0] // n_tiles
    nloc = loc.shape[1]
    step = pl.program_id(0)
    last = pl.num_programs(0) - 1
    used = runs_ref[0, 0, N_EXPERTS]

    def wait_block(s):
        pltpu.make_async_copy(loc.at[s], xs_ref.at[pl.ds(0, nloc)], sem.at[s]).wait()

    def send_block(table, s, enable=None):
        def copy_run(dst, src, n, prio):
            pltpu.make_async_copy(loc.at[s, pl.ds(src, n)], xs_ref.at[pl.ds(dst, n)], sem.at[s]).start(priority=prio)
        _run_copies(table, copy_run, tr, used + s * (nloc - TOP_K * tr), enable)

    def sort_tile(k, s):
        cols = slice(k * tr, (k + 1) * tr)
        jrow = lax.broadcasted_iota(jnp.int32, (nloc, tr), 0)
        perm = jnp.where(jrow == lpos_ref[0:1, cols], 1.0, jnp.where(jrow == lpos_ref[1:2, cols], 1.0, 0.0)).astype(BF16)
        loc[s] = _pack_rows(_dot(perm, h2_ref[cols, :]), rounded=True)

    for k in range(n_tiles):
        s = k % 2
        if k < 2:
            pl.when(step > 0)(functools.partial(wait_block, s))
        else:
            wait_block(s)
        if k == 0:
            send_block(prev_runs_ref.at[n_tiles - 1], 1, enable=step > 0)
        else:
            send_block(runs_ref.at[k - 1], 1 - s)
        sort_tile(k, s)

    @pl.when(step == last)
    def _():
        send_block(runs_ref.at[n_tiles - 1], 1)
        wait_block(0)
        wait_block(1)
        loc[0, 0:tr, :] = jnp.zeros((tr, loc.shape[2]), loc.dtype)
        spare = xs_ref.shape[0] - used

        def zero_copy(off, n):
            return pltpu.make_async_copy(loc.at[0, pl.ds(0, n)],
                                         xs_ref.at[pl.ds(pl.multiple_of(used + off, ROW_ALIGN), n)], sem.at[0])

        def pieces(do):
            for i in range(max_spare // tr):
                @pl.when((i + 1) * tr <= spare)
                def _(i=i):
                    do(zero_copy(i * tr, tr))
            p = tr // 2
            while p >= ROW_ALIGN:
                @pl.when((spare & p) != 0)
                def _(p=p):
                    do(zero_copy(spare & ~(2 * p - 1), p))
                p //= 2

        pieces(lambda c: c.start())
        pieces(lambda c: c.wait())


def _dispatch_call(runs, lpos, h2, tr, n_rows):
    t = h2.shape[0]
    k = DISPATCH_TILES
    groups = runs.reshape(t // (k * tr), k, 3, N_EXPERTS + 1)
    return pl.pallas_call(
        functools.partial(_dispatch_kernel, max_spare=n_rows - t * TOP_K),
        grid=(t // (k * tr),),
        in_specs=[pl.BlockSpec((None, k, 3, N_EXPERTS + 1), lambda i: (i, 0, 0, 0), memory_space=pltpu.SMEM),
                  pl.BlockSpec((None, k, 3, N_EXPERTS + 1), lambda i: (jnp.maximum(i - 1, 0), 0, 0, 0),
                               memory_space=pltpu.SMEM),
                  pl.BlockSpec((8, k * tr), lambda i: (0, i)),
                  pl.BlockSpec((k * tr, D_MODEL), lambda i: (i, 0))],
        out_specs=pl.BlockSpec(memory_space=pl.ANY),
        out_shape=jax.ShapeDtypeStruct((n_rows, D_MODEL // 2), jnp.uint32),
        scratch_shapes=[pltpu.VMEM((2, _local_rows(tr), D_MODEL // 2), jnp.uint32), pltpu.SemaphoreType.DMA((2,))],
        compiler_params=pltpu.CompilerParams(dimension_semantics=("arbitrary",), vmem_limit_bytes=V7X_VMEM_LIMIT),
        name="dispatch",
    )(groups, groups, lpos, h2)


def _experts_kernel(tile_ref, exp_ref, starts_ref, xs_ref, wg_ref, wu_ref, wd_ref, ys_ref, wg_b, wu_b, wd_b):
    i = pl.program_id(0)
    tm = xs_ref.shape[0]
    e = exp_ref[i]
    tile = tile_ref[i]
    is_first = jnp.logical_or(i == 0, tile_ref[jnp.maximum(i - 1, 0)] != tile)

    @pl.when(jnp.logical_and(e < N_EXPERTS, jnp.logical_or(i == 0, exp_ref[jnp.maximum(i - 1, 0)] != e)))
    def _():
        wg_b[...] = wg_ref[...].astype(BF16)
        wu_b[...] = wu_ref[...].astype(BF16)
        wd_b[...] = wd_ref[...].astype(BF16)

    def expert_rows(other):
        x = _unpack_rows(xs_ref[...])
        act = (_silu(_dot(x, wg_b[...])) * _dot(x, wu_b[...])).astype(BF16)
        y = _pack_rows(_dot(act, wd_b[...]), rounded=False)
        rows = tile * tm + lax.broadcasted_iota(jnp.int32, (tm, 1), 0)
        return jnp.where((rows >= starts_ref[e]) & (rows < starts_ref[e + 1]), y, other)

    @pl.when(jnp.logical_and(e < N_EXPERTS, is_first))
    def _():
        ys_ref[...] = expert_rows(jnp.zeros_like(ys_ref))

    @pl.when(jnp.logical_and(e < N_EXPERTS, jnp.logical_not(is_first)))
    def _():
        ys_ref[...] = expert_rows(ys_ref[...])

    @pl.when(e == N_EXPERTS)
    def _():
        ys_ref[...] = jnp.zeros_like(ys_ref)


def _experts_call(item_tile, item_exp, starts_ext, xs, wg, wu, wd, tm):
    n = xs.shape[0]
    n_items = item_tile.shape[0]
    w_idx = lambda i, tile, ex, st: (jnp.minimum(ex[i], N_EXPERTS - 1), 0, 0)
    grid_spec = pltpu.PrefetchScalarGridSpec(
        num_scalar_prefetch=3,
        grid=(n_items,),
        in_specs=[pl.BlockSpec((tm, D_MODEL // 2), lambda i, tile, ex, st: (tile[i], 0)),
                  pl.BlockSpec((None, D_MODEL, D_EXPERT), w_idx),
                  pl.BlockSpec((None, D_MODEL, D_EXPERT), w_idx),
                  pl.BlockSpec((None, D_EXPERT, D_MODEL), w_idx)],
        out_specs=pl.BlockSpec((tm, D_MODEL // 2), lambda i, tile, ex, st: (tile[i], 0)),
        scratch_shapes=[pltpu.VMEM((D_MODEL, D_EXPERT), BF16), pltpu.VMEM((D_MODEL, D_EXPERT), BF16),
                        pltpu.VMEM((D_EXPERT, D_MODEL), BF16)],
    )
    return pl.pallas_call(
        _experts_kernel,
        grid_spec=grid_spec,
        out_shape=jax.ShapeDtypeStruct((n, D_MODEL // 2), jnp.uint32),
        compiler_params=pltpu.CompilerParams(
            dimension_semantics=("arbitrary",), vmem_limit_bytes=V7X_VMEM_LIMIT),
        name="experts",
    )(item_tile, item_exp, starts_ext, xs, wg, wu, wd)


def _local_rows(tr):
    return TOP_K * tr + N_EXPERTS * ROW_ALIGN


def _expert_items(counts, n_rows, tm):
    n_tiles = n_rows // tm
    n_items = n_tiles + N_EXPERTS - 1
    starts = jnp.concatenate([jnp.zeros((1,), jnp.int32), jnp.cumsum(counts).astype(jnp.int32)])
    lo = starts[:-1] // tm
    hi = (starts[1:] - 1) // tm
    per = jnp.where(counts > 0, hi - lo + 1, 0)
    cum = jnp.cumsum(per)
    idx = jnp.arange(n_items, dtype=jnp.int32)
    e = jnp.sum(idx[:, None] >= cum[None, :], axis=1).astype(jnp.int32)
    valid = idx < cum[-1]
    e_c = jnp.minimum(e, N_EXPERTS - 1)
    first_item = jnp.sum(jnp.where(e_c[:, None] == jnp.arange(N_EXPERTS)[None, :], (lo - (cum - per))[None, :], 0), axis=1)
    tile = first_item + idx
    spare_tile = -(-starts[-1] // tm) + idx - cum[-1]
    item_tile = jnp.where(valid, tile, jnp.minimum(spare_tile, n_tiles - 1)).astype(jnp.int32)
    item_exp = jnp.where(valid, e_c, jnp.where(spare_tile < n_tiles, N_EXPERTS, N_EXPERTS + 1)).astype(jnp.int32)
    starts_ext = jnp.concatenate([starts, starts[-1:], starts[-1:]])
    return item_tile, item_exp, starts, starts_ext


def _by_parity(body):
    def kernel(*refs, **kw):
        for parity in range(2):
            @pl.when(pl.program_id(0) % 2 == parity)
            def _(parity=parity):
                body(*refs, slot=parity, **kw)
    return kernel


def _combine_step(runs_ref, next_runs_ref, lpos_ref, gate_ref, next_lpos_ref, next_gate_ref, x1_ref, nf_ref, ys_ref,
                  out_ref, loc, sel_buf, sem, *, slot):
    tr = x1_ref.shape[0]
    nloc = loc.shape[1]
    step = pl.program_id(0)

    def gather(table, s, enable=None):
        def copy_run(dst, src, n, prio):
            pltpu.make_async_copy(ys_ref.at[pl.ds(dst, n)], loc.at[s, pl.ds(src, n)], sem.at[s]).start(priority=prio)
        _run_copies(table, copy_run, tr, 0, enable)

    def build_sel(pos_ref, g_ref, s):
        jrow = lax.broadcasted_iota(jnp.int32, (nloc, tr), 0)
        sel_buf[s] = jnp.where(jrow == pos_ref[0:1, :], g_ref[0:1, :],
                               jnp.where(jrow == pos_ref[1:2, :], g_ref[1:2, :], 0.0)).astype(BF16)

    @pl.when(step == 0)
    def _():
        gather(runs_ref, slot)
        build_sel(lpos_ref, gate_ref, slot)

    pltpu.make_async_copy(ys_ref.at[pl.ds(0, nloc)], loc.at[slot], sem.at[slot]).wait()

    gather(next_runs_ref, 1 - slot, enable=step + 1 < pl.num_programs(0))
    y = _dot_tn(sel_buf[slot], _unpack_rows(loc[slot]))
    build_sel(next_lpos_ref, next_gate_ref, 1 - slot)
    out_ref[...] = _rms(x1_ref[...] + y, nf_ref[...])


def _combine_call(runs, lpos, gates, x1, nf_w, ys, tr):
    t = x1.shape[0]
    return pl.pallas_call(
        _by_parity(_combine_step),
        grid=(t // tr,),
        in_specs=[pl.BlockSpec((None, 3, N_EXPERTS + 1), lambda i: (i, 0, 0), memory_space=pltpu.SMEM),
                  pl.BlockSpec((None, 3, N_EXPERTS + 1), lambda i: (jnp.minimum(i + 1, t // tr - 1), 0, 0),
                               memory_space=pltpu.SMEM),
                  pl.BlockSpec((8, tr), lambda i: (0, i)),
                  pl.BlockSpec((8, tr), lambda i: (0, i)),
                  pl.BlockSpec((8, tr), lambda i: (0, jnp.minimum(i + 1, t // tr - 1))),
                  pl.BlockSpec((8, tr), lambda i: (0, jnp.minimum(i + 1, t // tr - 1))),
                  pl.BlockSpec((tr, D_MODEL), lambda i: (i, 0)),
                  pl.BlockSpec((1, D_MODEL), lambda i: (0, 0)),
                  pl.BlockSpec(memory_space=pl.ANY)],
        out_specs=pl.BlockSpec((tr, D_MODEL), lambda i: (i, 0)),
        out_shape=jax.ShapeDtypeStruct((t, D_MODEL), F32),
        scratch_shapes=[pltpu.VMEM((2, _local_rows(tr), D_MODEL // 2), jnp.uint32),
                        pltpu.VMEM((2, _local_rows(tr), tr), BF16), pltpu.SemaphoreType.DMA((2,))],
        compiler_params=pltpu.CompilerParams(dimension_semantics=("arbitrary",), vmem_limit_bytes=V7X_VMEM_LIMIT),
        name="combine",
    )(runs, runs, lpos, gates, lpos, gates, x1, nf_w, ys)


SEQ_TILE = 4 * CHUNK
TOKEN_TILE = 512
ROW_TILE = 512
ROUTE_TILES = 4
DISPATCH_TILES = 4


def kernel(x, norm1_w, w_in, ssd_conv_w, ssd_conv_b, ssd_dt_bias, ssd_a_log, ssd_d, ssd_norm_w, ml_conv_w, ml_conv_b, ml_i_bias, ml_f_bias, ml_norm_w, w_out, norm2_w, router_g_w, router_g_b, router_e_w, router_e_b, exp_w_gate, exp_w_up, exp_w_down, norm_f_w):
    bsz, seq, d = x.shape
    t = bsz * seq
    tl, tr, tm = SEQ_TILE, TOKEN_TILE, ROW_TILE
    assert d == D_MODEL and norm1_w.shape[0] == 1 and seq % tl == 0 and t % (2 * tl) == 0 and t % (max(ROUTE_TILES, DISPATCH_TILES) * tr) == 0

    consts = _mixer_consts(norm1_w[0], w_in[0], ssd_conv_w[0], ssd_conv_b[0], ssd_dt_bias[0], ssd_a_log[0],
                           ssd_d[0], ssd_norm_w[0], ml_conv_w[0], ml_conv_b[0], ml_i_bias[0], ml_f_bias[0],
                           ml_norm_w[0], w_out[0], norm2_w[0], router_g_w[0], router_e_w[0])
    x1, h2, lg_t = _mixer_call(x, consts, tl)
    x1 = x1.reshape(t, D_MODEL)
    h2 = h2.reshape(t, D_MODEL)

    bias_col = jnp.concatenate([router_e_b[0], router_g_b[0],
                                jnp.full((LOGIT_ROWS - N_EXPERTS - MOE_GROUPS,), STAB_INIT, F32)]).reshape(-1, 1)
    lpos, gates, cnt = _route_call(lg_t, bias_col.astype(F32), tr)

    seg = (cnt[:, :, 0].astype(jnp.int32) + (ROW_ALIGN - 1)) // ROW_ALIGN * ROW_ALIGN
    n_tok_tiles = t // tr
    n_rows = -(-(t * TOP_K + n_tok_tiles * N_EXPERTS * ROW_ALIGN + _local_rows(tr)) // tm) * tm
    item_tile, item_exp, starts, starts_ext = _expert_items(jnp.sum(seg, axis=0), n_rows, tm)
    run_dst = starts[None, :N_EXPERTS] + jnp.cumsum(seg, axis=0) - seg
    run_src = jnp.cumsum(seg, axis=1) - seg
    used = jnp.sum(seg, axis=1, keepdims=True)
    fill = jnp.concatenate([jnp.full_like(used, starts[N_EXPERTS]), used, _local_rows(tr) - used], axis=1)[:, :, None]
    runs = jnp.concatenate([jnp.stack([run_dst, run_src, seg], axis=1), fill], axis=2).astype(jnp.int32)

    xs = _dispatch_call(runs, lpos, h2, tr, n_rows)
    ys = _experts_call(item_tile, item_exp, starts_ext, xs, exp_w_gate[0], exp_w_up[0], exp_w_down[0], tm)
    out = _combine_call(runs, lpos, gates, x1, norm_f_w.reshape(1, -1).astype(F32), ys, tr)
    return out.reshape(bsz, seq, D_MODEL)
```

```python
import functools

import jax
import jax.numpy as jnp
import numpy as np
from jax import lax
from jax.experimental import pallas as pl
from jax.experimental.pallas import tpu as pltpu

D_MODEL = 1024
CHUNK = 64
SSD_WIDTH = 1024
SSD_HEAD_DIM = 64
SSD_HEADS = 16
SSD_GROUPS = 2
SSD_STATE = 128
SSD_CONV = 4
SSD_XBC = SSD_WIDTH + 2 * SSD_GROUPS * SSD_STATE
ML_WIDTH = 1024
ML_HEADS = 8
ML_HEAD_DIM = 128
ML_CONV = 4
D_MIX = SSD_WIDTH + ML_WIDTH
MOE_GROUPS = 4
EXPERTS_PER_GROUP = 8
N_EXPERTS = 32
TOP_K = 2
D_EXPERT = 512
EPS = 1e-6
STAB_INIT = -1e30

LANES = 128
GATE_COLS = 2 * LANES
ROW_ALIGN = 8
CONV_PAD = 8
LOGIT_ROWS = 64
V7X_VMEM_LIMIT = 60 * 1024 * 1024

F32 = jnp.float32
BF16 = jnp.bfloat16
NEG_INF = float("-inf")


def _dot(a, b):
    return jnp.dot(a, b, preferred_element_type=F32)


def _dot_nt(a, b):
    return lax.dot_general(a, b, (((1,), (1,)), ((), ())), preferred_element_type=F32)


def _dot_tn(a, b):
    return lax.dot_general(a, b, (((0,), (0,)), ((), ())), preferred_element_type=F32)


def _split3(x, axis):
    hi = x.astype(BF16)
    r1 = x - hi.astype(F32)
    mid = r1.astype(BF16)
    lo = (r1 - mid.astype(F32)).astype(BF16)
    return jnp.concatenate([hi, mid, lo], axis=axis)


def _pack_rows(x, rounded):
    if not rounded:
        x = x.astype(BF16).astype(F32)
    bits = lax.bitcast_convert_type(x, jnp.uint32)
    half = x.shape[1] // 2
    return bits[:, half:] | (bits[:, :half] >> 16)


def _unpack_rows(p):
    lo = lax.bitcast_convert_type(p << 16, F32)
    hi = lax.bitcast_convert_type(p & jnp.uint32(0xFFFF0000), F32)
    return jnp.concatenate([lo, hi], axis=1).astype(BF16)


def _silu(x):
    return x * jax.nn.sigmoid(x)


def _softplus(x):
    return jnp.maximum(x, 0.0) + jnp.log1p(jnp.exp(-jnp.abs(x)))


def _rms(x, w):
    return x * lax.rsqrt(jnp.mean(x * x, axis=-1, keepdims=True) + EPS) * w


def _mixer_kernel(x_ref, xc_ref, *refs, tl, nl):
    n_in = len(refs) - 3 - 15
    consts, (x1_ref, h2_ref, lg_ref), scratch = refs[:n_in], refs[n_in:n_in + 3], refs[n_in + 3:]
    for half in range(2):
        rows = pl.ds(half * tl, tl)
        _mixer_step(x_ref.at[rows], xc_ref.at[rows], *consts, x1_ref.at[rows], h2_ref.at[rows],
                    lg_ref.at[:, rows], *scratch, tl=tl, nl=nl, slot_a=half, step=2 * pl.program_id(0) + half)


def _mixer_step(x_ref, xc_ref, n1_ref, wz_ref, wxbc_ref, wqk_ref, wv_ref, wo_ref, wg_ref,
                  scw_ref, scb_ref, dtb_ref, alog_ref, dskip_ref, snw_ref,
                  mcw_ref, mcb_ref, gb_ref, mnw_ref, wout_ref, n2_ref, wr_ref,
                  e16_ref, e2_ref, tril_ref, seq_ref, sge_ref, meq_ref, mge_ref,
                  x1_ref, h2_ref, lg_ref,
                  xbc_buf, qk_buf, z_buf, v_buf, o_buf, g_buf, xbc_c, qk_c, mix_buf,
                  xbc_tail, qk_tail, sst, mst, mm, h_buf, *, tl, nl, slot_a, step):
    nchunk = tl // CHUNK
    slot_b = 1 - slot_a
    slot_c = slot_a

    @pl.when(step == 0)
    def _():
        xbc_buf[...] = jnp.zeros_like(xbc_buf)
        qk_buf[...] = jnp.zeros_like(qk_buf)
        z_buf[...] = jnp.zeros_like(z_buf)
        v_buf[...] = jnp.zeros_like(v_buf)
        o_buf[...] = jnp.zeros_like(o_buf)
        g_buf[...] = jnp.zeros_like(g_buf)
        mix_buf[...] = jnp.zeros_like(mix_buf)
        xbc_tail[...] = jnp.zeros_like(xbc_tail)
        qk_tail[...] = jnp.zeros_like(qk_tail)
        sst[...] = jnp.zeros_like(sst)
        mst[...] = jnp.zeros_like(mst)
        mm[...] = jnp.full(mm.shape, STAB_INIT, F32)

    h_buf[...] = _rms(x_ref[...], n1_ref[...]).astype(BF16)
    nblk = 2 * LANES
    tasks = []

    def proj_task(w_ref, store):
        for b0 in range(0, w_ref.shape[1], nblk):
            tasks.append(lambda b0=b0: store(slice(b0, b0 + nblk), _dot(h_buf[...], w_ref[:, b0:b0 + nblk])))

    def out_task(cols):
        x1_ref[:, cols] = xc_ref[:, cols] + _dot(mix_buf[slot_c], wout_ref[:, cols])

    def route_task():
        h2 = _rms(x1_ref[...], n2_ref[...])
        h2 = h2.astype(BF16)
        h2_ref[...] = h2
        lg_ref[...] = _dot_nt(wr_ref[...], h2)

    for b0 in range(0, D_MODEL, nblk):
        tasks.append(lambda b0=b0: out_task(slice(b0, b0 + nblk)))
    tasks.append(route_task)

    def put(buf):
        def store(cols, val):
            buf[slot_a, :, cols] = val.astype(buf.dtype)
        return store

    def put_tiles(buf):
        def store(cols, val):
            for k in range(nblk // LANES):
                buf[slot_a, cols.start // LANES + k, CONV_PAD:CONV_PAD + tl, :] = val[:, k * LANES:(k + 1) * LANES]
        return store

    proj_task(wz_ref, put(z_buf))
    proj_task(wg_ref, put(g_buf))
    proj_task(wxbc_ref, put_tiles(xbc_buf))
    proj_task(wqk_ref, put_tiles(qk_buf))
    proj_task(wv_ref, put(v_buf))
    proj_task(wo_ref, put(o_buf))
    per_chunk = -(-len(tasks) // nchunk)
    assert per_chunk <= 8

    def run_task(c, k):
        idx = c * per_chunk + k
        if idx < len(tasks):
            tasks[idx]()

    seq_start = (step + nl - 1) % nl == 0
    xbc_buf[slot_b, :, 0:CONV_PAD, :] = jnp.where(seq_start, 0.0, xbc_tail[...])
    qk_buf[slot_b, :, 0:CONV_PAD, :] = jnp.where(seq_start, 0.0, qk_tail[...])
    sst[...] = jnp.where(seq_start, 0.0, sst[...])
    mst[...] = jnp.where(seq_start, 0.0, mst[...])
    mm[...] = jnp.where(seq_start, STAB_INIT, mm[...])

    first = CONV_PAD - (SSD_CONV - 1)
    xbc_tail[...] = xbc_buf[slot_b, :, tl:tl + CONV_PAD, :]
    qk_tail[...] = qk_buf[slot_b, :, tl:tl + CONV_PAD, :]

    def conv_silu(buf, w_ref, b_ref, out, r0):
        for t in range(buf.shape[1]):
            cols = slice(t * LANES, (t + 1) * LANES)
            acc = b_ref[:, cols] + w_ref[0:1, cols] * buf[slot_b, t, r0 + first:r0 + first + CHUNK, :]
            for j in range(1, w_ref.shape[0]):
                acc = acc + w_ref[j:j + 1, cols] * buf[slot_b, t, r0 + first + j:r0 + first + j + CHUNK, :]
            out[r0:r0 + CHUNK, cols] = _silu(acc)

    def convs(c):
        run_task(c, 0)
        conv_silu(xbc_buf, scw_ref, scb_ref, xbc_c, c * CHUNK)
        run_task(c, 1)
        conv_silu(qk_buf, mcw_ref, mcb_ref, qk_c, c * CHUNK)
        run_task(c, 2)

    for c in range(nchunk):
        convs(c)
        rows = slice(c * CHUNK, (c + 1) * CHUNK)
        tril3 = tril_ref[...]

        xs = xbc_c[rows, 0:SSD_WIDTH]
        bm = xbc_c[rows, SSD_WIDTH:SSD_WIDTH + 2 * SSD_STATE].astype(BF16)
        cm = xbc_c[rows, SSD_WIDTH + 2 * SSD_STATE:SSD_XBC].astype(BF16)
        dt = _softplus(g_buf[slot_b, rows, 0:SSD_HEADS] + dtb_ref[...])
        a_cs = _dot(tril3, _split3(dt * (-jnp.exp(alog_ref[...])), 0))
        dt_e = _dot(_split3(dt, 1), e16_ref[...])
        a_col = _dot(_split3(a_cs, 1), e16_ref[...])
        a_row = jnp.sum(a_col * seq_ref[...], axis=0, keepdims=True)
        lmat = jnp.exp(a_col - a_row + sge_ref[...])
        cb = jnp.concatenate(
            [_dot_nt(cm[:, g * SSD_STATE:(g + 1) * SSD_STATE],
                     jnp.concatenate([bm[:, g * SSD_STATE:(g + 1) * SSD_STATE]] * (SSD_HEADS // SSD_GROUPS), axis=0))
             for g in range(SSD_GROUPS)], axis=1)
        m_all = (cb * lmat).astype(BF16)
        run_task(c, 3)
        xdt = xs * dt_e
        lane = lax.broadcasted_iota(jnp.int32, (CHUNK, LANES), 1)
        y_parts = []
        for j in range(SSD_HEADS // 2):
            xp = xdt[:, j * LANES:(j + 1) * LANES]
            xbd = jnp.concatenate([jnp.where(lane < SSD_HEAD_DIM, xp, 0.0),
                                   jnp.where(lane >= SSD_HEAD_DIM, xp, 0.0)], axis=0).astype(BF16)
            y_parts.append(_dot(m_all[:, j * LANES:(j + 1) * LANES], xbd))
        y = jnp.concatenate(y_parts, axis=1)
        half = SSD_WIDTH // SSD_GROUPS
        y_int = jnp.concatenate(
            [_dot(cm[:, g * SSD_STATE:(g + 1) * SSD_STATE], sst[g].astype(BF16)) for g in range(SSD_GROUPS)], axis=1)
        y = y + y_int * jnp.exp(a_col)
        a_last = a_col[CHUNK - 1:CHUNK, :]
        xd_b = (xdt * jnp.exp(a_last - a_col)).astype(BF16)
        st_scale = jnp.exp(a_last)
        for g in range(SSD_GROUPS):
            sst[g] = sst[g] * st_scale[:, g * half:(g + 1) * half] + _dot_tn(
                bm[:, g * SSD_STATE:(g + 1) * SSD_STATE], xd_b[:, g * half:(g + 1) * half])
        y = (y + dskip_ref[...] * xs) * _silu(z_buf[slot_b, rows, :])
        for g in range(SSD_GROUPS):
            yg = y[:, g * half:(g + 1) * half]
            yg = yg * lax.rsqrt(jnp.mean(yg * yg, axis=-1, keepdims=True) + EPS) * snw_ref[:, g * half:(g + 1) * half]
            mix_buf[slot_b, rows, g * half:(g + 1) * half] = yg.astype(BF16)

        run_task(c, 4)

        gi = g_buf[slot_b, rows, LANES:LANES + 2 * ML_HEADS] + gb_ref[...]
        col16 = lax.broadcasted_iota(jnp.int32, (CHUNK, 2 * ML_HEADS), 1)
        gi = jnp.where(col16 < ML_HEADS, gi, jnp.minimum(gi, 0.0) - jnp.log1p(jnp.exp(-jnp.abs(gi))))
        gi = jnp.where(col16 < ML_HEADS, gi, _dot(tril3, _split3(gi, 0)))
        gi_e = _dot(_split3(gi, 1), e2_ref[...])
        ig_e = gi_e[:, 0:ML_WIDTH]
        b_col = gi_e[:, ML_WIDTH:2 * ML_WIDTH]
        c_row = jnp.sum((ig_e - b_col) * meq_ref[...], axis=0, keepdims=True)
        dmat = b_col + c_row + mge_ref[...]
        heads = range(ML_HEADS)
        hs = lambda arr, hd: arr[:, hd * LANES:(hd + 1) * LANES]
        ones_blk = jnp.ones((CHUNK, LANES), BF16)
        m_prev = mm[...]
        m_inter = b_col + m_prev
        q = [qk_c[rows, hd * LANES:(hd + 1) * LANES].astype(BF16) for hd in heads]
        kf = [qk_c[rows, ML_WIDTH + hd * LANES:ML_WIDTH + (hd + 1) * LANES] * (ML_HEAD_DIM ** -0.5) for hd in heads]
        v_aug = [jnp.concatenate([v_buf[slot_b, rows, hd * LANES:(hd + 1) * LANES], ones_blk], axis=1)
                 for hd in heads]
        s = [_dot_nt(q[hd], kf[hd].astype(BF16)) for hd in heads]
        qc = [_dot(q[hd], mst[hd].astype(BF16)) for hd in heads]
        run_task(c, 5)
        m_t = jnp.maximum(m_inter, jnp.concatenate(
            [jnp.broadcast_to(jnp.max(hs(dmat, hd), axis=-1, keepdims=True), (CHUNK, LANES)) for hd in heads], axis=1))
        w = jnp.exp(dmat - m_t)
        inter = jnp.exp(m_inter - m_t)
        den_floor = jnp.exp(-m_t)
        nd = [_dot((s[hd] * hs(w, hd)[:, 0:CHUNK]).astype(BF16), v_aug[hd])
              + qc[hd] * jnp.concatenate([hs(inter, hd)] * 2, axis=1) for hd in heads]
        run_task(c, 6)
        hh = [nd[hd][:, 0:LANES] / jnp.maximum(jnp.abs(nd[hd][:, LANES:2 * LANES]), hs(den_floor, hd)) for hd in heads]
        hc = [hh[hd] - jnp.mean(hh[hd], axis=-1, keepdims=True) for hd in heads]
        var = [jnp.mean(hc[hd] * hc[hd], axis=-1, keepdims=True) for hd in heads]
        for hd in heads:
            hn = hc[hd] * lax.rsqrt(var[hd] + EPS) * mnw_ref[:, hd * LANES:(hd + 1) * LANES]
            mix_buf[slot_b, rows, SSD_WIDTH + hd * LANES:SSD_WIDTH + (hd + 1) * LANES] = (
                jax.nn.sigmoid(o_buf[slot_b, rows, hd * LANES:(hd + 1) * LANES]) * hn).astype(BF16)
        run_task(c, 7)
        g_row = b_col[CHUNK - 1:CHUNK, :]
        a = g_row - b_col + ig_e
        m_new = jnp.maximum(g_row + m_prev, jnp.max(a, axis=0, keepdims=True))
        wk = jnp.exp(a - m_new)
        cs = jnp.exp(g_row + m_prev - m_new)
        for hd in heads:
            mst[hd] = mst[hd] * jnp.concatenate([hs(cs, hd)] * 2, axis=1) + _dot_tn(
                (kf[hd] * hs(wk, hd)).astype(BF16), v_aug[hd])
        mm[...] = m_new


def _const_spec(shape):
    nd = len(shape)
    return pl.BlockSpec(shape, lambda s: (0,) * nd, pipeline_mode=pl.Buffered(1))


def _mixer_call(x, consts, tl):
    bsz, seq, _ = x.shape
    nl = seq // tl
    n_pairs = bsz * nl // 2
    x_pairs = x.reshape(n_pairs, 2 * tl, D_MODEL)
    pair_in = lambda i: (jnp.minimum(i, n_pairs - 1), 0, 0)
    pair_out = lambda i: (jnp.maximum(i - 1, 0), 0, 0)
    in_specs = ([pl.BlockSpec((None, 2 * tl, D_MODEL), pair_in), pl.BlockSpec((None, 2 * tl, D_MODEL), pair_out)]
                + [_const_spec(c.shape) for c in consts])
    out_specs = [pl.BlockSpec((None, 2 * tl, D_MODEL), pair_out), pl.BlockSpec((None, 2 * tl, D_MODEL), pair_out),
                 pl.BlockSpec((LOGIT_ROWS, 2 * tl), lambda i: (0, jnp.maximum(i - 1, 0)))]
    out_shape = [jax.ShapeDtypeStruct((n_pairs, 2 * tl, D_MODEL), F32),
                 jax.ShapeDtypeStruct((n_pairs, 2 * tl, D_MODEL), BF16),
                 jax.ShapeDtypeStruct((LOGIT_ROWS, bsz * seq), F32)]
    scratch = [
        pltpu.VMEM((2, SSD_XBC // LANES, tl + CONV_PAD, LANES), F32),
        pltpu.VMEM((2, 2 * ML_WIDTH // LANES, tl + CONV_PAD, LANES), F32),
        pltpu.VMEM((2, tl, SSD_WIDTH), F32),
        pltpu.VMEM((2, tl, ML_WIDTH), BF16),
        pltpu.VMEM((2, tl, ML_WIDTH), F32),
        pltpu.VMEM((2, tl, GATE_COLS), F32),
        pltpu.VMEM((tl, SSD_XBC), F32),
        pltpu.VMEM((tl, 2 * ML_WIDTH), F32),
        pltpu.VMEM((2, tl, D_MIX), BF16),
        pltpu.VMEM((SSD_XBC // LANES, CONV_PAD, LANES), F32),
        pltpu.VMEM((2 * ML_WIDTH // LANES, CONV_PAD, LANES), F32),
        pltpu.VMEM((SSD_GROUPS, SSD_STATE, SSD_WIDTH // SSD_GROUPS), F32),
        pltpu.VMEM((ML_HEADS, ML_HEAD_DIM, 2 * LANES), F32),
        pltpu.VMEM((1, ML_WIDTH), F32),
        pltpu.VMEM((tl, D_MODEL), BF16),
    ]
    return pl.pallas_call(
        functools.partial(_mixer_kernel, tl=tl, nl=nl),
        grid=(n_pairs + 1,),
        in_specs=in_specs,
        out_specs=out_specs,
        out_shape=out_shape,
        scratch_shapes=scratch,
        compiler_params=pltpu.CompilerParams(
            dimension_semantics=("arbitrary",), vmem_limit_bytes=V7X_VMEM_LIMIT),
        name="mixer",
    )(x_pairs, x_pairs, *consts)


def _mixer_consts(norm1_w, w_in, ssd_conv_w, ssd_conv_b, ssd_dt_bias, ssd_a_log, ssd_d, ssd_norm_w,
                  ml_conv_w, ml_conv_b, ml_i_bias, ml_f_bias, ml_norm_w, w_out, norm2_w,
                  router_g_w, router_e_w):
    o1 = SSD_WIDTH
    o2 = o1 + SSD_XBC
    o3 = o2 + SSD_HEADS
    o4 = o3 + 2 * ML_WIDTH
    o5 = o4 + ML_WIDTH
    o6 = o5 + ML_WIDTH
    o7 = o6 + ML_HEADS
    zpad = lambda n: jnp.zeros((D_MODEL, n), F32)
    w_gate = jnp.concatenate([w_in[:, o2:o3], zpad(LANES - SSD_HEADS),
                              w_in[:, o6:o7], w_in[:, o7:], zpad(LANES - 2 * ML_HEADS)], axis=1)
    row = lambda v: v.reshape(1, -1).astype(F32)
    w_route = jnp.concatenate([router_e_w, router_g_w,
                               jnp.zeros((D_MODEL, LOGIT_ROWS - N_EXPERTS - MOE_GROUPS), F32)], axis=1).T

    c1024 = np.arange(SSD_WIDTH)
    r64 = np.arange(CHUNK)[:, None]
    e16 = np.tile(c1024[None, :] // SSD_HEAD_DIM == np.arange(SSD_HEADS)[:, None], (3, 1))
    e2 = np.tile(np.arange(2 * ML_WIDTH)[None, :] // LANES == np.arange(2 * ML_HEADS)[:, None], (3, 1))
    tril = np.tile(np.arange(CHUNK)[None, :] <= r64, (1, 3))
    s_pos = (c1024 % SSD_HEAD_DIM)[None, :]
    m_pos = (c1024 % LANES)[None, :]
    as_bf16 = lambda m: jnp.asarray(m.astype(np.float32), BF16)
    as_f32 = lambda m: jnp.asarray(m.astype(np.float32))
    e16, e2, tril = as_bf16(e16), as_bf16(e2), as_bf16(tril)
    causal = lambda keep: jnp.asarray(np.where(keep, 0.0, -np.inf).astype(np.float32))
    seq, sge, meq, mge = as_f32(r64 == s_pos), causal(r64 >= s_pos), as_f32(r64 == m_pos), causal(r64 >= m_pos)
    return [
        row(norm1_w),
        w_in[:, :o1].astype(BF16), w_in[:, o1:o2].astype(BF16), w_in[:, o3:o4].astype(BF16),
        w_in[:, o4:o5].astype(BF16), w_in[:, o5:o6].astype(BF16), w_gate.astype(BF16),
        ssd_conv_w.astype(F32), row(ssd_conv_b), row(ssd_dt_bias),
        row(ssd_a_log), row(jnp.repeat(ssd_d, SSD_HEAD_DIM)), row(ssd_norm_w),
        ml_conv_w.astype(F32), row(ml_conv_b), row(jnp.concatenate([ml_i_bias, ml_f_bias])), row(ml_norm_w),
        w_out.astype(BF16), row(norm2_w), w_route.astype(BF16),
        e16, e2, tril, seq, sge, meq, mge,
    ]


def _route_kernel(lg_ref, bias_ref, striu_ref, lpos_ref, gate_ref, cnt_ref):
    tr = striu_ref.shape[0]
    for half in range(lg_ref.shape[1] // tr):
        _route_tile(lg_ref[:, half * tr:(half + 1) * tr] + bias_ref[...], striu_ref,
                    lpos_ref.at[:, pl.ds(half * tr, tr)], gate_ref.at[:, pl.ds(half * tr, tr)], cnt_ref.at[half])


def _route_tile(lg, striu_ref, lpos_ref, gate_ref, cnt_ref):
    tr = lg.shape[1]
    el = lg[0:N_EXPERTS]
    gl = lg[N_EXPERTS:N_EXPERTS + 8]
    gmax = jnp.max(gl, axis=0, keepdims=True)
    pg = 1.0 / jnp.sum(jnp.exp(gl - gmax), axis=0, keepdims=True)
    grow = lax.broadcasted_iota(jnp.int32, gl.shape, 0)
    gsel = jnp.min(jnp.where(gl == gmax, grow, 8), axis=0, keepdims=True)
    erow = lax.broadcasted_iota(jnp.int32, el.shape, 0)
    m1 = jnp.where((erow >> 3) == gsel, el, NEG_INF)
    v1 = jnp.max(m1, axis=0, keepdims=True)
    i1 = jnp.min(jnp.where(m1 == v1, erow, N_EXPERTS), axis=0, keepdims=True)
    m2 = jnp.where(erow == i1, NEG_INF, m1)
    v2 = jnp.max(m2, axis=0, keepdims=True)
    i2 = jnp.min(jnp.where(m2 == v2, erow, N_EXPERTS), axis=0, keepdims=True)
    e2 = jnp.exp(v2 - v1)
    g1 = pg / (1.0 + e2)
    g2 = g1 * e2
    hit1 = erow == i1
    hit2 = erow == i2
    oh = jnp.where(hit1, 1.0, jnp.where(hit2, 1.0, 0.0))
    earlier = _dot(oh.astype(BF16), striu_ref[...])
    cnt = jnp.sum(oh, axis=1, keepdims=True)
    seg = jnp.floor((cnt + (ROW_ALIGN - 1)) * (1.0 / ROW_ALIGN)) * ROW_ALIGN
    l1 = jnp.sum(jnp.where(hit1, earlier, jnp.where(erow < i1, seg, 0.0)), axis=0, keepdims=True)
    l2 = jnp.sum(jnp.where(hit2, earlier, jnp.where(erow < i2, seg, 0.0)), axis=0, keepdims=True)
    lpos_ref[...] = jnp.concatenate([l1.astype(jnp.int32), l2.astype(jnp.int32), jnp.zeros((6, tr), jnp.int32)], axis=0)
    gate_ref[...] = jnp.concatenate([g1, g2, jnp.zeros((6, tr), F32)], axis=0)
    cnt_ref[...] = jnp.broadcast_to(cnt, cnt_ref.shape)


def _route_call(lg_t, bias_col, tr):
    t = lg_t.shape[1]
    striu = jnp.asarray(np.triu(np.ones((tr, tr), np.float32), 1), BF16)
    return pl.pallas_call(
        _route_kernel,
        grid=(t // (ROUTE_TILES * tr),),
        in_specs=[pl.BlockSpec((LOGIT_ROWS, ROUTE_TILES * tr), lambda i: (0, i)),
                  pl.BlockSpec((LOGIT_ROWS, 1), lambda i: (0, 0)),
                  pl.BlockSpec((tr, tr), lambda i: (0, 0))],
        out_specs=[pl.BlockSpec((8, ROUTE_TILES * tr), lambda i: (0, i)),
                   pl.BlockSpec((8, ROUTE_TILES * tr), lambda i: (0, i)),
                   pl.BlockSpec((ROUTE_TILES, N_EXPERTS, LANES), lambda i: (i, 0, 0))],
        out_shape=[jax.ShapeDtypeStruct((8, t), jnp.int32),
                   jax.ShapeDtypeStruct((8, t), F32),
                   jax.ShapeDtypeStruct((t // tr, N_EXPERTS, LANES), F32)],
        compiler_params=pltpu.CompilerParams(dimension_semantics=("arbitrary",)),
        name="route",
    )(lg_t, bias_col, striu)


def _run_copies(runs_ref, copy_run, max_piece, filler_far, enable=None):
    for e in range(N_EXPERTS + 1):
        dst = runs_ref[0, e] if e < N_EXPERTS else filler_far
        src = runs_ref[1, e]
        n = runs_ref[2, e] if enable is None else jnp.where(enable, runs_ref[2, e], 0)
        p = max_piece
        while p >= ROW_ALIGN:
            done = n & ~(2 * p - 1)

            @pl.when((n & p) != 0)
            def _(p=p, done=done, dst=dst, src=src):
                copy_run(pl.multiple_of(dst + done, ROW_ALIGN), pl.multiple_of(src + done, ROW_ALIGN), p)
            p //= 2


def _dispatch_kernel(runs_ref, prev_runs_ref, lpos_ref, h2_ref, xs_ref, loc, sem, *, max_spare):
    n_tiles = runs_ref.shape[0]
    tr = h2_ref.shape[0] // n_tiles
    nloc = loc.shape[1]
    step = pl.program_id(0)
    last = pl.num_programs(0) - 1
    used = runs_ref[0, 0, N_EXPERTS]

    def wait_block(s):
        pltpu.make_async_copy(loc.at[s], xs_ref.at[pl.ds(0, nloc)], sem.at[s]).wait()

    def send_block(table, s, enable=None):
        def copy_run(dst, src, n):
            pltpu.make_async_copy(loc.at[s, pl.ds(src, n)], xs_ref.at[pl.ds(dst, n)], sem.at[s]).start()
        _run_copies(table, copy_run, tr, used + s * (nloc - TOP_K * tr), enable)

    def sort_tile(k, s):
        cols = slice(k * tr, (k + 1) * tr)
        jrow = lax.broadcasted_iota(jnp.int32, (nloc, tr), 0)
        perm = jnp.where(jrow == lpos_ref[0:1, cols], 1.0, jnp.where(jrow == lpos_ref[1:2, cols], 1.0, 0.0)).astype(BF16)
        loc[s] = _pack_rows(_dot(perm, h2_ref[cols, :]), rounded=True)

    for k in range(n_tiles):
        s = k % 2
        if k < 2:
            pl.when(step > 0)(functools.partial(wait_block, s))
        else:
            wait_block(s)
        if k == 0:
            send_block(prev_runs_ref.at[n_tiles - 1], 1, enable=step > 0)
        else:
            send_block(runs_ref.at[k - 1], 1 - s)
        sort_tile(k, s)

    @pl.when(step == last)
    def _():
        send_block(runs_ref.at[n_tiles - 1], 1)
        wait_block(0)
        wait_block(1)
        loc[0, 0:tr, :] = jnp.zeros((tr, loc.shape[2]), loc.dtype)
        spare = xs_ref.shape[0] - used

        def zero_copy(off, n):
            return pltpu.make_async_copy(loc.at[0, pl.ds(0, n)],
                                         xs_ref.at[pl.ds(pl.multiple_of(used + off, ROW_ALIGN), n)], sem.at[0])

        def pieces(do):
            for i in range(max_spare // tr):
                @pl.when((i + 1) * tr <= spare)
                def _(i=i):
                    do(zero_copy(i * tr, tr))
            p = tr // 2
            while p >= ROW_ALIGN:
                @pl.when((spare & p) != 0)
                def _(p=p):
                    do(zero_copy(spare & ~(2 * p - 1), p))
                p //= 2

        pieces(lambda c: c.start())
        pieces(lambda c: c.wait())


def _dispatch_call(runs, lpos, h2, tr, n_rows):
    t = h2.shape[0]
    k = DISPATCH_TILES
    groups = runs.reshape(t // (k * tr), k, 3, N_EXPERTS + 1)
    return pl.pallas_call(
        functools.partial(_dispatch_kernel, max_spare=n_rows - t * TOP_K),
        grid=(t // (k * tr),),
        in_specs=[pl.BlockSpec((None, k, 3, N_EXPERTS + 1), lambda i: (i, 0, 0, 0), memory_space=pltpu.SMEM),
                  pl.BlockSpec((None, k, 3, N_EXPERTS + 1), lambda i: (jnp.maximum(i - 1, 0), 0, 0, 0),
                               memory_space=pltpu.SMEM),
                  pl.BlockSpec((8, k * tr), lambda i: (0, i)),
                  pl.BlockSpec((k * tr, D_MODEL), lambda i: (i, 0))],
        out_specs=pl.BlockSpec(memory_space=pl.ANY),
        out_shape=jax.ShapeDtypeStruct((n_rows, D_MODEL // 2), jnp.uint32),
        scratch_shapes=[pltpu.VMEM((2, _local_rows(tr), D_MODEL // 2), jnp.uint32), pltpu.SemaphoreType.DMA((2,))],
        compiler_params=pltpu.CompilerParams(dimension_semantics=("arbitrary",), vmem_limit_bytes=V7X_VMEM_LIMIT),
        name="dispatch",
    )(groups, groups, lpos, h2)


def _experts_kernel(tile_ref, exp_ref, starts_ref, xs_ref, wg_ref, wu_ref, wd_ref, ys_ref, wg_b, wu_b, wd_b):
    i = pl.program_id(0)
    tm = xs_ref.shape[0]
    e = exp_ref[i]
    tile = tile_ref[i]
    is_first = jnp.logical_or(i == 0, tile_ref[jnp.maximum(i - 1, 0)] != tile)

    @pl.when(jnp.logical_and(e < N_EXPERTS, jnp.logical_or(i == 0, exp_ref[jnp.maximum(i - 1, 0)] != e)))
    def _():
        wg_b[...] = wg_ref[...].astype(BF16)
        wu_b[...] = wu_ref[...].astype(BF16)
        wd_b[...] = wd_ref[...].astype(BF16)

    def expert_rows(other):
        hm = tm // 2
        xh = [_unpack_rows(xs_ref[h * hm:(h + 1) * hm, :]) for h in range(2)]
        gu = [(_dot(xh[h], wg_b[...]), _dot(xh[h], wu_b[...])) for h in range(2)]
        act = [(_silu(gu[h][0]) * gu[h][1]).astype(BF16) for h in range(2)]
        y = jnp.concatenate([_pack_rows(_dot(act[h], wd_b[...]), rounded=False) for h in range(2)], axis=0)
        rows = tile * tm + lax.broadcasted_iota(jnp.int32, (tm, 1), 0)
        return jnp.where((rows >= starts_ref[e]) & (rows < starts_ref[e + 1]), y, other)

    @pl.when(jnp.logical_and(e < N_EXPERTS, is_first))
    def _():
        ys_ref[...] = expert_rows(jnp.zeros_like(ys_ref))

    @pl.when(jnp.logical_and(e < N_EXPERTS, jnp.logical_not(is_first)))
    def _():
        ys_ref[...] = expert_rows(ys_ref[...])

    @pl.when(e == N_EXPERTS)
    def _():
        ys_ref[...] = jnp.zeros_like(ys_ref)


def _experts_call(item_tile, item_exp, starts_ext, xs, wg, wu, wd, tm):
    n = xs.shape[0]
    n_items = item_tile.shape[0]
    w_idx = lambda i, tile, ex, st: (jnp.minimum(ex[i], N_EXPERTS - 1), 0, 0)
    grid_spec = pltpu.PrefetchScalarGridSpec(
        num_scalar_prefetch=3,
        grid=(n_items,),
        in_specs=[pl.BlockSpec((tm, D_MODEL // 2), lambda i, tile, ex, st: (tile[i], 0)),
                  pl.BlockSpec((None, D_MODEL, D_EXPERT), w_idx),
                  pl.BlockSpec((None, D_MODEL, D_EXPERT), w_idx),
                  pl.BlockSpec((None, D_EXPERT, D_MODEL), w_idx)],
        out_specs=pl.BlockSpec((tm, D_MODEL // 2), lambda i, tile, ex, st: (tile[i], 0)),
        scratch_shapes=[pltpu.VMEM((D_MODEL, D_EXPERT), BF16), pltpu.VMEM((D_MODEL, D_EXPERT), BF16),
                        pltpu.VMEM((D_EXPERT, D_MODEL), BF16)],
    )
    return pl.pallas_call(
        _experts_kernel,
        grid_spec=grid_spec,
        out_shape=jax.ShapeDtypeStruct((n, D_MODEL // 2), jnp.uint32),
        compiler_params=pltpu.CompilerParams(
            dimension_semantics=("arbitrary",), vmem_limit_bytes=V7X_VMEM_LIMIT),
        name="experts",
    )(item_tile, item_exp, starts_ext, xs, wg, wu, wd)


def _local_rows(tr):
    return TOP_K * tr + N_EXPERTS * ROW_ALIGN


def _expert_items(counts, n_rows, tm):
    n_tiles = n_rows // tm
    n_items = n_tiles + N_EXPERTS - 1
    starts = jnp.concatenate([jnp.zeros((1,), jnp.int32), jnp.cumsum(counts).astype(jnp.int32)])
    lo = starts[:-1] // tm
    hi = (starts[1:] - 1) // tm
    per = jnp.where(counts > 0, hi - lo + 1, 0)
    cum = jnp.cumsum(per)
    idx = jnp.arange(n_items, dtype=jnp.int32)
    e = jnp.sum(idx[:, None] >= cum[None, :], axis=1).astype(jnp.int32)
    valid = idx < cum[-1]
    e_c = jnp.minimum(e, N_EXPERTS - 1)
    first_item = jnp.sum(jnp.where(e_c[:, None] == jnp.arange(N_EXPERTS)[None, :], (lo - (cum - per))[None, :], 0), axis=1)
    tile = first_item + idx
    spare_tile = -(-starts[-1] // tm) + idx - cum[-1]
    item_tile = jnp.where(valid, tile, jnp.minimum(spare_tile, n_tiles - 1)).astype(jnp.int32)
    item_exp = jnp.where(valid, e_c, jnp.where(spare_tile < n_tiles, N_EXPERTS, N_EXPERTS + 1)).astype(jnp.int32)
    starts_ext = jnp.concatenate([starts, starts[-1:], starts[-1:]])
    return item_tile, item_exp, starts, starts_ext


def _by_parity(body):
    def kernel(*refs, **kw):
        for parity in range(2):
            @pl.when(pl.program_id(0) % 2 == parity)
            def _(parity=parity):
                body(*refs, slot=parity, **kw)
    return kernel


def _combine_step(runs_ref, next_runs_ref, lpos_ref, gate_ref, next_lpos_ref, next_gate_ref, x1_ref, nf_ref, ys_ref,
                  out_ref, loc, sel_buf, sem, *, slot):
    tr = x1_ref.shape[0]
    nloc = loc.shape[1]
    step = pl.program_id(0)

    def gather(table, s, enable=None):
        def copy_run(dst, src, n):
            pltpu.make_async_copy(ys_ref.at[pl.ds(dst, n)], loc.at[s, pl.ds(src, n)], sem.at[s]).start()
        _run_copies(table, copy_run, tr, 0, enable)

    def build_sel(pos_ref, g_ref, s):
        jrow = lax.broadcasted_iota(jnp.int32, (nloc, tr), 0)
        sel_buf[s] = jnp.where(jrow == pos_ref[0:1, :], g_ref[0:1, :],
                               jnp.where(jrow == pos_ref[1:2, :], g_ref[1:2, :], 0.0)).astype(BF16)

    @pl.when(step == 0)
    def _():
        gather(runs_ref, slot)
        build_sel(lpos_ref, gate_ref, slot)

    pltpu.make_async_copy(ys_ref.at[pl.ds(0, nloc)], loc.at[slot], sem.at[slot]).wait()

    gather(next_runs_ref, 1 - slot, enable=step + 1 < pl.num_programs(0))
    y = _dot_tn(sel_buf[slot], _unpack_rows(loc[slot]))
    build_sel(next_lpos_ref, next_gate_ref, 1 - slot)
    out_ref[...] = _rms(x1_ref[...] + y, nf_ref[...])


def _combine_call(runs, lpos, gates, x1, nf_w, ys, tr):
    t = x1.shape[0]
    return pl.pallas_call(
        _by_parity(_combine_step),
        grid=(t // tr,),
        in_specs=[pl.BlockSpec((None, 3, N_EXPERTS + 1), lambda i: (i, 0, 0), memory_space=pltpu.SMEM),
                  pl.BlockSpec((None, 3, N_EXPERTS + 1), lambda i: (jnp.minimum(i + 1, t // tr - 1), 0, 0),
                               memory_space=pltpu.SMEM),
                  pl.BlockSpec((8, tr), lambda i: (0, i)),
                  pl.BlockSpec((8, tr), lambda i: (0, i)),
                  pl.BlockSpec((8, tr), lambda i: (0, jnp.minimum(i + 1, t // tr - 1))),
                  pl.BlockSpec((8, tr), lambda i: (0, jnp.minimum(i + 1, t // tr - 1))),
                  pl.BlockSpec((tr, D_MODEL), lambda i: (i, 0)),
                  pl.BlockSpec((1, D_MODEL), lambda i: (0, 0)),
                  pl.BlockSpec(memory_space=pl.ANY)],
        out_specs=pl.BlockSpec((tr, D_MODEL), lambda i: (i, 0)),
        out_shape=jax.ShapeDtypeStruct((t, D_MODEL), F32),
        scratch_shapes=[pltpu.VMEM((2, _local_rows(tr), D_MODEL // 2), jnp.uint32),
                        pltpu.VMEM((2, _local_rows(tr), tr), BF16), pltpu.SemaphoreType.DMA((2,))],
        compiler_params=pltpu.CompilerParams(dimension_semantics=("arbitrary",), vmem_limit_bytes=V7X_VMEM_LIMIT),
        name="combine",
    )(runs, runs, lpos, gates, lpos, gates, x1, nf_w, ys)


SEQ_TILE = 4 * CHUNK
TOKEN_TILE = 512
ROW_TILE = 512
ROUTE_TILES = 4
DISPATCH_TILES = 4


def kernel(x, norm1_w, w_in, ssd_conv_w, ssd_conv_b, ssd_dt_bias, ssd_a_log, ssd_d, ssd_norm_w, ml_conv_w, ml_conv_b, ml_i_bias, ml_f_bias, ml_norm_w, w_out, norm2_w, router_g_w, router_g_b, router_e_w, router_e_b, exp_w_gate, exp_w_up, exp_w_down, norm_f_w):
    bsz, seq, d = x.shape
    t = bsz * seq
    tl, tr, tm = SEQ_TILE, TOKEN_TILE, ROW_TILE
    assert d == D_MODEL and norm1_w.shape[0] == 1 and seq % tl == 0 and t % (2 * tl) == 0 and t % (max(ROUTE_TILES, DISPATCH_TILES) * tr) == 0

    consts = _mixer_consts(norm1_w[0], w_in[0], ssd_conv_w[0], ssd_conv_b[0], ssd_dt_bias[0], ssd_a_log[0],
                           ssd_d[0], ssd_norm_w[0], ml_conv_w[0], ml_conv_b[0], ml_i_bias[0], ml_f_bias[0],
                           ml_norm_w[0], w_out[0], norm2_w[0], router_g_w[0], router_e_w[0])
    x1, h2, lg_t = _mixer_call(x, consts, tl)
    x1 = x1.reshape(t, D_MODEL)
    h2 = h2.reshape(t, D_MODEL)

    bias_col = jnp.concatenate([router_e_b[0], router_g_b[0],
                                jnp.full((LOGIT_ROWS - N_EXPERTS - MOE_GROUPS,), STAB_INIT, F32)]).reshape(-1, 1)
    lpos, gates, cnt = _route_call(lg_t, bias_col.astype(F32), tr)

    seg = (cnt[:, :, 0].astype(jnp.int32) + (ROW_ALIGN - 1)) // ROW_ALIGN * ROW_ALIGN
    n_tok_tiles = t // tr
    n_rows = -(-(t * TOP_K + n_tok_tiles * N_EXPERTS * ROW_ALIGN + _local_rows(tr)) // tm) * tm
    item_tile, item_exp, starts, starts_ext = _expert_items(jnp.sum(seg, axis=0), n_rows, tm)
    run_dst = starts[None, :N_EXPERTS] + jnp.cumsum(seg, axis=0) - seg
    run_src = jnp.cumsum(seg, axis=1) - seg
    used = jnp.sum(seg, axis=1, keepdims=True)
    fill = jnp.concatenate([jnp.full_like(used, starts[N_EXPERTS]), used, _local_rows(tr) - used], axis=1)[:, :, None]
    runs = jnp.concatenate([jnp.stack([run_dst, run_src, seg], axis=1), fill], axis=2).astype(jnp.int32)

    xs = _dispatch_call(runs, lpos, h2, tr, n_rows)
    ys = _experts_call(item_tile, item_exp, starts_ext, xs, exp_w_gate[0], exp_w_up[0], exp_w_down[0], tm)
    out = _combine_call(runs, lpos, gates, x1, norm_f_w.reshape(1, -1).astype(F32), ys, tr)
    return out.reshape(bsz, seq, D_MODEL)
```
